```python
import jax, jax.numpy as jnp
from jax import lax
import numpy as np

D_MODEL = 1024
BATCH = 8
SEQ = 2048
DEPTH = 1
DEC_BATCH = 128
DEC_SEQ = 4
PAST_LEN = 2048
PAGE_SIZE = 128

ATTN_HEADS = 8
HEAD_DIM = 64
ATTN_WIDTH = ATTN_HEADS * HEAD_DIM
MOBA_BLOCK = 256
MOBA_TOPK = 3
QUERY_BLOCK = 128
MLP_GROUPS = 8
MLP_CH = (D_MODEL - ATTN_WIDTH) // MLP_GROUPS
MLP_WIDTH = MLP_GROUPS * MLP_CH
CHUNK = 128
IN_WIDTH = 3 * ATTN_WIDTH + 2 * MLP_WIDTH
MOE_GROUPS = 4
EXPERTS_PER_GROUP = 4
N_EXPERTS = MOE_GROUPS * EXPERTS_PER_GROUP
TOPK_IN_GROUP = 2
D_EXPERT = D_MODEL // 2
EPS = 1e-6

kernel_name = 'hymba_moba_gmlp_hmoe_step'


def rmsnorm(x, g):
    xf = x.astype(jnp.float32)
    y = xf * lax.rsqrt(jnp.mean(xf * xf, axis=-1, keepdims=True) + EPS)
    return (y * g.astype(jnp.float32)).astype(x.dtype)


def alibi_slopes(n):
    return jnp.exp2(-8.0 * (jnp.arange(n, dtype=jnp.float32) + 1.0) / n)


def to_blocks(k):
    L = k.shape[0]
    nb = -(-L // MOBA_BLOCK)
    k = jnp.pad(k, ((0, nb * MOBA_BLOCK - L), (0, 0), (0, 0)))
    return k.reshape(nb, MOBA_BLOCK, ATTN_HEADS, HEAD_DIM).transpose(2, 0, 1, 3)


def moba_block(qb, pos, kbh, vbh, kmean, slopes):
    Q = qb.shape[0]
    nb = kbh.shape[1]
    n_sel = min(MOBA_TOPK, nb)
    qf = qb.astype(jnp.float32)
    cur = pos // MOBA_BLOCK
    gate = jnp.einsum('qhd,hnd->qhn', qf, kmean)
    past_ok = jnp.arange(nb)[None, None, :] < cur[:, None, None]
    gate = jnp.where(past_ok, gate, -jnp.inf)
    _, sel = lax.top_k(gate, n_sel)
    sel_ok = sel < cur[:, None, None]
    own = jnp.broadcast_to(cur[:, None, None], (Q, ATTN_HEADS, 1)).astype(sel.dtype)
    idx = jnp.concatenate([sel, own], axis=-1)
    ok = jnp.concatenate([sel_ok, jnp.ones((Q, ATTN_HEADS, 1), bool)], axis=-1)
    hidx = jnp.arange(ATTN_HEADS)[None, :, None]
    kg = kbh[hidx, idx].astype(jnp.float32)
    vg = vbh[hidx, idx].astype(jnp.float32)
    kpos = idx[..., None] * MOBA_BLOCK + jnp.arange(MOBA_BLOCK)
    dist = pos[:, None, None, None] - kpos
    keep = ok[..., None] & (dist >= 0)
    s = jnp.einsum('qhd,qhjbd->qhjb', qf, kg) * (HEAD_DIM ** -0.5)
    s = s - slopes[None, :, None, None] * dist.astype(jnp.float32)
    s = jnp.where(keep, s, -jnp.inf)
    J = idx.shape[-1]
    p = jax.nn.softmax(s.reshape(Q, ATTN_HEADS, J * MOBA_BLOCK), axis=-1)
    p = p.reshape(Q, ATTN_HEADS, J, MOBA_BLOCK)
    return jnp.einsum('qhjb,qhjbd->qhd', p, vg).astype(qb.dtype)


def moba_prompt(q, k, v, slopes):
    S = q.shape[1]
    qbl = min(QUERY_BLOCK, S)
    nq = S // qbl
    pos = jnp.arange(S, dtype=jnp.int32).reshape(nq, qbl)

    def one_seq(args):
        qs, ks, vs = args
        kbh, vbh = to_blocks(ks), to_blocks(vs)
        kmean = jnp.mean(kbh.astype(jnp.float32), axis=2)
        qblocks = qs.reshape(nq, qbl, ATTN_HEADS, HEAD_DIM)
        out = lax.map(lambda a: moba_block(a[0], a[1], kbh, vbh, kmean, slopes), (qblocks, pos))
        return out.reshape(S, ATTN_HEADS, HEAD_DIM)

    return lax.map(one_seq, (q, k, v))


def moba_sample(q, k_new, v_new, cache_k, cache_v, page_table, slopes):
    T = q.shape[1]
    past = page_table.shape[1] * cache_k.shape[1]
    pos = past + jnp.arange(T, dtype=jnp.int32)

    def one_seq(args):
        qs, kn, vn, pt = args
        kp = cache_k[pt].reshape(past, ATTN_HEADS, HEAD_DIM)
        vp = cache_v[pt].reshape(past, ATTN_HEADS, HEAD_DIM)
        kbh = to_blocks(jnp.concatenate([kp, kn.astype(kp.dtype)], axis=0))
        vbh = to_blocks(jnp.concatenate([vp, vn.astype(vp.dtype)], axis=0))
        kmean = jnp.mean(kbh.astype(jnp.float32), axis=2)
        return moba_block(qs, pos, kbh, vbh, kmean, slopes)

    return lax.map(one_seq, (q, k_new, v_new, page_table))


def spatial_gate(u, vg, w_s, b_s):
    N, T, _ = vg.shape
    cl = min(T, CHUNK)
    nc = T // cl
    wm = jnp.tril(w_s[:, :cl, :cl])
    vr = vg.reshape(N, nc, cl, MLP_GROUPS, MLP_CH)
    mixed = jnp.einsum('gts,ncsgd->nctgd', wm, vr) + b_s[:, :cl].T[None, None, :, :, None]
    return u * mixed.reshape(N, T, MLP_WIDTH)


def hier_moe(h, w_group, b_group, w_router, b_router, w_gate, w_up, w_down):
    N, T, D = h.shape
    ht = h.reshape(N * T, D)
    hf = ht.astype(jnp.float32)
    glog = hf @ w_group.astype(jnp.float32) + b_group.astype(jnp.float32)
    gprob = jax.nn.softmax(glog, axis=-1)
    p_g, g = lax.top_k(gprob, 1)
    elog_all = jnp.einsum('td,gde->tge', hf, w_router.astype(jnp.float32)) + b_router.astype(jnp.float32)
    elog = jnp.take_along_axis(elog_all, g[:, :, None], axis=1)[:, 0]
    top_l, top_i = lax.top_k(elog, TOPK_IN_GROUP)
    w_e = jax.nn.softmax(top_l, axis=-1) * p_g
    eid = g * EXPERTS_PER_GROUP + top_i
    combine = jnp.sum(jax.nn.one_hot(eid, N_EXPERTS, dtype=jnp.float32) * w_e[..., None], axis=1)
    hg = jnp.einsum('td,edf->tef', ht, w_gate)
    hu = jnp.einsum('td,edf->tef', ht, w_up)
    act = jax.nn.silu(hg) * hu * combine[..., None].astype(ht.dtype)
    out = jnp.einsum('tef,efd->td', act, w_down)
    return out.reshape(N, T, D)


def decoder_layer(x, attend, norm_attn_g, w_in, sgu_g, w_spatial, b_spatial, out_g_attn, out_g_mlp,
                  w_out, norm_ffn_g, w_group, b_group, w_router, b_router, w_gate, w_up, w_down):
    N, T, _ = x.shape
    h = rmsnorm(x, norm_attn_g)
    z = h @ w_in
    q, k, v, u, vg = jnp.split(z, [ATTN_WIDTH, 2 * ATTN_WIDTH, 3 * ATTN_WIDTH,
                                   3 * ATTN_WIDTH + MLP_WIDTH], axis=-1)
    heads = (N, T, ATTN_HEADS, HEAD_DIM)
    q, k, v = q.reshape(heads), k.reshape(heads), v.reshape(heads)
    vg = rmsnorm(jax.nn.gelu(vg), sgu_g)
    att = attend(q, k, v).reshape(N, T, ATTN_WIDTH)
    gm = spatial_gate(jax.nn.gelu(u), vg, w_spatial, b_spatial)
    mix = jnp.concatenate([rmsnorm(att, out_g_attn), rmsnorm(gm, out_g_mlp)], axis=-1)
    x = x + mix @ w_out
    x = x + hier_moe(rmsnorm(x, norm_ffn_g), w_group, b_group, w_router, b_router, w_gate, w_up, w_down)
    return x, k, v, vg


def setup_inputs(seed: int = 0) -> dict:
    key = jax.random.key(seed)
    ks = jax.random.split(key, 24)
    f32 = jnp.float32
    n_pages = PAST_LEN // PAGE_SIZE
    n_used = DEC_BATCH * n_pages
    n_pool = n_used + (n_used + 3) // 4

    def nrm(k, shape, scale):
        return jax.random.normal(k, shape, f32) * scale

    def gain(k, shape):
        return 1.0 + 0.02 * jax.random.normal(k, shape, f32)

    page_table = jax.random.permutation(ks[4], n_pool)[:n_used].reshape(DEC_BATCH, n_pages).astype(jnp.int32)
    return {
        'x_prompt': nrm(ks[0], (BATCH, SEQ, D_MODEL), 1.0),
        'x_sample': nrm(ks[1], (DEC_BATCH, DEC_SEQ, D_MODEL), 1.0),
        'cache_k': nrm(ks[2], (DEPTH, n_pool, PAGE_SIZE, ATTN_HEADS, HEAD_DIM), 1.0),
        'cache_v': nrm(ks[3], (DEPTH, n_pool, PAGE_SIZE, ATTN_HEADS, HEAD_DIM), 1.0),
        'page_table': page_table,
        'norm_attn_g': gain(ks[5], (DEPTH, D_MODEL)),
        'w_in': nrm(ks[6], (DEPTH, D_MODEL, IN_WIDTH), D_MODEL ** -0.5),
        'sgu_g': gain(ks[7], (DEPTH, MLP_WIDTH)),
        'w_spatial': nrm(ks[8], (DEPTH, MLP_GROUPS, CHUNK, CHUNK), CHUNK ** -0.5),
        'b_spatial': 1.0 + 0.1 * jax.random.normal(ks[9], (DEPTH, MLP_GROUPS, CHUNK), f32),
        'out_g_attn': gain(ks[10], (DEPTH, ATTN_WIDTH)),
        'out_g_mlp': gain(ks[11], (DEPTH, MLP_WIDTH)),
        'w_out': nrm(ks[12], (DEPTH, D_MODEL, D_MODEL), D_MODEL ** -0.5),
        'norm_ffn_g': gain(ks[13], (DEPTH, D_MODEL)),
        'w_group': nrm(ks[14], (DEPTH, D_MODEL, MOE_GROUPS), D_MODEL ** -0.5),
        'b_group': nrm(ks[15], (DEPTH, MOE_GROUPS), 0.01),
        'w_router': nrm(ks[16], (DEPTH, MOE_GROUPS, D_MODEL, EXPERTS_PER_GROUP), D_MODEL ** -0.5),
        'b_router': nrm(ks[17], (DEPTH, MOE_GROUPS, EXPERTS_PER_GROUP), 0.01),
        'w_gate': nrm(ks[18], (DEPTH, N_EXPERTS, D_MODEL, D_EXPERT), D_MODEL ** -0.5),
        'w_up': nrm(ks[19], (DEPTH, N_EXPERTS, D_MODEL, D_EXPERT), D_MODEL ** -0.5),
        'w_down': nrm(ks[20], (DEPTH, N_EXPERTS, D_EXPERT, D_MODEL), D_EXPERT ** -0.5),
        'final_g': gain(ks[21], (D_MODEL,)),
    }


def reference(x_prompt, x_sample, cache_k, cache_v, page_table, norm_attn_g, w_in, sgu_g, w_spatial,
              b_spatial, out_g_attn, out_g_mlp, w_out, norm_ffn_g, w_group, b_group, w_router, b_router,
              w_gate, w_up, w_down, final_g):
    slopes = alibi_slopes(ATTN_HEADS)
    hp, hs = x_prompt, x_sample
    kp_all, vp_all, ks_all, vs_all, gv_all = [], [], [], [], []
    for l in range(DEPTH):
        lp = (norm_attn_g[l], w_in[l], sgu_g[l], w_spatial[l], b_spatial[l], out_g_attn[l], out_g_mlp[l],
              w_out[l], norm_ffn_g[l], w_group[l], b_group[l], w_router[l], b_router[l],
              w_gate[l], w_up[l], w_down[l])
        ck, cv = cache_k[l], cache_v[l]
        hp, kp, vp, _ = decoder_layer(hp, lambda q, k, v: moba_prompt(q, k, v, slopes), *lp)
        hs, kn, vn, gvn = decoder_layer(
            hs, lambda q, k, v: moba_sample(q, k, v, ck, cv, page_table, slopes), *lp)
        kp_all.append(kp)
        vp_all.append(vp)
        ks_all.append(kn)
        vs_all.append(vn)
        gv_all.append(gvn)
    y_prompt = rmsnorm(hp, final_g)
    y_sample = rmsnorm(hs, final_g)
    return (y_prompt, y_sample, jnp.stack(kp_all), jnp.stack(vp_all), jnp.stack(ks_all),
            jnp.stack(vs_all), jnp.stack(gv_all))
```

```python
import functools

import jax
import jax.numpy as jnp
from jax import lax
from jax.experimental import pallas as pl
from jax.experimental.pallas import tpu as pltpu

D_MODEL = 1024
ATTN_HEADS = 8
HEAD_DIM = 64
ATTN_WIDTH = ATTN_HEADS * HEAD_DIM
MOBA_BLOCK = 256
MOBA_TOPK = 3
QUERY_BLOCK = 128
MLP_GROUPS = 8
MLP_CH = 64
MLP_WIDTH = MLP_GROUPS * MLP_CH
CHUNK = 128
IN_WIDTH = 3 * ATTN_WIDTH + 2 * MLP_WIDTH
MOE_GROUPS = 4
EXPERTS_PER_GROUP = 4
N_EXPERTS = MOE_GROUPS * EXPERTS_PER_GROUP
D_EXPERT = D_MODEL // 2
EPS = 1e-6

LANES = 128
VMEM_LIMIT = 56 * 1024 * 1024

F32 = jnp.float32
BF16 = jnp.bfloat16
NEG_INF = float("-inf")
_NT = (((1,), (1,)), ((), ()))


def _rms(x, g):
    return x * lax.rsqrt(jnp.mean(x * x, axis=-1, keepdims=True) + EPS) * g


def _gelu(x):
    return 0.5 * x * (1.0 + jnp.tanh(0.7978845608028654 * (x + 0.044715 * (x * x * x))))


def _top_rank_select(gate, n_past, n_keep):
    nb = gate.shape[-1]
    n_idx = lax.broadcasted_iota(jnp.int32, gate.shape, 1)
    rank = jnp.zeros(gate.shape, jnp.int32)
    for m in range(nb):
        gm = gate[:, m:m + 1]
        beats = (gm > gate) | ((gm == gate) & (m < n_idx))
        rank = rank + jnp.where(beats & (m < n_past), 1, 0)
    return (n_idx < n_past) & (rank < n_keep)


def _inproj_kernel(x_ref, g_ref, w_ref, sgu_ref, wm_ref, bias_ref, og_ref,
                   q_ref, k_ref, v_ref, gmn_ref, vgn_ref, *, n_chunks):
    h = _rms(x_ref[...], g_ref[...]).astype(BF16)

    def proj(lo, width):
        return jnp.dot(h, w_ref[:, lo:lo + width], preferred_element_type=F32)

    q_ref[...] = proj(0, ATTN_WIDTH)
    k_ref[...] = proj(ATTN_WIDTH, ATTN_WIDTH)
    v_ref[...] = proj(2 * ATTN_WIDTH, ATTN_WIDTH)
    gu = _gelu(proj(3 * ATTN_WIDTH, MLP_WIDTH))
    vgn = _rms(_gelu(proj(3 * ATTN_WIDTH + MLP_WIDTH, MLP_WIDTH)), sgu_ref[...])
    vgn_ref[...] = vgn

    lane_grp = lax.broadcasted_iota(jnp.int32, (CHUNK, MLP_WIDTH), 1) // MLP_CH
    for c in range(n_chunks):
        rows = slice(c * CHUNK, (c + 1) * CHUNK)
        vc = vgn[rows].astype(BF16)
        vbd = jnp.concatenate(
            [jnp.where(lane_grp == g, vc, jnp.zeros_like(vc)) for g in range(MLP_GROUPS)], axis=0)
        mixed = jnp.dot(wm_ref[...], vbd, preferred_element_type=F32) + bias_ref[...]
        gmn_ref[rows, :] = _rms(gu[rows] * mixed, og_ref[...])


def _inproj(x, g, w_bf, sgu_g, wm_cat, bias_full, og_mlp, tm):
    rows = x.shape[0]
    row_spec = lambda width: pl.BlockSpec((tm, width), lambda i: (i, 0))
    full = lambda a: pl.BlockSpec(a.shape, lambda i: (0,) * a.ndim)
    out = jax.ShapeDtypeStruct((rows, ATTN_WIDTH), F32)
    return pl.pallas_call(
        functools.partial(_inproj_kernel, n_chunks=tm // CHUNK),
        grid=(rows // tm,),
        in_specs=[row_spec(D_MODEL), full(g), full(w_bf), full(sgu_g), full(wm_cat),
                  full(bias_full), full(og_mlp)],
        out_specs=[row_spec(ATTN_WIDTH)] * 5,
        out_shape=[out] * 5,
        compiler_params=pltpu.CompilerParams(
            dimension_semantics=("arbitrary",), vmem_limit_bytes=VMEM_LIMIT),
    )(x, g, w_bf, sgu_g, wm_cat, bias_full, og_mlp)


def _attn_prompt_kernel(q_ref, k_ref, v_ref, o_ref, kh_scr, vh_scr, kmean_scr, *, n_blocks):
    qi = pl.program_id(1)

    @pl.when(qi == 0)
    def _():
        for n in range(n_blocks):
            blk = k_ref[n * MOBA_BLOCK:(n + 1) * MOBA_BLOCK, :]
            kmean_scr[n:n + 1, :] = jnp.sum(blk, axis=0, keepdims=True) * (1.0 / MOBA_BLOCK)
        for h in range(ATTN_HEADS):
            lanes = slice(h * HEAD_DIM, (h + 1) * HEAD_DIM)
            kh_scr[h] = k_ref[:, lanes].astype(BF16)
            vh_scr[h] = v_ref[:, lanes].astype(BF16)

    cur = (qi * QUERY_BLOCK) // MOBA_BLOCK
    q = q_ref[...]
    pos = qi * QUERY_BLOCK + lax.broadcasted_iota(jnp.int32, (QUERY_BLOCK, 1), 0)
    key_off = lax.broadcasted_iota(jnp.int32, (1, MOBA_BLOCK), 1)
    blk_idx = lax.broadcasted_iota(jnp.int32, (QUERY_BLOCK, n_blocks), 1)

    for h in range(ATTN_HEADS):
        lanes = slice(h * HEAD_DIM, (h + 1) * HEAD_DIM)
        slope = 2.0 ** (-8.0 * (h + 1) / ATTN_HEADS)
        qh = q[:, lanes]
        gate = lax.dot_general(qh, kmean_scr[:, lanes], _NT, precision=lax.Precision.HIGHEST,
                               preferred_element_type=F32)
        sel = jnp.where(_top_rank_select(gate, cur, MOBA_TOPK), 1.0, 0.0)
        qs = (qh * (HEAD_DIM ** -0.5)).astype(BF16)

        def scores(j):
            start = pl.multiple_of(j * MOBA_BLOCK, MOBA_BLOCK)
            kb = kh_scr[h, pl.ds(start, MOBA_BLOCK), :]
            vb = vh_scr[h, pl.ds(start, MOBA_BLOCK), :]
            dist = pos - (j * MOBA_BLOCK + key_off)
            s = lax.dot_general(qs, kb, _NT, preferred_element_type=F32)
            return s - slope * dist.astype(F32), dist, vb

        s, dist, vb = scores(cur)
        s = jnp.where(dist >= 0, s, NEG_INF)
        m0 = jnp.max(s, axis=-1, keepdims=True)
        p = jnp.exp(s - m0)
        l0 = jnp.sum(p, axis=-1, keepdims=True)
        acc0 = jnp.dot(p.astype(BF16), vb, preferred_element_type=F32)

        def past_block(j, carry):
            m, l, acc = carry
            s, _, vb = scores(j)
            sel_j = jnp.max(jnp.where(blk_idx == j, sel, 0.0), axis=-1, keepdims=True)
            s = jnp.where(sel_j > 0.0, s, NEG_INF)
            m_new = jnp.maximum(m, jnp.max(s, axis=-1, keepdims=True))
            alpha = jnp.exp(m - m_new)
            p = jnp.exp(s - m_new)
            l = alpha * l + jnp.sum(p, axis=-1, keepdims=True)
            acc = alpha * acc + jnp.dot(p.astype(BF16), vb, preferred_element_type=F32)
            return m_new, l, acc

        _, l, acc = lax.fori_loop(0, cur, past_block, (m0, l0, acc0))
        o_ref[:, lanes] = acc / l


def _attn_prompt(q, k, v, batch, seq):
    nq = seq // QUERY_BLOCK
    n_blocks = seq // MOBA_BLOCK
    return pl.pallas_call(
        functools.partial(_attn_prompt_kernel, n_blocks=n_blocks),
        grid=(batch, nq),
        in_specs=[pl.BlockSpec((QUERY_BLOCK, ATTN_WIDTH), lambda b, i: (b * nq + i, 0)),
                  pl.BlockSpec((seq, ATTN_WIDTH), lambda b, i: (b, 0)),
                  pl.BlockSpec((seq, ATTN_WIDTH), lambda b, i: (b, 0))],
        out_specs=pl.BlockSpec((QUERY_BLOCK, ATTN_WIDTH), lambda b, i: (b * nq + i, 0)),
        out_shape=jax.ShapeDtypeStruct(q.shape, F32),
        scratch_shapes=[pltpu.VMEM((ATTN_HEADS, seq, HEAD_DIM), BF16),
                        pltpu.VMEM((ATTN_HEADS, seq, HEAD_DIM), BF16),
                        pltpu.VMEM((n_blocks, ATTN_WIDTH), F32)],
        compiler_params=pltpu.CompilerParams(
            dimension_semantics=("arbitrary", "arbitrary"), vmem_limit_bytes=VMEM_LIMIT),
    )(q, k, v)


def _attn_sample_kernel(pt_ref, q_ref, kn_ref, vn_ref, *refs, n_pages, page, t_new):
    del pt_ref
    kp = refs[:n_pages]
    vp = refs[n_pages:2 * n_pages]
    o_ref = refs[2 * n_pages]
    n_cols = t_new * ATTN_HEADS
    pages_per_block = MOBA_BLOCK // page
    n_past = n_pages // pages_per_block
    past_len = n_pages * page

    q = q_ref[0]
    lane_h = lax.broadcasted_iota(jnp.int32, (n_cols, ATTN_WIDTH), 1) // HEAD_DIM
    row = lax.broadcasted_iota(jnp.int32, (n_cols, 1), 0)
    row_h = row % ATTN_HEADS
    row_t = row // ATTN_HEADS
    own_head = lane_h == row_h
    qrep = jnp.concatenate(
        [jnp.broadcast_to(q[t:t + 1, :], (ATTN_HEADS, ATTN_WIDTH)) for t in range(t_new)], axis=0)
    qbd = jnp.where(own_head, qrep, 0.0)
    qbd_s = (qbd * (HEAD_DIM ** -0.5)).astype(BF16)
    slope = jnp.exp2(-8.0 * (row_h + 1).astype(F32) / ATTN_HEADS)

    kmean = jnp.concatenate(
        [sum(jnp.sum(kp[n * pages_per_block + i][0], axis=0, keepdims=True)
             for i in range(pages_per_block)) * (1.0 / MOBA_BLOCK) for n in range(n_past)], axis=0)
    gate = lax.dot_general(qbd, kmean, _NT, precision=lax.Precision.HIGHEST,
                           preferred_element_type=F32)
    sel = _top_rank_select(gate, n_past, MOBA_TOPK)

    s = lax.dot_general(qbd_s, kn_ref[0].astype(BF16), _NT, preferred_element_type=F32)
    dist = row_t - lax.broadcasted_iota(jnp.int32, (1, t_new), 1)
    s = jnp.where(dist >= 0, s - slope * dist.astype(F32), NEG_INF)
    m = jnp.max(s, axis=-1, keepdims=True)
    p = jnp.exp(s - m)
    l = jnp.sum(p, axis=-1, keepdims=True)
    vn = vn_ref[0]
    acc = sum(p[:, t:t + 1] * vn[t:t + 1, :] for t in range(t_new))

    key_off = lax.broadcasted_iota(jnp.int32, (1, MOBA_BLOCK), 1)
    for n in range(n_past):
        pages = range(n * pages_per_block, (n + 1) * pages_per_block)
        kb = jnp.concatenate([kp[i][0] for i in pages], axis=0).astype(BF16)
        vb = jnp.concatenate([vp[i][0] for i in pages], axis=0).astype(BF16)
        dist = (past_len + row_t) - (n * MOBA_BLOCK + key_off)
        s = lax.dot_general(qbd_s, kb, _NT, preferred_element_type=F32) - slope * dist.astype(F32)
        s = jnp.where(sel[:, n:n + 1], s, NEG_INF)
        m_new = jnp.maximum(m, jnp.max(s, axis=-1, keepdims=True))
        alpha = jnp.exp(m - m_new)
        p = jnp.exp(s - m_new)
        l = alpha * l + jnp.sum(p, axis=-1, keepdims=True)
        acc = alpha * acc + jnp.dot(p.astype(BF16), vb, preferred_element_type=F32)
        m = m_new

    out = jnp.where(own_head, acc / l, 0.0)
    o_ref[0] = jnp.sum(out.reshape(t_new, ATTN_HEADS, ATTN_WIDTH), axis=1)


def _attn_sample(q, kn, vn, cache_k, cache_v, page_table):
    n_seq, t_new, _ = q.shape
    n_pages = page_table.shape[1]
    page = cache_k.shape[1]
    seq_spec = pl.BlockSpec((1, t_new, ATTN_WIDTH), lambda i, pt: (i, 0, 0))

    def page_spec(p):
        return pl.BlockSpec((1, page, ATTN_WIDTH), lambda i, pt: (pt[i * n_pages + p], 0, 0))

    page_specs = [page_spec(p) for p in range(n_pages)]
    return pl.pallas_call(
        functools.partial(_attn_sample_kernel, n_pages=n_pages, page=page, t_new=t_new),
        grid_spec=pltpu.PrefetchScalarGridSpec(
            num_scalar_prefetch=1,
            grid=(n_seq,),
            in_specs=[seq_spec] * 3 + page_specs * 2,
            out_specs=seq_spec),
        out_shape=jax.ShapeDtypeStruct(q.shape, F32),
        compiler_params=pltpu.CompilerParams(
            dimension_semantics=("arbitrary",), vmem_limit_bytes=VMEM_LIMIT),
    )(page_table.reshape(-1), q, kn, vn, *([cache_k] * n_pages), *([cache_v] * n_pages))


def _outproj_kernel(att_ref, gmn_ref, x_ref, wo_ref, oga_ref, nfg_ref, wr_ref, br_ref,
                    x1_ref, h2_ref, comb_ref):
    attn = _rms(att_ref[...], oga_ref[...]).astype(BF16)
    mix = (jnp.dot(attn, wo_ref[:ATTN_WIDTH, :], preferred_element_type=F32)
           + jnp.dot(gmn_ref[...].astype(BF16), wo_ref[ATTN_WIDTH:, :], preferred_element_type=F32))
    x1 = x_ref[...] + mix
    x1_ref[...] = x1
    h2 = _rms(x1, nfg_ref[...])
    h2_ref[...] = h2.astype(BF16)

    logits = jnp.dot(h2, wr_ref[...], precision=lax.Precision.HIGHEST,
                     preferred_element_type=F32) + br_ref[...]
    lt = logits.T
    tm = lt.shape[1]
    row4 = lax.broadcasted_iota(jnp.int32, (MOE_GROUPS, tm), 0)

    def first_argmax(v):
        vmax = jnp.max(v, axis=0, keepdims=True)
        idx = jnp.min(jnp.where(v == vmax, row4, MOE_GROUPS), axis=0, keepdims=True)
        return vmax, idx

    glog = lt[:MOE_GROUPS]
    ge = jnp.exp(glog - jnp.max(glog, axis=0, keepdims=True))
    gprob = ge / jnp.sum(ge, axis=0, keepdims=True)
    p_g, g_idx = first_argmax(gprob)
    elog = lt[MOE_GROUPS:MOE_GROUPS + EXPERTS_PER_GROUP]
    for g in range(1, MOE_GROUPS):
        lo = MOE_GROUPS + g * EXPERTS_PER_GROUP
        elog = jnp.where(g_idx == g, lt[lo:lo + EXPERTS_PER_GROUP], elog)
    l1, i1 = first_argmax(elog)
    l2, i2 = first_argmax(jnp.where(row4 == i1, NEG_INF, elog))
    e2 = jnp.exp(l2 - l1)
    denom = 1.0 + e2
    w1 = (1.0 / denom) * p_g
    w2 = (e2 / denom) * p_g
    lane_row = lax.broadcasted_iota(jnp.int32, (LANES, tm), 0)
    base = g_idx * EXPERTS_PER_GROUP
    comb_t = jnp.where(lane_row == base + i1, w1, 0.0) + jnp.where(lane_row == base + i2, w2, 0.0)
    comb_ref[...] = comb_t.T


def _outproj(att, gmn, x, wo_bf, og_attn, nf_g, w_r, b_r, tm):
    rows = x.shape[0]
    row_spec = lambda width: pl.BlockSpec((tm, width), lambda i: (i, 0))
    full = lambda a: pl.BlockSpec(a.shape, lambda i: (0,) * a.ndim)
    return pl.pallas_call(
        _outproj_kernel,
        grid=(rows // tm,),
        in_specs=[row_spec(ATTN_WIDTH), row_spec(MLP_WIDTH), row_spec(D_MODEL), full(wo_bf),
                  full(og_attn), full(nf_g), full(w_r), full(b_r)],
        out_specs=[row_spec(D_MODEL), row_spec(D_MODEL), row_spec(LANES)],
        out_shape=[jax.ShapeDtypeStruct((rows, D_MODEL), F32),
                   jax.ShapeDtypeStruct((rows, D_MODEL), BF16),
                   jax.ShapeDtypeStruct((rows, LANES), F32)],
        compiler_params=pltpu.CompilerParams(
            dimension_semantics=("arbitrary",), vmem_limit_bytes=VMEM_LIMIT),
    )(att, gmn, x, wo_bf, og_attn, nf_g, w_r, b_r)


def _moe_kernel(h_ref, comb_ref, wg_ref, wu_ref, wd_ref, x1_ref, fg_ref, y_ref, acc_ref):
    e = pl.program_id(1)

    @pl.when(e == 0)
    def _():
        acc_ref[...] = jnp.zeros_like(acc_ref)

    h = h_ref[...]
    hg = jnp.dot(h, wg_ref[0], preferred_element_type=F32)
    hu = jnp.dot(h, wu_ref[0], preferred_element_type=F32)
    comb = comb_ref[...]
    lane = lax.broadcasted_iota(jnp.int32, comb.shape, 1)
    c = jnp.sum(jnp.where(lane == e, comb, 0.0), axis=-1, keepdims=True)
    act = hg * (1.0 / (1.0 + jnp.exp(-hg))) * hu * c
    acc_ref[...] += jnp.dot(act.astype(BF16), wd_ref[0], preferred_element_type=F32)

    @pl.when(e == N_EXPERTS - 1)
    def _():
        y_ref[...] = _rms(x1_ref[...] + acc_ref[...], fg_ref[...])


def _moe(h2, comb, wg_bf, wu_bf, wd_bf, x1, final_g, tm):
    rows = h2.shape[0]
    row_spec = lambda width: pl.BlockSpec((tm, width), lambda i, e: (i, 0))
    return pl.pallas_call(
        _moe_kernel,
        grid=(rows // tm, N_EXPERTS),
        in_specs=[row_spec(D_MODEL), row_spec(LANES),
                  pl.BlockSpec((1, D_MODEL, D_EXPERT), lambda i, e: (e, 0, 0)),
                  pl.BlockSpec((1, D_MODEL, D_EXPERT), lambda i, e: (e, 0, 0)),
                  pl.BlockSpec((1, D_EXPERT, D_MODEL), lambda i, e: (e, 0, 0)),
                  row_spec(D_MODEL),
                  pl.BlockSpec(final_g.shape, lambda i, e: (0, 0))],
        out_specs=row_spec(D_MODEL),
        out_shape=jax.ShapeDtypeStruct((rows, D_MODEL), F32),
        scratch_shapes=[pltpu.VMEM((tm, D_MODEL), F32)],
        compiler_params=pltpu.CompilerParams(
            dimension_semantics=("arbitrary", "arbitrary"), vmem_limit_bytes=VMEM_LIMIT),
    )(h2, comb, wg_bf, wu_bf, wd_bf, x1, final_g)


def _spatial_operands(w_s, b_s, t_chunk):
    reps = CHUNK // t_chunk
    idx = jnp.arange(CHUNK)
    same = (idx[:, None] // t_chunk) == (idx[None, :] // t_chunk)
    causal = (idx[None, :] % t_chunk) <= (idx[:, None] % t_chunk)
    wm = jnp.tile(w_s[:, :t_chunk, :t_chunk], (1, reps, reps)) * (same & causal)
    wm_cat = wm.transpose(1, 0, 2).reshape(CHUNK, MLP_GROUPS * CHUNK).astype(BF16)
    bias = jnp.repeat(jnp.tile(b_s[:, :t_chunk], (1, reps)).T, MLP_CH, axis=1)
    return wm_cat, bias


def kernel(x_prompt, x_sample, cache_k, cache_v, page_table, norm_attn_g, w_in, sgu_g, w_spatial,
           b_spatial, out_g_attn, out_g_mlp, w_out, norm_ffn_g, w_group, b_group, w_router, b_router,
           w_gate, w_up, w_down, final_g):
    depth = w_in.shape[0]
    assert depth == 1, "single decoder layer"
    batch, seq, _ = x_prompt.shape
    n_seq, t_new, _ = x_sample.shape
    n_pool, page = cache_k.shape[1], cache_k.shape[2]
    assert seq % MOBA_BLOCK == 0 and seq % CHUNK == 0 and CHUNK % t_new == 0

    row2 = lambda a: a.reshape(1, -1)
    w_in_bf = w_in[0].astype(BF16)
    w_out_bf = w_out[0].astype(BF16)
    wg_bf, wu_bf, wd_bf = w_gate[0].astype(BF16), w_up[0].astype(BF16), w_down[0].astype(BF16)
    n_logits = MOE_GROUPS + N_EXPERTS
    w_r = jnp.concatenate(
        [w_group[0], w_router[0].transpose(1, 0, 2).reshape(D_MODEL, N_EXPERTS),
         jnp.zeros((D_MODEL, LANES - n_logits), F32)], axis=1)
    b_r = jnp.concatenate(
        [b_group[0], b_router[0].reshape(-1), jnp.zeros((LANES - n_logits,), F32)]).reshape(1, LANES)
    ck = cache_k[0].reshape(n_pool, page, ATTN_WIDTH)
    cv = cache_v[0].reshape(n_pool, page, ATTN_WIDTH)

    def layer(x, t_chunk, attend, tm_proj, tm_moe):
        wm_cat, bias_full = _spatial_operands(w_spatial[0], b_spatial[0], t_chunk)
        q, k, v, gmn, vgn = _inproj(x, row2(norm_attn_g[0]), w_in_bf, row2(sgu_g[0]), wm_cat,
                                    bias_full, row2(out_g_mlp[0]), tm_proj)
        att = attend(q, k, v)
        x1, h2, comb = _outproj(att, gmn, x, w_out_bf, row2(out_g_attn[0]), row2(norm_ffn_g[0]),
                                w_r, b_r, tm_proj)
        y = _moe(h2, comb, wg_bf, wu_bf, wd_bf, x1, row2(final_g), tm_moe)
        return y, k, v, vgn

    yp, kp, vp, _ = layer(
        x_prompt.reshape(batch * seq, D_MODEL), CHUNK,
        lambda q, k, v: _attn_prompt(q, k, v, batch, seq), 512, 1024)

    def attend_sample(q, k, v):
        shape = (n_seq, t_new, ATTN_WIDTH)
        return _attn_sample(q.reshape(shape), k.reshape(shape), v.reshape(shape), ck, cv,
                            page_table).reshape(n_seq * t_new, ATTN_WIDTH)

    ys, ks, vs, gvs = layer(x_sample.reshape(n_seq * t_new, D_MODEL), t_new, attend_sample, 128, 512)

    heads = (ATTN_HEADS, HEAD_DIM)
    return (yp.reshape(batch, seq, D_MODEL),
            ys.reshape(n_seq, t_new, D_MODEL),
            kp.reshape(depth, batch, seq, *heads),
            vp.reshape(depth, batch, seq, *heads),
            ks.reshape(depth, n_seq, t_new, *heads),
            vs.reshape(depth, n_seq, t_new, *heads),
            gvs.reshape(depth, n_seq, t_new, MLP_WIDTH))
```

```python
import functools

import jax
import jax.numpy as jnp
from jax import lax
from jax.experimental import pallas as pl
from jax.experimental.pallas import tpu as pltpu

D_MODEL = 1024
ATTN_HEADS = 8
HEAD_DIM = 64
ATTN_WIDTH = ATTN_HEADS * HEAD_DIM
MOBA_BLOCK = 256
MOBA_TOPK = 3
QUERY_BLOCK = 128
MLP_GROUPS = 8
MLP_CH = 64
MLP_WIDTH = MLP_GROUPS * MLP_CH
CHUNK = 128
IN_WIDTH = 3 * ATTN_WIDTH + 2 * MLP_WIDTH
MOE_GROUPS = 4
EXPERTS_PER_GROUP = 4
N_EXPERTS = MOE_GROUPS * EXPERTS_PER_GROUP
D_EXPERT = D_MODEL // 2
EPS = 1e-6

LANES = 128
BF16_ROWS = 16
VMEM_LIMIT = 56 * 1024 * 1024

F32 = jnp.float32
BF16 = jnp.bfloat16
NEG_INF = float("-inf")
_NT = (((1,), (1,)), ((), ()))


def _rms(x, g):
    return x * lax.rsqrt(jnp.mean(x * x, axis=-1, keepdims=True) + EPS) * g


def _gelu(x):
    return 0.5 * x * (1.0 + jnp.tanh(0.7978845608028654 * (x + 0.044715 * (x * x * x))))


def _top_rank_select(gate, n_past, n_keep, axis):
    nb = gate.shape[axis]
    n_idx = lax.broadcasted_iota(jnp.int32, gate.shape, axis)
    rank = jnp.zeros(gate.shape, jnp.int32)
    for m in range(nb):
        gm = gate[:, m:m + 1] if axis == 1 else gate[m:m + 1, :]
        beats = jnp.where(gm > gate, 1, jnp.where(gm == gate, jnp.where(m < n_idx, 1, 0), 0))
        rank = rank + jnp.where(m < n_past, beats, 0)
    return jnp.where(n_idx < n_past, rank, n_keep) < n_keep


def _inproj_kernel(x_ref, g_ref, w_ref, sgu_ref, wm_ref, bias_ref, og_ref,
                   q_ref, k_ref, v_ref, gmn_ref, vgn_ref, *, n_chunks):
    h = _rms(x_ref[...], g_ref[...]).astype(BF16)

    def proj(lo, width):
        return jnp.dot(h, w_ref[:, lo:lo + width], preferred_element_type=F32)

    q_ref[...] = proj(0, ATTN_WIDTH)
    k_ref[...] = proj(ATTN_WIDTH, ATTN_WIDTH)
    v_ref[...] = proj(2 * ATTN_WIDTH, ATTN_WIDTH)
    gu = _gelu(proj(3 * ATTN_WIDTH, MLP_WIDTH))
    vgn = _rms(_gelu(proj(3 * ATTN_WIDTH + MLP_WIDTH, MLP_WIDTH)), sgu_ref[...])
    vgn_ref[...] = vgn

    lane_grp = lax.broadcasted_iota(jnp.int32, (CHUNK, MLP_WIDTH), 1) // MLP_CH
    for c in range(n_chunks):
        rows = slice(c * CHUNK, (c + 1) * CHUNK)
        vc = vgn[rows].astype(BF16)
        vbd = jnp.concatenate(
            [jnp.where(lane_grp == g, vc, jnp.zeros_like(vc)) for g in range(MLP_GROUPS)], axis=0)
        mixed = jnp.dot(wm_ref[...], vbd, preferred_element_type=F32) + bias_ref[...]
        gmn_ref[rows, :] = _rms(gu[rows] * mixed, og_ref[...])


def _inproj(x, g, w_bf, sgu_g, wm_cat, bias_full, og_mlp, tm):
    rows = x.shape[0]
    row_spec = lambda width: pl.BlockSpec((tm, width), lambda i: (i, 0))
    full = lambda a: pl.BlockSpec(a.shape, lambda i: (0,) * a.ndim)
    out = jax.ShapeDtypeStruct((rows, ATTN_WIDTH), F32)
    return pl.pallas_call(
        functools.partial(_inproj_kernel, n_chunks=tm // CHUNK),
        grid=(rows // tm,),
        in_specs=[row_spec(D_MODEL), full(g), full(w_bf), full(sgu_g), full(wm_cat),
                  full(bias_full), full(og_mlp)],
        out_specs=[row_spec(ATTN_WIDTH)] * 5,
        out_shape=[out] * 5,
        compiler_params=pltpu.CompilerParams(
            dimension_semantics=("arbitrary",), vmem_limit_bytes=VMEM_LIMIT),
    )(x, g, w_bf, sgu_g, wm_cat, bias_full, og_mlp)


V_ROWS = HEAD_DIM + BF16_ROWS


def _attn_prompt_kernel(q_ref, k_ref, v_ref, o_ref, ka_scr, vt_scr, kmean_scr, qa_scr, sel_scr,
                        m_scr, alpha_scr, acc_scr, s_scr, *, n_blocks):
    qi = pl.program_id(1)

    @pl.when(qi == 0)
    def _():
        key = lax.broadcasted_iota(jnp.int32, (MOBA_BLOCK, HEAD_DIM), 0)
        col = lax.broadcasted_iota(jnp.int32, (MOBA_BLOCK, HEAD_DIM), 1)
        ones = jnp.ones((BF16_ROWS, MOBA_BLOCK), BF16)
        for n in range(n_blocks):
            rows = slice(n * MOBA_BLOCK, (n + 1) * MOBA_BLOCK)
            kb = k_ref[rows, :]
            kmean_scr[n:n + 1, :] = jnp.sum(kb, axis=0, keepdims=True) * (1.0 / MOBA_BLOCK)
            vt = v_ref[rows, :].T
            extra = jnp.where(col == 0, key, jnp.where(col == 1, n, jnp.where(col == 2, 1, 0)))
            extra = extra.astype(F32).astype(BF16)
            for h in range(ATTN_HEADS):
                lanes = slice(h * HEAD_DIM, (h + 1) * HEAD_DIM)
                ka_scr[h, n] = jnp.concatenate([kb[:, lanes].astype(BF16), extra], axis=1)
                vt_scr[h, n, :HEAD_DIM, :] = vt[lanes, :].astype(BF16)
                vt_scr[h, n, HEAD_DIM:, :] = ones

    cur = (qi * QUERY_BLOCK) // MOBA_BLOCK
    q_t = q_ref[...].T
    row = lax.broadcasted_iota(jnp.int32, (HEAD_DIM, QUERY_BLOCK), 0)
    cur_f = cur.astype(F32)
    for h in range(ATTN_HEADS):
        lanes = slice(h * HEAD_DIM, (h + 1) * HEAD_DIM)
        slope = 2.0 ** (-8.0 * (h + 1) / ATTN_HEADS)
        qh_t = q_t[lanes, :]
        gate = jnp.dot(kmean_scr[:, lanes], qh_t, precision=lax.Precision.HIGHEST,
                       preferred_element_type=F32)
        sel_scr[h] = jnp.where(_top_rank_select(gate, cur, MOBA_TOPK, 0), 1.0, 0.0)
        extra = jnp.where(row == 0, slope,
                          jnp.where(row == 1, slope * MOBA_BLOCK,
                                    jnp.where(row == 2, -slope * MOBA_BLOCK * cur_f, 0.0)))
        qa_scr[h] = jnp.concatenate([qh_t * (HEAD_DIM ** -0.5), extra], axis=0).astype(BF16)

    def block_scores(h, j):
        return jnp.dot(ka_scr[h, j], qa_scr[h], preferred_element_type=F32)

    def attend_block(j, keep, first):
        for h in range(ATTN_HEADS):
            s = jnp.where(keep(h), block_scores(h, j), NEG_INF)
            s_scr[h] = s
            m_blk = jnp.max(s, axis=0, keepdims=True)
            if first:
                m_scr[h:h + 1, :] = m_blk
            else:
                m_old = m_scr[h:h + 1, :]
                m_new = jnp.maximum(m_old, m_blk)
                alpha_scr[h:h + 1, :] = jnp.exp(m_old - m_new)
                m_scr[h:h + 1, :] = m_new
        for h in range(ATTN_HEADS):
            p = jnp.exp(s_scr[h] - m_scr[h:h + 1, :]).astype(BF16)
            pv = jnp.dot(vt_scr[h, j], p, preferred_element_type=F32)
            acc_scr[h] = pv if first else alpha_scr[h:h + 1, :] * acc_scr[h] + pv

    key_minus_q = (lax.broadcasted_iota(jnp.int32, (MOBA_BLOCK, QUERY_BLOCK), 0)
                   - lax.broadcasted_iota(jnp.int32, (MOBA_BLOCK, QUERY_BLOCK), 1))
    causal = key_minus_q <= qi * QUERY_BLOCK - cur * MOBA_BLOCK
    attend_block(cur, lambda h: causal, True)

    def past_block(j, carry):
        attend_block(j, lambda h: sel_scr[h, pl.ds(j, 1), :] > 0.0, False)
        return carry

    lax.fori_loop(0, cur, past_block, 0)

    outs = []
    for h in range(ATTN_HEADS):
        acc = acc_scr[h]
        outs.append(acc[:HEAD_DIM] / acc[HEAD_DIM:HEAD_DIM + 1])
    o_ref[...] = jnp.concatenate(outs, axis=0).T


def _attn_prompt(q, k, v, batch, seq):
    nq = seq // QUERY_BLOCK
    n_blocks = seq // MOBA_BLOCK
    return pl.pallas_call(
        functools.partial(_attn_prompt_kernel, n_blocks=n_blocks),
        grid=(batch, nq),
        in_specs=[pl.BlockSpec((QUERY_BLOCK, ATTN_WIDTH), lambda b, i: (b * nq + i, 0)),
                  pl.BlockSpec((seq, ATTN_WIDTH), lambda b, i: (b, 0)),
                  pl.BlockSpec((seq, ATTN_WIDTH), lambda b, i: (b, 0))],
        out_specs=pl.BlockSpec((QUERY_BLOCK, ATTN_WIDTH), lambda b, i: (b * nq + i, 0)),
        out_shape=jax.ShapeDtypeStruct(q.shape, F32),
        scratch_shapes=[pltpu.VMEM((ATTN_HEADS, n_blocks, MOBA_BLOCK, 2 * HEAD_DIM), BF16),
                        pltpu.VMEM((ATTN_HEADS, n_blocks, V_ROWS, MOBA_BLOCK), BF16),
                        pltpu.VMEM((n_blocks, ATTN_WIDTH), F32),
                        pltpu.VMEM((ATTN_HEADS, 2 * HEAD_DIM, QUERY_BLOCK), BF16),
                        pltpu.VMEM((ATTN_HEADS, n_blocks, QUERY_BLOCK), F32),
                        pltpu.VMEM((ATTN_HEADS, QUERY_BLOCK), F32),
                        pltpu.VMEM((ATTN_HEADS, QUERY_BLOCK), F32),
                        pltpu.VMEM((ATTN_HEADS, V_ROWS, QUERY_BLOCK), F32),
                        pltpu.VMEM((ATTN_HEADS, MOBA_BLOCK, QUERY_BLOCK), F32)],
        compiler_params=pltpu.CompilerParams(
            dimension_semantics=("arbitrary", "arbitrary"), vmem_limit_bytes=VMEM_LIMIT),
    )(q, k, v)


def _attn_sample_kernel(pt_ref, q_ref, kn_ref, vn_ref, *refs, n_pages, page, t_new):
    del pt_ref
    kp = refs[:n_pages]
    vp = refs[n_pages:2 * n_pages]
    o_ref = refs[2 * n_pages]
    n_cols = t_new * ATTN_HEADS
    pages_per_block = MOBA_BLOCK // page
    n_past = n_pages // pages_per_block
    past_len = n_pages * page

    q = q_ref[0]
    lane_h = lax.broadcasted_iota(jnp.int32, (n_cols, ATTN_WIDTH), 1) // HEAD_DIM
    row = lax.broadcasted_iota(jnp.int32, (n_cols, 1), 0)
    row_h = row % ATTN_HEADS
    row_t = row // ATTN_HEADS
    own_head = lane_h == row_h
    qrep = jnp.concatenate(
        [jnp.broadcast_to(q[t:t + 1, :], (ATTN_HEADS, ATTN_WIDTH)) for t in range(t_new)], axis=0)
    qbd = jnp.where(own_head, qrep, 0.0)
    qbd_s = (qbd * (HEAD_DIM ** -0.5)).astype(BF16)
    slope = jnp.exp2(-8.0 * (row_h + 1).astype(F32) / ATTN_HEADS)

    def block_pages(refs_, n):
        return [refs_[i][0].reshape(ATTN_WIDTH, page)
                for i in range(n * pages_per_block, (n + 1) * pages_per_block)]

    kmean = jnp.concatenate(
        [jnp.sum(sum(block_pages(kp, n)), axis=1, keepdims=True) for n in range(n_past)],
        axis=1) * (1.0 / MOBA_BLOCK)
    gate = jnp.dot(qbd, kmean, precision=lax.Precision.HIGHEST, preferred_element_type=F32)
    sel = _top_rank_select(gate, n_past, MOBA_TOPK, 1)

    s = lax.dot_general(qbd_s, kn_ref[0].astype(BF16), _NT, preferred_element_type=F32)
    dist = row_t - lax.broadcasted_iota(jnp.int32, (1, t_new), 1)
    s = jnp.where(dist >= 0, s - slope * dist.astype(F32), NEG_INF)
    m = jnp.max(s, axis=-1, keepdims=True)
    p = jnp.exp(s - m)
    l = jnp.sum(p, axis=-1, keepdims=True)
    vn = vn_ref[0]
    acc = sum(p[:, t:t + 1] * vn[t:t + 1, :] for t in range(t_new))

    key_off = lax.broadcasted_iota(jnp.int32, (1, MOBA_BLOCK), 1)
    for n in range(n_past):
        kb_t = jnp.concatenate(block_pages(kp, n), axis=1).astype(BF16)
        vb_t = jnp.concatenate(block_pages(vp, n), axis=1).astype(BF16)
        dist = (past_len + row_t) - (n * MOBA_BLOCK + key_off)
        s = jnp.dot(qbd_s, kb_t, preferred_element_type=F32) - slope * dist.astype(F32)
        s = jnp.where(sel[:, n:n + 1], s, NEG_INF)
        m_new = jnp.maximum(m, jnp.max(s, axis=-1, keepdims=True))
        alpha = jnp.exp(m - m_new)
        p = jnp.exp(s - m_new)
        l = alpha * l + jnp.sum(p, axis=-1, keepdims=True)
        acc = alpha * acc + lax.dot_general(p.astype(BF16), vb_t, _NT, preferred_element_type=F32)
        m = m_new

    out = jnp.where(own_head, acc / l, 0.0)
    o_ref[0] = jnp.sum(out.reshape(t_new, ATTN_HEADS, ATTN_WIDTH), axis=1)


def _attn_sample(q, kn, vn, cache_kt, cache_vt, page_table):
    n_seq, t_new, _ = q.shape
    n_pages = page_table.shape[1]
    page = cache_kt.shape[-1]
    seq_spec = pl.BlockSpec((1, t_new, ATTN_WIDTH), lambda i, pt: (i, 0, 0))

    def page_spec(p):
        return pl.BlockSpec((1, ATTN_HEADS, HEAD_DIM, page),
                            lambda i, pt: (pt[i * n_pages + p], 0, 0, 0))

    page_specs = [page_spec(p) for p in range(n_pages)]
    return pl.pallas_call(
        functools.partial(_attn_sample_kernel, n_pages=n_pages, page=page, t_new=t_new),
        grid_spec=pltpu.PrefetchScalarGridSpec(
            num_scalar_prefetch=1,
            grid=(n_seq,),
            in_specs=[seq_spec] * 3 + page_specs * 2,
            out_specs=seq_spec),
        out_shape=jax.ShapeDtypeStruct(q.shape, F32),
        compiler_params=pltpu.CompilerParams(
            dimension_semantics=("arbitrary",), vmem_limit_bytes=VMEM_LIMIT),
    )(page_table.reshape(-1), q, kn, vn, *([cache_kt] * n_pages), *([cache_vt] * n_pages))


def _outproj_kernel(att_ref, gmn_ref, x_ref, wo_ref, oga_ref, nfg_ref, wr_ref, br_ref,
                    x1_ref, h2_ref, comb_ref):
    attn = _rms(att_ref[...], oga_ref[...]).astype(BF16)
    mix = (jnp.dot(attn, wo_ref[:ATTN_WIDTH, :], preferred_element_type=F32)
           + jnp.dot(gmn_ref[...].astype(BF16), wo_ref[ATTN_WIDTH:, :], preferred_element_type=F32))
    x1 = x_ref[...] + mix
    x1_ref[...] = x1
    h2 = _rms(x1, nfg_ref[...])
    h2_ref[...] = h2.astype(BF16)

    logits = jnp.dot(h2, wr_ref[...], precision=lax.Precision.HIGHEST,
                     preferred_element_type=F32) + br_ref[...]
    lt = logits.T
    tm = lt.shape[1]
    row4 = lax.broadcasted_iota(jnp.int32, (MOE_GROUPS, tm), 0)

    def first_argmax(v):
        vmax = jnp.max(v, axis=0, keepdims=True)
        idx = jnp.min(jnp.where(v == vmax, row4, MOE_GROUPS), axis=0, keepdims=True)
        return vmax, idx

    glog = lt[:MOE_GROUPS]
    ge = jnp.exp(glog - jnp.max(glog, axis=0, keepdims=True))
    gprob = ge / jnp.sum(ge, axis=0, keepdims=True)
    p_g, g_idx = first_argmax(gprob)
    elog = lt[MOE_GROUPS:MOE_GROUPS + EXPERTS_PER_GROUP]
    for g in range(1, MOE_GROUPS):
        lo = MOE_GROUPS + g * EXPERTS_PER_GROUP
        elog = jnp.where(g_idx == g, lt[lo:lo + EXPERTS_PER_GROUP], elog)
    l1, i1 = first_argmax(elog)
    l2, i2 = first_argmax(jnp.where(row4 == i1, NEG_INF, elog))
    e2 = jnp.exp(l2 - l1)
    denom = 1.0 + e2
    w1 = (1.0 / denom) * p_g
    w2 = (e2 / denom) * p_g
    lane_row = lax.broadcasted_iota(jnp.int32, (LANES, tm), 0)
    base = g_idx * EXPERTS_PER_GROUP
    comb_t = jnp.where(lane_row == base + i1, w1, 0.0) + jnp.where(lane_row == base + i2, w2, 0.0)
    comb_ref[...] = comb_t.T


def _outproj(att, gmn, x, wo_bf, og_attn, nf_g, w_r, b_r, tm):
    rows = x.shape[0]
    row_spec = lambda width: pl.BlockSpec((tm, width), lambda i: (i, 0))
    full = lambda a: pl.BlockSpec(a.shape, lambda i: (0,) * a.ndim)
    return pl.pallas_call(
        _outproj_kernel,
        grid=(rows // tm,),
        in_specs=[row_spec(ATTN_WIDTH), row_spec(MLP_WIDTH), row_spec(D_MODEL), full(wo_bf),
                  full(og_attn), full(nf_g), full(w_r), full(b_r)],
        out_specs=[row_spec(D_MODEL), row_spec(D_MODEL), row_spec(LANES)],
        out_shape=[jax.ShapeDtypeStruct((rows, D_MODEL), F32),
                   jax.ShapeDtypeStruct((rows, D_MODEL), BF16),
                   jax.ShapeDtypeStruct((rows, LANES), F32)],
        compiler_params=pltpu.CompilerParams(
            dimension_semantics=("arbitrary",), vmem_limit_bytes=VMEM_LIMIT),
    )(att, gmn, x, wo_bf, og_attn, nf_g, w_r, b_r)


def _moe_kernel(h_ref, comb_ref, wg_ref, wu_ref, wd_ref, x1_ref, fg_ref, y_ref, acc_ref):
    e = pl.program_id(1)

    @pl.when(e == 0)
    def _():
        acc_ref[...] = jnp.zeros_like(acc_ref)

    h = h_ref[...]
    hg = jnp.dot(h, wg_ref[0], preferred_element_type=F32)
    hu = jnp.dot(h, wu_ref[0], preferred_element_type=F32)
    comb = comb_ref[...]
    lane = lax.broadcasted_iota(jnp.int32, comb.shape, 1)
    c = jnp.sum(jnp.where(lane == e, comb, 0.0), axis=-1, keepdims=True)
    act = hg * (1.0 / (1.0 + jnp.exp(-hg))) * hu * c
    acc_ref[...] += jnp.dot(act.astype(BF16), wd_ref[0], preferred_element_type=F32)

    @pl.when(e == N_EXPERTS - 1)
    def _():
        y_ref[...] = _rms(x1_ref[...] + acc_ref[...], fg_ref[...])


def _moe(h2, comb, wg_bf, wu_bf, wd_bf, x1, final_g, tm):
    rows = h2.shape[0]
    row_spec = lambda width: pl.BlockSpec((tm, width), lambda i, e: (i, 0))
    return pl.pallas_call(
        _moe_kernel,
        grid=(rows // tm, N_EXPERTS),
        in_specs=[row_spec(D_MODEL), row_spec(LANES),
                  pl.BlockSpec((1, D_MODEL, D_EXPERT), lambda i, e: (e, 0, 0)),
                  pl.BlockSpec((1, D_MODEL, D_EXPERT), lambda i, e: (e, 0, 0)),
                  pl.BlockSpec((1, D_EXPERT, D_MODEL), lambda i, e: (e, 0, 0)),
                  row_spec(D_MODEL),
                  pl.BlockSpec(final_g.shape, lambda i, e: (0, 0))],
        out_specs=row_spec(D_MODEL),
        out_shape=jax.ShapeDtypeStruct((rows, D_MODEL), F32),
        scratch_shapes=[pltpu.VMEM((tm, D_MODEL), F32)],
        compiler_params=pltpu.CompilerParams(
            dimension_semantics=("arbitrary", "arbitrary"), vmem_limit_bytes=VMEM_LIMIT),
    )(h2, comb, wg_bf, wu_bf, wd_bf, x1, final_g)


def _spatial_operands(w_s, b_s, t_chunk):
    reps = CHUNK // t_chunk
    idx = jnp.arange(CHUNK)
    same = (idx[:, None] // t_chunk) == (idx[None, :] // t_chunk)
    causal = (idx[None, :] % t_chunk) <= (idx[:, None] % t_chunk)
    wm = jnp.tile(w_s[:, :t_chunk, :t_chunk], (1, reps, reps)) * (same & causal)
    wm_cat = wm.transpose(1, 0, 2).reshape(CHUNK, MLP_GROUPS * CHUNK).astype(BF16)
    bias = jnp.repeat(jnp.tile(b_s[:, :t_chunk], (1, reps)).T, MLP_CH, axis=1)
    return wm_cat, bias


def kernel(x_prompt, x_sample, cache_k, cache_v, page_table, norm_attn_g, w_in, sgu_g, w_spatial,
           b_spatial, out_g_attn, out_g_mlp, w_out, norm_ffn_g, w_group, b_group, w_router, b_router,
           w_gate, w_up, w_down, final_g):
    depth = w_in.shape[0]
    assert depth == 1, "single decoder layer"
    batch, seq, _ = x_prompt.shape
    n_seq, t_new, _ = x_sample.shape
    assert seq % MOBA_BLOCK == 0 and seq % CHUNK == 0 and CHUNK % t_new == 0
    assert MOBA_BLOCK % cache_k.shape[2] == 0

    row2 = lambda a: a.reshape(1, -1)
    w_in_bf = w_in[0].astype(BF16)
    w_out_bf = w_out[0].astype(BF16)
    wg_bf, wu_bf, wd_bf = w_gate[0].astype(BF16), w_up[0].astype(BF16), w_down[0].astype(BF16)
    n_logits = MOE_GROUPS + N_EXPERTS
    w_r = jnp.concatenate(
        [w_group[0], w_router[0].transpose(1, 0, 2).reshape(D_MODEL, N_EXPERTS),
         jnp.zeros((D_MODEL, LANES - n_logits), F32)], axis=1)
    b_r = jnp.concatenate(
        [b_group[0], b_router[0].reshape(-1), jnp.zeros((LANES - n_logits,), F32)]).reshape(1, LANES)
    ck_t = jnp.transpose(cache_k[0], (0, 2, 3, 1))
    cv_t = jnp.transpose(cache_v[0], (0, 2, 3, 1))

    def layer(x, t_chunk, attend, tm_proj, tm_moe):
        wm_cat, bias_full = _spatial_operands(w_spatial[0], b_spatial[0], t_chunk)
        q, k, v, gmn, vgn = _inproj(x, row2(norm_attn_g[0]), w_in_bf, row2(sgu_g[0]), wm_cat,
                                    bias_full, row2(out_g_mlp[0]), tm_proj)
        att = attend(q, k, v)
        x1, h2, comb = _outproj(att, gmn, x, w_out_bf, row2(out_g_attn[0]), row2(norm_ffn_g[0]),
                                w_r, b_r, tm_proj)
        y = _moe(h2, comb, wg_bf, wu_bf, wd_bf, x1, row2(final_g), tm_moe)
        return y, k, v, vgn

    yp, kp, vp, _ = layer(
        x_prompt.reshape(batch * seq, D_MODEL), CHUNK,
        lambda q, k, v: _attn_prompt(q, k, v, batch, seq), 512, 1024)

    def attend_sample(q, k, v):
        shape = (n_seq, t_new, ATTN_WIDTH)
        return _attn_sample(q.reshape(shape), k.reshape(shape), v.reshape(shape), ck_t, cv_t,
                            page_table).reshape(n_seq * t_new, ATTN_WIDTH)

    ys, ks, vs, gvs = layer(x_sample.reshape(n_seq * t_new, D_MODEL), t_new, attend_sample, 128, 512)

    heads = (ATTN_HEADS, HEAD_DIM)
    return (yp.reshape(batch, seq, D_MODEL),
            ys.reshape(n_seq, t_new, D_MODEL),
            kp.reshape(depth, batch, seq, *heads),
            vp.reshape(depth, batch, seq, *heads),
            ks.reshape(depth, n_seq, t_new, *heads),
            vs.reshape(depth, n_seq, t_new, *heads),
            gvs.reshape(depth, n_seq, t_new, MLP_WIDTH))
```

```python
import functools

import jax
import jax.numpy as jnp
from jax import lax
from jax.experimental import pallas as pl
from jax.experimental.pallas import tpu as pltpu

D_MODEL = 1024
ATTN_HEADS = 8
HEAD_DIM = 64
ATTN_WIDTH = ATTN_HEADS * HEAD_DIM
MOBA_BLOCK = 256
MOBA_TOPK = 3
QUERY_BLOCK = 128
MLP_GROUPS = 8
MLP_CH = 64
MLP_WIDTH = MLP_GROUPS * MLP_CH
CHUNK = 128
IN_WIDTH = 3 * ATTN_WIDTH + 2 * MLP_WIDTH
MOE_GROUPS = 4
EXPERTS_PER_GROUP = 4
N_EXPERTS = MOE_GROUPS * EXPERTS_PER_GROUP
D_EXPERT = D_MODEL // 2
EPS = 1e-6

LANES = 128
BF16_ROWS = 16
VMEM_LIMIT = 56 * 1024 * 1024

F32 = jnp.float32
BF16 = jnp.bfloat16
NEG_INF = float("-inf")
_NT = (((1,), (1,)), ((), ()))


def _rms(x, g):
    return x * lax.rsqrt(jnp.mean(x * x, axis=-1, keepdims=True) + EPS) * g


def _gelu(x):
    return 0.5 * x * (1.0 + jnp.tanh(0.7978845608028654 * (x + 0.044715 * (x * x * x))))


def _top_rank_select(gate, n_past, n_keep, axis):
    nb = gate.shape[axis]
    n_idx = lax.broadcasted_iota(jnp.int32, gate.shape, axis)
    rank = jnp.zeros(gate.shape, jnp.int32)
    for m in range(nb):
        gm = gate[:, m:m + 1] if axis == 1 else gate[m:m + 1, :]
        beats = jnp.where(gm > gate, 1, jnp.where(gm == gate, jnp.where(m < n_idx, 1, 0), 0))
        rank = rank + jnp.where(m < n_past, beats, 0)
    return jnp.where(n_idx < n_past, rank, n_keep) < n_keep


def _inproj_kernel(x_ref, g_ref, w_ref, wkv_t_ref, sgu_ref, wm_ref, bias_ref, og_ref,
                   q_ref, k_ref, v_ref, gmn_ref, vgn_ref, *, n_chunks, kv_transposed):
    h = _rms(x_ref[...], g_ref[...]).astype(BF16)

    def proj(lo, width):
        return jnp.dot(h, w_ref[:, lo:lo + width], preferred_element_type=F32)

    q_ref[...] = proj(0, ATTN_WIDTH)
    if kv_transposed:
        kv_t = lax.dot_general(wkv_t_ref[...], h, _NT, preferred_element_type=F32)
        k_ref[0] = kv_t[:ATTN_WIDTH]
        v_ref[0] = kv_t[ATTN_WIDTH:]
    else:
        k_ref[...] = proj(ATTN_WIDTH, ATTN_WIDTH)
        v_ref[...] = proj(2 * ATTN_WIDTH, ATTN_WIDTH)
    gu = _gelu(proj(3 * ATTN_WIDTH, MLP_WIDTH))
    vgn = _rms(_gelu(proj(3 * ATTN_WIDTH + MLP_WIDTH, MLP_WIDTH)), sgu_ref[...])
    vgn_ref[...] = vgn

    lane_grp = lax.broadcasted_iota(jnp.int32, (CHUNK, MLP_WIDTH), 1) // MLP_CH
    for c in range(n_chunks):
        rows = slice(c * CHUNK, (c + 1) * CHUNK)
        vc = vgn[rows].astype(BF16)
        vbd = jnp.concatenate(
            [jnp.where(lane_grp == g, vc, jnp.zeros_like(vc)) for g in range(MLP_GROUPS)], axis=0)
        mixed = jnp.dot(wm_ref[...], vbd, preferred_element_type=F32) + bias_ref[...]
        gmn_ref[rows, :] = _rms(gu[rows] * mixed, og_ref[...])


def _inproj(x, g, w_bf, wkv_t_bf, sgu_g, wm_cat, bias_full, og_mlp, tm, seq_transposed=None):
    rows = x.shape[0]
    row_spec = lambda width: pl.BlockSpec((tm, width), lambda i: (i, 0))
    full = lambda a: pl.BlockSpec(a.shape, lambda i: (0,) * a.ndim)
    out = jax.ShapeDtypeStruct((rows, ATTN_WIDTH), F32)
    kv_spec, kv_out = row_spec(ATTN_WIDTH), out
    if seq_transposed is not None:
        tiles = seq_transposed // tm
        kv_spec = pl.BlockSpec((1, ATTN_WIDTH, tm), lambda i: (i // tiles, 0, i % tiles))
        kv_out = jax.ShapeDtypeStruct((rows // seq_transposed, ATTN_WIDTH, seq_transposed), F32)
    return pl.pallas_call(
        functools.partial(_inproj_kernel, n_chunks=tm // CHUNK,
                          kv_transposed=seq_transposed is not None),
        grid=(rows // tm,),
        in_specs=[row_spec(D_MODEL), full(g), full(w_bf), full(wkv_t_bf), full(sgu_g), full(wm_cat),
                  full(bias_full), full(og_mlp)],
        out_specs=[row_spec(ATTN_WIDTH), kv_spec, kv_spec, row_spec(ATTN_WIDTH),
                   row_spec(ATTN_WIDTH)],
        out_shape=[out, kv_out, kv_out, out, out],
        compiler_params=pltpu.CompilerParams(
            dimension_semantics=("arbitrary",), vmem_limit_bytes=VMEM_LIMIT),
    )(x, g, w_bf, wkv_t_bf, sgu_g, wm_cat, bias_full, og_mlp)


V_ROWS = HEAD_DIM + BF16_ROWS


def _attn_prompt_kernel(q_ref, k_ref, v_ref, o_ref, ka_scr, vt_scr, kmean_scr, qa_scr, sel_scr,
                        m_scr, alpha_scr, acc_scr, s_scr, *, n_blocks):
    qi = pl.program_id(1)

    @pl.when(qi == 0)
    def _():
        key = lax.broadcasted_iota(jnp.int32, (MOBA_BLOCK, HEAD_DIM), 0)
        col = lax.broadcasted_iota(jnp.int32, (MOBA_BLOCK, HEAD_DIM), 1)
        ones = jnp.ones((BF16_ROWS, MOBA_BLOCK), BF16)
        for n in range(n_blocks):
            keys = slice(n * MOBA_BLOCK, (n + 1) * MOBA_BLOCK)
            kb = k_ref[0, :, keys].T
            kmean_scr[n:n + 1, :] = jnp.sum(kb, axis=0, keepdims=True) * (1.0 / MOBA_BLOCK)
            vt = v_ref[0, :, keys]
            extra = jnp.where(col == 0, key, jnp.where(col == 1, n, jnp.where(col == 2, 1, 0)))
            extra = extra.astype(F32).astype(BF16)
            for h in range(ATTN_HEADS):
                lanes = slice(h * HEAD_DIM, (h + 1) * HEAD_DIM)
                ka_scr[h, n] = jnp.concatenate([kb[:, lanes].astype(BF16), extra], axis=1)
                vt_scr[h, n, :HEAD_DIM, :] = vt[lanes, :].astype(BF16)
                vt_scr[h, n, HEAD_DIM:, :] = ones

    cur = (qi * QUERY_BLOCK) // MOBA_BLOCK
    q_t = q_ref[...].T
    row = lax.broadcasted_iota(jnp.int32, (HEAD_DIM, QUERY_BLOCK), 0)
    cur_f = cur.astype(F32)
    for h in range(ATTN_HEADS):
        lanes = slice(h * HEAD_DIM, (h + 1) * HEAD_DIM)
        slope = 2.0 ** (-8.0 * (h + 1) / ATTN_HEADS)
        qh_t = q_t[lanes, :]
        gate = jnp.dot(kmean_scr[:, lanes], qh_t, precision=lax.Precision.HIGHEST,
                       preferred_element_type=F32)
        sel_scr[h] = jnp.where(_top_rank_select(gate, cur, MOBA_TOPK, 0), 1.0, 0.0)
        extra = jnp.where(row == 0, slope,
                          jnp.where(row == 1, slope * MOBA_BLOCK,
                                    jnp.where(row == 2, -slope * MOBA_BLOCK * cur_f, 0.0)))
        qa_scr[h] = jnp.concatenate([qh_t * (HEAD_DIM ** -0.5), extra], axis=0).astype(BF16)

    def block_scores(h, j):
        return jnp.dot(ka_scr[h, j], qa_scr[h], preferred_element_type=F32)

    def attend_block(j, keep, first):
        for h in range(ATTN_HEADS):
            s = jnp.where(keep(h), block_scores(h, j), NEG_INF)
            s_scr[h] = s
            m_blk = jnp.max(s, axis=0, keepdims=True)
            if first:
                m_scr[h:h + 1, :] = m_blk
            else:
                m_old = m_scr[h:h + 1, :]
                m_new = jnp.maximum(m_old, m_blk)
                alpha_scr[h:h + 1, :] = jnp.exp(m_old - m_new)
                m_scr[h:h + 1, :] = m_new
        for h in range(ATTN_HEADS):
            p = jnp.exp(s_scr[h] - m_scr[h:h + 1, :]).astype(BF16)
            pv = jnp.dot(vt_scr[h, j], p, preferred_element_type=F32)
            acc_scr[h] = pv if first else alpha_scr[h:h + 1, :] * acc_scr[h] + pv

    key_minus_q = (lax.broadcasted_iota(jnp.int32, (MOBA_BLOCK, QUERY_BLOCK), 0)
                   - lax.broadcasted_iota(jnp.int32, (MOBA_BLOCK, QUERY_BLOCK), 1))
    causal = key_minus_q <= qi * QUERY_BLOCK - cur * MOBA_BLOCK
    attend_block(cur, lambda h: causal, True)

    def past_block(j, carry):
        attend_block(j, lambda h: sel_scr[h, pl.ds(j, 1), :] > 0.0, False)
        return carry

    lax.fori_loop(0, cur, past_block, 0)

    outs = []
    for h in range(ATTN_HEADS):
        acc = acc_scr[h]
        outs.append(acc[:HEAD_DIM] / acc[HEAD_DIM:HEAD_DIM + 1])
    o_ref[...] = jnp.concatenate(outs, axis=0).T


def _attn_prompt(q, k_t, v_t, batch, seq):
    nq = seq // QUERY_BLOCK
    n_blocks = seq // MOBA_BLOCK
    return pl.pallas_call(
        functools.partial(_attn_prompt_kernel, n_blocks=n_blocks),
        grid=(batch, nq),
        in_specs=[pl.BlockSpec((QUERY_BLOCK, ATTN_WIDTH), lambda b, i: (b * nq + i, 0)),
                  pl.BlockSpec((1, ATTN_WIDTH, seq), lambda b, i: (b, 0, 0)),
                  pl.BlockSpec((1, ATTN_WIDTH, seq), lambda b, i: (b, 0, 0))],
        out_specs=pl.BlockSpec((QUERY_BLOCK, ATTN_WIDTH), lambda b, i: (b * nq + i, 0)),
        out_shape=jax.ShapeDtypeStruct(q.shape, F32),
        scratch_shapes=[pltpu.VMEM((ATTN_HEADS, n_blocks, MOBA_BLOCK, 2 * HEAD_DIM), BF16),
                        pltpu.VMEM((ATTN_HEADS, n_blocks, V_ROWS, MOBA_BLOCK), BF16),
                        pltpu.VMEM((n_blocks, ATTN_WIDTH), F32),
                        pltpu.VMEM((ATTN_HEADS, 2 * HEAD_DIM, QUERY_BLOCK), BF16),
                        pltpu.VMEM((ATTN_HEADS, n_blocks, QUERY_BLOCK), F32),
                        pltpu.VMEM((ATTN_HEADS, QUERY_BLOCK), F32),
                        pltpu.VMEM((ATTN_HEADS, QUERY_BLOCK), F32),
                        pltpu.VMEM((ATTN_HEADS, V_ROWS, QUERY_BLOCK), F32),
                        pltpu.VMEM((ATTN_HEADS, MOBA_BLOCK, QUERY_BLOCK), F32)],
        compiler_params=pltpu.CompilerParams(
            dimension_semantics=("arbitrary", "arbitrary"), vmem_limit_bytes=VMEM_LIMIT),
    )(q, k_t, v_t)


def _attn_sample_kernel(pt_ref, q_ref, kn_ref, vn_ref, *refs, n_pages, page, t_new):
    del pt_ref
    kp = refs[:n_pages]
    vp = refs[n_pages:2 * n_pages]
    o_ref = refs[2 * n_pages]
    n_cols = t_new * ATTN_HEADS
    pages_per_block = MOBA_BLOCK // page
    n_past = n_pages // pages_per_block
    past_len = n_pages * page

    q = q_ref[0]
    lane_h = lax.broadcasted_iota(jnp.int32, (n_cols, ATTN_WIDTH), 1) // HEAD_DIM
    row = lax.broadcasted_iota(jnp.int32, (n_cols, 1), 0)
    row_h = row % ATTN_HEADS
    row_t = row // ATTN_HEADS
    own_head = lane_h == row_h
    qrep = jnp.concatenate(
        [jnp.broadcast_to(q[t:t + 1, :], (ATTN_HEADS, ATTN_WIDTH)) for t in range(t_new)], axis=0)
    qbd = jnp.where(own_head, qrep, 0.0)
    qbd_s = (qbd * (HEAD_DIM ** -0.5)).astype(BF16)
    slope = jnp.exp2(-8.0 * (row_h + 1).astype(F32) / ATTN_HEADS)

    def block_pages(refs_, n):
        return [refs_[i][0].reshape(ATTN_WIDTH, page)
                for i in range(n * pages_per_block, (n + 1) * pages_per_block)]

    kmean = jnp.concatenate(
        [jnp.sum(sum(block_pages(kp, n)), axis=1, keepdims=True) for n in range(n_past)],
        axis=1) * (1.0 / MOBA_BLOCK)
    gate = jnp.dot(qbd, kmean, precision=lax.Precision.HIGHEST, preferred_element_type=F32)
    sel = _top_rank_select(gate, n_past, MOBA_TOPK, 1)

    s_own = lax.dot_general(qbd_s, kn_ref[0].astype(BF16), _NT, preferred_element_type=F32)
    dist = row_t - lax.broadcasted_iota(jnp.int32, (1, t_new), 1)
    s_own = jnp.where(dist >= 0, s_own - slope * dist.astype(F32), NEG_INF)
    m = jnp.max(s_own, axis=-1, keepdims=True)

    key_off = lax.broadcasted_iota(jnp.int32, (1, MOBA_BLOCK), 1)
    s_past = []
    for n in range(n_past):
        kb_t = jnp.concatenate(block_pages(kp, n), axis=1).astype(BF16)
        dist = (past_len + row_t) - (n * MOBA_BLOCK + key_off)
        s = jnp.dot(qbd_s, kb_t, preferred_element_type=F32) - slope * dist.astype(F32)
        s = jnp.where(sel[:, n:n + 1], s, NEG_INF)
        m = jnp.maximum(m, jnp.max(s, axis=-1, keepdims=True))
        s_past.append(s)

    p = jnp.exp(s_own - m)
    l = jnp.sum(p, axis=-1, keepdims=True)
    vn = vn_ref[0]
    acc = sum(p[:, t:t + 1] * vn[t:t + 1, :] for t in range(t_new))
    for n in range(n_past):
        vb_t = jnp.concatenate(block_pages(vp, n), axis=1).astype(BF16)
        p = jnp.exp(s_past[n] - m)
        l = l + jnp.sum(p, axis=-1, keepdims=True)
        acc = acc + lax.dot_general(p.astype(BF16), vb_t, _NT, preferred_element_type=F32)

    out = jnp.where(own_head, acc / l, 0.0)
    o_ref[0] = jnp.sum(out.reshape(t_new, ATTN_HEADS, ATTN_WIDTH), axis=1)


def _attn_sample(q, kn, vn, cache_kt, cache_vt, page_table):
    n_seq, t_new, _ = q.shape
    n_pages = page_table.shape[1]
    page = cache_kt.shape[-1]
    seq_spec = pl.BlockSpec((1, t_new, ATTN_WIDTH), lambda i, pt: (i, 0, 0))

    def page_spec(p):
        return pl.BlockSpec((1, ATTN_HEADS, HEAD_DIM, page),
                            lambda i, pt: (pt[i * n_pages + p], 0, 0, 0))

    page_specs = [page_spec(p) for p in range(n_pages)]
    return pl.pallas_call(
        functools.partial(_attn_sample_kernel, n_pages=n_pages, page=page, t_new=t_new),
        grid_spec=pltpu.PrefetchScalarGridSpec(
            num_scalar_prefetch=1,
            grid=(n_seq,),
            in_specs=[seq_spec] * 3 + page_specs * 2,
            out_specs=seq_spec),
        out_shape=jax.ShapeDtypeStruct(q.shape, F32),
        compiler_params=pltpu.CompilerParams(
            dimension_semantics=("arbitrary",), vmem_limit_bytes=VMEM_LIMIT),
    )(page_table.reshape(-1), q, kn, vn, *([cache_kt] * n_pages), *([cache_vt] * n_pages))


def _outproj_kernel(att_ref, gmn_ref, x_ref, wo_ref, oga_ref, nfg_ref, wr_ref, br_ref,
                    x1_ref, h2_ref, comb_ref):
    attn = _rms(att_ref[...], oga_ref[...]).astype(BF16)
    mix = (jnp.dot(attn, wo_ref[:ATTN_WIDTH, :], preferred_element_type=F32)
           + jnp.dot(gmn_ref[...].astype(BF16), wo_ref[ATTN_WIDTH:, :], preferred_element_type=F32))
    x1 = x_ref[...] + mix
    x1_ref[...] = x1
    h2 = _rms(x1, nfg_ref[...])
    h2_hi = h2.astype(BF16)
    h2_ref[...] = h2_hi

    h2_lo = (h2 - h2_hi.astype(F32)).astype(BF16)
    hi_dot = jnp.dot(h2_hi, wr_ref[...], preferred_element_type=F32)
    lo_dot = jnp.dot(h2_lo, wr_ref[:, :LANES], preferred_element_type=F32)
    logits = hi_dot[:, :LANES] + hi_dot[:, LANES:] + lo_dot + br_ref[...]
    lt = logits.T
    tm = lt.shape[1]
    row4 = lax.broadcasted_iota(jnp.int32, (MOE_GROUPS, tm), 0)

    def first_argmax(v):
        vmax = jnp.max(v, axis=0, keepdims=True)
        idx = jnp.min(jnp.where(v == vmax, row4, MOE_GROUPS), axis=0, keepdims=True)
        return vmax, idx

    glog = lt[:MOE_GROUPS]
    ge = jnp.exp(glog - jnp.max(glog, axis=0, keepdims=True))
    gprob = ge / jnp.sum(ge, axis=0, keepdims=True)
    p_g, g_idx = first_argmax(gprob)
    elog = lt[MOE_GROUPS:MOE_GROUPS + EXPERTS_PER_GROUP]
    for g in range(1, MOE_GROUPS):
        lo = MOE_GROUPS + g * EXPERTS_PER_GROUP
        elog = jnp.where(g_idx == g, lt[lo:lo + EXPERTS_PER_GROUP], elog)
    l1, i1 = first_argmax(elog)
    l2, i2 = first_argmax(jnp.where(row4 == i1, NEG_INF, elog))
    e2 = jnp.exp(l2 - l1)
    denom = 1.0 + e2
    w1 = (1.0 / denom) * p_g
    w2 = (e2 / denom) * p_g
    lane_row = lax.broadcasted_iota(jnp.int32, (LANES, tm), 0)
    base = g_idx * EXPERTS_PER_GROUP
    comb_t = jnp.where(lane_row == base + i1, w1, 0.0) + jnp.where(lane_row == base + i2, w2, 0.0)
    comb_ref[...] = comb_t.T


def _outproj(att, gmn, x, wo_bf, og_attn, nf_g, w_r, b_r, tm):
    rows = x.shape[0]
    row_spec = lambda width: pl.BlockSpec((tm, width), lambda i: (i, 0))
    full = lambda a: pl.BlockSpec(a.shape, lambda i: (0,) * a.ndim)
    return pl.pallas_call(
        _outproj_kernel,
        grid=(rows // tm,),
        in_specs=[row_spec(ATTN_WIDTH), row_spec(MLP_WIDTH), row_spec(D_MODEL), full(wo_bf),
                  full(og_attn), full(nf_g), full(w_r), full(b_r)],
        out_specs=[row_spec(D_MODEL), row_spec(D_MODEL), row_spec(LANES)],
        out_shape=[jax.ShapeDtypeStruct((rows, D_MODEL), F32),
                   jax.ShapeDtypeStruct((rows, D_MODEL), BF16),
                   jax.ShapeDtypeStruct((rows, LANES), F32)],
        compiler_params=pltpu.CompilerParams(
            dimension_semantics=("arbitrary",), vmem_limit_bytes=VMEM_LIMIT),
    )(att, gmn, x, wo_bf, og_attn, nf_g, w_r, b_r)


def _moe_kernel(h_ref, comb_ref, wg_ref, wu_ref, wd_ref, x1_ref, fg_ref, y_ref, acc_ref):
    e = pl.program_id(1)

    @pl.when(e == 0)
    def _():
        acc_ref[...] = jnp.zeros_like(acc_ref)

    h = h_ref[...]
    hg = jnp.dot(h, wg_ref[0], preferred_element_type=F32)
    hu = jnp.dot(h, wu_ref[0], preferred_element_type=F32)
    comb = comb_ref[...]
    lane = lax.broadcasted_iota(jnp.int32, comb.shape, 1)
    c = jnp.sum(jnp.where(lane == e, comb, 0.0), axis=-1, keepdims=True)
    act = hg * (1.0 / (1.0 + jnp.exp(-hg))) * hu * c
    acc_ref[...] += jnp.dot(act.astype(BF16), wd_ref[0], preferred_element_type=F32)

    @pl.when(e == N_EXPERTS - 1)
    def _():
        y_ref[...] = _rms(x1_ref[...] + acc_ref[...], fg_ref[...])


def _moe(h2, comb, wg_bf, wu_bf, wd_bf, x1, final_g, tm):
    rows = h2.shape[0]
    row_spec = lambda width: pl.BlockSpec((tm, width), lambda i, e: (i, 0))
    return pl.pallas_call(
        _moe_kernel,
        grid=(rows // tm, N_EXPERTS),
        in_specs=[row_spec(D_MODEL), row_spec(LANES),
                  pl.BlockSpec((1, D_MODEL, D_EXPERT), lambda i, e: (e, 0, 0)),
                  pl.BlockSpec((1, D_MODEL, D_EXPERT), lambda i, e: (e, 0, 0)),
                  pl.BlockSpec((1, D_EXPERT, D_MODEL), lambda i, e: (e, 0, 0)),
                  row_spec(D_MODEL),
                  pl.BlockSpec(final_g.shape, lambda i, e: (0, 0))],
        out_specs=row_spec(D_MODEL),
        out_shape=jax.ShapeDtypeStruct((rows, D_MODEL), F32),
        scratch_shapes=[pltpu.VMEM((tm, D_MODEL), F32)],
        compiler_params=pltpu.CompilerParams(
            dimension_semantics=("arbitrary", "arbitrary"), vmem_limit_bytes=VMEM_LIMIT),
    )(h2, comb, wg_bf, wu_bf, wd_bf, x1, final_g)


def _spatial_operands(w_s, b_s, t_chunk):
    reps = CHUNK // t_chunk
    idx = jnp.arange(CHUNK)
    same = (idx[:, None] // t_chunk) == (idx[None, :] // t_chunk)
    causal = (idx[None, :] % t_chunk) <= (idx[:, None] % t_chunk)
    wm = jnp.tile(w_s[:, :t_chunk, :t_chunk], (1, reps, reps)) * (same & causal)
    wm_cat = wm.transpose(1, 0, 2).reshape(CHUNK, MLP_GROUPS * CHUNK).astype(BF16)
    bias = jnp.repeat(jnp.tile(b_s[:, :t_chunk], (1, reps)).T, MLP_CH, axis=1)
    return wm_cat, bias


def kernel(x_prompt, x_sample, cache_k, cache_v, page_table, norm_attn_g, w_in, sgu_g, w_spatial,
           b_spatial, out_g_attn, out_g_mlp, w_out, norm_ffn_g, w_group, b_group, w_router, b_router,
           w_gate, w_up, w_down, final_g):
    depth = w_in.shape[0]
    assert depth == 1, "single decoder layer"
    batch, seq, _ = x_prompt.shape
    n_seq, t_new, _ = x_sample.shape
    assert seq % MOBA_BLOCK == 0 and seq % CHUNK == 0 and CHUNK % t_new == 0
    assert MOBA_BLOCK % cache_k.shape[2] == 0

    row2 = lambda a: a.reshape(1, -1)
    w_in_bf = w_in[0].astype(BF16)
    w_out_bf = w_out[0].astype(BF16)
    wg_bf, wu_bf, wd_bf = w_gate[0].astype(BF16), w_up[0].astype(BF16), w_down[0].astype(BF16)
    n_logits = MOE_GROUPS + N_EXPERTS
    w_r = jnp.concatenate(
        [w_group[0], w_router[0].transpose(1, 0, 2).reshape(D_MODEL, N_EXPERTS),
         jnp.zeros((D_MODEL, LANES - n_logits), F32)], axis=1)
    b_r = jnp.concatenate(
        [b_group[0], b_router[0].reshape(-1), jnp.zeros((LANES - n_logits,), F32)]).reshape(1, LANES)
    w_r_hi = w_r.astype(BF16)
    w_r_hl = jnp.concatenate([w_r_hi, (w_r - w_r_hi.astype(F32)).astype(BF16)], axis=1)
    wkv_t_bf = w_in_bf[:, ATTN_WIDTH:3 * ATTN_WIDTH].T
    ck_t = jnp.transpose(cache_k[0], (0, 2, 3, 1))
    cv_t = jnp.transpose(cache_v[0], (0, 2, 3, 1))

    def layer(x, t_chunk, attend, tm_proj, tm_moe, seq_transposed):
        wm_cat, bias_full = _spatial_operands(w_spatial[0], b_spatial[0], t_chunk)
        q, k, v, gmn, vgn = _inproj(x, row2(norm_attn_g[0]), w_in_bf, wkv_t_bf, row2(sgu_g[0]),
                                    wm_cat, bias_full, row2(out_g_mlp[0]), tm_proj, seq_transposed)
        att = attend(q, k, v)
        x1, h2, comb = _outproj(att, gmn, x, w_out_bf, row2(out_g_attn[0]), row2(norm_ffn_g[0]),
                                w_r_hl, b_r, tm_proj)
        y = _moe(h2, comb, wg_bf, wu_bf, wd_bf, x1, row2(final_g), tm_moe)
        return y, k, v, vgn

    yp, kp_t, vp_t, _ = layer(
        x_prompt.reshape(batch * seq, D_MODEL), CHUNK,
        lambda q, k_t, v_t: _attn_prompt(q, k_t, v_t, batch, seq), 512, 1024, seq)

    def attend_sample(q, k, v):
        shape = (n_seq, t_new, ATTN_WIDTH)
        return _attn_sample(q.reshape(shape), k.reshape(shape), v.reshape(shape), ck_t, cv_t,
                            page_table).reshape(n_seq * t_new, ATTN_WIDTH)

    ys, ks, vs, gvs = layer(x_sample.reshape(n_seq * t_new, D_MODEL), t_new, attend_sample, 128, 512,
                            None)

    heads = (ATTN_HEADS, HEAD_DIM)
    rows_last = lambda a_t: a_t.reshape(batch, *heads, seq).transpose(0, 3, 1, 2)[None]
    return (yp.reshape(batch, seq, D_MODEL),
            ys.reshape(n_seq, t_new, D_MODEL),
            rows_last(kp_t),
            rows_last(vp_t),
            ks.reshape(depth, n_seq, t_new, *heads),
            vs.reshape(depth, n_seq, t_new, *heads),
            gvs.reshape(depth, n_seq, t_new, MLP_WIDTH))
```

```python
import functools

import jax
import jax.numpy as jnp
from jax import lax
from jax.experimental import pallas as pl
from jax.experimental.pallas import tpu as pltpu

D_MODEL = 1024
ATTN_HEADS = 8
HEAD_DIM = 64
ATTN_WIDTH = ATTN_HEADS * HEAD_DIM
MOBA_BLOCK = 256
MOBA_TOPK = 3
QUERY_BLOCK = 128
MLP_GROUPS = 8
MLP_CH = 64
MLP_WIDTH = MLP_GROUPS * MLP_CH
CHUNK = 128
IN_WIDTH = 3 * ATTN_WIDTH + 2 * MLP_WIDTH
MOE_GROUPS = 4
EXPERTS_PER_GROUP = 4
N_EXPERTS = MOE_GROUPS * EXPERTS_PER_GROUP
D_EXPERT = D_MODEL // 2
EPS = 1e-6
PAIR_LO = (0, 0, 0, 1, 1, 2)
PAIR_HI = (1, 2, 3, 2, 3, 3)
PAIRS_PER_GROUP = len(PAIR_LO)
N_BUCKETS = MOE_GROUPS * PAIRS_PER_GROUP

LANES = 128
BF16_ROWS = 16
VMEM_LIMIT = 56 * 1024 * 1024

F32 = jnp.float32
BF16 = jnp.bfloat16
NEG_INF = float("-inf")
_NT = (((1,), (1,)), ((), ()))


def _rms(x, g):
    return x * lax.rsqrt(jnp.mean(x * x, axis=-1, keepdims=True) + EPS) * g


def _gelu(x):
    return 0.5 * x * (1.0 + jnp.tanh(0.7978845608028654 * (x + 0.044715 * (x * x * x))))


def _top_rank_select(gate, n_past, n_keep, axis):
    nb = gate.shape[axis]
    n_idx = lax.broadcasted_iota(jnp.int32, gate.shape, axis)
    rank = jnp.zeros(gate.shape, jnp.int32)
    for m in range(nb):
        gm = gate[:, m:m + 1] if axis == 1 else gate[m:m + 1, :]
        beats = jnp.where(gm > gate, 1, jnp.where(gm == gate, jnp.where(m < n_idx, 1, 0), 0))
        rank = rank + jnp.where(m < n_past, beats, 0)
    return jnp.where(n_idx < n_past, rank, n_keep) < n_keep


def _inproj_kernel(x_ref, g_ref, w_ref, wkv_t_ref, sgu_ref, wm_ref, bias_ref, og_ref,
                   q_ref, k_ref, v_ref, gmn_ref, vgn_ref, *, n_chunks, kv_transposed):
    h = _rms(x_ref[...], g_ref[...]).astype(BF16)

    def proj(lo, width):
        return jnp.dot(h, w_ref[:, lo:lo + width], preferred_element_type=F32)

    q_ref[...] = proj(0, ATTN_WIDTH)
    if kv_transposed:
        kv_t = lax.dot_general(wkv_t_ref[...], h, _NT, preferred_element_type=F32)
        k_ref[0] = kv_t[:ATTN_WIDTH]
        v_ref[0] = kv_t[ATTN_WIDTH:]
    else:
        k_ref[...] = proj(ATTN_WIDTH, ATTN_WIDTH)
        v_ref[...] = proj(2 * ATTN_WIDTH, ATTN_WIDTH)
    gu = _gelu(proj(3 * ATTN_WIDTH, MLP_WIDTH))
    vgn = _rms(_gelu(proj(3 * ATTN_WIDTH + MLP_WIDTH, MLP_WIDTH)), sgu_ref[...])
    vgn_ref[...] = vgn

    lane_grp = lax.broadcasted_iota(jnp.int32, (CHUNK, MLP_WIDTH), 1) // MLP_CH
    for c in range(n_chunks):
        rows = slice(c * CHUNK, (c + 1) * CHUNK)
        vc = vgn[rows].astype(BF16)
        vbd = jnp.concatenate(
            [jnp.where(lane_grp == g, vc, jnp.zeros_like(vc)) for g in range(MLP_GROUPS)], axis=0)
        mixed = jnp.dot(wm_ref[...], vbd, preferred_element_type=F32) + bias_ref[...]
        gmn_ref[rows, :] = _rms(gu[rows] * mixed, og_ref[...])


def _inproj(x, g, w_bf, wkv_t_bf, sgu_g, wm_cat, bias_full, og_mlp, tm, seq_transposed=None):
    rows = x.shape[0]
    row_spec = lambda width: pl.BlockSpec((tm, width), lambda i: (i, 0))
    full = lambda a: pl.BlockSpec(a.shape, lambda i: (0,) * a.ndim)
    out = jax.ShapeDtypeStruct((rows, ATTN_WIDTH), F32)
    kv_spec, kv_out = row_spec(ATTN_WIDTH), out
    if seq_transposed is not None:
        tiles = seq_transposed // tm
        kv_spec = pl.BlockSpec((1, ATTN_WIDTH, tm), lambda i: (i // tiles, 0, i % tiles))
        kv_out = jax.ShapeDtypeStruct((rows // seq_transposed, ATTN_WIDTH, seq_transposed), F32)
    return pl.pallas_call(
        functools.partial(_inproj_kernel, n_chunks=tm // CHUNK,
                          kv_transposed=seq_transposed is not None),
        grid=(rows // tm,),
        in_specs=[row_spec(D_MODEL), full(g), full(w_bf), full(wkv_t_bf), full(sgu_g), full(wm_cat),
                  full(bias_full), full(og_mlp)],
        out_specs=[row_spec(ATTN_WIDTH), kv_spec, kv_spec, row_spec(ATTN_WIDTH),
                   row_spec(ATTN_WIDTH)],
        out_shape=[out, kv_out, kv_out, out, out],
        compiler_params=pltpu.CompilerParams(
            dimension_semantics=("arbitrary",), vmem_limit_bytes=VMEM_LIMIT),
    )(x, g, w_bf, wkv_t_bf, sgu_g, wm_cat, bias_full, og_mlp)


V_ROWS = HEAD_DIM + BF16_ROWS


def _attn_prompt_kernel(q_ref, k_ref, v_ref, o_ref, ka_scr, vt_scr, kmean_scr, qa_scr, sel_scr,
                        m_scr, alpha_scr, acc_scr, s_scr, *, n_blocks):
    qi = pl.program_id(1)

    @pl.when(qi == 0)
    def _():
        key = lax.broadcasted_iota(jnp.int32, (MOBA_BLOCK, HEAD_DIM), 0)
        col = lax.broadcasted_iota(jnp.int32, (MOBA_BLOCK, HEAD_DIM), 1)
        ones = jnp.ones((BF16_ROWS, MOBA_BLOCK), BF16)
        for n in range(n_blocks):
            keys = slice(n * MOBA_BLOCK, (n + 1) * MOBA_BLOCK)
            kb = k_ref[0, :, keys].T
            kmean_scr[n:n + 1, :] = jnp.sum(kb, axis=0, keepdims=True) * (1.0 / MOBA_BLOCK)
            vt = v_ref[0, :, keys]
            extra = jnp.where(col == 0, key, jnp.where(col == 1, n, jnp.where(col == 2, 1, 0)))
            extra = extra.astype(F32).astype(BF16)
            for h in range(ATTN_HEADS):
                lanes = slice(h * HEAD_DIM, (h + 1) * HEAD_DIM)
                ka_scr[h, n] = jnp.concatenate([kb[:, lanes].astype(BF16), extra], axis=1)
                vt_scr[h, n, :HEAD_DIM, :] = vt[lanes, :].astype(BF16)
                vt_scr[h, n, HEAD_DIM:, :] = ones

    cur = (qi * QUERY_BLOCK) // MOBA_BLOCK
    q_t = q_ref[...].T
    row = lax.broadcasted_iota(jnp.int32, (HEAD_DIM, QUERY_BLOCK), 0)
    cur_f = cur.astype(F32)
    for h in range(ATTN_HEADS):
        lanes = slice(h * HEAD_DIM, (h + 1) * HEAD_DIM)
        slope = 2.0 ** (-8.0 * (h + 1) / ATTN_HEADS)
        qh_t = q_t[lanes, :]
        gate = jnp.dot(kmean_scr[:, lanes], qh_t, precision=lax.Precision.HIGHEST,
                       preferred_element_type=F32)
        sel_scr[h] = jnp.where(_top_rank_select(gate, cur, MOBA_TOPK, 0), 1.0, 0.0)
        extra = jnp.where(row == 0, slope,
                          jnp.where(row == 1, slope * MOBA_BLOCK,
                                    jnp.where(row == 2, -slope * MOBA_BLOCK * cur_f, 0.0)))
        qa_scr[h] = jnp.concatenate([qh_t * (HEAD_DIM ** -0.5), extra], axis=0).astype(BF16)

    def block_scores(h, j):
        return jnp.dot(ka_scr[h, j], qa_scr[h], preferred_element_type=F32)

    def attend_block(j, keep, first):
        for h in range(ATTN_HEADS):
            s = jnp.where(keep(h), block_scores(h, j), NEG_INF)
            s_scr[h] = s
            m_blk = jnp.max(s, axis=0, keepdims=True)
            if first:
                m_scr[h:h + 1, :] = m_blk
            else:
                m_old = m_scr[h:h + 1, :]
                m_new = jnp.maximum(m_old, m_blk)
                alpha_scr[h:h + 1, :] = jnp.exp(m_old - m_new)
                m_scr[h:h + 1, :] = m_new
        for h in range(ATTN_HEADS):
            p = jnp.exp(s_scr[h] - m_scr[h:h + 1, :]).astype(BF16)
            pv = jnp.dot(vt_scr[h, j], p, preferred_element_type=F32)
            acc_scr[h] = pv if first else alpha_scr[h:h + 1, :] * acc_scr[h] + pv

    key_minus_q = (lax.broadcasted_iota(jnp.int32, (MOBA_BLOCK, QUERY_BLOCK), 0)
                   - lax.broadcasted_iota(jnp.int32, (MOBA_BLOCK, QUERY_BLOCK), 1))
    causal = key_minus_q <= qi * QUERY_BLOCK - cur * MOBA_BLOCK
    attend_block(cur, lambda h: causal, True)

    def past_block(j, carry):
        attend_block(j, lambda h: sel_scr[h, pl.ds(j, 1), :] > 0.0, False)
        return carry

    lax.fori_loop(0, cur, past_block, 0)

    outs = []
    for h in range(ATTN_HEADS):
        acc = acc_scr[h]
        outs.append(acc[:HEAD_DIM] / acc[HEAD_DIM:HEAD_DIM + 1])
    o_ref[...] = jnp.concatenate(outs, axis=0).T


def _attn_prompt(q, k_t, v_t, batch, seq):
    nq = seq // QUERY_BLOCK
    n_blocks = seq // MOBA_BLOCK
    return pl.pallas_call(
        functools.partial(_attn_prompt_kernel, n_blocks=n_blocks),
        grid=(batch, nq),
        in_specs=[pl.BlockSpec((QUERY_BLOCK, ATTN_WIDTH), lambda b, i: (b * nq + i, 0)),
                  pl.BlockSpec((1, ATTN_WIDTH, seq), lambda b, i: (b, 0, 0)),
                  pl.BlockSpec((1, ATTN_WIDTH, seq), lambda b, i: (b, 0, 0))],
        out_specs=pl.BlockSpec((QUERY_BLOCK, ATTN_WIDTH), lambda b, i: (b * nq + i, 0)),
        out_shape=jax.ShapeDtypeStruct(q.shape, F32),
        scratch_shapes=[pltpu.VMEM((ATTN_HEADS, n_blocks, MOBA_BLOCK, 2 * HEAD_DIM), BF16),
                        pltpu.VMEM((ATTN_HEADS, n_blocks, V_ROWS, MOBA_BLOCK), BF16),
                        pltpu.VMEM((n_blocks, ATTN_WIDTH), F32),
                        pltpu.VMEM((ATTN_HEADS, 2 * HEAD_DIM, QUERY_BLOCK), BF16),
                        pltpu.VMEM((ATTN_HEADS, n_blocks, QUERY_BLOCK), F32),
                        pltpu.VMEM((ATTN_HEADS, QUERY_BLOCK), F32),
                        pltpu.VMEM((ATTN_HEADS, QUERY_BLOCK), F32),
                        pltpu.VMEM((ATTN_HEADS, V_ROWS, QUERY_BLOCK), F32),
                        pltpu.VMEM((ATTN_HEADS, MOBA_BLOCK, QUERY_BLOCK), F32)],
        compiler_params=pltpu.CompilerParams(
            dimension_semantics=("arbitrary", "arbitrary"), vmem_limit_bytes=VMEM_LIMIT),
    )(q, k_t, v_t)


def _attn_sample_kernel(pt_ref, q_ref, kn_ref, vn_ref, *refs, n_pages, page, t_new):
    del pt_ref
    kp = refs[:n_pages]
    vp = refs[n_pages:2 * n_pages]
    o_ref = refs[2 * n_pages]
    n_cols = t_new * ATTN_HEADS
    pages_per_block = MOBA_BLOCK // page
    n_past = n_pages // pages_per_block
    past_len = n_pages * page

    q = q_ref[0]
    lane_h = lax.broadcasted_iota(jnp.int32, (n_cols, ATTN_WIDTH), 1) // HEAD_DIM
    row = lax.broadcasted_iota(jnp.int32, (n_cols, 1), 0)
    row_h = row % ATTN_HEADS
    row_t = row // ATTN_HEADS
    own_head = lane_h == row_h
    qrep = jnp.concatenate(
        [jnp.broadcast_to(q[t:t + 1, :], (ATTN_HEADS, ATTN_WIDTH)) for t in range(t_new)], axis=0)
    qbd = jnp.where(own_head, qrep, 0.0)
    qbd_s = (qbd * (HEAD_DIM ** -0.5)).astype(BF16)
    slope = jnp.exp2(-8.0 * (row_h + 1).astype(F32) / ATTN_HEADS)

    def block_pages(refs_, n):
        return [refs_[i][0].reshape(ATTN_WIDTH, page)
                for i in range(n * pages_per_block, (n + 1) * pages_per_block)]

    kmean = jnp.concatenate(
        [jnp.sum(sum(block_pages(kp, n)), axis=1, keepdims=True) for n in range(n_past)],
        axis=1) * (1.0 / MOBA_BLOCK)
    gate = jnp.dot(qbd, kmean, precision=lax.Precision.HIGHEST, preferred_element_type=F32)
    sel = _top_rank_select(gate, n_past, MOBA_TOPK, 1)

    s_own = lax.dot_general(qbd_s, kn_ref[0].astype(BF16), _NT, preferred_element_type=F32)
    dist = row_t - lax.broadcasted_iota(jnp.int32, (1, t_new), 1)
    s_own = jnp.where(dist >= 0, s_own - slope * dist.astype(F32), NEG_INF)
    m = jnp.max(s_own, axis=-1, keepdims=True)

    key_off = lax.broadcasted_iota(jnp.int32, (1, MOBA_BLOCK), 1)
    s_past = []
    for n in range(n_past):
        kb_t = jnp.concatenate(block_pages(kp, n), axis=1).astype(BF16)
        dist = (past_len + row_t) - (n * MOBA_BLOCK + key_off)
        s = jnp.dot(qbd_s, kb_t, preferred_element_type=F32) - slope * dist.astype(F32)
        s = jnp.where(sel[:, n:n + 1], s, NEG_INF)
        m = jnp.maximum(m, jnp.max(s, axis=-1, keepdims=True))
        s_past.append(s)

    p = jnp.exp(s_own - m)
    l = jnp.sum(p, axis=-1, keepdims=True)
    vn = vn_ref[0]
    acc = sum(p[:, t:t + 1] * vn[t:t + 1, :] for t in range(t_new))
    for n in range(n_past):
        vb_t = jnp.concatenate(block_pages(vp, n), axis=1).astype(BF16)
        p = jnp.exp(s_past[n] - m)
        l = l + jnp.sum(p, axis=-1, keepdims=True)
        acc = acc + lax.dot_general(p.astype(BF16), vb_t, _NT, preferred_element_type=F32)

    out = jnp.where(own_head, acc / l, 0.0)
    o_ref[0] = jnp.sum(out.reshape(t_new, ATTN_HEADS, ATTN_WIDTH), axis=1)


def _attn_sample(q, kn, vn, cache_kt, cache_vt, page_table):
    n_seq, t_new, _ = q.shape
    n_pages = page_table.shape[1]
    page = cache_kt.shape[-1]
    seq_spec = pl.BlockSpec((1, t_new, ATTN_WIDTH), lambda i, pt: (i, 0, 0))

    def page_spec(p):
        return pl.BlockSpec((1, ATTN_HEADS, HEAD_DIM, page),
                            lambda i, pt: (pt[i * n_pages + p], 0, 0, 0))

    page_specs = [page_spec(p) for p in range(n_pages)]
    return pl.pallas_call(
        functools.partial(_attn_sample_kernel, n_pages=n_pages, page=page, t_new=t_new),
        grid_spec=pltpu.PrefetchScalarGridSpec(
            num_scalar_prefetch=1,
            grid=(n_seq,),
            in_specs=[seq_spec] * 3 + page_specs * 2,
            out_specs=seq_spec),
        out_shape=jax.ShapeDtypeStruct(q.shape, F32),
        compiler_params=pltpu.CompilerParams(
            dimension_semantics=("arbitrary",), vmem_limit_bytes=VMEM_LIMIT),
    )(page_table.reshape(-1), q, kn, vn, *([cache_kt] * n_pages), *([cache_vt] * n_pages))


def _outproj_kernel(att_ref, gmn_ref, x_ref, wo_ref, oga_ref, nfg_ref, wr_ref, br_ref,
                    x1_ref, h2_ref, comb_ref, bucket_ref, wpair_ref):
    attn = _rms(att_ref[...], oga_ref[...]).astype(BF16)
    mix = (jnp.dot(attn, wo_ref[:ATTN_WIDTH, :], preferred_element_type=F32)
           + jnp.dot(gmn_ref[...].astype(BF16), wo_ref[ATTN_WIDTH:, :], preferred_element_type=F32))
    x1 = x_ref[...] + mix
    x1_ref[...] = x1
    h2 = _rms(x1, nfg_ref[...])
    h2_hi = h2.astype(BF16)
    h2_ref[...] = h2_hi

    h2_lo = (h2 - h2_hi.astype(F32)).astype(BF16)
    hi_dot = jnp.dot(h2_hi, wr_ref[...], preferred_element_type=F32)
    lo_dot = jnp.dot(h2_lo, wr_ref[:, :LANES], preferred_element_type=F32)
    logits = hi_dot[:, :LANES] + hi_dot[:, LANES:] + lo_dot + br_ref[...]
    lt = logits.T
    tm = lt.shape[1]
    row4 = lax.broadcasted_iota(jnp.int32, (MOE_GROUPS, tm), 0)

    def first_argmax(v):
        vmax = jnp.max(v, axis=0, keepdims=True)
        idx = jnp.min(jnp.where(v == vmax, row4, MOE_GROUPS), axis=0, keepdims=True)
        return vmax, idx

    glog = lt[:MOE_GROUPS]
    ge = jnp.exp(glog - jnp.max(glog, axis=0, keepdims=True))
    gprob = ge / jnp.sum(ge, axis=0, keepdims=True)
    p_g, g_idx = first_argmax(gprob)
    elog = lt[MOE_GROUPS:MOE_GROUPS + EXPERTS_PER_GROUP]
    for g in range(1, MOE_GROUPS):
        lo = MOE_GROUPS + g * EXPERTS_PER_GROUP
        elog = jnp.where(g_idx == g, lt[lo:lo + EXPERTS_PER_GROUP], elog)
    l1, i1 = first_argmax(elog)
    l2, i2 = first_argmax(jnp.where(row4 == i1, NEG_INF, elog))
    e2 = jnp.exp(l2 - l1)
    denom = 1.0 + e2
    w1 = (1.0 / denom) * p_g
    w2 = (e2 / denom) * p_g
    lane_row = lax.broadcasted_iota(jnp.int32, (LANES, tm), 0)
    base = g_idx * EXPERTS_PER_GROUP
    comb_t = jnp.where(lane_row == base + i1, w1, 0.0) + jnp.where(lane_row == base + i2, w2, 0.0)
    comb_ref[...] = comb_t.T

    e_lo = jnp.minimum(i1, i2)
    e_hi = jnp.maximum(i1, i2)
    pair = jnp.where(e_lo == 0, e_hi - 1, jnp.where(e_lo == 1, e_hi + 1, PAIRS_PER_GROUP - 1))
    row8 = lax.broadcasted_iota(jnp.int32, (8, tm), 0)
    bucket_ref[...] = jnp.where(row8 == 0, g_idx * PAIRS_PER_GROUP + pair, 0)
    w_lo = jnp.where(i1 < i2, w1, w2)
    w_hi = jnp.where(i1 < i2, w2, w1)
    wpair_ref[...] = jnp.where(row8 == 0, w_lo, jnp.where(row8 == 1, w_hi, 0.0))


def _outproj(att, gmn, x, wo_bf, og_attn, nf_g, w_r, b_r, tm):
    rows = x.shape[0]
    row_spec = lambda width: pl.BlockSpec((tm, width), lambda i: (i, 0))
    full = lambda a: pl.BlockSpec(a.shape, lambda i: (0,) * a.ndim)
    return pl.pallas_call(
        _outproj_kernel,
        grid=(rows // tm,),
        in_specs=[row_spec(ATTN_WIDTH), row_spec(MLP_WIDTH), row_spec(D_MODEL), full(wo_bf),
                  full(og_attn), full(nf_g), full(w_r), full(b_r)],
        out_specs=[row_spec(D_MODEL), row_spec(D_MODEL), row_spec(LANES),
                   pl.BlockSpec((8, tm), lambda i: (0, i)), pl.BlockSpec((8, tm), lambda i: (0, i))],
        out_shape=[jax.ShapeDtypeStruct((rows, D_MODEL), F32),
                   jax.ShapeDtypeStruct((rows, D_MODEL), BF16),
                   jax.ShapeDtypeStruct((rows, LANES), F32),
                   jax.ShapeDtypeStruct((8, rows), jnp.int32),
                   jax.ShapeDtypeStruct((8, rows), F32)],
        compiler_params=pltpu.CompilerParams(
            dimension_semantics=("arbitrary",), vmem_limit_bytes=VMEM_LIMIT),
    )(att, gmn, x, wo_bf, og_attn, nf_g, w_r, b_r)


def _moe_kernel(h_ref, comb_ref, wg_ref, wu_ref, wd_ref, x1_ref, fg_ref, y_ref, acc_ref):
    e = pl.program_id(1)

    @pl.when(e == 0)
    def _():
        acc_ref[...] = jnp.zeros_like(acc_ref)

    h = h_ref[...]
    hg = jnp.dot(h, wg_ref[0], preferred_element_type=F32)
    hu = jnp.dot(h, wu_ref[0], preferred_element_type=F32)
    comb = comb_ref[...]
    lane = lax.broadcasted_iota(jnp.int32, comb.shape, 1)
    c = jnp.sum(jnp.where(lane == e, comb, 0.0), axis=-1, keepdims=True)
    act = hg * (1.0 / (1.0 + jnp.exp(-hg))) * hu * c
    acc_ref[...] += jnp.dot(act.astype(BF16), wd_ref[0], preferred_element_type=F32)

    @pl.when(e == N_EXPERTS - 1)
    def _():
        y_ref[...] = _rms(x1_ref[...] + acc_ref[...], fg_ref[...])


def _moe(h2, comb, wg_bf, wu_bf, wd_bf, x1, final_g, tm):
    rows = h2.shape[0]
    row_spec = lambda width: pl.BlockSpec((tm, width), lambda i, e: (i, 0))
    return pl.pallas_call(
        _moe_kernel,
        grid=(rows // tm, N_EXPERTS),
        in_specs=[row_spec(D_MODEL), row_spec(LANES),
                  pl.BlockSpec((1, D_MODEL, D_EXPERT), lambda i, e: (e, 0, 0)),
                  pl.BlockSpec((1, D_MODEL, D_EXPERT), lambda i, e: (e, 0, 0)),
                  pl.BlockSpec((1, D_EXPERT, D_MODEL), lambda i, e: (e, 0, 0)),
                  row_spec(D_MODEL),
                  pl.BlockSpec(final_g.shape, lambda i, e: (0, 0))],
        out_specs=row_spec(D_MODEL),
        out_shape=jax.ShapeDtypeStruct((rows, D_MODEL), F32),
        scratch_shapes=[pltpu.VMEM((tm, D_MODEL), F32)],
        compiler_params=pltpu.CompilerParams(
            dimension_semantics=("arbitrary", "arbitrary"), vmem_limit_bytes=VMEM_LIMIT),
    )(h2, comb, wg_bf, wu_bf, wd_bf, x1, final_g)


def _moe_plan(bucket, wpair, tm):
    rows = bucket.shape[0]
    n_tiles_max = rows // tm + N_BUCKETS
    order = jnp.argsort(bucket, stable=True).astype(jnp.int32)
    counts = jnp.sum((bucket[:, None] == jnp.arange(N_BUCKETS)[None, :]).astype(jnp.int32), axis=0)
    starts = jnp.cumsum(counts) - counts
    tiles_b = (counts + tm - 1) // tm
    tile_end = jnp.cumsum(tiles_b)
    tile_start = tile_end - tiles_b
    n_tiles = tile_end[-1]
    t = jnp.arange(n_tiles_max, dtype=jnp.int32)
    tb = jnp.searchsorted(tile_end, jnp.minimum(t, n_tiles - 1), side="right").astype(jnp.int32)
    local = (t - tile_start[tb]) * tm
    n_valid = jnp.where(t < n_tiles, jnp.clip(counts[tb] - local, 0, tm), 0).astype(jnp.int32)
    slot = local[:, None] + jnp.arange(tm, dtype=jnp.int32)[None, :]
    src = starts[tb][:, None] + jnp.minimum(slot, counts[tb][:, None] - 1)
    row_ids = order[jnp.clip(src, 0, rows - 1)].reshape(-1)
    group = tb // PAIRS_PER_GROUP
    pair = tb % PAIRS_PER_GROUP
    e_lo = group * EXPERTS_PER_GROUP + jnp.asarray(PAIR_LO, jnp.int32)[pair]
    e_hi = group * EXPERTS_PER_GROUP + jnp.asarray(PAIR_HI, jnp.int32)[pair]
    w_slots = wpair[:, row_ids].T
    return row_ids, e_lo, e_hi, n_valid, w_slots


def _moe_grouped_kernel(rows_ref, elo_ref, ehi_ref, nv_ref, x1_hbm, ws_ref, wg_lo, wu_lo, wd_lo,
                        wg_hi, wu_hi, wd_hi, nfg_ref, fg_ref, y_hbm, xbuf, ybuf, sem_in, sem_out,
                        *, tm, n_steps):
    del elo_ref, ehi_ref
    t = pl.program_id(0)
    slot = t % 2

    def row_in(tile, s, slot_):
        r = rows_ref[tile * tm + s]
        return pltpu.make_async_copy(x1_hbm.at[pl.ds(r, 1)], xbuf.at[slot_, pl.ds(s, 1)],
                                     sem_in.at[slot_])

    def row_out(tile, s, slot_):
        r = rows_ref[tile * tm + s]
        return pltpu.make_async_copy(ybuf.at[slot_, pl.ds(s, 1)], y_hbm.at[pl.ds(r, 1)],
                                     sem_out.at[slot_])

    def start_gather(tile, slot_):
        @pl.when(nv_ref[tile] > 0)
        def _():
            def body(s, c):
                row_in(tile, s, slot_).start()
                return c
            lax.fori_loop(0, tm, body, 0, unroll=8)

    def wait_scatter(tile, slot_):
        n = nv_ref[tile]
        p = tm
        while p >= 1:
            @pl.when((n & p) != 0)
            def _(p=p):
                pltpu.make_async_copy(ybuf.at[slot_, pl.ds(0, p)], y_hbm.at[pl.ds(0, p)],
                                      sem_out.at[slot_]).wait()
            p //= 2

    @pl.when(t == 0)
    def _():
        start_gather(0, 0)

    @pl.when(t + 1 < n_steps)
    def _():
        start_gather(t + 1, 1 - slot)

    @pl.when(t >= 2)
    def _():
        wait_scatter(t - 2, slot)

    n_valid = nv_ref[t]

    @pl.when(n_valid > 0)
    def _():
        pltpu.make_async_copy(x1_hbm.at[pl.ds(0, tm)], xbuf.at[slot], sem_in.at[slot]).wait()
        x1 = xbuf[slot]
        h = _rms(x1, nfg_ref[...]).astype(BF16)
        ws = ws_ref[...]
        moe = jnp.zeros((tm, D_MODEL), F32)
        for col, (wg, wu, wd) in enumerate(((wg_lo, wu_lo, wd_lo), (wg_hi, wu_hi, wd_hi))):
            hg = jnp.dot(h, wg[0], preferred_element_type=F32)
            hu = jnp.dot(h, wu[0], preferred_element_type=F32)
            act = hg * (1.0 / (1.0 + jnp.exp(-hg))) * hu * ws[:, col:col + 1]
            moe = moe + jnp.dot(act.astype(BF16), wd[0], preferred_element_type=F32)
        ybuf[slot] = _rms(x1 + moe, fg_ref[...])

        def body(s, c):
            row_out(t, s, slot).start()
            return c
        lax.fori_loop(0, n_valid, body, 0)

    @pl.when(t == n_steps - 1)
    def _():
        wait_scatter(t - 1, 1 - slot)
        wait_scatter(t, slot)


def _moe_grouped(x1, bucket, wpair, wg_bf, wu_bf, wd_bf, nf_g, final_g, tm):
    rows = x1.shape[0]
    row_ids, e_lo, e_hi, n_valid, w_slots = _moe_plan(bucket, wpair, tm)
    n_steps = n_valid.shape[0]
    lo_spec = lambda shape: pl.BlockSpec(shape, lambda t, r, elo, ehi, nv: (elo[t], 0, 0))
    hi_spec = lambda shape: pl.BlockSpec(shape, lambda t, r, elo, ehi, nv: (ehi[t], 0, 0))
    up_shape, down_shape = (1, D_MODEL, D_EXPERT), (1, D_EXPERT, D_MODEL)
    const = lambda a: pl.BlockSpec(a.shape, lambda t, r, elo, ehi, nv: (0, 0))
    return pl.pallas_call(
        functools.partial(_moe_grouped_kernel, tm=tm, n_steps=n_steps),
        grid_spec=pltpu.PrefetchScalarGridSpec(
            num_scalar_prefetch=4,
            grid=(n_steps,),
            in_specs=[pl.BlockSpec(memory_space=pl.ANY),
                      pl.BlockSpec((tm, 2), lambda t, r, elo, ehi, nv: (t, 0)),
                      lo_spec(up_shape), lo_spec(up_shape), lo_spec(down_shape),
                      hi_spec(up_shape), hi_spec(up_shape), hi_spec(down_shape),
                      const(nf_g), const(final_g)],
            out_specs=pl.BlockSpec(memory_space=pl.ANY),
            scratch_shapes=[pltpu.VMEM((2, tm, D_MODEL), F32),
                            pltpu.VMEM((2, tm, D_MODEL), F32),
                            pltpu.SemaphoreType.DMA((2,)),
                            pltpu.SemaphoreType.DMA((2,))]),
        out_shape=jax.ShapeDtypeStruct((rows, D_MODEL), F32),
        compiler_params=pltpu.CompilerParams(
            dimension_semantics=("arbitrary",), vmem_limit_bytes=VMEM_LIMIT),
    )(row_ids, e_lo, e_hi, n_valid, x1, w_slots, wg_bf, wu_bf, wd_bf, wg_bf, wu_bf, wd_bf,
      nf_g, final_g)


def _spatial_operands(w_s, b_s, t_chunk):
    reps = CHUNK // t_chunk
    idx = jnp.arange(CHUNK)
    same = (idx[:, None] // t_chunk) == (idx[None, :] // t_chunk)
    causal = (idx[None, :] % t_chunk) <= (idx[:, None] % t_chunk)
    wm = jnp.tile(w_s[:, :t_chunk, :t_chunk], (1, reps, reps)) * (same & causal)
    wm_cat = wm.transpose(1, 0, 2).reshape(CHUNK, MLP_GROUPS * CHUNK).astype(BF16)
    bias = jnp.repeat(jnp.tile(b_s[:, :t_chunk], (1, reps)).T, MLP_CH, axis=1)
    return wm_cat, bias


def kernel(x_prompt, x_sample, cache_k, cache_v, page_table, norm_attn_g, w_in, sgu_g, w_spatial,
           b_spatial, out_g_attn, out_g_mlp, w_out, norm_ffn_g, w_group, b_group, w_router, b_router,
           w_gate, w_up, w_down, final_g):
    depth = w_in.shape[0]
    assert depth == 1, "single decoder layer"
    batch, seq, _ = x_prompt.shape
    n_seq, t_new, _ = x_sample.shape
    assert seq % MOBA_BLOCK == 0 and seq % CHUNK == 0 and CHUNK % t_new == 0
    assert MOBA_BLOCK % cache_k.shape[2] == 0

    row2 = lambda a: a.reshape(1, -1)
    w_in_bf = w_in[0].astype(BF16)
    w_out_bf = w_out[0].astype(BF16)
    wg_bf, wu_bf, wd_bf = w_gate[0].astype(BF16), w_up[0].astype(BF16), w_down[0].astype(BF16)
    n_logits = MOE_GROUPS + N_EXPERTS
    w_r = jnp.concatenate(
        [w_group[0], w_router[0].transpose(1, 0, 2).reshape(D_MODEL, N_EXPERTS),
         jnp.zeros((D_MODEL, LANES - n_logits), F32)], axis=1)
    b_r = jnp.concatenate(
        [b_group[0], b_router[0].reshape(-1), jnp.zeros((LANES - n_logits,), F32)]).reshape(1, LANES)
    w_r_hi = w_r.astype(BF16)
    w_r_hl = jnp.concatenate([w_r_hi, (w_r - w_r_hi.astype(F32)).astype(BF16)], axis=1)
    wkv_t_bf = w_in_bf[:, ATTN_WIDTH:3 * ATTN_WIDTH].T
    ck_t = jnp.transpose(cache_k[0], (0, 2, 3, 1))
    cv_t = jnp.transpose(cache_v[0], (0, 2, 3, 1))

    def layer(x, t_chunk, attend, tm_proj, tm_moe, seq_transposed):
        wm_cat, bias_full = _spatial_operands(w_spatial[0], b_spatial[0], t_chunk)
        q, k, v, gmn, vgn = _inproj(x, row2(norm_attn_g[0]), w_in_bf, wkv_t_bf, row2(sgu_g[0]),
                                    wm_cat, bias_full, row2(out_g_mlp[0]), tm_proj, seq_transposed)
        att = attend(q, k, v)
        x1, h2, comb, bucket, wpair = _outproj(att, gmn, x, w_out_bf, row2(out_g_attn[0]),
                                               row2(norm_ffn_g[0]), w_r_hl, b_r, tm_proj)
        if seq_transposed is not None:
            y = _moe_grouped(x1, bucket[0], wpair[:2], wg_bf, wu_bf, wd_bf, row2(norm_ffn_g[0]),
                             row2(final_g), tm_moe)
        else:
            y = _moe(h2, comb, wg_bf, wu_bf, wd_bf, x1, row2(final_g), tm_moe)
        return y, k, v, vgn

    yp, kp_t, vp_t, _ = layer(
        x_prompt.reshape(batch * seq, D_MODEL), CHUNK,
        lambda q, k_t, v_t: _attn_prompt(q, k_t, v_t, batch, seq), 512, 256, seq)

    def attend_sample(q, k, v):
        shape = (n_seq, t_new, ATTN_WIDTH)
        return _attn_sample(q.reshape(shape), k.reshape(shape), v.reshape(shape), ck_t, cv_t,
                            page_table).reshape(n_seq * t_new, ATTN_WIDTH)

    ys, ks, vs, gvs = layer(x_sample.reshape(n_seq * t_new, D_MODEL), t_new, attend_sample, 128, 512,
                            None)

    heads = (ATTN_HEADS, HEAD_DIM)
    rows_last = lambda a_t: a_t.reshape(batch, *heads, seq).transpose(0, 3, 1, 2)[None]
    return (yp.reshape(batch, seq, D_MODEL),
            ys.reshape(n_seq, t_new, D_MODEL),
            rows_last(kp_t),
            rows_last(vp_t),
            ks.reshape(depth, n_seq, t_new, *heads),
            vs.reshape(depth, n_seq, t_new, *heads),
            gvs.reshape(depth, n_seq, t_new, MLP_WIDTH))
```

```python
import functools

import jax
import jax.numpy as jnp
from jax import lax
from jax.experimental import pallas as pl
from jax.experimental.pallas import tpu as pltpu
from jax.experimental.pallas import tpu_sc as plsc

D_MODEL = 1024
ATTN_HEADS = 8
HEAD_DIM = 64
ATTN_WIDTH = ATTN_HEADS * HEAD_DIM
MOBA_BLOCK = 256
MOBA_TOPK = 3
QUERY_BLOCK = 128
MLP_GROUPS = 8
MLP_CH = 64
MLP_WIDTH = MLP_GROUPS * MLP_CH
CHUNK = 128
IN_WIDTH = 3 * ATTN_WIDTH + 2 * MLP_WIDTH
MOE_GROUPS = 4
EXPERTS_PER_GROUP = 4
N_EXPERTS = MOE_GROUPS * EXPERTS_PER_GROUP
D_EXPERT = D_MODEL // 2
EPS = 1e-6
PAIR_LO = (0, 0, 0, 1, 1, 2)
PAIR_HI = (1, 2, 3, 2, 3, 3)
PAIRS_PER_GROUP = len(PAIR_LO)
N_BUCKETS = MOE_GROUPS * PAIRS_PER_GROUP

LANES = 128
BF16_ROWS = 16
VMEM_LIMIT = 56 * 1024 * 1024

F32 = jnp.float32
BF16 = jnp.bfloat16
NEG_INF = float("-inf")
_NT = (((1,), (1,)), ((), ()))


def _rms(x, g):
    return x * lax.rsqrt(jnp.mean(x * x, axis=-1, keepdims=True) + EPS) * g


def _gelu(x):
    return 0.5 * x * (1.0 + jnp.tanh(0.7978845608028654 * (x + 0.044715 * (x * x * x))))


def _top_rank_select(gate, n_past, n_keep, axis):
    nb = gate.shape[axis]
    n_idx = lax.broadcasted_iota(jnp.int32, gate.shape, axis)
    rank = jnp.zeros(gate.shape, jnp.int32)
    for m in range(nb):
        gm = gate[:, m:m + 1] if axis == 1 else gate[m:m + 1, :]
        beats = jnp.where(gm > gate, 1, jnp.where(gm == gate, jnp.where(m < n_idx, 1, 0), 0))
        rank = rank + jnp.where(m < n_past, beats, 0)
    return jnp.where(n_idx < n_past, rank, n_keep) < n_keep


def _inproj_kernel(x_ref, g_ref, w_ref, wkv_t_ref, sgu_ref, wm_ref, bias_ref, og_ref,
                   q_ref, k_ref, v_ref, gmn_ref, vgn_ref, *, n_chunks, kv_transposed):
    h = _rms(x_ref[...], g_ref[...]).astype(BF16)

    def proj(lo, width):
        return jnp.dot(h, w_ref[:, lo:lo + width], preferred_element_type=F32)

    q_ref[...] = proj(0, ATTN_WIDTH)
    if kv_transposed:
        kv_t = lax.dot_general(wkv_t_ref[...], h, _NT, preferred_element_type=F32)
        k_ref[0] = kv_t[:ATTN_WIDTH]
        v_ref[0] = kv_t[ATTN_WIDTH:]
    else:
        k_ref[...] = proj(ATTN_WIDTH, ATTN_WIDTH)
        v_ref[...] = proj(2 * ATTN_WIDTH, ATTN_WIDTH)
    gu = _gelu(proj(3 * ATTN_WIDTH, MLP_WIDTH))
    vgn = _rms(_gelu(proj(3 * ATTN_WIDTH + MLP_WIDTH, MLP_WIDTH)), sgu_ref[...])
    vgn_ref[...] = vgn

    lane_grp = lax.broadcasted_iota(jnp.int32, (CHUNK, MLP_WIDTH), 1) // MLP_CH
    for c in range(n_chunks):
        rows = slice(c * CHUNK, (c + 1) * CHUNK)
        vc = vgn[rows].astype(BF16)
        vbd = jnp.concatenate(
            [jnp.where(lane_grp == g, vc, jnp.zeros_like(vc)) for g in range(MLP_GROUPS)], axis=0)
        mixed = jnp.dot(wm_ref[...], vbd, preferred_element_type=F32) + bias_ref[...]
        gmn_ref[rows, :] = _rms(gu[rows] * mixed, og_ref[...])


def _inproj(x, g, w_bf, wkv_t_bf, sgu_g, wm_cat, bias_full, og_mlp, tm, seq_transposed=None):
    rows = x.shape[0]
    row_spec = lambda width: pl.BlockSpec((tm, width), lambda i: (i, 0))
    full = lambda a: pl.BlockSpec(a.shape, lambda i: (0,) * a.ndim)
    out = jax.ShapeDtypeStruct((rows, ATTN_WIDTH), F32)
    kv_spec, kv_out = row_spec(ATTN_WIDTH), out
    if seq_transposed is not None:
        tiles = seq_transposed // tm
        kv_spec = pl.BlockSpec((1, ATTN_WIDTH, tm), lambda i: (i // tiles, 0, i % tiles))
        kv_out = jax.ShapeDtypeStruct((rows // seq_transposed, ATTN_WIDTH, seq_transposed), F32)
    return pl.pallas_call(
        functools.partial(_inproj_kernel, n_chunks=tm // CHUNK,
                          kv_transposed=seq_transposed is not None),
        grid=(rows // tm,),
        in_specs=[row_spec(D_MODEL), full(g), full(w_bf), full(wkv_t_bf), full(sgu_g), full(wm_cat),
                  full(bias_full), full(og_mlp)],
        out_specs=[row_spec(ATTN_WIDTH), kv_spec, kv_spec, row_spec(ATTN_WIDTH),
                   row_spec(ATTN_WIDTH)],
        out_shape=[out, kv_out, kv_out, out, out],
        compiler_params=pltpu.CompilerParams(
            dimension_semantics=("arbitrary",), vmem_limit_bytes=VMEM_LIMIT),
    )(x, g, w_bf, wkv_t_bf, sgu_g, wm_cat, bias_full, og_mlp)


V_ROWS = HEAD_DIM + BF16_ROWS


def _attn_prompt_kernel(q_ref, k_ref, v_ref, o_ref, ka_scr, vt_scr, kmean_scr, qa_scr, sel_scr,
                        m_scr, alpha_scr, acc_scr, s_scr, *, n_blocks):
    qi = pl.program_id(1)

    @pl.when(qi == 0)
    def _():
        key = lax.broadcasted_iota(jnp.int32, (MOBA_BLOCK, HEAD_DIM), 0)
        col = lax.broadcasted_iota(jnp.int32, (MOBA_BLOCK, HEAD_DIM), 1)
        ones = jnp.ones((BF16_ROWS, MOBA_BLOCK), BF16)
        for n in range(n_blocks):
            keys = slice(n * MOBA_BLOCK, (n + 1) * MOBA_BLOCK)
            kb = k_ref[0, :, keys].T
            kmean_scr[n:n + 1, :] = jnp.sum(kb, axis=0, keepdims=True) * (1.0 / MOBA_BLOCK)
            vt = v_ref[0, :, keys]
            extra = jnp.where(col == 0, key, jnp.where(col == 1, n, jnp.where(col == 2, 1, 0)))
            extra = extra.astype(F32).astype(BF16)
            for h in range(ATTN_HEADS):
                lanes = slice(h * HEAD_DIM, (h + 1) * HEAD_DIM)
                ka_scr[h, n] = jnp.concatenate([kb[:, lanes].astype(BF16), extra], axis=1)
                vt_scr[h, n, :HEAD_DIM, :] = vt[lanes, :].astype(BF16)
                vt_scr[h, n, HEAD_DIM:, :] = ones

    cur = (qi * QUERY_BLOCK) // MOBA_BLOCK
    q_t = q_ref[...].T
    row = lax.broadcasted_iota(jnp.int32, (HEAD_DIM, QUERY_BLOCK), 0)
    cur_f = cur.astype(F32)
    for h in range(ATTN_HEADS):
        lanes = slice(h * HEAD_DIM, (h + 1) * HEAD_DIM)
        slope = 2.0 ** (-8.0 * (h + 1) / ATTN_HEADS)
        qh_t = q_t[lanes, :]
        gate = jnp.dot(kmean_scr[:, lanes], qh_t, precision=lax.Precision.HIGHEST,
                       preferred_element_type=F32)
        sel_scr[h] = jnp.where(_top_rank_select(gate, cur, MOBA_TOPK, 0), 1.0, 0.0)
        extra = jnp.where(row == 0, slope,
                          jnp.where(row == 1, slope * MOBA_BLOCK,
                                    jnp.where(row == 2, -slope * MOBA_BLOCK * cur_f, 0.0)))
        qa_scr[h] = jnp.concatenate([qh_t * (HEAD_DIM ** -0.5), extra], axis=0).astype(BF16)

    def block_scores(h, j):
        return jnp.dot(ka_scr[h, j], qa_scr[h], preferred_element_type=F32)

    def attend_block(j, keep, first):
        for h in range(ATTN_HEADS):
            s = jnp.where(keep(h), block_scores(h, j), NEG_INF)
            s_scr[h] = s
            m_blk = jnp.max(s, axis=0, keepdims=True)
            if first:
                m_scr[h:h + 1, :] = m_blk
            else:
                m_old = m_scr[h:h + 1, :]
                m_new = jnp.maximum(m_old, m_blk)
                alpha_scr[h:h + 1, :] = jnp.exp(m_old - m_new)
                m_scr[h:h + 1, :] = m_new
        for h in range(ATTN_HEADS):
            p = jnp.exp(s_scr[h] - m_scr[h:h + 1, :]).astype(BF16)
            pv = jnp.dot(vt_scr[h, j], p, preferred_element_type=F32)
            acc_scr[h] = pv if first else alpha_scr[h:h + 1, :] * acc_scr[h] + pv

    key_minus_q = (lax.broadcasted_iota(jnp.int32, (MOBA_BLOCK, QUERY_BLOCK), 0)
                   - lax.broadcasted_iota(jnp.int32, (MOBA_BLOCK, QUERY_BLOCK), 1))
    causal = key_minus_q <= qi * QUERY_BLOCK - cur * MOBA_BLOCK
    attend_block(cur, lambda h: causal, True)

    def past_block(j, carry):
        attend_block(j, lambda h: sel_scr[h, pl.ds(j, 1), :] > 0.0, False)
        return carry

    lax.fori_loop(0, cur, past_block, 0)

    outs = []
    for h in range(ATTN_HEADS):
        acc = acc_scr[h]
        outs.append(acc[:HEAD_DIM] / acc[HEAD_DIM:HEAD_DIM + 1])
    o_ref[...] = jnp.concatenate(outs, axis=0).T


def _attn_prompt(q, k_t, v_t, batch, seq):
    nq = seq // QUERY_BLOCK
    n_blocks = seq // MOBA_BLOCK
    return pl.pallas_call(
        functools.partial(_attn_prompt_kernel, n_blocks=n_blocks),
        grid=(batch, nq),
        in_specs=[pl.BlockSpec((QUERY_BLOCK, ATTN_WIDTH), lambda b, i: (b * nq + i, 0)),
                  pl.BlockSpec((1, ATTN_WIDTH, seq), lambda b, i: (b, 0, 0)),
                  pl.BlockSpec((1, ATTN_WIDTH, seq), lambda b, i: (b, 0, 0))],
        out_specs=pl.BlockSpec((QUERY_BLOCK, ATTN_WIDTH), lambda b, i: (b * nq + i, 0)),
        out_shape=jax.ShapeDtypeStruct(q.shape, F32),
        scratch_shapes=[pltpu.VMEM((ATTN_HEADS, n_blocks, MOBA_BLOCK, 2 * HEAD_DIM), BF16),
                        pltpu.VMEM((ATTN_HEADS, n_blocks, V_ROWS, MOBA_BLOCK), BF16),
                        pltpu.VMEM((n_blocks, ATTN_WIDTH), F32),
                        pltpu.VMEM((ATTN_HEADS, 2 * HEAD_DIM, QUERY_BLOCK), BF16),
                        pltpu.VMEM((ATTN_HEADS, n_blocks, QUERY_BLOCK), F32),
                        pltpu.VMEM((ATTN_HEADS, QUERY_BLOCK), F32),
                        pltpu.VMEM((ATTN_HEADS, QUERY_BLOCK), F32),
                        pltpu.VMEM((ATTN_HEADS, V_ROWS, QUERY_BLOCK), F32),
                        pltpu.VMEM((ATTN_HEADS, MOBA_BLOCK, QUERY_BLOCK), F32)],
        compiler_params=pltpu.CompilerParams(
            dimension_semantics=("arbitrary", "arbitrary"), vmem_limit_bytes=VMEM_LIMIT),
    )(q, k_t, v_t)


def _attn_sample_kernel(pt_ref, q_ref, kn_ref, vn_ref, *refs, n_pages, page, t_new):
    del pt_ref
    kp = refs[:n_pages]
    vp = refs[n_pages:2 * n_pages]
    o_ref = refs[2 * n_pages]
    n_cols = t_new * ATTN_HEADS
    pages_per_block = MOBA_BLOCK // page
    n_past = n_pages // pages_per_block
    past_len = n_pages * page

    q = q_ref[0]
    lane_h = lax.broadcasted_iota(jnp.int32, (n_cols, ATTN_WIDTH), 1) // HEAD_DIM
    row = lax.broadcasted_iota(jnp.int32, (n_cols, 1), 0)
    row_h = row % ATTN_HEADS
    row_t = row // ATTN_HEADS
    own_head = lane_h == row_h
    qrep = jnp.concatenate(
        [jnp.broadcast_to(q[t:t + 1, :], (ATTN_HEADS, ATTN_WIDTH)) for t in range(t_new)], axis=0)
    qbd = jnp.where(own_head, qrep, 0.0)
    qbd_s = (qbd * (HEAD_DIM ** -0.5)).astype(BF16)
    slope = jnp.exp2(-8.0 * (row_h + 1).astype(F32) / ATTN_HEADS)

    def block_pages(refs_, n):
        return [refs_[i][0].reshape(ATTN_WIDTH, page)
                for i in range(n * pages_per_block, (n + 1) * pages_per_block)]

    kmean = jnp.concatenate(
        [jnp.sum(sum(block_pages(kp, n)), axis=1, keepdims=True) for n in range(n_past)],
        axis=1) * (1.0 / MOBA_BLOCK)
    gate = jnp.dot(qbd, kmean, precision=lax.Precision.HIGHEST, preferred_element_type=F32)
    sel = _top_rank_select(gate, n_past, MOBA_TOPK, 1)

    s_own = lax.dot_general(qbd_s, kn_ref[0].astype(BF16), _NT, preferred_element_type=F32)
    dist = row_t - lax.broadcasted_iota(jnp.int32, (1, t_new), 1)
    s_own = jnp.where(dist >= 0, s_own - slope * dist.astype(F32), NEG_INF)
    m = jnp.max(s_own, axis=-1, keepdims=True)

    key_off = lax.broadcasted_iota(jnp.int32, (1, MOBA_BLOCK), 1)
    s_past = []
    for n in range(n_past):
        kb_t = jnp.concatenate(block_pages(kp, n), axis=1).astype(BF16)
        dist = (past_len + row_t) - (n * MOBA_BLOCK + key_off)
        s = jnp.dot(qbd_s, kb_t, preferred_element_type=F32) - slope * dist.astype(F32)
        s = jnp.where(sel[:, n:n + 1], s, NEG_INF)
        m = jnp.maximum(m, jnp.max(s, axis=-1, keepdims=True))
        s_past.append(s)

    p = jnp.exp(s_own - m)
    l = jnp.sum(p, axis=-1, keepdims=True)
    vn = vn_ref[0]
    acc = sum(p[:, t:t + 1] * vn[t:t + 1, :] for t in range(t_new))
    for n in range(n_past):
        vb_t = jnp.concatenate(block_pages(vp, n), axis=1).astype(BF16)
        p = jnp.exp(s_past[n] - m)
        l = l + jnp.sum(p, axis=-1, keepdims=True)
        acc = acc + lax.dot_general(p.astype(BF16), vb_t, _NT, preferred_element_type=F32)

    out = jnp.where(own_head, acc / l, 0.0)
    o_ref[0] = jnp.sum(out.reshape(t_new, ATTN_HEADS, ATTN_WIDTH), axis=1)


def _attn_sample(q, kn, vn, cache_kt, cache_vt, page_table):
    n_seq, t_new, _ = q.shape
    n_pages = page_table.shape[1]
    page = cache_kt.shape[-1]
    seq_spec = pl.BlockSpec((1, t_new, ATTN_WIDTH), lambda i, pt: (i, 0, 0))

    def page_spec(p):
        return pl.BlockSpec((1, ATTN_HEADS, HEAD_DIM, page),
                            lambda i, pt: (pt[i * n_pages + p], 0, 0, 0))

    page_specs = [page_spec(p) for p in range(n_pages)]
    return pl.pallas_call(
        functools.partial(_attn_sample_kernel, n_pages=n_pages, page=page, t_new=t_new),
        grid_spec=pltpu.PrefetchScalarGridSpec(
            num_scalar_prefetch=1,
            grid=(n_seq,),
            in_specs=[seq_spec] * 3 + page_specs * 2,
            out_specs=seq_spec),
        out_shape=jax.ShapeDtypeStruct(q.shape, F32),
        compiler_params=pltpu.CompilerParams(
            dimension_semantics=("arbitrary",), vmem_limit_bytes=VMEM_LIMIT),
    )(page_table.reshape(-1), q, kn, vn, *([cache_kt] * n_pages), *([cache_vt] * n_pages))


def _outproj_kernel(att_ref, gmn_ref, x_ref, wo_ref, oga_ref, nfg_ref, wr_ref, br_ref,
                    x1_ref, h2_ref, comb_ref, bucket_ref, wpair_ref):
    attn = _rms(att_ref[...], oga_ref[...]).astype(BF16)
    mix = (jnp.dot(attn, wo_ref[:ATTN_WIDTH, :], preferred_element_type=F32)
           + jnp.dot(gmn_ref[...].astype(BF16), wo_ref[ATTN_WIDTH:, :], preferred_element_type=F32))
    x1 = x_ref[...] + mix
    x1_ref[...] = x1
    h2 = _rms(x1, nfg_ref[...])
    h2_hi = h2.astype(BF16)
    h2_ref[...] = h2_hi

    h2_lo = (h2 - h2_hi.astype(F32)).astype(BF16)
    hi_dot = jnp.dot(h2_hi, wr_ref[...], preferred_element_type=F32)
    lo_dot = jnp.dot(h2_lo, wr_ref[:, :LANES], preferred_element_type=F32)
    logits = hi_dot[:, :LANES] + hi_dot[:, LANES:] + lo_dot + br_ref[...]
    lt = logits.T
    tm = lt.shape[1]
    row4 = lax.broadcasted_iota(jnp.int32, (MOE_GROUPS, tm), 0)

    def first_argmax(v):
        vmax = jnp.max(v, axis=0, keepdims=True)
        idx = jnp.min(jnp.where(v == vmax, row4, MOE_GROUPS), axis=0, keepdims=True)
        return vmax, idx

    glog = lt[:MOE_GROUPS]
    ge = jnp.exp(glog - jnp.max(glog, axis=0, keepdims=True))
    gprob = ge / jnp.sum(ge, axis=0, keepdims=True)
    p_g, g_idx = first_argmax(gprob)
    elog = lt[MOE_GROUPS:MOE_GROUPS + EXPERTS_PER_GROUP]
    for g in range(1, MOE_GROUPS):
        lo = MOE_GROUPS + g * EXPERTS_PER_GROUP
        elog = jnp.where(g_idx == g, lt[lo:lo + EXPERTS_PER_GROUP], elog)
    l1, i1 = first_argmax(elog)
    l2, i2 = first_argmax(jnp.where(row4 == i1, NEG_INF, elog))
    e2 = jnp.exp(l2 - l1)
    denom = 1.0 + e2
    w1 = (1.0 / denom) * p_g
    w2 = (e2 / denom) * p_g
    lane_row = lax.broadcasted_iota(jnp.int32, (LANES, tm), 0)
    base = g_idx * EXPERTS_PER_GROUP
    comb_t = jnp.where(lane_row == base + i1, w1, 0.0) + jnp.where(lane_row == base + i2, w2, 0.0)
    comb_ref[...] = comb_t.T

    e_lo = jnp.minimum(i1, i2)
    e_hi = jnp.maximum(i1, i2)
    pair = jnp.where(e_lo == 0, e_hi - 1, jnp.where(e_lo == 1, e_hi + 1, PAIRS_PER_GROUP - 1))
    row8 = lax.broadcasted_iota(jnp.int32, (8, tm), 0)
    bucket_ref[...] = jnp.where(row8 == 0, g_idx * PAIRS_PER_GROUP + pair, 0)
    w_lo = jnp.where(i1 < i2, w1, w2)
    w_hi = jnp.where(i1 < i2, w2, w1)
    wpair_ref[...] = jnp.where(row8 == 0, w_lo, jnp.where(row8 == 1, w_hi, 0.0))


def _outproj(att, gmn, x, wo_bf, og_attn, nf_g, w_r, b_r, tm):
    rows = x.shape[0]
    row_spec = lambda width: pl.BlockSpec((tm, width), lambda i: (i, 0))
    full = lambda a: pl.BlockSpec(a.shape, lambda i: (0,) * a.ndim)
    return pl.pallas_call(
        _outproj_kernel,
        grid=(rows // tm,),
        in_specs=[row_spec(ATTN_WIDTH), row_spec(MLP_WIDTH), row_spec(D_MODEL), full(wo_bf),
                  full(og_attn), full(nf_g), full(w_r), full(b_r)],
        out_specs=[row_spec(D_MODEL), row_spec(D_MODEL), row_spec(LANES),
                   pl.BlockSpec((8, tm), lambda i: (0, i)), pl.BlockSpec((8, tm), lambda i: (0, i))],
        out_shape=[jax.ShapeDtypeStruct((rows, D_MODEL), F32),
                   jax.ShapeDtypeStruct((rows, D_MODEL), BF16),
                   jax.ShapeDtypeStruct((rows, LANES), F32),
                   jax.ShapeDtypeStruct((8, rows), jnp.int32),
                   jax.ShapeDtypeStruct((8, rows), F32)],
        compiler_params=pltpu.CompilerParams(
            dimension_semantics=("arbitrary",), vmem_limit_bytes=VMEM_LIMIT),
    )(att, gmn, x, wo_bf, og_attn, nf_g, w_r, b_r)


def _moe_kernel(h_ref, comb_ref, wg_ref, wu_ref, wd_ref, x1_ref, fg_ref, y_ref, acc_ref):
    e = pl.program_id(1)

    @pl.when(e == 0)
    def _():
        acc_ref[...] = jnp.zeros_like(acc_ref)

    h = h_ref[...]
    hg = jnp.dot(h, wg_ref[0], preferred_element_type=F32)
    hu = jnp.dot(h, wu_ref[0], preferred_element_type=F32)
    comb = comb_ref[...]
    lane = lax.broadcasted_iota(jnp.int32, comb.shape, 1)
    c = jnp.sum(jnp.where(lane == e, comb, 0.0), axis=-1, keepdims=True)
    act = hg * (1.0 / (1.0 + jnp.exp(-hg))) * hu * c
    acc_ref[...] += jnp.dot(act.astype(BF16), wd_ref[0], preferred_element_type=F32)

    @pl.when(e == N_EXPERTS - 1)
    def _():
        y_ref[...] = _rms(x1_ref[...] + acc_ref[...], fg_ref[...])


def _moe(h2, comb, wg_bf, wu_bf, wd_bf, x1, final_g, tm):
    rows = h2.shape[0]
    row_spec = lambda width: pl.BlockSpec((tm, width), lambda i, e: (i, 0))
    return pl.pallas_call(
        _moe_kernel,
        grid=(rows // tm, N_EXPERTS),
        in_specs=[row_spec(D_MODEL), row_spec(LANES),
                  pl.BlockSpec((1, D_MODEL, D_EXPERT), lambda i, e: (e, 0, 0)),
                  pl.BlockSpec((1, D_MODEL, D_EXPERT), lambda i, e: (e, 0, 0)),
                  pl.BlockSpec((1, D_EXPERT, D_MODEL), lambda i, e: (e, 0, 0)),
                  row_spec(D_MODEL),
                  pl.BlockSpec(final_g.shape, lambda i, e: (0, 0))],
        out_specs=row_spec(D_MODEL),
        out_shape=jax.ShapeDtypeStruct((rows, D_MODEL), F32),
        scratch_shapes=[pltpu.VMEM((tm, D_MODEL), F32)],
        compiler_params=pltpu.CompilerParams(
            dimension_semantics=("arbitrary", "arbitrary"), vmem_limit_bytes=VMEM_LIMIT),
    )(h2, comb, wg_bf, wu_bf, wd_bf, x1, final_g)


SC_CORES = 2
SC_SUBCORES = 16
SC_WINDOW = 32


def _sc_gather_rows(x, idx):
    n = idx.shape[0]
    width = x.shape[1]
    assert n % SC_WINDOW == 0
    mesh = plsc.VectorSubcoreMesh(core_axis_name="core", subcore_axis_name="subcore",
                                  num_cores=SC_CORES, num_subcores=SC_SUBCORES)

    @pl.kernel(out_type=jax.ShapeDtypeStruct((n, width), x.dtype), mesh=mesh)
    def gather_kernel(x_hbm, idx_hbm, out_hbm):
        def body(idx_vmem, out_vmem):
            pltpu.sync_copy(x_hbm.at[idx_vmem.at[0, pl.ds(0, SC_WINDOW)]], out_vmem)

        pltpu.emit_pipeline(
            body,
            grid=(n // SC_WINDOW,),
            in_specs=[pl.BlockSpec((1, LANES), lambda i: (i, 0))],
            out_specs=[pl.BlockSpec((SC_WINDOW, width), lambda i: (i, 0))],
            core_axis_name=("core", "subcore"),
            dimension_semantics=(pltpu.PARALLEL,),
        )(idx_hbm, out_hbm)

    idx_rows = jnp.pad(idx.reshape(n // SC_WINDOW, SC_WINDOW), ((0, 0), (0, LANES - SC_WINDOW)))
    return gather_kernel(x, idx_rows)


def _moe_plan(bucket, wpair, tm):
    rows = bucket.shape[0]
    n_tiles_max = rows // tm + N_BUCKETS
    n_slots = n_tiles_max * tm
    onehot = (bucket[:, None] == jnp.arange(N_BUCKETS, dtype=jnp.int32)[None, :]).astype(jnp.int32)
    running = jnp.cumsum(onehot, axis=0)
    rank = jnp.sum(onehot * running, axis=1) - 1
    counts = running[-1]
    tiles_b = (counts + tm - 1) // tm
    tile_end = jnp.cumsum(tiles_b)
    tile_start = tile_end - tiles_b
    n_tiles = tile_end[-1]
    token_slot = (tile_start[bucket] * tm + rank).astype(jnp.int32)
    slot_row = jnp.zeros((n_slots,), jnp.int32).at[token_slot].set(
        jnp.arange(rows, dtype=jnp.int32), unique_indices=True)
    w_slots = jnp.zeros((n_slots, 2), F32).at[token_slot].set(wpair.T, unique_indices=True)
    t = jnp.arange(n_tiles_max, dtype=jnp.int32)
    tb = jnp.sum((jnp.minimum(t, n_tiles - 1)[:, None] >= tile_end[None, :]).astype(jnp.int32), axis=1)
    n_valid = jnp.where(t < n_tiles, jnp.clip(counts[tb] - (t - tile_start[tb]) * tm, 0, tm), 0)
    group = tb // PAIRS_PER_GROUP
    pair = tb % PAIRS_PER_GROUP
    e_lo = group * EXPERTS_PER_GROUP + jnp.asarray(PAIR_LO, jnp.int32)[pair]
    e_hi = group * EXPERTS_PER_GROUP + jnp.asarray(PAIR_HI, jnp.int32)[pair]
    return slot_row, token_slot, e_lo, e_hi, n_valid.astype(jnp.int32), w_slots


def _moe_grouped_kernel(elo_ref, ehi_ref, nv_ref, x1_ref, ws_ref, wg_lo, wu_lo, wd_lo,
                        wg_hi, wu_hi, wd_hi, nfg_ref, fg_ref, y_ref):
    del elo_ref, ehi_ref
    t = pl.program_id(0)

    @pl.when(nv_ref[t] > 0)
    def _():
        x1 = x1_ref[...]
        h = _rms(x1, nfg_ref[...]).astype(BF16)
        ws = ws_ref[...]
        moe = jnp.zeros(x1.shape, F32)
        for col, (wg, wu, wd) in enumerate(((wg_lo, wu_lo, wd_lo), (wg_hi, wu_hi, wd_hi))):
            hg = jnp.dot(h, wg[0], preferred_element_type=F32)
            hu = jnp.dot(h, wu[0], preferred_element_type=F32)
            act = hg * (1.0 / (1.0 + jnp.exp(-hg))) * hu * ws[:, col:col + 1]
            moe = moe + jnp.dot(act.astype(BF16), wd[0], preferred_element_type=F32)
        y_ref[...] = _rms(x1 + moe, fg_ref[...])

    @pl.when(nv_ref[t] == 0)
    def _():
        y_ref[...] = jnp.zeros_like(y_ref)


def _moe_grouped(x1, bucket, wpair, wg_bf, wu_bf, wd_bf, nf_g, final_g, tm):
    slot_row, token_slot, e_lo, e_hi, n_valid, w_slots = _moe_plan(bucket, wpair, tm)
    n_steps = n_valid.shape[0]
    x1_slots = _sc_gather_rows(x1, slot_row)
    lo_spec = lambda shape: pl.BlockSpec(shape, lambda t, elo, ehi, nv: (elo[t], 0, 0))
    hi_spec = lambda shape: pl.BlockSpec(shape, lambda t, elo, ehi, nv: (ehi[t], 0, 0))
    up_shape, down_shape = (1, D_MODEL, D_EXPERT), (1, D_EXPERT, D_MODEL)
    const = lambda a: pl.BlockSpec(a.shape, lambda t, elo, ehi, nv: (0, 0))
    row_spec = lambda width: pl.BlockSpec((tm, width), lambda t, elo, ehi, nv: (t, 0))
    y_slots = pl.pallas_call(
        _moe_grouped_kernel,
        grid_spec=pltpu.PrefetchScalarGridSpec(
            num_scalar_prefetch=3,
            grid=(n_steps,),
            in_specs=[row_spec(D_MODEL), row_spec(2),
                      lo_spec(up_shape), lo_spec(up_shape), lo_spec(down_shape),
                      hi_spec(up_shape), hi_spec(up_shape), hi_spec(down_shape),
                      const(nf_g), const(final_g)],
            out_specs=row_spec(D_MODEL)),
        out_shape=jax.ShapeDtypeStruct(x1_slots.shape, F32),
        compiler_params=pltpu.CompilerParams(
            dimension_semantics=("arbitrary",), vmem_limit_bytes=VMEM_LIMIT),
    )(e_lo, e_hi, n_valid, x1_slots, w_slots, wg_bf, wu_bf, wd_bf, wg_bf, wu_bf, wd_bf,
      nf_g, final_g)
    return _sc_gather_rows(y_slots, token_slot)


def _spatial_operands(w_s, b_s, t_chunk):
    reps = CHUNK // t_chunk
    idx = jnp.arange(CHUNK)
    same = (idx[:, None] // t_chunk) == (idx[None, :] // t_chunk)
    causal = (idx[None, :] % t_chunk) <= (idx[:, None] % t_chunk)
    wm = jnp.tile(w_s[:, :t_chunk, :t_chunk], (1, reps, reps)) * (same & causal)
    wm_cat = wm.transpose(1, 0, 2).reshape(CHUNK, MLP_GROUPS * CHUNK).astype(BF16)
    bias = jnp.repeat(jnp.tile(b_s[:, :t_chunk], (1, reps)).T, MLP_CH, axis=1)
    return wm_cat, bias


def kernel(x_prompt, x_sample, cache_k, cache_v, page_table, norm_attn_g, w_in, sgu_g, w_spatial,
           b_spatial, out_g_attn, out_g_mlp, w_out, norm_ffn_g, w_group, b_group, w_router, b_router,
           w_gate, w_up, w_down, final_g):
    depth = w_in.shape[0]
    assert depth == 1, "single decoder layer"
    batch, seq, _ = x_prompt.shape
    n_seq, t_new, _ = x_sample.shape
    assert seq % MOBA_BLOCK == 0 and seq % CHUNK == 0 and CHUNK % t_new == 0
    assert MOBA_BLOCK % cache_k.shape[2] == 0

    row2 = lambda a: a.reshape(1, -1)
    w_in_bf = w_in[0].astype(BF16)
    w_out_bf = w_out[0].astype(BF16)
    wg_bf, wu_bf, wd_bf = w_gate[0].astype(BF16), w_up[0].astype(BF16), w_down[0].astype(BF16)
    n_logits = MOE_GROUPS + N_EXPERTS
    w_r = jnp.concatenate(
        [w_group[0], w_router[0].transpose(1, 0, 2).reshape(D_MODEL, N_EXPERTS),
         jnp.zeros((D_MODEL, LANES - n_logits), F32)], axis=1)
    b_r = jnp.concatenate(
        [b_group[0], b_router[0].reshape(-1), jnp.zeros((LANES - n_logits,), F32)]).reshape(1, LANES)
    w_r_hi = w_r.astype(BF16)
    w_r_hl = jnp.concatenate([w_r_hi, (w_r - w_r_hi.astype(F32)).astype(BF16)], axis=1)
    wkv_t_bf = w_in_bf[:, ATTN_WIDTH:3 * ATTN_WIDTH].T
    ck_t = jnp.transpose(cache_k[0], (0, 2, 3, 1))
    cv_t = jnp.transpose(cache_v[0], (0, 2, 3, 1))

    def layer(x, t_chunk, attend, tm_proj, tm_moe, seq_transposed):
        wm_cat, bias_full = _spatial_operands(w_spatial[0], b_spatial[0], t_chunk)
        q, k, v, gmn, vgn = _inproj(x, row2(norm_attn_g[0]), w_in_bf, wkv_t_bf, row2(sgu_g[0]),
                                    wm_cat, bias_full, row2(out_g_mlp[0]), tm_proj, seq_transposed)
        att = attend(q, k, v)
        x1, h2, comb, bucket, wpair = _outproj(att, gmn, x, w_out_bf, row2(out_g_attn[0]),
                                               row2(norm_ffn_g[0]), w_r_hl, b_r, tm_proj)
        if seq_transposed is not None:
            y = _moe_grouped(x1, bucket[0], wpair[:2], wg_bf, wu_bf, wd_bf, row2(norm_ffn_g[0]),
                             row2(final_g), tm_moe)
        else:
            y = _moe(h2, comb, wg_bf, wu_bf, wd_bf, x1, row2(final_g), tm_moe)
        return y, k, v, vgn

    yp, kp_t, vp_t, _ = layer(
        x_prompt.reshape(batch * seq, D_MODEL), CHUNK,
        lambda q, k_t, v_t: _attn_prompt(q, k_t, v_t, batch, seq), 512, 256, seq)

    def attend_sample(q, k, v):
        shape = (n_seq, t_new, ATTN_WIDTH)
        return _attn_sample(q.reshape(shape), k.reshape(shape), v.reshape(shape), ck_t, cv_t,
                            page_table).reshape(n_seq * t_new, ATTN_WIDTH)

    ys, ks, vs, gvs = layer(x_sample.reshape(n_seq * t_new, D_MODEL), t_new, attend_sample, 128, 512,
                            None)

    heads = (ATTN_HEADS, HEAD_DIM)
    rows_last = lambda a_t: a_t.reshape(batch, *heads, seq).transpose(0, 3, 1, 2)[None]
    return (yp.reshape(batch, seq, D_MODEL),
            ys.reshape(n_seq, t_new, D_MODEL),
            rows_last(kp_t),
            rows_last(vp_t),
            ks.reshape(depth, n_seq, t_new, *heads),
            vs.reshape(depth, n_seq, t_new, *heads),
            gvs.reshape(depth, n_seq, t_new, MLP_WIDTH))
```

```python
import functools
from typing import NamedTuple

import jax
import jax.numpy as jnp
from jax import lax
from jax.experimental import pallas as pl
from jax.experimental.pallas import tpu as pltpu
from jax.experimental.pallas import tpu_sc as plsc

D_MODEL = 1024
ATTN_HEADS = 8
HEAD_DIM = 64
ATTN_WIDTH = ATTN_HEADS * HEAD_DIM
MOBA_BLOCK = 256
MOBA_TOPK = 3
QUERY_BLOCK = 128
MLP_GROUPS = 8
MLP_CH = 64
MLP_WIDTH = MLP_GROUPS * MLP_CH
CHUNK = 128
IN_WIDTH = 3 * ATTN_WIDTH + 2 * MLP_WIDTH
MOE_GROUPS = 4
EXPERTS_PER_GROUP = 4
N_EXPERTS = MOE_GROUPS * EXPERTS_PER_GROUP
D_EXPERT = D_MODEL // 2
EPS = 1e-6
PAIR_LO = (0, 0, 0, 1, 1, 2)
PAIR_HI = (1, 2, 3, 2, 3, 3)
PAIRS_PER_GROUP = len(PAIR_LO)
N_BUCKETS = MOE_GROUPS * PAIRS_PER_GROUP

LANES = 128
BF16_ROWS = 16
VMEM_LIMIT = 56 * 1024 * 1024

TM_PROJ_PROMPT = 512
TM_PROJ_SAMPLE = 128
TM_MOE_PROMPT = 256
TM_MOE_SAMPLE = 512

F32 = jnp.float32
BF16 = jnp.bfloat16
NEG_INF = float("-inf")
_NT = (((1,), (1,)), ((), ()))


def _rms(x, g):
    return x * lax.rsqrt(jnp.mean(x * x, axis=-1, keepdims=True) + EPS) * g


def _gelu(x):
    return 0.5 * x * (1.0 + jnp.tanh(0.7978845608028654 * (x + 0.044715 * (x * x * x))))


def _top_rank_select(gate, n_past, n_keep, axis):
    nb = gate.shape[axis]
    n_idx = lax.broadcasted_iota(jnp.int32, gate.shape, axis)
    rank = jnp.zeros(gate.shape, jnp.int32)
    for m in range(nb):
        gm = gate[:, m:m + 1] if axis == 1 else gate[m:m + 1, :]
        beats = jnp.where(gm > gate, 1, jnp.where(gm == gate, jnp.where(m < n_idx, 1, 0), 0))
        rank = rank + jnp.where(m < n_past, beats, 0)
    return jnp.where(n_idx < n_past, rank, n_keep) < n_keep


def _inproj_kernel(x_ref, g_ref, w_ref, wkv_t_ref, sgu_ref, wm_ref, bias_ref, og_ref,
                   q_ref, k_ref, v_ref, gmn_ref, vgn_ref, *, n_chunks, kv_transposed):
    h = _rms(x_ref[...], g_ref[...]).astype(BF16)

    def proj(lo, width):
        return jnp.dot(h, w_ref[:, lo:lo + width], preferred_element_type=F32)

    q_ref[...] = proj(0, ATTN_WIDTH)
    if kv_transposed:
        kv_t = lax.dot_general(wkv_t_ref[...], h, _NT, preferred_element_type=F32)
        k_ref[0] = kv_t[:ATTN_WIDTH]
        v_ref[0] = kv_t[ATTN_WIDTH:]
    else:
        k_ref[...] = proj(ATTN_WIDTH, ATTN_WIDTH)
        v_ref[...] = proj(2 * ATTN_WIDTH, ATTN_WIDTH)
    gu = _gelu(proj(3 * ATTN_WIDTH, MLP_WIDTH))
    vgn = _rms(_gelu(proj(3 * ATTN_WIDTH + MLP_WIDTH, MLP_WIDTH)), sgu_ref[...])
    vgn_ref[...] = vgn

    lane_grp = lax.broadcasted_iota(jnp.int32, (CHUNK, MLP_WIDTH), 1) // MLP_CH
    for c in range(n_chunks):
        rows = slice(c * CHUNK, (c + 1) * CHUNK)
        vc = vgn[rows].astype(BF16)
        vbd = jnp.concatenate(
            [jnp.where(lane_grp == g, vc, jnp.zeros_like(vc)) for g in range(MLP_GROUPS)], axis=0)
        mixed = jnp.dot(wm_ref[...], vbd, preferred_element_type=F32) + bias_ref[...]
        gmn_ref[rows, :] = _rms(gu[rows] * mixed, og_ref[...])


def _inproj(x, g, w_bf, wkv_t_bf, sgu_g, wm_cat, bias_full, og_mlp, tm, seq_transposed=None):
    rows = x.shape[0]
    row_spec = lambda width: pl.BlockSpec((tm, width), lambda i: (i, 0))
    full = lambda a: pl.BlockSpec(a.shape, lambda i: (0,) * a.ndim)
    out = jax.ShapeDtypeStruct((rows, ATTN_WIDTH), F32)
    kv_spec, kv_out = row_spec(ATTN_WIDTH), out
    if seq_transposed is not None:
        tiles = seq_transposed // tm
        kv_spec = pl.BlockSpec((1, ATTN_WIDTH, tm), lambda i: (i // tiles, 0, i % tiles))
        kv_out = jax.ShapeDtypeStruct((rows // seq_transposed, ATTN_WIDTH, seq_transposed), F32)
    return pl.pallas_call(
        functools.partial(_inproj_kernel, n_chunks=tm // CHUNK,
                          kv_transposed=seq_transposed is not None),
        grid=(rows // tm,),
        in_specs=[row_spec(D_MODEL), full(g), full(w_bf), full(wkv_t_bf), full(sgu_g), full(wm_cat),
                  full(bias_full), full(og_mlp)],
        out_specs=[row_spec(ATTN_WIDTH), kv_spec, kv_spec, row_spec(ATTN_WIDTH),
                   row_spec(ATTN_WIDTH)],
        out_shape=[out, kv_out, kv_out, out, out],
        compiler_params=pltpu.CompilerParams(
            dimension_semantics=("arbitrary",), vmem_limit_bytes=VMEM_LIMIT),
    )(x, g, w_bf, wkv_t_bf, sgu_g, wm_cat, bias_full, og_mlp)


V_ROWS = HEAD_DIM + BF16_ROWS


def _attn_prompt_kernel(q_ref, k_ref, v_ref, o_ref, ka_scr, vt_scr, kmean_scr, qa_scr, sel_scr,
                        m_scr, alpha_scr, acc_scr, s_scr, *, n_blocks):
    qi = pl.program_id(1)

    @pl.when(qi == 0)
    def _():
        key = lax.broadcasted_iota(jnp.int32, (MOBA_BLOCK, HEAD_DIM), 0)
        col = lax.broadcasted_iota(jnp.int32, (MOBA_BLOCK, HEAD_DIM), 1)
        ones = jnp.ones((BF16_ROWS, MOBA_BLOCK), BF16)
        for n in range(n_blocks):
            keys = slice(n * MOBA_BLOCK, (n + 1) * MOBA_BLOCK)
            kb = k_ref[0, :, keys].T
            kmean_scr[n:n + 1, :] = jnp.sum(kb, axis=0, keepdims=True) * (1.0 / MOBA_BLOCK)
            vt = v_ref[0, :, keys]
            extra = jnp.where(col == 0, key, jnp.where(col == 1, n, jnp.where(col == 2, 1, 0)))
            extra = extra.astype(F32).astype(BF16)
            for h in range(ATTN_HEADS):
                lanes = slice(h * HEAD_DIM, (h + 1) * HEAD_DIM)
                ka_scr[h, n] = jnp.concatenate([kb[:, lanes].astype(BF16), extra], axis=1)
                vt_scr[h, n, :HEAD_DIM, :] = vt[lanes, :].astype(BF16)
                vt_scr[h, n, HEAD_DIM:, :] = ones

    cur = (qi * QUERY_BLOCK) // MOBA_BLOCK
    q_t = q_ref[...].T
    row = lax.broadcasted_iota(jnp.int32, (HEAD_DIM, QUERY_BLOCK), 0)
    cur_f = cur.astype(F32)
    for h in range(ATTN_HEADS):
        lanes = slice(h * HEAD_DIM, (h + 1) * HEAD_DIM)
        slope = 2.0 ** (-8.0 * (h + 1) / ATTN_HEADS)
        qh_t = q_t[lanes, :]
        gate = jnp.dot(kmean_scr[:, lanes], qh_t, precision=lax.Precision.HIGHEST,
                       preferred_element_type=F32)
        sel_scr[h] = jnp.where(_top_rank_select(gate, cur, MOBA_TOPK, 0), 1.0, 0.0)
        extra = jnp.where(row == 0, slope,
                          jnp.where(row == 1, slope * MOBA_BLOCK,
                                    jnp.where(row == 2, -slope * MOBA_BLOCK * cur_f, 0.0)))
        qa_scr[h] = jnp.concatenate([qh_t * (HEAD_DIM ** -0.5), extra], axis=0).astype(BF16)

    def block_scores(h, j):
        return jnp.dot(ka_scr[h, j], qa_scr[h], preferred_element_type=F32)

    def attend_block(j, keep, first):
        for h in range(ATTN_HEADS):
            s = jnp.where(keep(h), block_scores(h, j), NEG_INF)
            s_scr[h] = s
            m_blk = jnp.max(s, axis=0, keepdims=True)
            if first:
                m_scr[h:h + 1, :] = m_blk
            else:
                m_old = m_scr[h:h + 1, :]
                m_new = jnp.maximum(m_old, m_blk)
                alpha_scr[h:h + 1, :] = jnp.exp(m_old - m_new)
                m_scr[h:h + 1, :] = m_new
        for h in range(ATTN_HEADS):
            p = jnp.exp(s_scr[h] - m_scr[h:h + 1, :]).astype(BF16)
            pv = jnp.dot(vt_scr[h, j], p, preferred_element_type=F32)
            acc_scr[h] = pv if first else alpha_scr[h:h + 1, :] * acc_scr[h] + pv

    key_minus_q = (lax.broadcasted_iota(jnp.int32, (MOBA_BLOCK, QUERY_BLOCK), 0)
                   - lax.broadcasted_iota(jnp.int32, (MOBA_BLOCK, QUERY_BLOCK), 1))
    causal = key_minus_q <= qi * QUERY_BLOCK - cur * MOBA_BLOCK
    attend_block(cur, lambda h: causal, True)

    def past_block(j, carry):
        attend_block(j, lambda h: sel_scr[h, pl.ds(j, 1), :] > 0.0, False)
        return carry

    lax.fori_loop(0, cur, past_block, 0)

    outs = []
    for h in range(ATTN_HEADS):
        acc = acc_scr[h]
        outs.append(acc[:HEAD_DIM] / acc[HEAD_DIM:HEAD_DIM + 1])
    o_ref[...] = jnp.concatenate(outs, axis=0).T


def _attn_prompt(q, k_t, v_t, batch, seq):
    nq = seq // QUERY_BLOCK
    n_blocks = seq // MOBA_BLOCK
    return pl.pallas_call(
        functools.partial(_attn_prompt_kernel, n_blocks=n_blocks),
        grid=(batch, nq),
        in_specs=[pl.BlockSpec((QUERY_BLOCK, ATTN_WIDTH), lambda b, i: (b * nq + i, 0)),
                  pl.BlockSpec((1, ATTN_WIDTH, seq), lambda b, i: (b, 0, 0)),
                  pl.BlockSpec((1, ATTN_WIDTH, seq), lambda b, i: (b, 0, 0))],
        out_specs=pl.BlockSpec((QUERY_BLOCK, ATTN_WIDTH), lambda b, i: (b * nq + i, 0)),
        out_shape=jax.ShapeDtypeStruct(q.shape, F32),
        scratch_shapes=[pltpu.VMEM((ATTN_HEADS, n_blocks, MOBA_BLOCK, 2 * HEAD_DIM), BF16),
                        pltpu.VMEM((ATTN_HEADS, n_blocks, V_ROWS, MOBA_BLOCK), BF16),
                        pltpu.VMEM((n_blocks, ATTN_WIDTH), F32),
                        pltpu.VMEM((ATTN_HEADS, 2 * HEAD_DIM, QUERY_BLOCK), BF16),
                        pltpu.VMEM((ATTN_HEADS, n_blocks, QUERY_BLOCK), F32),
                        pltpu.VMEM((ATTN_HEADS, QUERY_BLOCK), F32),
                        pltpu.VMEM((ATTN_HEADS, QUERY_BLOCK), F32),
                        pltpu.VMEM((ATTN_HEADS, V_ROWS, QUERY_BLOCK), F32),
                        pltpu.VMEM((ATTN_HEADS, MOBA_BLOCK, QUERY_BLOCK), F32)],
        compiler_params=pltpu.CompilerParams(
            dimension_semantics=("arbitrary", "arbitrary"), vmem_limit_bytes=VMEM_LIMIT),
    )(q, k_t, v_t)


def _attn_sample_kernel(pt_ref, q_ref, kn_ref, vn_ref, *refs, n_pages, page, t_new):
    del pt_ref
    kp = refs[:n_pages]
    vp = refs[n_pages:2 * n_pages]
    o_ref = refs[2 * n_pages]
    n_cols = t_new * ATTN_HEADS
    pages_per_block = MOBA_BLOCK // page
    n_past = n_pages // pages_per_block
    past_len = n_pages * page

    q = q_ref[0]
    lane_h = lax.broadcasted_iota(jnp.int32, (n_cols, ATTN_WIDTH), 1) // HEAD_DIM
    row = lax.broadcasted_iota(jnp.int32, (n_cols, 1), 0)
    row_h = row % ATTN_HEADS
    row_t = row // ATTN_HEADS
    own_head = lane_h == row_h
    qrep = jnp.concatenate(
        [jnp.broadcast_to(q[t:t + 1, :], (ATTN_HEADS, ATTN_WIDTH)) for t in range(t_new)], axis=0)
    qbd = jnp.where(own_head, qrep, 0.0)
    qbd_s = (qbd * (HEAD_DIM ** -0.5)).astype(BF16)
    slope = jnp.exp2(-8.0 * (row_h + 1).astype(F32) / ATTN_HEADS)

    def block_pages(refs_, n):
        return [refs_[i][0].reshape(ATTN_WIDTH, page)
                for i in range(n * pages_per_block, (n + 1) * pages_per_block)]

    kmean = jnp.concatenate(
        [jnp.sum(sum(block_pages(kp, n)), axis=1, keepdims=True) for n in range(n_past)],
        axis=1) * (1.0 / MOBA_BLOCK)
    gate = jnp.dot(qbd, kmean, precision=lax.Precision.HIGHEST, preferred_element_type=F32)
    sel = _top_rank_select(gate, n_past, MOBA_TOPK, 1)

    s_own = lax.dot_general(qbd_s, kn_ref[0].astype(BF16), _NT, preferred_element_type=F32)
    dist = row_t - lax.broadcasted_iota(jnp.int32, (1, t_new), 1)
    s_own = jnp.where(dist >= 0, s_own - slope * dist.astype(F32), NEG_INF)
    m = jnp.max(s_own, axis=-1, keepdims=True)

    key_off = lax.broadcasted_iota(jnp.int32, (1, MOBA_BLOCK), 1)
    s_past = []
    for n in range(n_past):
        kb_t = jnp.concatenate(block_pages(kp, n), axis=1).astype(BF16)
        dist = (past_len + row_t) - (n * MOBA_BLOCK + key_off)
        s = jnp.dot(qbd_s, kb_t, preferred_element_type=F32) - slope * dist.astype(F32)
        s = jnp.where(sel[:, n:n + 1], s, NEG_INF)
        m = jnp.maximum(m, jnp.max(s, axis=-1, keepdims=True))
        s_past.append(s)

    p = jnp.exp(s_own - m)
    l = jnp.sum(p, axis=-1, keepdims=True)
    vn = vn_ref[0]
    acc = sum(p[:, t:t + 1] * vn[t:t + 1, :] for t in range(t_new))
    for n in range(n_past):
        vb_t = jnp.concatenate(block_pages(vp, n), axis=1).astype(BF16)
        p = jnp.exp(s_past[n] - m)
        l = l + jnp.sum(p, axis=-1, keepdims=True)
        acc = acc + lax.dot_general(p.astype(BF16), vb_t, _NT, preferred_element_type=F32)

    out = jnp.where(own_head, acc / l, 0.0)
    o_ref[0] = jnp.sum(out.reshape(t_new, ATTN_HEADS, ATTN_WIDTH), axis=1)


def _attn_sample(q, kn, vn, cache_kt, cache_vt, page_table):
    n_seq, t_new, _ = q.shape
    n_pages = page_table.shape[1]
    page = cache_kt.shape[-1]
    seq_spec = pl.BlockSpec((1, t_new, ATTN_WIDTH), lambda i, pt: (i, 0, 0))

    def page_spec(p):
        return pl.BlockSpec((1, ATTN_HEADS, HEAD_DIM, page),
                            lambda i, pt: (pt[i * n_pages + p], 0, 0, 0))

    page_specs = [page_spec(p) for p in range(n_pages)]
    return pl.pallas_call(
        functools.partial(_attn_sample_kernel, n_pages=n_pages, page=page, t_new=t_new),
        grid_spec=pltpu.PrefetchScalarGridSpec(
            num_scalar_prefetch=1,
            grid=(n_seq,),
            in_specs=[seq_spec] * 3 + page_specs * 2,
            out_specs=seq_spec),
        out_shape=jax.ShapeDtypeStruct(q.shape, F32),
        compiler_params=pltpu.CompilerParams(
            dimension_semantics=("arbitrary",), vmem_limit_bytes=VMEM_LIMIT),
    )(page_table.reshape(-1), q, kn, vn, *([cache_kt] * n_pages), *([cache_vt] * n_pages))


def _outproj_kernel(att_ref, gmn_ref, x_ref, wo_ref, oga_ref, nfg_ref, wr_ref, br_ref,
                    x1_ref, h2_ref, comb_ref, bucket_ref, wpair_ref):
    attn = _rms(att_ref[...], oga_ref[...]).astype(BF16)
    mix = (jnp.dot(attn, wo_ref[:ATTN_WIDTH, :], preferred_element_type=F32)
           + jnp.dot(gmn_ref[...].astype(BF16), wo_ref[ATTN_WIDTH:, :], preferred_element_type=F32))
    x1 = x_ref[...] + mix
    x1_ref[...] = x1
    h2 = _rms(x1, nfg_ref[...])
    h2_hi = h2.astype(BF16)
    h2_ref[...] = h2_hi

    h2_lo = (h2 - h2_hi.astype(F32)).astype(BF16)
    hi_dot = jnp.dot(h2_hi, wr_ref[...], preferred_element_type=F32)
    lo_dot = jnp.dot(h2_lo, wr_ref[:, :LANES], preferred_element_type=F32)
    logits = hi_dot[:, :LANES] + hi_dot[:, LANES:] + lo_dot + br_ref[...]
    lt = logits.T
    tm = lt.shape[1]
    row4 = lax.broadcasted_iota(jnp.int32, (MOE_GROUPS, tm), 0)

    def first_argmax(v):
        vmax = jnp.max(v, axis=0, keepdims=True)
        idx = jnp.min(jnp.where(v == vmax, row4, MOE_GROUPS), axis=0, keepdims=True)
        return vmax, idx

    glog = lt[:MOE_GROUPS]
    ge = jnp.exp(glog - jnp.max(glog, axis=0, keepdims=True))
    gprob = ge / jnp.sum(ge, axis=0, keepdims=True)
    p_g, g_idx = first_argmax(gprob)
    elog = lt[MOE_GROUPS:MOE_GROUPS + EXPERTS_PER_GROUP]
    for g in range(1, MOE_GROUPS):
        lo = MOE_GROUPS + g * EXPERTS_PER_GROUP
        elog = jnp.where(g_idx == g, lt[lo:lo + EXPERTS_PER_GROUP], elog)
    l1, i1 = first_argmax(elog)
    l2, i2 = first_argmax(jnp.where(row4 == i1, NEG_INF, elog))
    e2 = jnp.exp(l2 - l1)
    denom = 1.0 + e2
    w1 = (1.0 / denom) * p_g
    w2 = (e2 / denom) * p_g
    lane_row = lax.broadcasted_iota(jnp.int32, (LANES, tm), 0)
    base = g_idx * EXPERTS_PER_GROUP
    comb_t = jnp.where(lane_row == base + i1, w1, 0.0) + jnp.where(lane_row == base + i2, w2, 0.0)
    comb_ref[...] = comb_t.T

    e_lo = jnp.minimum(i1, i2)
    e_hi = jnp.maximum(i1, i2)
    pair = jnp.where(e_lo == 0, e_hi - 1, jnp.where(e_lo == 1, e_hi + 1, PAIRS_PER_GROUP - 1))
    row8 = lax.broadcasted_iota(jnp.int32, (8, tm), 0)
    bucket_ref[...] = jnp.where(row8 == 0, g_idx * PAIRS_PER_GROUP + pair, 0)
    w_lo = jnp.where(i1 < i2, w1, w2)
    w_hi = jnp.where(i1 < i2, w2, w1)
    wpair_ref[...] = jnp.where(row8 == 0, w_lo, jnp.where(row8 == 1, w_hi, 0.0))


def _outproj(att, gmn, x, wo_bf, og_attn, nf_g, w_r, b_r, tm):
    rows = x.shape[0]
    row_spec = lambda width: pl.BlockSpec((tm, width), lambda i: (i, 0))
    full = lambda a: pl.BlockSpec(a.shape, lambda i: (0,) * a.ndim)
    return pl.pallas_call(
        _outproj_kernel,
        grid=(rows // tm,),
        in_specs=[row_spec(ATTN_WIDTH), row_spec(MLP_WIDTH), row_spec(D_MODEL), full(wo_bf),
                  full(og_attn), full(nf_g), full(w_r), full(b_r)],
        out_specs=[row_spec(D_MODEL), row_spec(D_MODEL), row_spec(LANES),
                   pl.BlockSpec((8, tm), lambda i: (0, i)), pl.BlockSpec((8, tm), lambda i: (0, i))],
        out_shape=[jax.ShapeDtypeStruct((rows, D_MODEL), F32),
                   jax.ShapeDtypeStruct((rows, D_MODEL), BF16),
                   jax.ShapeDtypeStruct((rows, LANES), F32),
                   jax.ShapeDtypeStruct((8, rows), jnp.int32),
                   jax.ShapeDtypeStruct((8, rows), F32)],
        compiler_params=pltpu.CompilerParams(
            dimension_semantics=("arbitrary",), vmem_limit_bytes=VMEM_LIMIT),
    )(att, gmn, x, wo_bf, og_attn, nf_g, w_r, b_r)


def _moe_kernel(h_ref, comb_ref, wg_ref, wu_ref, wd_ref, x1_ref, fg_ref, y_ref, acc_ref):
    e = pl.program_id(1)

    @pl.when(e == 0)
    def _():
        acc_ref[...] = jnp.zeros_like(acc_ref)

    h = h_ref[...]
    hg = jnp.dot(h, wg_ref[0], preferred_element_type=F32)
    hu = jnp.dot(h, wu_ref[0], preferred_element_type=F32)
    comb = comb_ref[...]
    lane = lax.broadcasted_iota(jnp.int32, comb.shape, 1)
    c = jnp.sum(jnp.where(lane == e, comb, 0.0), axis=-1, keepdims=True)
    act = hg * (1.0 / (1.0 + jnp.exp(-hg))) * hu * c
    acc_ref[...] += jnp.dot(act.astype(BF16), wd_ref[0], preferred_element_type=F32)

    @pl.when(e == N_EXPERTS - 1)
    def _():
        y_ref[...] = _rms(x1_ref[...] + acc_ref[...], fg_ref[...])


def _moe(h2, comb, wg_bf, wu_bf, wd_bf, x1, final_g, tm):
    rows = h2.shape[0]
    row_spec = lambda width: pl.BlockSpec((tm, width), lambda i, e: (i, 0))
    return pl.pallas_call(
        _moe_kernel,
        grid=(rows // tm, N_EXPERTS),
        in_specs=[row_spec(D_MODEL), row_spec(LANES),
                  pl.BlockSpec((1, D_MODEL, D_EXPERT), lambda i, e: (e, 0, 0)),
                  pl.BlockSpec((1, D_MODEL, D_EXPERT), lambda i, e: (e, 0, 0)),
                  pl.BlockSpec((1, D_EXPERT, D_MODEL), lambda i, e: (e, 0, 0)),
                  row_spec(D_MODEL),
                  pl.BlockSpec(final_g.shape, lambda i, e: (0, 0))],
        out_specs=row_spec(D_MODEL),
        out_shape=jax.ShapeDtypeStruct((rows, D_MODEL), F32),
        scratch_shapes=[pltpu.VMEM((tm, D_MODEL), F32)],
        compiler_params=pltpu.CompilerParams(
            dimension_semantics=("arbitrary", "arbitrary"), vmem_limit_bytes=VMEM_LIMIT),
    )(h2, comb, wg_bf, wu_bf, wd_bf, x1, final_g)


SC_CORES = 2
SC_SUBCORES = 16
SC_WINDOW = 32


def _sc_gather_rows(x, idx):
    n = idx.shape[0]
    width = x.shape[1]
    assert n % SC_WINDOW == 0
    mesh = plsc.VectorSubcoreMesh(core_axis_name="core", subcore_axis_name="subcore",
                                  num_cores=SC_CORES, num_subcores=SC_SUBCORES)

    @pl.kernel(out_type=jax.ShapeDtypeStruct((n, width), x.dtype), mesh=mesh)
    def gather_kernel(x_hbm, idx_hbm, out_hbm):
        def body(idx_vmem, out_vmem):
            pltpu.sync_copy(x_hbm.at[idx_vmem.at[0, pl.ds(0, SC_WINDOW)]], out_vmem)

        pltpu.emit_pipeline(
            body,
            grid=(n // SC_WINDOW,),
            in_specs=[pl.BlockSpec((1, LANES), lambda i: (i, 0))],
            out_specs=[pl.BlockSpec((SC_WINDOW, width), lambda i: (i, 0))],
            core_axis_name=("core", "subcore"),
            dimension_semantics=(pltpu.PARALLEL,),
        )(idx_hbm, out_hbm)

    idx_rows = jnp.pad(idx.reshape(n // SC_WINDOW, SC_WINDOW), ((0, 0), (0, LANES - SC_WINDOW)))
    return gather_kernel(x, idx_rows)


class MoePlan(NamedTuple):
    slot_row: jax.Array
    token_slot: jax.Array
    e_lo: jax.Array
    e_hi: jax.Array
    n_valid: jax.Array
    w_slots: jax.Array


def _moe_plan(bucket, wpair, tm):
    rows = bucket.shape[0]
    n_tiles_max = rows // tm + N_BUCKETS
    order = jnp.argsort(bucket, stable=True).astype(jnp.int32)
    position = jnp.argsort(order).astype(jnp.int32)
    counts = jnp.sum((bucket[:, None] == jnp.arange(N_BUCKETS)[None, :]).astype(jnp.int32), axis=0)
    starts = jnp.cumsum(counts) - counts
    tiles_b = (counts + tm - 1) // tm
    tile_end = jnp.cumsum(tiles_b)
    tile_start = tile_end - tiles_b
    n_tiles = tile_end[-1]
    token_slot = tile_start[bucket] * tm + (position - starts[bucket])
    t = jnp.arange(n_tiles_max, dtype=jnp.int32)
    tb = jnp.sum((jnp.minimum(t, n_tiles - 1)[:, None] >= tile_end[None, :]).astype(jnp.int32), axis=1)
    local = (t - tile_start[tb]) * tm
    n_valid = jnp.where(t < n_tiles, jnp.clip(counts[tb] - local, 0, tm), 0)
    slot = local[:, None] + jnp.arange(tm, dtype=jnp.int32)[None, :]
    src = starts[tb][:, None] + jnp.minimum(slot, counts[tb][:, None] - 1)
    slot_row = order[jnp.clip(src, 0, rows - 1)].reshape(-1)
    group = tb // PAIRS_PER_GROUP
    pair = tb % PAIRS_PER_GROUP
    e_lo = group * EXPERTS_PER_GROUP + jnp.asarray(PAIR_LO, jnp.int32)[pair]
    e_hi = group * EXPERTS_PER_GROUP + jnp.asarray(PAIR_HI, jnp.int32)[pair]
    return MoePlan(slot_row, token_slot.astype(jnp.int32), e_lo, e_hi, n_valid.astype(jnp.int32),
                   wpair[:, slot_row].T)


def _moe_grouped_kernel(elo_ref, ehi_ref, nv_ref, x1_ref, ws_ref, wg_lo, wu_lo, wd_lo,
                        wg_hi, wu_hi, wd_hi, nfg_ref, fg_ref, y_ref):
    del elo_ref, ehi_ref
    t = pl.program_id(0)

    @pl.when(nv_ref[t] > 0)
    def _():
        x1 = x1_ref[...]
        h = _rms(x1, nfg_ref[...]).astype(BF16)
        ws = ws_ref[...]
        moe = jnp.zeros(x1.shape, F32)
        for col, (wg, wu, wd) in enumerate(((wg_lo, wu_lo, wd_lo), (wg_hi, wu_hi, wd_hi))):
            hg = jnp.dot(h, wg[0], preferred_element_type=F32)
            hu = jnp.dot(h, wu[0], preferred_element_type=F32)
            act = hg * (1.0 / (1.0 + jnp.exp(-hg))) * hu * ws[:, col:col + 1]
            moe = moe + jnp.dot(act.astype(BF16), wd[0], preferred_element_type=F32)
        y_ref[...] = _rms(x1 + moe, fg_ref[...])

    @pl.when(nv_ref[t] == 0)
    def _():
        y_ref[...] = jnp.zeros_like(y_ref)


def _moe_grouped(x1_slots, plan, wg_bf, wu_bf, wd_bf, nf_g, final_g, tm):
    n_steps = plan.n_valid.shape[0]
    lo_spec = lambda shape: pl.BlockSpec(shape, lambda t, elo, ehi, nv: (elo[t], 0, 0))
    hi_spec = lambda shape: pl.BlockSpec(shape, lambda t, elo, ehi, nv: (ehi[t], 0, 0))
    up_shape, down_shape = (1, D_MODEL, D_EXPERT), (1, D_EXPERT, D_MODEL)
    const = lambda a: pl.BlockSpec(a.shape, lambda t, elo, ehi, nv: (0, 0))
    row_spec = lambda width: pl.BlockSpec((tm, width), lambda t, elo, ehi, nv: (t, 0))
    return pl.pallas_call(
        _moe_grouped_kernel,
        grid_spec=pltpu.PrefetchScalarGridSpec(
            num_scalar_prefetch=3,
            grid=(n_steps,),
            in_specs=[row_spec(D_MODEL), row_spec(2),
                      lo_spec(up_shape), lo_spec(up_shape), lo_spec(down_shape),
                      hi_spec(up_shape), hi_spec(up_shape), hi_spec(down_shape),
                      const(nf_g), const(final_g)],
            out_specs=row_spec(D_MODEL)),
        out_shape=jax.ShapeDtypeStruct(x1_slots.shape, F32),
        compiler_params=pltpu.CompilerParams(
            dimension_semantics=("arbitrary",), vmem_limit_bytes=VMEM_LIMIT),
    )(plan.e_lo, plan.e_hi, plan.n_valid, x1_slots, plan.w_slots, wg_bf, wu_bf, wd_bf,
      wg_bf, wu_bf, wd_bf, nf_g, final_g)


def _spatial_operands(w_s, b_s, t_chunk):
    reps = CHUNK // t_chunk
    idx = jnp.arange(CHUNK)
    same = (idx[:, None] // t_chunk) == (idx[None, :] // t_chunk)
    causal = (idx[None, :] % t_chunk) <= (idx[:, None] % t_chunk)
    wm = jnp.tile(w_s[:, :t_chunk, :t_chunk], (1, reps, reps)) * (same & causal)
    wm_cat = wm.transpose(1, 0, 2).reshape(CHUNK, MLP_GROUPS * CHUNK).astype(BF16)
    bias = jnp.repeat(jnp.tile(b_s[:, :t_chunk], (1, reps)).T, MLP_CH, axis=1)
    return wm_cat, bias


def kernel(x_prompt, x_sample, cache_k, cache_v, page_table, norm_attn_g, w_in, sgu_g, w_spatial,
           b_spatial, out_g_attn, out_g_mlp, w_out, norm_ffn_g, w_group, b_group, w_router, b_router,
           w_gate, w_up, w_down, final_g):
    depth = w_in.shape[0]
    assert depth == 1, "single decoder layer"
    batch, seq, _ = x_prompt.shape
    n_seq, t_new, _ = x_sample.shape
    assert seq % MOBA_BLOCK == 0 and seq % CHUNK == 0 and CHUNK % t_new == 0
    assert MOBA_BLOCK % cache_k.shape[2] == 0

    row2 = lambda a: a.reshape(1, -1)
    w_in_bf = w_in[0].astype(BF16)
    w_out_bf = w_out[0].astype(BF16)
    wg_bf, wu_bf, wd_bf = w_gate[0].astype(BF16), w_up[0].astype(BF16), w_down[0].astype(BF16)
    n_logits = MOE_GROUPS + N_EXPERTS
    w_r = jnp.concatenate(
        [w_group[0], w_router[0].transpose(1, 0, 2).reshape(D_MODEL, N_EXPERTS),
         jnp.zeros((D_MODEL, LANES - n_logits), F32)], axis=1)
    b_r = jnp.concatenate(
        [b_group[0], b_router[0].reshape(-1), jnp.zeros((LANES - n_logits,), F32)]).reshape(1, LANES)
    w_r_hi = w_r.astype(BF16)
    w_r_hl = jnp.concatenate([w_r_hi, (w_r - w_r_hi.astype(F32)).astype(BF16)], axis=1)
    wkv_t_bf = w_in_bf[:, ATTN_WIDTH:3 * ATTN_WIDTH].T
    ck_t = jnp.transpose(cache_k[0], (0, 2, 3, 1))
    cv_t = jnp.transpose(cache_v[0], (0, 2, 3, 1))

    def project_in(x, t_chunk, tm, seq_transposed):
        wm_cat, bias_full = _spatial_operands(w_spatial[0], b_spatial[0], t_chunk)
        return _inproj(x, row2(norm_attn_g[0]), w_in_bf, wkv_t_bf, row2(sgu_g[0]), wm_cat, bias_full,
                       row2(out_g_mlp[0]), tm, seq_transposed)

    def project_out(att, gmn, x, tm):
        return _outproj(att, gmn, x, w_out_bf, row2(out_g_attn[0]), row2(norm_ffn_g[0]), w_r_hl, b_r,
                        tm)

    xp = x_prompt.reshape(batch * seq, D_MODEL)
    qp, kp_t, vp_t, gmn_p, _ = project_in(xp, CHUNK, TM_PROJ_PROMPT, seq)
    att_p = _attn_prompt(qp, kp_t, vp_t, batch, seq)
    x1_p, _, _, bucket_p, wpair_p = project_out(att_p, gmn_p, xp, TM_PROJ_PROMPT)
    plan = _moe_plan(bucket_p[0], wpair_p[:2], TM_MOE_PROMPT)
    x1_slots = _sc_gather_rows(x1_p, plan.slot_row)

    xs = x_sample.reshape(n_seq * t_new, D_MODEL)
    qs, ks, vs, gmn_s, gvs = project_in(xs, t_new, TM_PROJ_SAMPLE, None)
    seq_shape = (n_seq, t_new, ATTN_WIDTH)
    qs3, ks3, vs3 = qs.reshape(seq_shape), ks.reshape(seq_shape), vs.reshape(seq_shape)
    half = n_seq // 2
    attend_half = lambda q_half, lo: _attn_sample(q_half, ks3[lo:lo + half], vs3[lo:lo + half],
                                                  ck_t, cv_t, page_table[lo:lo + half])
    att_s0 = attend_half(qs3[:half], 0)
    x1_slots, att_s0 = lax.optimization_barrier((x1_slots, att_s0))
    y_slots = _moe_grouped(x1_slots, plan, wg_bf, wu_bf, wd_bf, row2(norm_ffn_g[0]),
                           row2(final_g), TM_MOE_PROMPT)
    y_slots, q_half1 = lax.optimization_barrier((y_slots, qs3[half:]))
    yp = _sc_gather_rows(y_slots, plan.token_slot)
    att_s1 = attend_half(q_half1, half)
    att_s = jnp.concatenate([att_s0, att_s1], axis=0).reshape(n_seq * t_new, ATTN_WIDTH)
    x1_s, h2_s, comb_s, _, _ = project_out(att_s, gmn_s, xs, TM_PROJ_SAMPLE)
    ys = _moe(h2_s, comb_s, wg_bf, wu_bf, wd_bf, x1_s, row2(final_g), TM_MOE_SAMPLE)

    heads = (ATTN_HEADS, HEAD_DIM)
    rows_last = lambda a_t: a_t.reshape(batch, *heads, seq).transpose(0, 3, 1, 2)[None]
    return (yp.reshape(batch, seq, D_MODEL),
            ys.reshape(n_seq, t_new, D_MODEL),
            rows_last(kp_t),
            rows_last(vp_t),
            ks.reshape(depth, n_seq, t_new, *heads),
            vs.reshape(depth, n_seq, t_new, *heads),
            gvs.reshape(depth, n_seq, t_new, MLP_WIDTH))
```

```python
import functools
from typing import NamedTuple

import jax
import jax.numpy as jnp
from jax import lax
from jax.experimental import pallas as pl
from jax.experimental.pallas import tpu as pltpu
from jax.experimental.pallas import tpu_sc as plsc

D_MODEL = 1024
ATTN_HEADS = 8
HEAD_DIM = 64
ATTN_WIDTH = ATTN_HEADS * HEAD_DIM
MOBA_BLOCK = 256
MOBA_TOPK = 3
QUERY_BLOCK = 128
MLP_GROUPS = 8
MLP_CH = 64
MLP_WIDTH = MLP_GROUPS * MLP_CH
CHUNK = 128
IN_WIDTH = 3 * ATTN_WIDTH + 2 * MLP_WIDTH
MOE_GROUPS = 4
EXPERTS_PER_GROUP = 4
N_EXPERTS = MOE_GROUPS * EXPERTS_PER_GROUP
D_EXPERT = D_MODEL // 2
EPS = 1e-6
PAIR_LO = (0, 0, 0, 1, 1, 2)
PAIR_HI = (1, 2, 3, 2, 3, 3)
PAIRS_PER_GROUP = len(PAIR_LO)
N_BUCKETS = MOE_GROUPS * PAIRS_PER_GROUP

LANES = 128
BF16_ROWS = 16
VMEM_LIMIT = 56 * 1024 * 1024

TM_PROJ_PROMPT = 512
TM_PROJ_SAMPLE = 128
TM_MOE_PROMPT = 256
TM_MOE_SAMPLE = 512

F32 = jnp.float32
BF16 = jnp.bfloat16
NEG_INF = float("-inf")
_NT = (((1,), (1,)), ((), ()))


def _rms(x, g):
    return x * lax.rsqrt(jnp.mean(x * x, axis=-1, keepdims=True) + EPS) * g


def _gelu(x):
    return 0.5 * x * (1.0 + jnp.tanh(0.7978845608028654 * (x + 0.044715 * (x * x * x))))


def _top_rank_select(gate, n_past, n_keep, axis):
    nb = gate.shape[axis]
    n_idx = lax.broadcasted_iota(jnp.int32, gate.shape, axis)
    rank = jnp.zeros(gate.shape, jnp.int32)
    for m in range(nb):
        gm = gate[:, m:m + 1] if axis == 1 else gate[m:m + 1, :]
        beats = jnp.where(gm > gate, 1, jnp.where(gm == gate, jnp.where(m < n_idx, 1, 0), 0))
        rank = rank + jnp.where(m < n_past, beats, 0)
    return jnp.where(n_idx < n_past, rank, n_keep) < n_keep


def _inproj_kernel(x_ref, g_ref, w_ref, wkv_t_ref, sgu_ref, wm_ref, bias_ref, og_ref,
                   q_ref, k_ref, v_ref, gmn_ref, vgn_ref, *, n_chunks, kv_transposed):
    h = _rms(x_ref[...], g_ref[...]).astype(BF16)

    def proj(lo, width):
        return jnp.dot(h, w_ref[:, lo:lo + width], preferred_element_type=F32)

    q_ref[...] = proj(0, ATTN_WIDTH)
    if kv_transposed:
        kv_t = lax.dot_general(wkv_t_ref[...], h, _NT, preferred_element_type=F32)
        k_ref[0] = kv_t[:ATTN_WIDTH]
        v_ref[0] = kv_t[ATTN_WIDTH:]
    else:
        k_ref[...] = proj(ATTN_WIDTH, ATTN_WIDTH)
        v_ref[...] = proj(2 * ATTN_WIDTH, ATTN_WIDTH)
    gu = _gelu(proj(3 * ATTN_WIDTH, MLP_WIDTH))
    vgn = _rms(_gelu(proj(3 * ATTN_WIDTH + MLP_WIDTH, MLP_WIDTH)), sgu_ref[...])
    vgn_ref[...] = vgn

    lane_grp = lax.broadcasted_iota(jnp.int32, (CHUNK, MLP_WIDTH), 1) // MLP_CH
    for c in range(n_chunks):
        rows = slice(c * CHUNK, (c + 1) * CHUNK)
        vc = vgn[rows].astype(BF16)
        vbd = jnp.concatenate(
            [jnp.where(lane_grp == g, vc, jnp.zeros_like(vc)) for g in range(MLP_GROUPS)], axis=0)
        mixed = jnp.dot(wm_ref[...], vbd, preferred_element_type=F32) + bias_ref[...]
        gmn_ref[rows, :] = _rms(gu[rows] * mixed, og_ref[...])


def _inproj(x, g, w_bf, wkv_t_bf, sgu_g, wm_cat, bias_full, og_mlp, tm, seq_transposed=None):
    rows = x.shape[0]
    row_spec = lambda width: pl.BlockSpec((tm, width), lambda i: (i, 0))
    full = lambda a: pl.BlockSpec(a.shape, lambda i: (0,) * a.ndim)
    out = jax.ShapeDtypeStruct((rows, ATTN_WIDTH), F32)
    kv_spec, kv_out = row_spec(ATTN_WIDTH), out
    if seq_transposed is not None:
        tiles = seq_transposed // tm
        kv_spec = pl.BlockSpec((1, ATTN_WIDTH, tm), lambda i: (i // tiles, 0, i % tiles))
        kv_out = jax.ShapeDtypeStruct((rows // seq_transposed, ATTN_WIDTH, seq_transposed), F32)
    return pl.pallas_call(
        functools.partial(_inproj_kernel, n_chunks=tm // CHUNK,
                          kv_transposed=seq_transposed is not None),
        grid=(rows // tm,),
        in_specs=[row_spec(D_MODEL), full(g), full(w_bf), full(wkv_t_bf), full(sgu_g), full(wm_cat),
                  full(bias_full), full(og_mlp)],
        out_specs=[row_spec(ATTN_WIDTH), kv_spec, kv_spec, row_spec(ATTN_WIDTH),
                   row_spec(ATTN_WIDTH)],
        out_shape=[out, kv_out, kv_out, out, out],
        compiler_params=pltpu.CompilerParams(
            dimension_semantics=("arbitrary",), vmem_limit_bytes=VMEM_LIMIT),
    )(x, g, w_bf, wkv_t_bf, sgu_g, wm_cat, bias_full, og_mlp)


V_ROWS = HEAD_DIM + BF16_ROWS


def _attn_prompt_kernel(q_ref, k_ref, v_ref, o_ref, ka_scr, vt_scr, kmean_scr, qa_scr, sel_scr,
                        m_scr, alpha_scr, acc_scr, s_scr, *, n_blocks):
    qi = pl.program_id(1)

    @pl.when(qi == 0)
    def _():
        key = lax.broadcasted_iota(jnp.int32, (MOBA_BLOCK, HEAD_DIM), 0)
        col = lax.broadcasted_iota(jnp.int32, (MOBA_BLOCK, HEAD_DIM), 1)
        ones = jnp.ones((BF16_ROWS, MOBA_BLOCK), BF16)
        for n in range(n_blocks):
            keys = slice(n * MOBA_BLOCK, (n + 1) * MOBA_BLOCK)
            kb = k_ref[0, :, keys].T
            kmean_scr[n:n + 1, :] = jnp.sum(kb, axis=0, keepdims=True) * (1.0 / MOBA_BLOCK)
            vt = v_ref[0, :, keys]
            extra = jnp.where(col == 0, key, jnp.where(col == 1, n, jnp.where(col == 2, 1, 0)))
            extra = extra.astype(F32).astype(BF16)
            for h in range(ATTN_HEADS):
                lanes = slice(h * HEAD_DIM, (h + 1) * HEAD_DIM)
                ka_scr[h, n] = jnp.concatenate([kb[:, lanes].astype(BF16), extra], axis=1)
                vt_scr[h, n, :HEAD_DIM, :] = vt[lanes, :].astype(BF16)
                vt_scr[h, n, HEAD_DIM:, :] = ones

    cur = (qi * QUERY_BLOCK) // MOBA_BLOCK
    q_t = q_ref[...].T
    row = lax.broadcasted_iota(jnp.int32, (HEAD_DIM, QUERY_BLOCK), 0)
    cur_f = cur.astype(F32)
    for h in range(ATTN_HEADS):
        lanes = slice(h * HEAD_DIM, (h + 1) * HEAD_DIM)
        slope = 2.0 ** (-8.0 * (h + 1) / ATTN_HEADS)
        qh_t = q_t[lanes, :]
        gate = jnp.dot(kmean_scr[:, lanes], qh_t, precision=lax.Precision.HIGHEST,
                       preferred_element_type=F32)
        sel_scr[h] = jnp.where(_top_rank_select(gate, cur, MOBA_TOPK, 0), 1.0, 0.0)
        extra = jnp.where(row == 0, slope,
                          jnp.where(row == 1, slope * MOBA_BLOCK,
                                    jnp.where(row == 2, -slope * MOBA_BLOCK * cur_f, 0.0)))
        qa_scr[h] = jnp.concatenate([qh_t * (HEAD_DIM ** -0.5), extra], axis=0).astype(BF16)

    def block_scores(h, j):
        return jnp.dot(ka_scr[h, j], qa_scr[h], preferred_element_type=F32)

    def attend_block(j, keep, first):
        for h in range(ATTN_HEADS):
            s = jnp.where(keep(h), block_scores(h, j), NEG_INF)
            s_scr[h] = s
            m_blk = jnp.max(s, axis=0, keepdims=True)
            if first:
                m_scr[h:h + 1, :] = m_blk
            else:
                m_old = m_scr[h:h + 1, :]
                m_new = jnp.maximum(m_old, m_blk)
                alpha_scr[h:h + 1, :] = jnp.exp(m_old - m_new)
                m_scr[h:h + 1, :] = m_new
        for h in range(ATTN_HEADS):
            p = jnp.exp(s_scr[h] - m_scr[h:h + 1, :]).astype(BF16)
            pv = jnp.dot(vt_scr[h, j], p, preferred_element_type=F32)
            acc_scr[h] = pv if first else alpha_scr[h:h + 1, :] * acc_scr[h] + pv

    key_minus_q = (lax.broadcasted_iota(jnp.int32, (MOBA_BLOCK, QUERY_BLOCK), 0)
                   - lax.broadcasted_iota(jnp.int32, (MOBA_BLOCK, QUERY_BLOCK), 1))
    causal = key_minus_q <= qi * QUERY_BLOCK - cur * MOBA_BLOCK
    attend_block(cur, lambda h: causal, True)

    def past_block(j, carry):
        attend_block(j, lambda h: sel_scr[h, pl.ds(j, 1), :] > 0.0, False)
        return carry

    lax.fori_loop(0, cur, past_block, 0)

    outs = []
    for h in range(ATTN_HEADS):
        acc = acc_scr[h]
        outs.append(acc[:HEAD_DIM] / acc[HEAD_DIM:HEAD_DIM + 1])
    o_ref[...] = jnp.concatenate(outs, axis=0).T


def _attn_prompt(q, k_t, v_t, batch, seq):
    nq = seq // QUERY_BLOCK
    n_blocks = seq // MOBA_BLOCK
    return pl.pallas_call(
        functools.partial(_attn_prompt_kernel, n_blocks=n_blocks),
        grid=(batch, nq),
        in_specs=[pl.BlockSpec((QUERY_BLOCK, ATTN_WIDTH), lambda b, i: (b * nq + i, 0)),
                  pl.BlockSpec((1, ATTN_WIDTH, seq), lambda b, i: (b, 0, 0)),
                  pl.BlockSpec((1, ATTN_WIDTH, seq), lambda b, i: (b, 0, 0))],
        out_specs=pl.BlockSpec((QUERY_BLOCK, ATTN_WIDTH), lambda b, i: (b * nq + i, 0)),
        out_shape=jax.ShapeDtypeStruct(q.shape, F32),
        scratch_shapes=[pltpu.VMEM((ATTN_HEADS, n_blocks, MOBA_BLOCK, 2 * HEAD_DIM), BF16),
                        pltpu.VMEM((ATTN_HEADS, n_blocks, V_ROWS, MOBA_BLOCK), BF16),
                        pltpu.VMEM((n_blocks, ATTN_WIDTH), F32),
                        pltpu.VMEM((ATTN_HEADS, 2 * HEAD_DIM, QUERY_BLOCK), BF16),
                        pltpu.VMEM((ATTN_HEADS, n_blocks, QUERY_BLOCK), F32),
                        pltpu.VMEM((ATTN_HEADS, QUERY_BLOCK), F32),
                        pltpu.VMEM((ATTN_HEADS, QUERY_BLOCK), F32),
                        pltpu.VMEM((ATTN_HEADS, V_ROWS, QUERY_BLOCK), F32),
                        pltpu.VMEM((ATTN_HEADS, MOBA_BLOCK, QUERY_BLOCK), F32)],
        compiler_params=pltpu.CompilerParams(
            dimension_semantics=("arbitrary", "arbitrary"), vmem_limit_bytes=VMEM_LIMIT),
    )(q, k_t, v_t)


def _attn_sample_kernel(pt_ref, q_ref, kn_ref, vn_ref, *refs, n_pages, page, t_new):
    del pt_ref
    kp = refs[:n_pages]
    vp = refs[n_pages:2 * n_pages]
    o_ref = refs[2 * n_pages]
    n_cols = t_new * ATTN_HEADS
    pages_per_block = MOBA_BLOCK // page
    n_past = n_pages // pages_per_block
    past_len = n_pages * page

    q = q_ref[0]
    lane_h = lax.broadcasted_iota(jnp.int32, (n_cols, ATTN_WIDTH), 1) // HEAD_DIM
    row = lax.broadcasted_iota(jnp.int32, (n_cols, 1), 0)
    row_h = row % ATTN_HEADS
    row_t = row // ATTN_HEADS
    own_head = lane_h == row_h
    qrep = jnp.concatenate(
        [jnp.broadcast_to(q[t:t + 1, :], (ATTN_HEADS, ATTN_WIDTH)) for t in range(t_new)], axis=0)
    qbd = jnp.where(own_head, qrep, 0.0)
    qbd_s = (qbd * (HEAD_DIM ** -0.5)).astype(BF16)
    slope = jnp.exp2(-8.0 * (row_h + 1).astype(F32) / ATTN_HEADS)

    def block_pages(refs_, n):
        return [refs_[i][0].reshape(ATTN_WIDTH, page)
                for i in range(n * pages_per_block, (n + 1) * pages_per_block)]

    kmean = jnp.concatenate(
        [jnp.sum(sum(block_pages(kp, n)), axis=1, keepdims=True) for n in range(n_past)],
        axis=1) * (1.0 / MOBA_BLOCK)
    gate = jnp.dot(qbd, kmean, precision=lax.Precision.HIGHEST, preferred_element_type=F32)
    sel = _top_rank_select(gate, n_past, MOBA_TOPK, 1)

    s_own = lax.dot_general(qbd_s, kn_ref[0].astype(BF16), _NT, preferred_element_type=F32)
    dist = row_t - lax.broadcasted_iota(jnp.int32, (1, t_new), 1)
    s_own = jnp.where(dist >= 0, s_own - slope * dist.astype(F32), NEG_INF)
    m = jnp.max(s_own, axis=-1, keepdims=True)

    key_off = lax.broadcasted_iota(jnp.int32, (1, MOBA_BLOCK), 1)
    s_past = []
    for n in range(n_past):
        kb_t = jnp.concatenate(block_pages(kp, n), axis=1).astype(BF16)
        dist = (past_len + row_t) - (n * MOBA_BLOCK + key_off)
        s = jnp.dot(qbd_s, kb_t, preferred_element_type=F32) - slope * dist.astype(F32)
        s = jnp.where(sel[:, n:n + 1], s, NEG_INF)
        m = jnp.maximum(m, jnp.max(s, axis=-1, keepdims=True))
        s_past.append(s)

    p = jnp.exp(s_own - m)
    l = jnp.sum(p, axis=-1, keepdims=True)
    vn = vn_ref[0]
    acc = sum(p[:, t:t + 1] * vn[t:t + 1, :] for t in range(t_new))
    for n in range(n_past):
        vb_t = jnp.concatenate(block_pages(vp, n), axis=1).astype(BF16)
        p = jnp.exp(s_past[n] - m)
        l = l + jnp.sum(p, axis=-1, keepdims=True)
        acc = acc + lax.dot_general(p.astype(BF16), vb_t, _NT, preferred_element_type=F32)

    out = jnp.where(own_head, acc / l, 0.0)
    o_ref[0] = jnp.sum(out.reshape(t_new, ATTN_HEADS, ATTN_WIDTH), axis=1)


def _attn_sample(q, kn, vn, cache_kt, cache_vt, page_table):
    n_seq, t_new, _ = q.shape
    n_pages = page_table.shape[1]
    page = cache_kt.shape[-1]
    seq_spec = pl.BlockSpec((1, t_new, ATTN_WIDTH), lambda i, pt: (i, 0, 0))

    def page_spec(p):
        return pl.BlockSpec((1, ATTN_HEADS, HEAD_DIM, page),
                            lambda i, pt: (pt[i * n_pages + p], 0, 0, 0))

    page_specs = [page_spec(p) for p in range(n_pages)]
    return pl.pallas_call(
        functools.partial(_attn_sample_kernel, n_pages=n_pages, page=page, t_new=t_new),
        grid_spec=pltpu.PrefetchScalarGridSpec(
            num_scalar_prefetch=1,
            grid=(n_seq,),
            in_specs=[seq_spec] * 3 + page_specs * 2,
            out_specs=seq_spec),
        out_shape=jax.ShapeDtypeStruct(q.shape, F32),
        compiler_params=pltpu.CompilerParams(
            dimension_semantics=("arbitrary",), vmem_limit_bytes=VMEM_LIMIT),
    )(page_table.reshape(-1), q, kn, vn, *([cache_kt] * n_pages), *([cache_vt] * n_pages))


def _outproj_kernel(att_ref, gmn_ref, x_ref, wo_ref, oga_ref, nfg_ref, wr_ref, br_ref,
                    x1_ref, h2_ref, comb_ref, bucket_ref, wpair_ref):
    attn = _rms(att_ref[...], oga_ref[...]).astype(BF16)
    mix = (jnp.dot(attn, wo_ref[:ATTN_WIDTH, :], preferred_element_type=F32)
           + jnp.dot(gmn_ref[...].astype(BF16), wo_ref[ATTN_WIDTH:, :], preferred_element_type=F32))
    x1 = x_ref[...] + mix
    x1_ref[...] = x1
    h2 = _rms(x1, nfg_ref[...])
    h2_hi = h2.astype(BF16)
    h2_ref[...] = h2_hi

    h2_lo = (h2 - h2_hi.astype(F32)).astype(BF16)
    hi_dot = jnp.dot(h2_hi, wr_ref[...], preferred_element_type=F32)
    lo_dot = jnp.dot(h2_lo, wr_ref[:, :LANES], preferred_element_type=F32)
    logits = hi_dot[:, :LANES] + hi_dot[:, LANES:] + lo_dot + br_ref[...]
    lt = logits.T
    tm = lt.shape[1]
    row4 = lax.broadcasted_iota(jnp.int32, (MOE_GROUPS, tm), 0)

    def first_argmax(v):
        vmax = jnp.max(v, axis=0, keepdims=True)
        idx = jnp.min(jnp.where(v == vmax, row4, MOE_GROUPS), axis=0, keepdims=True)
        return vmax, idx

    glog = lt[:MOE_GROUPS]
    ge = jnp.exp(glog - jnp.max(glog, axis=0, keepdims=True))
    gprob = ge / jnp.sum(ge, axis=0, keepdims=True)
    p_g, g_idx = first_argmax(gprob)
    elog = lt[MOE_GROUPS:MOE_GROUPS + EXPERTS_PER_GROUP]
    for g in range(1, MOE_GROUPS):
        lo = MOE_GROUPS + g * EXPERTS_PER_GROUP
        elog = jnp.where(g_idx == g, lt[lo:lo + EXPERTS_PER_GROUP], elog)
    l1, i1 = first_argmax(elog)
    l2, i2 = first_argmax(jnp.where(row4 == i1, NEG_INF, elog))
    e2 = jnp.exp(l2 - l1)
    denom = 1.0 + e2
    w1 = (1.0 / denom) * p_g
    w2 = (e2 / denom) * p_g
    lane_row = lax.broadcasted_iota(jnp.int32, (LANES, tm), 0)
    base = g_idx * EXPERTS_PER_GROUP
    comb_t = jnp.where(lane_row == base + i1, w1, 0.0) + jnp.where(lane_row == base + i2, w2, 0.0)
    comb_ref[...] = comb_t.T

    e_lo = jnp.minimum(i1, i2)
    e_hi = jnp.maximum(i1, i2)
    pair = jnp.where(e_lo == 0, e_hi - 1, jnp.where(e_lo == 1, e_hi + 1, PAIRS_PER_GROUP - 1))
    row8 = lax.broadcasted_iota(jnp.int32, (8, tm), 0)
    bucket_ref[...] = jnp.where(row8 == 0, g_idx * PAIRS_PER_GROUP + pair, 0)
    w_lo = jnp.where(i1 < i2, w1, w2)
    w_hi = jnp.where(i1 < i2, w2, w1)
    wpair_ref[...] = jnp.where(row8 == 0, w_lo, jnp.where(row8 == 1, w_hi, 0.0))


def _outproj(att, gmn, x, wo_bf, og_attn, nf_g, w_r, b_r, tm):
    rows = x.shape[0]
    row_spec = lambda width: pl.BlockSpec((tm, width), lambda i: (i, 0))
    full = lambda a: pl.BlockSpec(a.shape, lambda i: (0,) * a.ndim)
    return pl.pallas_call(
        _outproj_kernel,
        grid=(rows // tm,),
        in_specs=[row_spec(ATTN_WIDTH), row_spec(MLP_WIDTH), row_spec(D_MODEL), full(wo_bf),
                  full(og_attn), full(nf_g), full(w_r), full(b_r)],
        out_specs=[row_spec(D_MODEL), row_spec(D_MODEL), row_spec(LANES),
                   pl.BlockSpec((8, tm), lambda i: (0, i)), pl.BlockSpec((8, tm), lambda i: (0, i))],
        out_shape=[jax.ShapeDtypeStruct((rows, D_MODEL), F32),
                   jax.ShapeDtypeStruct((rows, D_MODEL), BF16),
                   jax.ShapeDtypeStruct((rows, LANES), F32),
                   jax.ShapeDtypeStruct((8, rows), jnp.int32),
                   jax.ShapeDtypeStruct((8, rows), F32)],
        compiler_params=pltpu.CompilerParams(
            dimension_semantics=("arbitrary",), vmem_limit_bytes=VMEM_LIMIT),
    )(att, gmn, x, wo_bf, og_attn, nf_g, w_r, b_r)


def _moe_kernel(h_ref, comb_ref, wg_ref, wu_ref, wd_ref, x1_ref, fg_ref, y_ref, acc_ref):
    e = pl.program_id(1)

    @pl.when(e == 0)
    def _():
        acc_ref[...] = jnp.zeros_like(acc_ref)

    h = h_ref[...]
    hg = jnp.dot(h, wg_ref[0], preferred_element_type=F32)
    hu = jnp.dot(h, wu_ref[0], preferred_element_type=F32)
    comb = comb_ref[...]
    lane = lax.broadcasted_iota(jnp.int32, comb.shape, 1)
    c = jnp.sum(jnp.where(lane == e, comb, 0.0), axis=-1, keepdims=True)
    act = hg * (1.0 / (1.0 + jnp.exp(-hg))) * hu * c
    acc_ref[...] += jnp.dot(act.astype(BF16), wd_ref[0], preferred_element_type=F32)

    @pl.when(e == N_EXPERTS - 1)
    def _():
        y_ref[...] = _rms(x1_ref[...] + acc_ref[...], fg_ref[...])


def _moe(h2, comb, wg_bf, wu_bf, wd_bf, x1, final_g, tm):
    rows = h2.shape[0]
    row_spec = lambda width: pl.BlockSpec((tm, width), lambda i, e: (i, 0))
    return pl.pallas_call(
        _moe_kernel,
        grid=(rows // tm, N_EXPERTS),
        in_specs=[row_spec(D_MODEL), row_spec(LANES),
                  pl.BlockSpec((1, D_MODEL, D_EXPERT), lambda i, e: (e, 0, 0)),
                  pl.BlockSpec((1, D_MODEL, D_EXPERT), lambda i, e: (e, 0, 0)),
                  pl.BlockSpec((1, D_EXPERT, D_MODEL), lambda i, e: (e, 0, 0)),
                  row_spec(D_MODEL),
                  pl.BlockSpec(final_g.shape, lambda i, e: (0, 0))],
        out_specs=row_spec(D_MODEL),
        out_shape=jax.ShapeDtypeStruct((rows, D_MODEL), F32),
        scratch_shapes=[pltpu.VMEM((tm, D_MODEL), F32)],
        compiler_params=pltpu.CompilerParams(
            dimension_semantics=("arbitrary", "arbitrary"), vmem_limit_bytes=VMEM_LIMIT),
    )(h2, comb, wg_bf, wu_bf, wd_bf, x1, final_g)


SC_CORES = 2
SC_SUBCORES = 16
SC_WINDOW = 32


def _sc_gather_rows(x, idx):
    n = idx.shape[0]
    width = x.shape[1]
    assert n % SC_WINDOW == 0
    mesh = plsc.VectorSubcoreMesh(core_axis_name="core", subcore_axis_name="subcore",
                                  num_cores=SC_CORES, num_subcores=SC_SUBCORES)

    @pl.kernel(out_type=jax.ShapeDtypeStruct((n, width), x.dtype), mesh=mesh)
    def gather_kernel(x_hbm, idx_hbm, out_hbm):
        def body(idx_vmem, out_vmem):
            pltpu.sync_copy(x_hbm.at[idx_vmem.at[0, pl.ds(0, SC_WINDOW)]], out_vmem)

        pltpu.emit_pipeline(
            body,
            grid=(n // SC_WINDOW,),
            in_specs=[pl.BlockSpec((1, LANES), lambda i: (i, 0))],
            out_specs=[pl.BlockSpec((SC_WINDOW, width), lambda i: (i, 0))],
            core_axis_name=("core", "subcore"),
            dimension_semantics=(pltpu.PARALLEL,),
        )(idx_hbm, out_hbm)

    idx_rows = jnp.pad(idx.reshape(n // SC_WINDOW, SC_WINDOW), ((0, 0), (0, LANES - SC_WINDOW)))
    return gather_kernel(x, idx_rows)


class MoePlan(NamedTuple):
    slot_row: jax.Array
    token_slot: jax.Array
    e_lo: jax.Array
    e_hi: jax.Array
    n_valid: jax.Array
    w_slots: jax.Array


def _moe_plan(bucket, wpair, tm):
    rows = bucket.shape[0]
    n_tiles_max = rows // tm + N_BUCKETS
    order = jnp.argsort(bucket, stable=True).astype(jnp.int32)
    position = jnp.argsort(order).astype(jnp.int32)
    counts = jnp.sum((bucket[:, None] == jnp.arange(N_BUCKETS)[None, :]).astype(jnp.int32), axis=0)
    starts = jnp.cumsum(counts) - counts
    tiles_b = (counts + tm - 1) // tm
    tile_end = jnp.cumsum(tiles_b)
    tile_start = tile_end - tiles_b
    n_tiles = tile_end[-1]
    token_slot = tile_start[bucket] * tm + (position - starts[bucket])
    t = jnp.arange(n_tiles_max, dtype=jnp.int32)
    tb = jnp.sum((jnp.minimum(t, n_tiles - 1)[:, None] >= tile_end[None, :]).astype(jnp.int32), axis=1)
    local = (t - tile_start[tb]) * tm
    n_valid = jnp.where(t < n_tiles, jnp.clip(counts[tb] - local, 0, tm), 0)
    slot = local[:, None] + jnp.arange(tm, dtype=jnp.int32)[None, :]
    src = (starts[tb][:, None] + slot) % rows
    slot_row = order[src].reshape(-1)
    group = tb // PAIRS_PER_GROUP
    pair = tb % PAIRS_PER_GROUP
    e_lo = group * EXPERTS_PER_GROUP + jnp.asarray(PAIR_LO, jnp.int32)[pair]
    e_hi = group * EXPERTS_PER_GROUP + jnp.asarray(PAIR_HI, jnp.int32)[pair]
    return MoePlan(slot_row, token_slot.astype(jnp.int32), e_lo, e_hi, n_valid.astype(jnp.int32),
                   wpair[:, slot_row].T)


def _moe_grouped_kernel(elo_ref, ehi_ref, nv_ref, x1_ref, ws_ref, wg_lo, wu_lo, wd_lo,
                        wg_hi, wu_hi, wd_hi, nfg_ref, fg_ref, y_ref):
    del elo_ref, ehi_ref
    t = pl.program_id(0)

    @pl.when(nv_ref[t] > 0)
    def _():
        x1 = x1_ref[...]
        h = _rms(x1, nfg_ref[...]).astype(BF16)
        ws = ws_ref[...]
        moe = jnp.zeros(x1.shape, F32)
        for col, (wg, wu, wd) in enumerate(((wg_lo, wu_lo, wd_lo), (wg_hi, wu_hi, wd_hi))):
            hg = jnp.dot(h, wg[0], preferred_element_type=F32)
            hu = jnp.dot(h, wu[0], preferred_element_type=F32)
            act = hg * (1.0 / (1.0 + jnp.exp(-hg))) * hu * ws[:, col:col + 1]
            moe = moe + jnp.dot(act.astype(BF16), wd[0], preferred_element_type=F32)
        y_ref[...] = _rms(x1 + moe, fg_ref[...])

    @pl.when(nv_ref[t] == 0)
    def _():
        y_ref[...] = jnp.zeros_like(y_ref)


def _moe_grouped(x1_slots, plan, wg_bf, wu_bf, wd_bf, nf_g, final_g, tm):
    n_steps = plan.n_valid.shape[0]
    lo_spec = lambda shape: pl.BlockSpec(shape, lambda t, elo, ehi, nv: (elo[t], 0, 0))
    hi_spec = lambda shape: pl.BlockSpec(shape, lambda t, elo, ehi, nv: (ehi[t], 0, 0))
    up_shape, down_shape = (1, D_MODEL, D_EXPERT), (1, D_EXPERT, D_MODEL)
    const = lambda a: pl.BlockSpec(a.shape, lambda t, elo, ehi, nv: (0, 0))
    row_spec = lambda width: pl.BlockSpec((tm, width), lambda t, elo, ehi, nv: (t, 0))
    return pl.pallas_call(
        _moe_grouped_kernel,
        grid_spec=pltpu.PrefetchScalarGridSpec(
            num_scalar_prefetch=3,
            grid=(n_steps,),
            in_specs=[row_spec(D_MODEL), row_spec(2),
                      lo_spec(up_shape), lo_spec(up_shape), lo_spec(down_shape),
                      hi_spec(up_shape), hi_spec(up_shape), hi_spec(down_shape),
                      const(nf_g), const(final_g)],
            out_specs=row_spec(D_MODEL)),
        out_shape=jax.ShapeDtypeStruct(x1_slots.shape, F32),
        compiler_params=pltpu.CompilerParams(
            dimension_semantics=("arbitrary",), vmem_limit_bytes=VMEM_LIMIT),
    )(plan.e_lo, plan.e_hi, plan.n_valid, x1_slots, plan.w_slots, wg_bf, wu_bf, wd_bf,
      wg_bf, wu_bf, wd_bf, nf_g, final_g)


def _spatial_operands(w_s, b_s, t_chunk):
    reps = CHUNK // t_chunk
    idx = jnp.arange(CHUNK)
    same = (idx[:, None] // t_chunk) == (idx[None, :] // t_chunk)
    causal = (idx[None, :] % t_chunk) <= (idx[:, None] % t_chunk)
    wm = jnp.tile(w_s[:, :t_chunk, :t_chunk], (1, reps, reps)) * (same & causal)
    wm_cat = wm.transpose(1, 0, 2).reshape(CHUNK, MLP_GROUPS * CHUNK).astype(BF16)
    bias = jnp.repeat(jnp.tile(b_s[:, :t_chunk], (1, reps)).T, MLP_CH, axis=1)
    return wm_cat, bias


def kernel(x_prompt, x_sample, cache_k, cache_v, page_table, norm_attn_g, w_in, sgu_g, w_spatial,
           b_spatial, out_g_attn, out_g_mlp, w_out, norm_ffn_g, w_group, b_group, w_router, b_router,
           w_gate, w_up, w_down, final_g):
    depth = w_in.shape[0]
    assert depth == 1, "single decoder layer"
    batch, seq, _ = x_prompt.shape
    n_seq, t_new, _ = x_sample.shape
    assert seq % MOBA_BLOCK == 0 and seq % CHUNK == 0 and CHUNK % t_new == 0
    assert MOBA_BLOCK % cache_k.shape[2] == 0

    row2 = lambda a: a.reshape(1, -1)
    w_in_bf = w_in[0].astype(BF16)
    w_out_bf = w_out[0].astype(BF16)
    wg_bf, wu_bf, wd_bf = w_gate[0].astype(BF16), w_up[0].astype(BF16), w_down[0].astype(BF16)
    n_logits = MOE_GROUPS + N_EXPERTS
    w_r = jnp.concatenate(
        [w_group[0], w_router[0].transpose(1, 0, 2).reshape(D_MODEL, N_EXPERTS),
         jnp.zeros((D_MODEL, LANES - n_logits), F32)], axis=1)
    b_r = jnp.concatenate(
        [b_group[0], b_router[0].reshape(-1), jnp.zeros((LANES - n_logits,), F32)]).reshape(1, LANES)
    w_r_hi = w_r.astype(BF16)
    w_r_hl = jnp.concatenate([w_r_hi, (w_r - w_r_hi.astype(F32)).astype(BF16)], axis=1)
    wkv_t_bf = w_in_bf[:, ATTN_WIDTH:3 * ATTN_WIDTH].T
    ck_t = jnp.transpose(cache_k[0], (0, 2, 3, 1))
    cv_t = jnp.transpose(cache_v[0], (0, 2, 3, 1))

    def project_in(x, t_chunk, tm, seq_transposed):
        wm_cat, bias_full = _spatial_operands(w_spatial[0], b_spatial[0], t_chunk)
        return _inproj(x, row2(norm_attn_g[0]), w_in_bf, wkv_t_bf, row2(sgu_g[0]), wm_cat, bias_full,
                       row2(out_g_mlp[0]), tm, seq_transposed)

    def project_out(att, gmn, x, tm):
        return _outproj(att, gmn, x, w_out_bf, row2(out_g_attn[0]), row2(norm_ffn_g[0]), w_r_hl, b_r,
                        tm)

    xp = x_prompt.reshape(batch * seq, D_MODEL)
    xp, wg_bf, wu_bf, wd_bf = lax.optimization_barrier((xp, wg_bf, wu_bf, wd_bf))
    qp, kp_t, vp_t, gmn_p, _ = project_in(xp, CHUNK, TM_PROJ_PROMPT, seq)
    att_p = _attn_prompt(qp, kp_t, vp_t, batch, seq)
    x1_p, _, _, bucket_p, wpair_p = project_out(att_p, gmn_p, xp, TM_PROJ_PROMPT)
    plan = _moe_plan(bucket_p[0], wpair_p[:2], TM_MOE_PROMPT)
    x1_slots = _sc_gather_rows(x1_p, plan.slot_row)

    xs = x_sample.reshape(n_seq * t_new, D_MODEL)
    qs, ks, vs, gmn_s, gvs = project_in(xs, t_new, TM_PROJ_SAMPLE, None)
    seq_shape = (n_seq, t_new, ATTN_WIDTH)
    qs3, ks3, vs3 = qs.reshape(seq_shape), ks.reshape(seq_shape), vs.reshape(seq_shape)
    half = n_seq // 2
    attend_half = lambda q_half, lo: _attn_sample(q_half, ks3[lo:lo + half], vs3[lo:lo + half],
                                                  ck_t, cv_t, page_table[lo:lo + half])
    att_s0 = attend_half(qs3[:half], 0)
    x1_slots, att_s0 = lax.optimization_barrier((x1_slots, att_s0))
    y_slots = _moe_grouped(x1_slots, plan, wg_bf, wu_bf, wd_bf, row2(norm_ffn_g[0]),
                           row2(final_g), TM_MOE_PROMPT)
    y_slots, q_half1 = lax.optimization_barrier((y_slots, qs3[half:]))
    yp = _sc_gather_rows(y_slots, plan.token_slot)
    att_s1 = attend_half(q_half1, half)
    att_s = jnp.concatenate([att_s0, att_s1], axis=0).reshape(n_seq * t_new, ATTN_WIDTH)
    x1_s, h2_s, comb_s, _, _ = project_out(att_s, gmn_s, xs, TM_PROJ_SAMPLE)
    ys = _moe(h2_s, comb_s, wg_bf, wu_bf, wd_bf, x1_s, row2(final_g), TM_MOE_SAMPLE)

    heads = (ATTN_HEADS, HEAD_DIM)
    rows_last = lambda a_t: a_t.reshape(batch, *heads, seq).transpose(0, 3, 1, 2)[None]
    return (yp.reshape(batch, seq, D_MODEL),
            ys.reshape(n_seq, t_new, D_MODEL),
            rows_last(kp_t),
            rows_last(vp_t),
            ks.reshape(depth, n_seq, t_new, *heads),
            vs.reshape(depth, n_seq, t_new, *heads),
            gvs.reshape(depth, n_seq, t_new, MLP_WIDTH))
```

```python
import functools
import math
from typing import NamedTuple

import jax
import jax.numpy as jnp
from jax import lax
from jax.experimental import pallas as pl
from jax.experimental.pallas import tpu as pltpu
from jax.experimental.pallas import tpu_sc as plsc

D_MODEL = 1024
ATTN_HEADS = 8
HEAD_DIM = 64
ATTN_WIDTH = ATTN_HEADS * HEAD_DIM
MOBA_BLOCK = 256
MOBA_TOPK = 3
QUERY_BLOCK = 128
MLP_GROUPS = 8
MLP_CH = 64
MLP_WIDTH = MLP_GROUPS * MLP_CH
CHUNK = 128
IN_WIDTH = 3 * ATTN_WIDTH + 2 * MLP_WIDTH
MOE_GROUPS = 4
EXPERTS_PER_GROUP = 4
N_EXPERTS = MOE_GROUPS * EXPERTS_PER_GROUP
D_EXPERT = D_MODEL // 2
EPS = 1e-6
PAIR_LO = (0, 0, 0, 1, 1, 2)
PAIR_HI = (1, 2, 3, 2, 3, 3)
PAIRS_PER_GROUP = len(PAIR_LO)
N_BUCKETS = MOE_GROUPS * PAIRS_PER_GROUP

LANES = 128
BF16_ROWS = 16
VMEM_LIMIT = 56 * 1024 * 1024

TM_PROJ_PROMPT = 512
TM_PROJ_SAMPLE = 128
TM_MOE_PROMPT = 256
TM_MOE_SAMPLE = 512

F32 = jnp.float32
BF16 = jnp.bfloat16
NEG_INF = float("-inf")
_NT = (((1,), (1,)), ((), ()))


def _rms(x, g):
    return x * lax.rsqrt(jnp.mean(x * x, axis=-1, keepdims=True) + EPS) * g


def _gelu(x):
    return 0.5 * x * (1.0 + jnp.tanh(0.7978845608028654 * (x + 0.044715 * (x * x * x))))


def _top_rank_select(gate, n_past, n_keep, axis):
    nb = gate.shape[axis]
    n_idx = lax.broadcasted_iota(jnp.int32, gate.shape, axis)
    rank = jnp.zeros(gate.shape, jnp.int32)
    for m in range(nb):
        gm = gate[:, m:m + 1] if axis == 1 else gate[m:m + 1, :]
        beats = jnp.where(gm > gate, 1, jnp.where(gm == gate, jnp.where(m < n_idx, 1, 0), 0))
        rank = rank + jnp.where(m < n_past, beats, 0)
    return jnp.where(n_idx < n_past, rank, n_keep) < n_keep


def _inproj_kernel(x_ref, g_ref, w_ref, wkv_t_ref, sgu_ref, wm_ref, bias_ref, og_ref,
                   q_ref, k_ref, v_ref, gmn_ref, vgn_ref, *, n_chunks, kv_transposed):
    h = _rms(x_ref[...], g_ref[...]).astype(BF16)

    def proj(lo, width):
        return jnp.dot(h, w_ref[:, lo:lo + width], preferred_element_type=F32)

    q_ref[...] = proj(0, ATTN_WIDTH)
    if kv_transposed:
        kv_t = lax.dot_general(wkv_t_ref[...], h, _NT, preferred_element_type=F32)
        k_ref[0] = kv_t[:ATTN_WIDTH]
        v_ref[0] = kv_t[ATTN_WIDTH:]
    else:
        k_ref[...] = proj(ATTN_WIDTH, ATTN_WIDTH)
        v_ref[...] = proj(2 * ATTN_WIDTH, ATTN_WIDTH)
    gu = _gelu(proj(3 * ATTN_WIDTH, MLP_WIDTH))
    vgn = _rms(_gelu(proj(3 * ATTN_WIDTH + MLP_WIDTH, MLP_WIDTH)), sgu_ref[...])
    vgn_ref[...] = vgn

    lane_grp = lax.broadcasted_iota(jnp.int32, (CHUNK, MLP_WIDTH), 1) // MLP_CH
    for c in range(n_chunks):
        rows = slice(c * CHUNK, (c + 1) * CHUNK)
        vc = vgn[rows].astype(BF16)
        vbd = jnp.concatenate(
            [jnp.where(lane_grp == g, vc, jnp.zeros_like(vc)) for g in range(MLP_GROUPS)], axis=0)
        mixed = jnp.dot(wm_ref[...], vbd, preferred_element_type=F32) + bias_ref[...]
        gmn_ref[rows, :] = _rms(gu[rows] * mixed, og_ref[...])


def _inproj(x, g, w_bf, wkv_t_bf, sgu_g, wm_cat, bias_full, og_mlp, tm, seq_transposed=None):
    rows = x.shape[0]
    row_spec = lambda width: pl.BlockSpec((tm, width), lambda i: (i, 0))
    full = lambda a: pl.BlockSpec(a.shape, lambda i: (0,) * a.ndim)
    out = jax.ShapeDtypeStruct((rows, ATTN_WIDTH), F32)
    kv_spec, kv_out = row_spec(ATTN_WIDTH), out
    if seq_transposed is not None:
        tiles = seq_transposed // tm
        kv_spec = pl.BlockSpec((1, ATTN_WIDTH, tm), lambda i: (i // tiles, 0, i % tiles))
        kv_out = jax.ShapeDtypeStruct((rows // seq_transposed, ATTN_WIDTH, seq_transposed), F32)
    return pl.pallas_call(
        functools.partial(_inproj_kernel, n_chunks=tm // CHUNK,
                          kv_transposed=seq_transposed is not None),
        grid=(rows // tm,),
        in_specs=[row_spec(D_MODEL), full(g), full(w_bf), full(wkv_t_bf), full(sgu_g), full(wm_cat),
                  full(bias_full), full(og_mlp)],
        out_specs=[row_spec(ATTN_WIDTH), kv_spec, kv_spec, row_spec(ATTN_WIDTH),
                   row_spec(ATTN_WIDTH)],
        out_shape=[out, kv_out, kv_out, out, out],
        compiler_params=pltpu.CompilerParams(
            dimension_semantics=("arbitrary",), vmem_limit_bytes=VMEM_LIMIT),
    )(x, g, w_bf, wkv_t_bf, sgu_g, wm_cat, bias_full, og_mlp)


V_ROWS = HEAD_DIM + BF16_ROWS
LOG2_E = 1.4426950408889634
MASKED = -1e30
Q_TILE = MOBA_BLOCK
COL_R, COL_J, COL_SEL = 0, 2, 8


def _bf16_split(x):
    mantissa, exponent = math.frexp(x)
    high = math.ldexp(round(mantissa * 256.0) / 256.0, exponent)
    return high, x - high


def _attn_prompt_kernel(q_ref, k_ref, v_ref, o_ref, ka_scr, vt_scr, kmean_scr, qa_scr,
                        m_scr, alpha_scr, acc_scr, s_scr, *, n_blocks):
    cur = pl.program_id(1)

    @pl.when(cur == 0)
    def _():
        key = lax.broadcasted_iota(jnp.int32, (MOBA_BLOCK, HEAD_DIM), 0)
        col = lax.broadcasted_iota(jnp.int32, (MOBA_BLOCK, HEAD_DIM), 1)
        ones = jnp.ones((BF16_ROWS, MOBA_BLOCK), BF16)
        for n in range(n_blocks):
            keys = slice(n * MOBA_BLOCK, (n + 1) * MOBA_BLOCK)
            kb = k_ref[0, :, keys].T
            kmean_scr[n:n + 1, :] = jnp.sum(kb, axis=0, keepdims=True) * (1.0 / MOBA_BLOCK)
            vt = v_ref[0, :, keys]
            extra = jnp.where(col < COL_J, key,
                              jnp.where(col < COL_SEL, n, jnp.where(col == COL_SEL + n, 1, 0)))
            extra = extra.astype(F32).astype(BF16)
            for h in range(ATTN_HEADS):
                lanes = slice(h * HEAD_DIM, (h + 1) * HEAD_DIM)
                ka_scr[h, n] = jnp.concatenate([kb[:, lanes].astype(BF16), extra], axis=1)
                vt_scr[h, n, :HEAD_DIM, :] = vt[lanes, :].astype(BF16)
                vt_scr[h, n, HEAD_DIM:, :] = ones

    q_t = q_ref[...].T
    row = lax.broadcasted_iota(jnp.int32, (HEAD_DIM, Q_TILE), 0)
    blk = lax.broadcasted_iota(jnp.int32, (n_blocks, Q_TILE), 0)
    for h in range(ATTN_HEADS):
        lanes = slice(h * HEAD_DIM, (h + 1) * HEAD_DIM)
        hi, lo = _bf16_split(LOG2_E * 2.0 ** (-8.0 * (h + 1) / ATTN_HEADS))
        qh_t = q_t[lanes, :]
        gate = jnp.dot(kmean_scr[:, lanes], qh_t, precision=lax.Precision.HIGHEST,
                       preferred_element_type=F32)
        keep = _top_rank_select(gate, cur, MOBA_TOPK, 0) | (blk >= cur)
        sel_rows = jnp.where(keep, 0.0, MASKED)
        alibi = jnp.where(row == COL_R, hi,
                          jnp.where(row == COL_R + 1, lo,
                                    jnp.where(row == COL_J, hi * MOBA_BLOCK,
                                              jnp.where(row == COL_J + 1, lo * MOBA_BLOCK, 0.0))))
        extra = alibi + jnp.concatenate(
            [jnp.zeros((COL_SEL, Q_TILE), F32), sel_rows,
             jnp.zeros((HEAD_DIM - COL_SEL - n_blocks, Q_TILE), F32)], axis=0)
        qa_scr[h] = jnp.concatenate([qh_t * (LOG2_E * HEAD_DIM ** -0.5), extra], axis=0).astype(BF16)

    def attend_block(j, causal, first):
        for h in range(ATTN_HEADS):
            s = jnp.dot(ka_scr[h, j], qa_scr[h], preferred_element_type=F32)
            if causal is not None:
                s = jnp.where(causal, s, NEG_INF)
            s_scr[h] = s
            m_blk = jnp.max(s, axis=0, keepdims=True)
            if first:
                m_scr[h:h + 1, :] = m_blk
            else:
                m_old = m_scr[h:h + 1, :]
                m_new = jnp.maximum(m_old, m_blk)
                alpha_scr[h:h + 1, :] = jnp.exp2(m_old - m_new)
                m_scr[h:h + 1, :] = m_new
        for h in range(ATTN_HEADS):
            p = jnp.exp2(s_scr[h] - m_scr[h:h + 1, :]).astype(BF16)
            pv = jnp.dot(vt_scr[h, j], p, preferred_element_type=F32)
            acc_scr[h] = pv if first else alpha_scr[h:h + 1, :] * acc_scr[h] + pv

    causal = (lax.broadcasted_iota(jnp.int32, (MOBA_BLOCK, Q_TILE), 0)
              <= lax.broadcasted_iota(jnp.int32, (MOBA_BLOCK, Q_TILE), 1))
    attend_block(cur, causal, True)

    def past_block(j, carry):
        attend_block(j, None, False)
        return carry

    lax.fori_loop(0, cur, past_block, 0)

    outs = []
    for h in range(ATTN_HEADS):
        acc = acc_scr[h]
        outs.append(acc[:HEAD_DIM] / acc[HEAD_DIM:HEAD_DIM + 1])
    o_ref[...] = jnp.concatenate(outs, axis=0).T


def _attn_prompt(q, k_t, v_t, batch, seq):
    n_blocks = seq // MOBA_BLOCK
    assert COL_SEL + n_blocks <= HEAD_DIM and n_blocks % 8 == 0
    q_spec = pl.BlockSpec((Q_TILE, ATTN_WIDTH), lambda b, i: (b * n_blocks + i, 0))
    return pl.pallas_call(
        functools.partial(_attn_prompt_kernel, n_blocks=n_blocks),
        grid=(batch, n_blocks),
        in_specs=[q_spec,
                  pl.BlockSpec((1, ATTN_WIDTH, seq), lambda b, i: (b, 0, 0)),
                  pl.BlockSpec((1, ATTN_WIDTH, seq), lambda b, i: (b, 0, 0))],
        out_specs=q_spec,
        out_shape=jax.ShapeDtypeStruct(q.shape, F32),
        scratch_shapes=[pltpu.VMEM((ATTN_HEADS, n_blocks, MOBA_BLOCK, 2 * HEAD_DIM), BF16),
                        pltpu.VMEM((ATTN_HEADS, n_blocks, V_ROWS, MOBA_BLOCK), BF16),
                        pltpu.VMEM((n_blocks, ATTN_WIDTH), F32),
                        pltpu.VMEM((ATTN_HEADS, 2 * HEAD_DIM, Q_TILE), BF16),
                        pltpu.VMEM((ATTN_HEADS, Q_TILE), F32),
                        pltpu.VMEM((ATTN_HEADS, Q_TILE), F32),
                        pltpu.VMEM((ATTN_HEADS, V_ROWS, Q_TILE), F32),
                        pltpu.VMEM((ATTN_HEADS, MOBA_BLOCK, Q_TILE), F32)],
        compiler_params=pltpu.CompilerParams(
            dimension_semantics=("arbitrary", "arbitrary"), vmem_limit_bytes=VMEM_LIMIT),
    )(q, k_t, v_t)


def _attn_sample_kernel(pt_ref, q_ref, kn_ref, vn_ref, *refs, n_pages, page, t_new):
    del pt_ref
    kp = refs[:n_pages]
    vp = refs[n_pages:2 * n_pages]
    o_ref = refs[2 * n_pages]
    n_cols = t_new * ATTN_HEADS
    pages_per_block = MOBA_BLOCK // page
    n_past = n_pages // pages_per_block
    past_len = n_pages * page

    q = q_ref[0]
    lane_h = lax.broadcasted_iota(jnp.int32, (n_cols, ATTN_WIDTH), 1) // HEAD_DIM
    row = lax.broadcasted_iota(jnp.int32, (n_cols, 1), 0)
    row_h = row % ATTN_HEADS
    row_t = row // ATTN_HEADS
    own_head = lane_h == row_h
    qrep = jnp.concatenate(
        [jnp.broadcast_to(q[t:t + 1, :], (ATTN_HEADS, ATTN_WIDTH)) for t in range(t_new)], axis=0)
    qbd = jnp.where(own_head, qrep, 0.0)
    qbd_s = (qbd * (HEAD_DIM ** -0.5)).astype(BF16)
    slope = jnp.exp2(-8.0 * (row_h + 1).astype(F32) / ATTN_HEADS)

    def block_pages(refs_, n):
        return [refs_[i][0].reshape(ATTN_WIDTH, page)
                for i in range(n * pages_per_block, (n + 1) * pages_per_block)]

    kmean = jnp.concatenate(
        [jnp.sum(sum(block_pages(kp, n)), axis=1, keepdims=True) for n in range(n_past)],
        axis=1) * (1.0 / MOBA_BLOCK)
    gate = jnp.dot(qbd, kmean, precision=lax.Precision.HIGHEST, preferred_element_type=F32)
    sel = _top_rank_select(gate, n_past, MOBA_TOPK, 1)

    s_own = lax.dot_general(qbd_s, kn_ref[0].astype(BF16), _NT, preferred_element_type=F32)
    dist = row_t - lax.broadcasted_iota(jnp.int32, (1, t_new), 1)
    s_own = jnp.where(dist >= 0, s_own - slope * dist.astype(F32), NEG_INF)
    m = jnp.max(s_own, axis=-1, keepdims=True)

    key_off = lax.broadcasted_iota(jnp.int32, (1, MOBA_BLOCK), 1)
    s_past = []
    for n in range(n_past):
        kb_t = jnp.concatenate(block_pages(kp, n), axis=1).astype(BF16)
        dist = (past_len + row_t) - (n * MOBA_BLOCK + key_off)
        s = jnp.dot(qbd_s, kb_t, preferred_element_type=F32) - slope * dist.astype(F32)
        s = jnp.where(sel[:, n:n + 1], s, NEG_INF)
        m = jnp.maximum(m, jnp.max(s, axis=-1, keepdims=True))
        s_past.append(s)

    p = jnp.exp(s_own - m)
    l = jnp.sum(p, axis=-1, keepdims=True)
    vn = vn_ref[0]
    acc = sum(p[:, t:t + 1] * vn[t:t + 1, :] for t in range(t_new))
    for n in range(n_past):
        vb_t = jnp.concatenate(block_pages(vp, n), axis=1).astype(BF16)
        p = jnp.exp(s_past[n] - m)
        l = l + jnp.sum(p, axis=-1, keepdims=True)
        acc = acc + lax.dot_general(p.astype(BF16), vb_t, _NT, preferred_element_type=F32)

    out = jnp.where(own_head, acc / l, 0.0)
    o_ref[0] = jnp.sum(out.reshape(t_new, ATTN_HEADS, ATTN_WIDTH), axis=1)


def _attn_sample(q, kn, vn, cache_kt, cache_vt, page_table):
    n_seq, t_new, _ = q.shape
    n_pages = page_table.shape[1]
    page = cache_kt.shape[-1]
    seq_spec = pl.BlockSpec((1, t_new, ATTN_WIDTH), lambda i, pt: (i, 0, 0))

    def page_spec(p):
        return pl.BlockSpec((1, ATTN_HEADS, HEAD_DIM, page),
                            lambda i, pt: (pt[i * n_pages + p], 0, 0, 0))

    page_specs = [page_spec(p) for p in range(n_pages)]
    return pl.pallas_call(
        functools.partial(_attn_sample_kernel, n_pages=n_pages, page=page, t_new=t_new),
        grid_spec=pltpu.PrefetchScalarGridSpec(
            num_scalar_prefetch=1,
            grid=(n_seq,),
            in_specs=[seq_spec] * 3 + page_specs * 2,
            out_specs=seq_spec),
        out_shape=jax.ShapeDtypeStruct(q.shape, F32),
        compiler_params=pltpu.CompilerParams(
            dimension_semantics=("arbitrary",), vmem_limit_bytes=VMEM_LIMIT),
    )(page_table.reshape(-1), q, kn, vn, *([cache_kt] * n_pages), *([cache_vt] * n_pages))


def _outproj_kernel(att_ref, gmn_ref, x_ref, wo_ref, oga_ref, nfg_ref, wr_ref, br_ref,
                    x1_ref, h2_ref, comb_ref, bucket_ref, wpair_ref):
    attn = _rms(att_ref[...], oga_ref[...]).astype(BF16)
    mix = (jnp.dot(attn, wo_ref[:ATTN_WIDTH, :], preferred_element_type=F32)
           + jnp.dot(gmn_ref[...].astype(BF16), wo_ref[ATTN_WIDTH:, :], preferred_element_type=F32))
    x1 = x_ref[...] + mix
    x1_ref[...] = x1
    h2 = _rms(x1, nfg_ref[...])
    h2_hi = h2.astype(BF16)
    h2_ref[...] = h2_hi

    h2_lo = (h2 - h2_hi.astype(F32)).astype(BF16)
    hi_dot = jnp.dot(h2_hi, wr_ref[...], preferred_element_type=F32)
    lo_dot = jnp.dot(h2_lo, wr_ref[:, :LANES], preferred_element_type=F32)
    logits = hi_dot[:, :LANES] + hi_dot[:, LANES:] + lo_dot + br_ref[...]
    lt = logits.T
    tm = lt.shape[1]
    row4 = lax.broadcasted_iota(jnp.int32, (MOE_GROUPS, tm), 0)

    def first_argmax(v):
        vmax = jnp.max(v, axis=0, keepdims=True)
        idx = jnp.min(jnp.where(v == vmax, row4, MOE_GROUPS), axis=0, keepdims=True)
        return vmax, idx

    glog = lt[:MOE_GROUPS]
    ge = jnp.exp(glog - jnp.max(glog, axis=0, keepdims=True))
    gprob = ge / jnp.sum(ge, axis=0, keepdims=True)
    p_g, g_idx = first_argmax(gprob)
    elog = lt[MOE_GROUPS:MOE_GROUPS + EXPERTS_PER_GROUP]
    for g in range(1, MOE_GROUPS):
        lo = MOE_GROUPS + g * EXPERTS_PER_GROUP
        elog = jnp.where(g_idx == g, lt[lo:lo + EXPERTS_PER_GROUP], elog)
    l1, i1 = first_argmax(elog)
    l2, i2 = first_argmax(jnp.where(row4 == i1, NEG_INF, elog))
    e2 = jnp.exp(l2 - l1)
    denom = 1.0 + e2
    w1 = (1.0 / denom) * p_g
    w2 = (e2 / denom) * p_g
    lane_row = lax.broadcasted_iota(jnp.int32, (LANES, tm), 0)
    base = g_idx * EXPERTS_PER_GROUP
    comb_t = jnp.where(lane_row == base + i1, w1, 0.0) + jnp.where(lane_row == base + i2, w2, 0.0)
    comb_ref[...] = comb_t.T

    e_lo = jnp.minimum(i1, i2)
    e_hi = jnp.maximum(i1, i2)
    pair = jnp.where(e_lo == 0, e_hi - 1, jnp.where(e_lo == 1, e_hi + 1, PAIRS_PER_GROUP - 1))
    row8 = lax.broadcasted_iota(jnp.int32, (8, tm), 0)
    bucket_ref[...] = jnp.where(row8 == 0, g_idx * PAIRS_PER_GROUP + pair, 0)
    w_lo = jnp.where(i1 < i2, w1, w2)
    w_hi = jnp.where(i1 < i2, w2, w1)
    wpair_ref[...] = jnp.where(row8 == 0, w_lo, jnp.where(row8 == 1, w_hi, 0.0))


def _outproj(att, gmn, x, wo_bf, og_attn, nf_g, w_r, b_r, tm):
    rows = x.shape[0]
    row_spec = lambda width: pl.BlockSpec((tm, width), lambda i: (i, 0))
    full = lambda a: pl.BlockSpec(a.shape, lambda i: (0,) * a.ndim)
    return pl.pallas_call(
        _outproj_kernel,
        grid=(rows // tm,),
        in_specs=[row_spec(ATTN_WIDTH), row_spec(MLP_WIDTH), row_spec(D_MODEL), full(wo_bf),
                  full(og_attn), full(nf_g), full(w_r), full(b_r)],
        out_specs=[row_spec(D_MODEL), row_spec(D_MODEL), row_spec(LANES),
                   pl.BlockSpec((8, tm), lambda i: (0, i)), pl.BlockSpec((8, tm), lambda i: (0, i))],
        out_shape=[jax.ShapeDtypeStruct((rows, D_MODEL), F32),
                   jax.ShapeDtypeStruct((rows, D_MODEL), BF16),
                   jax.ShapeDtypeStruct((rows, LANES), F32),
                   jax.ShapeDtypeStruct((8, rows), jnp.int32),
                   jax.ShapeDtypeStruct((8, rows), F32)],
        compiler_params=pltpu.CompilerParams(
            dimension_semantics=("arbitrary",), vmem_limit_bytes=VMEM_LIMIT),
    )(att, gmn, x, wo_bf, og_attn, nf_g, w_r, b_r)


def _moe_kernel(h_ref, comb_ref, wg_ref, wu_ref, wd_ref, x1_ref, fg_ref, y_ref, acc_ref):
    e = pl.program_id(1)

    @pl.when(e == 0)
    def _():
        acc_ref[...] = jnp.zeros_like(acc_ref)

    h = h_ref[...]
    hg = jnp.dot(h, wg_ref[0], preferred_element_type=F32)
    hu = jnp.dot(h, wu_ref[0], preferred_element_type=F32)
    comb = comb_ref[...]
    lane = lax.broadcasted_iota(jnp.int32, comb.shape, 1)
    c = jnp.sum(jnp.where(lane == e, comb, 0.0), axis=-1, keepdims=True)
    act = hg * (1.0 / (1.0 + jnp.exp(-hg))) * hu * c
    acc_ref[...] += jnp.dot(act.astype(BF16), wd_ref[0], preferred_element_type=F32)

    @pl.when(e == N_EXPERTS - 1)
    def _():
        y_ref[...] = _rms(x1_ref[...] + acc_ref[...], fg_ref[...])


def _moe(h2, comb, wg_bf, wu_bf, wd_bf, x1, final_g, tm):
    rows = h2.shape[0]
    row_spec = lambda width: pl.BlockSpec((tm, width), lambda i, e: (i, 0))
    return pl.pallas_call(
        _moe_kernel,
        grid=(rows // tm, N_EXPERTS),
        in_specs=[row_spec(D_MODEL), row_spec(LANES),
                  pl.BlockSpec((1, D_MODEL, D_EXPERT), lambda i, e: (e, 0, 0)),
                  pl.BlockSpec((1, D_MODEL, D_EXPERT), lambda i, e: (e, 0, 0)),
                  pl.BlockSpec((1, D_EXPERT, D_MODEL), lambda i, e: (e, 0, 0)),
                  row_spec(D_MODEL),
                  pl.BlockSpec(final_g.shape, lambda i, e: (0, 0))],
        out_specs=row_spec(D_MODEL),
        out_shape=jax.ShapeDtypeStruct((rows, D_MODEL), F32),
        scratch_shapes=[pltpu.VMEM((tm, D_MODEL), F32)],
        compiler_params=pltpu.CompilerParams(
            dimension_semantics=("arbitrary", "arbitrary"), vmem_limit_bytes=VMEM_LIMIT),
    )(h2, comb, wg_bf, wu_bf, wd_bf, x1, final_g)


SC_CORES = 2
SC_SUBCORES = 16
SC_WINDOW = 32


def _sc_gather_rows(x, idx):
    n = idx.shape[0]
    width = x.shape[1]
    assert n % SC_WINDOW == 0
    mesh = plsc.VectorSubcoreMesh(core_axis_name="core", subcore_axis_name="subcore",
                                  num_cores=SC_CORES, num_subcores=SC_SUBCORES)

    @pl.kernel(out_type=jax.ShapeDtypeStruct((n, width), x.dtype), mesh=mesh)
    def gather_kernel(x_hbm, idx_hbm, out_hbm):
        def body(idx_vmem, out_vmem):
            pltpu.sync_copy(x_hbm.at[idx_vmem.at[0, pl.ds(0, SC_WINDOW)]], out_vmem)

        pltpu.emit_pipeline(
            body,
            grid=(n // SC_WINDOW,),
            in_specs=[pl.BlockSpec((1, LANES), lambda i: (i, 0))],
            out_specs=[pl.BlockSpec((SC_WINDOW, width), lambda i: (i, 0))],
            core_axis_name=("core", "subcore"),
            dimension_semantics=(pltpu.PARALLEL,),
        )(idx_hbm, out_hbm)

    idx_rows = jnp.pad(idx.reshape(n // SC_WINDOW, SC_WINDOW), ((0, 0), (0, LANES - SC_WINDOW)))
    return gather_kernel(x, idx_rows)


class MoePlan(NamedTuple):
    slot_row: jax.Array
    token_slot: jax.Array
    e_lo: jax.Array
    e_hi: jax.Array
    n_valid: jax.Array
    w_slots: jax.Array


def _moe_plan(bucket, wpair, tm):
    rows = bucket.shape[0]
    n_tiles_max = rows // tm + N_BUCKETS
    order = jnp.argsort(bucket, stable=True).astype(jnp.int32)
    position = jnp.argsort(order).astype(jnp.int32)
    counts = jnp.sum((bucket[:, None] == jnp.arange(N_BUCKETS)[None, :]).astype(jnp.int32), axis=0)
    starts = jnp.cumsum(counts) - counts
    tiles_b = (counts + tm - 1) // tm
    tile_end = jnp.cumsum(tiles_b)
    tile_start = tile_end - tiles_b
    n_tiles = tile_end[-1]
    token_slot = tile_start[bucket] * tm + (position - starts[bucket])
    t = jnp.arange(n_tiles_max, dtype=jnp.int32)
    tb = jnp.sum((jnp.minimum(t, n_tiles - 1)[:, None] >= tile_end[None, :]).astype(jnp.int32), axis=1)
    local = (t - tile_start[tb]) * tm
    n_valid = jnp.where(t < n_tiles, jnp.clip(counts[tb] - local, 0, tm), 0)
    slot = local[:, None] + jnp.arange(tm, dtype=jnp.int32)[None, :]
    src = (starts[tb][:, None] + slot) % rows
    slot_row = order[src].reshape(-1)
    group = tb // PAIRS_PER_GROUP
    pair = tb % PAIRS_PER_GROUP
    e_lo = group * EXPERTS_PER_GROUP + jnp.asarray(PAIR_LO, jnp.int32)[pair]
    e_hi = group * EXPERTS_PER_GROUP + jnp.asarray(PAIR_HI, jnp.int32)[pair]
    return MoePlan(slot_row, token_slot.astype(jnp.int32), e_lo, e_hi, n_valid.astype(jnp.int32),
                   wpair[:, slot_row].T)


def _moe_grouped_kernel(elo_ref, ehi_ref, nv_ref, x1_ref, ws_ref, wg_lo, wu_lo, wd_lo,
                        wg_hi, wu_hi, wd_hi, nfg_ref, fg_ref, y_ref):
    del elo_ref, ehi_ref
    t = pl.program_id(0)

    @pl.when(nv_ref[t] > 0)
    def _():
        x1 = x1_ref[...]
        h = _rms(x1, nfg_ref[...]).astype(BF16)
        ws = ws_ref[...]
        moe = jnp.zeros(x1.shape, F32)
        for col, (wg, wu, wd) in enumerate(((wg_lo, wu_lo, wd_lo), (wg_hi, wu_hi, wd_hi))):
            hg = jnp.dot(h, wg[0], preferred_element_type=F32)
            hu = jnp.dot(h, wu[0], preferred_element_type=F32)
            act = hg * (1.0 / (1.0 + jnp.exp(-hg))) * hu * ws[:, col:col + 1]
            moe = moe + jnp.dot(act.astype(BF16), wd[0], preferred_element_type=F32)
        y_ref[...] = _rms(x1 + moe, fg_ref[...])

    @pl.when(nv_ref[t] == 0)
    def _():
        y_ref[...] = jnp.zeros_like(y_ref)


def _moe_grouped(x1_slots, plan, wg_bf, wu_bf, wd_bf, nf_g, final_g, tm):
    n_steps = plan.n_valid.shape[0]
    lo_spec = lambda shape: pl.BlockSpec(shape, lambda t, elo, ehi, nv: (elo[t], 0, 0))
    hi_spec = lambda shape: pl.BlockSpec(shape, lambda t, elo, ehi, nv: (ehi[t], 0, 0))
    up_shape, down_shape = (1, D_MODEL, D_EXPERT), (1, D_EXPERT, D_MODEL)
    const = lambda a: pl.BlockSpec(a.shape, lambda t, elo, ehi, nv: (0, 0))
    row_spec = lambda width: pl.BlockSpec((tm, width), lambda t, elo, ehi, nv: (t, 0))
    return pl.pallas_call(
        _moe_grouped_kernel,
        grid_spec=pltpu.PrefetchScalarGridSpec(
            num_scalar_prefetch=3,
            grid=(n_steps,),
            in_specs=[row_spec(D_MODEL), row_spec(2),
                      lo_spec(up_shape), lo_spec(up_shape), lo_spec(down_shape),
                      hi_spec(up_shape), hi_spec(up_shape), hi_spec(down_shape),
                      const(nf_g), const(final_g)],
            out_specs=row_spec(D_MODEL)),
        out_shape=jax.ShapeDtypeStruct(x1_slots.shape, F32),
        compiler_params=pltpu.CompilerParams(
            dimension_semantics=("arbitrary",), vmem_limit_bytes=VMEM_LIMIT),
    )(plan.e_lo, plan.e_hi, plan.n_valid, x1_slots, plan.w_slots, wg_bf, wu_bf, wd_bf,
      wg_bf, wu_bf, wd_bf, nf_g, final_g)


def _spatial_operands(w_s, b_s, t_chunk):
    reps = CHUNK // t_chunk
    idx = jnp.arange(CHUNK)
    same = (idx[:, None] // t_chunk) == (idx[None, :] // t_chunk)
    causal = (idx[None, :] % t_chunk) <= (idx[:, None] % t_chunk)
    wm = jnp.tile(w_s[:, :t_chunk, :t_chunk], (1, reps, reps)) * (same & causal)
    wm_cat = wm.transpose(1, 0, 2).reshape(CHUNK, MLP_GROUPS * CHUNK).astype(BF16)
    bias = jnp.repeat(jnp.tile(b_s[:, :t_chunk], (1, reps)).T, MLP_CH, axis=1)
    return wm_cat, bias


def kernel(x_prompt, x_sample, cache_k, cache_v, page_table, norm_attn_g, w_in, sgu_g, w_spatial,
           b_spatial, out_g_attn, out_g_mlp, w_out, norm_ffn_g, w_group, b_group, w_router, b_router,
           w_gate, w_up, w_down, final_g):
    depth = w_in.shape[0]
    assert depth == 1, "single decoder layer"
    batch, seq, _ = x_prompt.shape
    n_seq, t_new, _ = x_sample.shape
    assert seq % MOBA_BLOCK == 0 and seq % CHUNK == 0 and CHUNK % t_new == 0
    assert MOBA_BLOCK % cache_k.shape[2] == 0

    row2 = lambda a: a.reshape(1, -1)
    w_in_bf = w_in[0].astype(BF16)
    w_out_bf = w_out[0].astype(BF16)
    wg_bf, wu_bf, wd_bf = w_gate[0].astype(BF16), w_up[0].astype(BF16), w_down[0].astype(BF16)
    n_logits = MOE_GROUPS + N_EXPERTS
    w_r = jnp.concatenate(
        [w_group[0], w_router[0].transpose(1, 0, 2).reshape(D_MODEL, N_EXPERTS),
         jnp.zeros((D_MODEL, LANES - n_logits), F32)], axis=1)
    b_r = jnp.concatenate(
        [b_group[0], b_router[0].reshape(-1), jnp.zeros((LANES - n_logits,), F32)]).reshape(1, LANES)
    w_r_hi = w_r.astype(BF16)
    w_r_hl = jnp.concatenate([w_r_hi, (w_r - w_r_hi.astype(F32)).astype(BF16)], axis=1)
    wkv_t_bf = w_in_bf[:, ATTN_WIDTH:3 * ATTN_WIDTH].T
    ck_t = jnp.transpose(cache_k[0], (0, 2, 3, 1))
    cv_t = jnp.transpose(cache_v[0], (0, 2, 3, 1))

    def project_in(x, t_chunk, tm, seq_transposed):
        wm_cat, bias_full = _spatial_operands(w_spatial[0], b_spatial[0], t_chunk)
        return _inproj(x, row2(norm_attn_g[0]), w_in_bf, wkv_t_bf, row2(sgu_g[0]), wm_cat, bias_full,
                       row2(out_g_mlp[0]), tm, seq_transposed)

    def project_out(att, gmn, x, tm):
        return _outproj(att, gmn, x, w_out_bf, row2(out_g_attn[0]), row2(norm_ffn_g[0]), w_r_hl, b_r,
                        tm)

    xp = x_prompt.reshape(batch * seq, D_MODEL)
    xp, wg_bf, wu_bf, wd_bf = lax.optimization_barrier((xp, wg_bf, wu_bf, wd_bf))
    qp, kp_t, vp_t, gmn_p, _ = project_in(xp, CHUNK, TM_PROJ_PROMPT, seq)
    att_p = _attn_prompt(qp, kp_t, vp_t, batch, seq)
    x1_p, _, _, bucket_p, wpair_p = project_out(att_p, gmn_p, xp, TM_PROJ_PROMPT)
    plan = _moe_plan(bucket_p[0], wpair_p[:2], TM_MOE_PROMPT)
    x1_slots = _sc_gather_rows(x1_p, plan.slot_row)

    xs = x_sample.reshape(n_seq * t_new, D_MODEL)
    qs, ks, vs, gmn_s, gvs = project_in(xs, t_new, TM_PROJ_SAMPLE, None)
    seq_shape = (n_seq, t_new, ATTN_WIDTH)
    qs3, ks3, vs3 = qs.reshape(seq_shape), ks.reshape(seq_shape), vs.reshape(seq_shape)
    half = n_seq // 2
    attend_half = lambda q_half, lo: _attn_sample(q_half, ks3[lo:lo + half], vs3[lo:lo + half],
                                                  ck_t, cv_t, page_table[lo:lo + half])
    att_s0 = attend_half(qs3[:half], 0)
    x1_slots, att_s0 = lax.optimization_barrier((x1_slots, att_s0))
    y_slots = _moe_grouped(x1_slots, plan, wg_bf, wu_bf, wd_bf, row2(norm_ffn_g[0]),
                           row2(final_g), TM_MOE_PROMPT)
    y_slots, q_half1 = lax.optimization_barrier((y_slots, qs3[half:]))
    yp = _sc_gather_rows(y_slots, plan.token_slot)
    att_s1 = attend_half(q_half1, half)
    att_s = jnp.concatenate([att_s0, att_s1], axis=0).reshape(n_seq * t_new, ATTN_WIDTH)
    x1_s, h2_s, comb_s, _, _ = project_out(att_s, gmn_s, xs, TM_PROJ_SAMPLE)
    ys = _moe(h2_s, comb_s, wg_bf, wu_bf, wd_bf, x1_s, row2(final_g), TM_MOE_SAMPLE)

    heads = (ATTN_HEADS, HEAD_DIM)
    rows_last = lambda a_t: a_t.reshape(batch, *heads, seq).transpose(0, 3, 1, 2)[None]
    return (yp.reshape(batch, seq, D_MODEL),
            ys.reshape(n_seq, t_new, D_MODEL),
            rows_last(kp_t),
            rows_last(vp_t),
            ks.reshape(depth, n_seq, t_new, *heads),
            vs.reshape(depth, n_seq, t_new, *heads),
            gvs.reshape(depth, n_seq, t_new, MLP_WIDTH))
```

```python
import functools
import math
from typing import NamedTuple

import jax
import jax.numpy as jnp
from jax import lax
from jax.experimental import pallas as pl
from jax.experimental.pallas import tpu as pltpu
from jax.experimental.pallas import tpu_sc as plsc

D_MODEL = 1024
ATTN_HEADS = 8
HEAD_DIM = 64
ATTN_WIDTH = ATTN_HEADS * HEAD_DIM
MOBA_BLOCK = 256
MOBA_TOPK = 3
QUERY_BLOCK = 128
MLP_GROUPS = 8
MLP_CH = 64
MLP_WIDTH = MLP_GROUPS * MLP_CH
CHUNK = 128
IN_WIDTH = 3 * ATTN_WIDTH + 2 * MLP_WIDTH
MOE_GROUPS = 4
EXPERTS_PER_GROUP = 4
N_EXPERTS = MOE_GROUPS * EXPERTS_PER_GROUP
D_EXPERT = D_MODEL // 2
EPS = 1e-6
PAIR_LO = (0, 0, 0, 1, 1, 2)
PAIR_HI = (1, 2, 3, 2, 3, 3)
PAIRS_PER_GROUP = len(PAIR_LO)
N_BUCKETS = MOE_GROUPS * PAIRS_PER_GROUP

LANES = 128
BF16_ROWS = 16
VMEM_LIMIT = 56 * 1024 * 1024

TM_PROJ_PROMPT = 512
TM_PROJ_SAMPLE = 128
TM_MOE_PROMPT = 256
TM_MOE_SAMPLE = 512

F32 = jnp.float32
BF16 = jnp.bfloat16
NEG_INF = float("-inf")
_NT = (((1,), (1,)), ((), ()))


def _rms(x, g):
    return x * lax.rsqrt(jnp.mean(x * x, axis=-1, keepdims=True) + EPS) * g


def _gelu(x):
    return 0.5 * x * (1.0 + jnp.tanh(0.7978845608028654 * (x + 0.044715 * (x * x * x))))


def _top_rank_select(gate, n_past, n_keep, axis):
    nb = gate.shape[axis]
    n_idx = lax.broadcasted_iota(jnp.int32, gate.shape, axis)
    rank = jnp.zeros(gate.shape, jnp.int32)
    for m in range(nb):
        gm = gate[:, m:m + 1] if axis == 1 else gate[m:m + 1, :]
        beats = jnp.where(gm > gate, 1, jnp.where(gm == gate, jnp.where(m < n_idx, 1, 0), 0))
        rank = rank + jnp.where(m < n_past, beats, 0)
    return jnp.where(n_idx < n_past, rank, n_keep) < n_keep


def _inproj_kernel(x_ref, g_ref, w_ref, wkv_t_ref, sgu_ref, wm_ref, bias_ref, og_ref,
                   q_ref, k_ref, v_ref, gmn_ref, vgn_ref, *, n_chunks, kv_transposed):
    h = _rms(x_ref[...], g_ref[...]).astype(BF16)

    def proj(lo, width):
        return jnp.dot(h, w_ref[:, lo:lo + width], preferred_element_type=F32)

    q_ref[...] = proj(0, ATTN_WIDTH)
    if kv_transposed:
        kv_t = lax.dot_general(wkv_t_ref[...], h, _NT, preferred_element_type=F32)
        k_ref[0] = kv_t[:ATTN_WIDTH]
        v_ref[0] = kv_t[ATTN_WIDTH:]
    else:
        k_ref[...] = proj(ATTN_WIDTH, ATTN_WIDTH)
        v_ref[...] = proj(2 * ATTN_WIDTH, ATTN_WIDTH)
    gu = _gelu(proj(3 * ATTN_WIDTH, MLP_WIDTH))
    vgn = _rms(_gelu(proj(3 * ATTN_WIDTH + MLP_WIDTH, MLP_WIDTH)), sgu_ref[...])
    vgn_ref[...] = vgn

    lane_grp = lax.broadcasted_iota(jnp.int32, (CHUNK, MLP_WIDTH), 1) // MLP_CH
    for c in range(n_chunks):
        rows = slice(c * CHUNK, (c + 1) * CHUNK)
        vc = vgn[rows].astype(BF16)
        vbd = jnp.concatenate(
            [jnp.where(lane_grp == g, vc, jnp.zeros_like(vc)) for g in range(MLP_GROUPS)], axis=0)
        mixed = jnp.dot(wm_ref[...], vbd, preferred_element_type=F32) + bias_ref[...]
        gmn_ref[rows, :] = _rms(gu[rows] * mixed, og_ref[...]).astype(BF16)


def _inproj(x, g, w_bf, wkv_t_bf, sgu_g, wm_cat, bias_full, og_mlp, tm, seq_transposed=None):
    rows = x.shape[0]
    row_spec = lambda width: pl.BlockSpec((tm, width), lambda i: (i, 0))
    full = lambda a: pl.BlockSpec(a.shape, lambda i: (0,) * a.ndim)
    out = jax.ShapeDtypeStruct((rows, ATTN_WIDTH), F32)
    kv_spec, kv_out = row_spec(ATTN_WIDTH), out
    if seq_transposed is not None:
        tiles = seq_transposed // tm
        kv_spec = pl.BlockSpec((1, ATTN_WIDTH, tm), lambda i: (i // tiles, 0, i % tiles))
        kv_out = jax.ShapeDtypeStruct((rows // seq_transposed, ATTN_WIDTH, seq_transposed), F32)
    return pl.pallas_call(
        functools.partial(_inproj_kernel, n_chunks=tm // CHUNK,
                          kv_transposed=seq_transposed is not None),
        grid=(rows // tm,),
        in_specs=[row_spec(D_MODEL), full(g), full(w_bf), full(wkv_t_bf), full(sgu_g), full(wm_cat),
                  full(bias_full), full(og_mlp)],
        out_specs=[row_spec(ATTN_WIDTH), kv_spec, kv_spec, row_spec(ATTN_WIDTH),
                   row_spec(ATTN_WIDTH)],
        out_shape=[out, kv_out, kv_out, jax.ShapeDtypeStruct((rows, MLP_WIDTH), BF16), out],
        compiler_params=pltpu.CompilerParams(
            dimension_semantics=("arbitrary",), vmem_limit_bytes=VMEM_LIMIT),
    )(x, g, w_bf, wkv_t_bf, sgu_g, wm_cat, bias_full, og_mlp)


V_ROWS = HEAD_DIM + BF16_ROWS
LOG2_E = 1.4426950408889634
MASKED = -1e30
Q_TILE = MOBA_BLOCK
COL_R, COL_J, COL_SEL = 0, 2, 8


def _bf16_split(x):
    mantissa, exponent = math.frexp(x)
    high = math.ldexp(round(mantissa * 256.0) / 256.0, exponent)
    return high, x - high


def _attn_prompt_kernel(q_ref, k_ref, v_ref, og_ref, o_ref, ka_scr, vt_scr, kmean_scr, qa_scr,
                        m_scr, alpha_scr, acc_scr, s_scr, *, n_blocks):
    cur = pl.program_id(1)

    @pl.when(cur == 0)
    def _():
        key = lax.broadcasted_iota(jnp.int32, (MOBA_BLOCK, HEAD_DIM), 0)
        col = lax.broadcasted_iota(jnp.int32, (MOBA_BLOCK, HEAD_DIM), 1)
        ones = jnp.ones((BF16_ROWS, MOBA_BLOCK), BF16)
        for n in range(n_blocks):
            keys = slice(n * MOBA_BLOCK, (n + 1) * MOBA_BLOCK)
            kb = k_ref[0, :, keys].T
            kmean_scr[n:n + 1, :] = jnp.sum(kb, axis=0, keepdims=True) * (1.0 / MOBA_BLOCK)
            vt = v_ref[0, :, keys]
            extra = jnp.where(col < COL_J, key,
                              jnp.where(col < COL_SEL, n, jnp.where(col == COL_SEL + n, 1, 0)))
            extra = extra.astype(F32).astype(BF16)
            for h in range(ATTN_HEADS):
                lanes = slice(h * HEAD_DIM, (h + 1) * HEAD_DIM)
                ka_scr[h, n] = jnp.concatenate([kb[:, lanes].astype(BF16), extra], axis=1)
                vt_scr[h, n, :HEAD_DIM, :] = vt[lanes, :].astype(BF16)
                vt_scr[h, n, HEAD_DIM:, :] = ones

    q_t = q_ref[...].T
    row = lax.broadcasted_iota(jnp.int32, (HEAD_DIM, Q_TILE), 0)
    blk = lax.broadcasted_iota(jnp.int32, (n_blocks, Q_TILE), 0)
    for h in range(ATTN_HEADS):
        lanes = slice(h * HEAD_DIM, (h + 1) * HEAD_DIM)
        hi, lo = _bf16_split(LOG2_E * 2.0 ** (-8.0 * (h + 1) / ATTN_HEADS))
        qh_t = q_t[lanes, :]
        gate = jnp.dot(kmean_scr[:, lanes], qh_t, precision=lax.Precision.HIGHEST,
                       preferred_element_type=F32)
        keep = _top_rank_select(gate, cur, MOBA_TOPK, 0) | (blk >= cur)
        sel_rows = jnp.where(keep, 0.0, MASKED)
        alibi = jnp.where(row == COL_R, hi,
                          jnp.where(row == COL_R + 1, lo,
                                    jnp.where(row == COL_J, hi * MOBA_BLOCK,
                                              jnp.where(row == COL_J + 1, lo * MOBA_BLOCK, 0.0))))
        extra = alibi + jnp.concatenate(
            [jnp.zeros((COL_SEL, Q_TILE), F32), sel_rows,
             jnp.zeros((HEAD_DIM - COL_SEL - n_blocks, Q_TILE), F32)], axis=0)
        qa_scr[h] = jnp.concatenate([qh_t * (LOG2_E * HEAD_DIM ** -0.5), extra], axis=0).astype(BF16)

    def attend_block(j, causal, first):
        for h in range(ATTN_HEADS):
            s = jnp.dot(ka_scr[h, j], qa_scr[h], preferred_element_type=F32)
            if causal is not None:
                s = jnp.where(causal, s, NEG_INF)
            s_scr[h] = s
            m_blk = jnp.max(s, axis=0, keepdims=True)
            if first:
                m_scr[h:h + 1, :] = m_blk
            else:
                m_old = m_scr[h:h + 1, :]
                m_new = jnp.maximum(m_old, m_blk)
                alpha_scr[h:h + 1, :] = jnp.exp2(m_old - m_new)
                m_scr[h:h + 1, :] = m_new
        for h in range(ATTN_HEADS):
            p = jnp.exp2(s_scr[h] - m_scr[h:h + 1, :]).astype(BF16)
            pv = jnp.dot(vt_scr[h, j], p, preferred_element_type=F32)
            acc_scr[h] = pv if first else alpha_scr[h:h + 1, :] * acc_scr[h] + pv

    causal = (lax.broadcasted_iota(jnp.int32, (MOBA_BLOCK, Q_TILE), 0)
              <= lax.broadcasted_iota(jnp.int32, (MOBA_BLOCK, Q_TILE), 1))
    attend_block(cur, causal, True)

    def past_block(j, carry):
        attend_block(j, None, False)
        return carry

    lax.fori_loop(0, cur, past_block, 0)

    outs = []
    for h in range(ATTN_HEADS):
        acc = acc_scr[h]
        outs.append(acc[:HEAD_DIM] / acc[HEAD_DIM:HEAD_DIM + 1])
    o_ref[...] = _rms(jnp.concatenate(outs, axis=0).T, og_ref[...]).astype(BF16)


def _attn_prompt(q, k_t, v_t, og_attn, batch, seq):
    n_blocks = seq // MOBA_BLOCK
    assert COL_SEL + n_blocks <= HEAD_DIM and n_blocks % 8 == 0
    q_spec = pl.BlockSpec((Q_TILE, ATTN_WIDTH), lambda b, i: (b * n_blocks + i, 0))
    return pl.pallas_call(
        functools.partial(_attn_prompt_kernel, n_blocks=n_blocks),
        grid=(batch, n_blocks),
        in_specs=[q_spec,
                  pl.BlockSpec((1, ATTN_WIDTH, seq), lambda b, i: (b, 0, 0)),
                  pl.BlockSpec((1, ATTN_WIDTH, seq), lambda b, i: (b, 0, 0)),
                  pl.BlockSpec(og_attn.shape, lambda b, i: (0, 0))],
        out_specs=q_spec,
        out_shape=jax.ShapeDtypeStruct(q.shape, BF16),
        scratch_shapes=[pltpu.VMEM((ATTN_HEADS, n_blocks, MOBA_BLOCK, 2 * HEAD_DIM), BF16),
                        pltpu.VMEM((ATTN_HEADS, n_blocks, V_ROWS, MOBA_BLOCK), BF16),
                        pltpu.VMEM((n_blocks, ATTN_WIDTH), F32),
                        pltpu.VMEM((ATTN_HEADS, 2 * HEAD_DIM, Q_TILE), BF16),
                        pltpu.VMEM((ATTN_HEADS, Q_TILE), F32),
                        pltpu.VMEM((ATTN_HEADS, Q_TILE), F32),
                        pltpu.VMEM((ATTN_HEADS, V_ROWS, Q_TILE), F32),
                        pltpu.VMEM((ATTN_HEADS, MOBA_BLOCK, Q_TILE), F32)],
        compiler_params=pltpu.CompilerParams(
            dimension_semantics=("arbitrary", "arbitrary"), vmem_limit_bytes=VMEM_LIMIT),
    )(q, k_t, v_t, og_attn)


SEQS_PER_STEP = 2


def _attn_sample_kernel(pt_ref, q_ref, kn_ref, vn_ref, og_ref, *refs, n_pages, page, t_new):
    del pt_ref
    o_ref = refs[2 * SEQS_PER_STEP * n_pages]
    for i in range(SEQS_PER_STEP):
        kp = refs[i * n_pages:(i + 1) * n_pages]
        vp = refs[(SEQS_PER_STEP + i) * n_pages:(SEQS_PER_STEP + i + 1) * n_pages]
        att = _attend_one_sample(q_ref[i], kn_ref[i], vn_ref[i], kp, vp, page, t_new)
        o_ref[i] = _rms(att, og_ref[...])


def _attend_one_sample(q, kn, vn, kp, vp, page, t_new):
    n_pages = len(kp)
    n_cols = t_new * ATTN_HEADS
    pages_per_block = MOBA_BLOCK // page
    n_past = n_pages // pages_per_block
    past_len = n_pages * page

    lane_h = lax.broadcasted_iota(jnp.int32, (n_cols, ATTN_WIDTH), 1) // HEAD_DIM
    row = lax.broadcasted_iota(jnp.int32, (n_cols, 1), 0)
    row_h = row % ATTN_HEADS
    row_t = row // ATTN_HEADS
    own_head = lane_h == row_h
    qrep = jnp.concatenate(
        [jnp.broadcast_to(q[t:t + 1, :], (ATTN_HEADS, ATTN_WIDTH)) for t in range(t_new)], axis=0)
    qbd = jnp.where(own_head, qrep, 0.0)
    qbd_s = (qbd * (HEAD_DIM ** -0.5)).astype(BF16)
    slope = jnp.exp2(-8.0 * (row_h + 1).astype(F32) / ATTN_HEADS)

    def block_pages(refs_, n):
        return [refs_[i][0].reshape(ATTN_WIDTH, page)
                for i in range(n * pages_per_block, (n + 1) * pages_per_block)]

    key_off = lax.broadcasted_iota(jnp.int32, (1, MOBA_BLOCK), 1)
    k_sums, s_past = [], []
    for n in range(n_past):
        pages = block_pages(kp, n)
        k_sums.append(jnp.sum(sum(pages), axis=1, keepdims=True))
        kb_t = jnp.concatenate(pages, axis=1).astype(BF16)
        dist = (past_len + row_t) - (n * MOBA_BLOCK + key_off)
        s_past.append(jnp.dot(qbd_s, kb_t, preferred_element_type=F32) - slope * dist.astype(F32))
    kmean = jnp.concatenate(k_sums, axis=1) * (1.0 / MOBA_BLOCK)
    gate = jnp.dot(qbd, kmean, precision=lax.Precision.HIGHEST, preferred_element_type=F32)
    sel = _top_rank_select(gate, n_past, MOBA_TOPK, 1)

    s_own = lax.dot_general(qbd_s, kn.astype(BF16), _NT, preferred_element_type=F32)
    dist = row_t - lax.broadcasted_iota(jnp.int32, (1, t_new), 1)
    s_own = jnp.where(dist >= 0, s_own - slope * dist.astype(F32), NEG_INF)
    m = jnp.max(s_own, axis=-1, keepdims=True)
    for n in range(n_past):
        s_past[n] = jnp.where(sel[:, n:n + 1], s_past[n], NEG_INF)
        m = jnp.maximum(m, jnp.max(s_past[n], axis=-1, keepdims=True))

    p = jnp.exp(s_own - m)
    l = jnp.sum(p, axis=-1, keepdims=True)
    acc = sum(p[:, t:t + 1] * vn[t:t + 1, :] for t in range(t_new))
    for n in range(n_past):
        vb_t = jnp.concatenate(block_pages(vp, n), axis=1).astype(BF16)
        p = jnp.exp(s_past[n] - m)
        l = l + jnp.sum(p, axis=-1, keepdims=True)
        acc = acc + lax.dot_general(p.astype(BF16), vb_t, _NT, preferred_element_type=F32)

    out = jnp.where(own_head, acc / l, 0.0)
    return jnp.sum(out.reshape(t_new, ATTN_HEADS, ATTN_WIDTH), axis=1)


def _attn_sample(q, kn, vn, og_attn, cache_kt, cache_vt, page_table):
    n_seq, t_new, _ = q.shape
    n_pages = page_table.shape[1]
    page = cache_kt.shape[-1]
    assert n_seq % SEQS_PER_STEP == 0
    seq_spec = pl.BlockSpec((SEQS_PER_STEP, t_new, ATTN_WIDTH), lambda i, pt: (i, 0, 0))

    def page_spec(s, p):
        return pl.BlockSpec(
            (1, ATTN_HEADS, HEAD_DIM, page),
            lambda i, pt: (pt[(i * SEQS_PER_STEP + s) * n_pages + p], 0, 0, 0))

    page_specs = [page_spec(s, p) for s in range(SEQS_PER_STEP) for p in range(n_pages)]
    n_refs = len(page_specs)
    return pl.pallas_call(
        functools.partial(_attn_sample_kernel, n_pages=n_pages, page=page, t_new=t_new),
        grid_spec=pltpu.PrefetchScalarGridSpec(
            num_scalar_prefetch=1,
            grid=(n_seq // SEQS_PER_STEP,),
            in_specs=([seq_spec] * 3 + [pl.BlockSpec(og_attn.shape, lambda i, pt: (0, 0))]
                      + page_specs * 2),
            out_specs=seq_spec),
        out_shape=jax.ShapeDtypeStruct(q.shape, F32),
        compiler_params=pltpu.CompilerParams(
            dimension_semantics=("arbitrary",), vmem_limit_bytes=VMEM_LIMIT),
    )(page_table.reshape(-1), q, kn, vn, og_attn, *([cache_kt] * n_refs), *([cache_vt] * n_refs))


def _outproj_kernel(att_ref, gmn_ref, x_ref, wo_ref, nfg_ref, wr_ref, br_ref, x1_ref, *route_refs,
                    grouped):
    mix = (jnp.dot(att_ref[...].astype(BF16), wo_ref[:ATTN_WIDTH, :], preferred_element_type=F32)
           + jnp.dot(gmn_ref[...], wo_ref[ATTN_WIDTH:, :], preferred_element_type=F32))
    x1 = x_ref[...] + mix
    x1_ref[...] = x1
    h2 = _rms(x1, nfg_ref[...])
    h2_hi = h2.astype(BF16)

    h2_lo = (h2 - h2_hi.astype(F32)).astype(BF16)
    hi_dot = jnp.dot(h2_hi, wr_ref[...], preferred_element_type=F32)
    lo_dot = jnp.dot(h2_lo, wr_ref[:, :LANES], preferred_element_type=F32)
    logits = hi_dot[:, :LANES] + hi_dot[:, LANES:] + lo_dot + br_ref[...]
    lt = logits.T
    tm = lt.shape[1]
    row4 = lax.broadcasted_iota(jnp.int32, (MOE_GROUPS, tm), 0)

    def first_argmax(v):
        vmax = jnp.max(v, axis=0, keepdims=True)
        idx = jnp.min(jnp.where(v == vmax, row4, MOE_GROUPS), axis=0, keepdims=True)
        return vmax, idx

    glog = lt[:MOE_GROUPS]
    ge = jnp.exp(glog - jnp.max(glog, axis=0, keepdims=True))
    gprob = ge / jnp.sum(ge, axis=0, keepdims=True)
    p_g, g_idx = first_argmax(gprob)
    elog = lt[MOE_GROUPS:MOE_GROUPS + EXPERTS_PER_GROUP]
    for g in range(1, MOE_GROUPS):
        lo = MOE_GROUPS + g * EXPERTS_PER_GROUP
        elog = jnp.where(g_idx == g, lt[lo:lo + EXPERTS_PER_GROUP], elog)
    l1, i1 = first_argmax(elog)
    l2, i2 = first_argmax(jnp.where(row4 == i1, NEG_INF, elog))
    e2 = jnp.exp(l2 - l1)
    denom = 1.0 + e2
    w1 = (1.0 / denom) * p_g
    w2 = (e2 / denom) * p_g
    if not grouped:
        h2_ref, comb_ref = route_refs
        h2_ref[...] = h2_hi
        lane_row = lax.broadcasted_iota(jnp.int32, (LANES, tm), 0)
        base = g_idx * EXPERTS_PER_GROUP
        comb_t = (jnp.where(lane_row == base + i1, w1, 0.0)
                  + jnp.where(lane_row == base + i2, w2, 0.0))
        comb_ref[...] = comb_t.T
        return

    bucket_ref, wpair_ref = route_refs
    e_lo = jnp.minimum(i1, i2)
    e_hi = jnp.maximum(i1, i2)
    pair = jnp.where(e_lo == 0, e_hi - 1, jnp.where(e_lo == 1, e_hi + 1, PAIRS_PER_GROUP - 1))
    row8 = lax.broadcasted_iota(jnp.int32, (8, tm), 0)
    bucket_ref[...] = jnp.where(row8 == 0, g_idx * PAIRS_PER_GROUP + pair, 0)
    w_lo = jnp.where(i1 < i2, w1, w2)
    w_hi = jnp.where(i1 < i2, w2, w1)
    wpair_ref[...] = jnp.where(row8 == 0, w_lo, jnp.where(row8 == 1, w_hi, 0.0))


def _outproj(att_n, gmn, x, wo_bf, nf_g, w_r, b_r, tm, grouped):
    rows = x.shape[0]
    row_spec = lambda width: pl.BlockSpec((tm, width), lambda i: (i, 0))
    full = lambda a: pl.BlockSpec(a.shape, lambda i: (0,) * a.ndim)
    lane_spec = pl.BlockSpec((8, tm), lambda i: (0, i))
    if grouped:
        route_specs = [lane_spec, lane_spec]
        route_shapes = [jax.ShapeDtypeStruct((8, rows), jnp.int32),
                        jax.ShapeDtypeStruct((8, rows), F32)]
    else:
        route_specs = [row_spec(D_MODEL), row_spec(LANES)]
        route_shapes = [jax.ShapeDtypeStruct((rows, D_MODEL), BF16),
                        jax.ShapeDtypeStruct((rows, LANES), F32)]
    return pl.pallas_call(
        functools.partial(_outproj_kernel, grouped=grouped),
        grid=(rows // tm,),
        in_specs=[row_spec(ATTN_WIDTH), row_spec(MLP_WIDTH), row_spec(D_MODEL), full(wo_bf),
                  full(nf_g), full(w_r), full(b_r)],
        out_specs=[row_spec(D_MODEL)] + route_specs,
        out_shape=[jax.ShapeDtypeStruct((rows, D_MODEL), F32)] + route_shapes,
        compiler_params=pltpu.CompilerParams(
            dimension_semantics=("arbitrary",), vmem_limit_bytes=VMEM_LIMIT),
    )(att_n, gmn, x, wo_bf, nf_g, w_r, b_r)


def _moe_kernel(h_ref, comb_ref, wg_ref, wu_ref, wd_ref, x1_ref, fg_ref, y_ref, acc_ref):
    e = pl.program_id(1)

    @pl.when(e == 0)
    def _():
        acc_ref[...] = jnp.zeros_like(acc_ref)

    h = h_ref[...]
    hg = jnp.dot(h, wg_ref[0], preferred_element_type=F32)
    hu = jnp.dot(h, wu_ref[0], preferred_element_type=F32)
    comb = comb_ref[...]
    lane = lax.broadcasted_iota(jnp.int32, comb.shape, 1)
    c = jnp.sum(jnp.where(lane == e, comb, 0.0), axis=-1, keepdims=True)
    act = hg * (1.0 / (1.0 + jnp.exp(-hg))) * hu * c
    acc_ref[...] += jnp.dot(act.astype(BF16), wd_ref[0], preferred_element_type=F32)

    @pl.when(e == N_EXPERTS - 1)
    def _():
        y_ref[...] = _rms(x1_ref[...] + acc_ref[...], fg_ref[...])


def _moe(h2, comb, wg_bf, wu_bf, wd_bf, x1, final_g, tm):
    rows = h2.shape[0]
    row_spec = lambda width: pl.BlockSpec((tm, width), lambda i, e: (i, 0))
    return pl.pallas_call(
        _moe_kernel,
        grid=(rows // tm, N_EXPERTS),
        in_specs=[row_spec(D_MODEL), row_spec(LANES),
                  pl.BlockSpec((1, D_MODEL, D_EXPERT), lambda i, e: (e, 0, 0)),
                  pl.BlockSpec((1, D_MODEL, D_EXPERT), lambda i, e: (e, 0, 0)),
                  pl.BlockSpec((1, D_EXPERT, D_MODEL), lambda i, e: (e, 0, 0)),
                  row_spec(D_MODEL),
                  pl.BlockSpec(final_g.shape, lambda i, e: (0, 0))],
        out_specs=row_spec(D_MODEL),
        out_shape=jax.ShapeDtypeStruct((rows, D_MODEL), F32),
        scratch_shapes=[pltpu.VMEM((tm, D_MODEL), F32)],
        compiler_params=pltpu.CompilerParams(
            dimension_semantics=("arbitrary", "arbitrary"), vmem_limit_bytes=VMEM_LIMIT),
    )(h2, comb, wg_bf, wu_bf, wd_bf, x1, final_g)


SC_CORES = 2
SC_SUBCORES = 16
SC_WINDOW = 32


def _sc_gather_rows(x, idx):
    n = idx.shape[0]
    width = x.shape[1]
    assert n % SC_WINDOW == 0
    mesh = plsc.VectorSubcoreMesh(core_axis_name="core", subcore_axis_name="subcore",
                                  num_cores=SC_CORES, num_subcores=SC_SUBCORES)

    @pl.kernel(out_type=jax.ShapeDtypeStruct((n, width), x.dtype), mesh=mesh)
    def gather_kernel(x_hbm, idx_hbm, out_hbm):
        def body(idx_vmem, out_vmem):
            pltpu.sync_copy(x_hbm.at[idx_vmem.at[0, pl.ds(0, SC_WINDOW)]], out_vmem)

        pltpu.emit_pipeline(
            body,
            grid=(n // SC_WINDOW,),
            in_specs=[pl.BlockSpec((1, LANES), lambda i: (i, 0))],
            out_specs=[pl.BlockSpec((SC_WINDOW, width), lambda i: (i, 0))],
            core_axis_name=("core", "subcore"),
            dimension_semantics=(pltpu.PARALLEL,),
        )(idx_hbm, out_hbm)

    idx_rows = jnp.pad(idx.reshape(n // SC_WINDOW, SC_WINDOW), ((0, 0), (0, LANES - SC_WINDOW)))
    return gather_kernel(x, idx_rows)


class MoePlan(NamedTuple):
    slot_row: jax.Array
    token_slot: jax.Array
    e_lo: jax.Array
    e_hi: jax.Array
    n_valid: jax.Array
    w_slots: jax.Array


def _moe_plan(bucket, wpair, tm):
    rows = bucket.shape[0]
    n_tiles_max = rows // tm + N_BUCKETS
    order = jnp.argsort(bucket, stable=True).astype(jnp.int32)
    position = jnp.argsort(order).astype(jnp.int32)
    counts = jnp.sum((bucket[:, None] == jnp.arange(N_BUCKETS)[None, :]).astype(jnp.int32), axis=0)
    starts = jnp.cumsum(counts) - counts
    tiles_b = (counts + tm - 1) // tm
    tile_end = jnp.cumsum(tiles_b)
    tile_start = tile_end - tiles_b
    n_tiles = tile_end[-1]
    token_slot = tile_start[bucket] * tm + (position - starts[bucket])
    t = jnp.arange(n_tiles_max, dtype=jnp.int32)
    tb = jnp.sum((jnp.minimum(t, n_tiles - 1)[:, None] >= tile_end[None, :]).astype(jnp.int32), axis=1)
    local = (t - tile_start[tb]) * tm
    n_valid = jnp.where(t < n_tiles, jnp.clip(counts[tb] - local, 0, tm), 0)
    slot = local[:, None] + jnp.arange(tm, dtype=jnp.int32)[None, :]
    src = (starts[tb][:, None] + slot) % rows
    slot_row = order[src].reshape(-1)
    group = tb // PAIRS_PER_GROUP
    pair = tb % PAIRS_PER_GROUP
    e_lo = group * EXPERTS_PER_GROUP + jnp.asarray(PAIR_LO, jnp.int32)[pair]
    e_hi = group * EXPERTS_PER_GROUP + jnp.asarray(PAIR_HI, jnp.int32)[pair]
    return MoePlan(slot_row, token_slot.astype(jnp.int32), e_lo, e_hi, n_valid.astype(jnp.int32),
                   wpair[:, slot_row].T)


def _moe_grouped_kernel(elo_ref, ehi_ref, nv_ref, x1_ref, ws_ref, wg_lo, wu_lo, wd_lo,
                        wg_hi, wu_hi, wd_hi, nfg_ref, fg_ref, y_ref):
    del elo_ref, ehi_ref
    t = pl.program_id(0)

    @pl.when(nv_ref[t] > 0)
    def _():
        x1 = x1_ref[...]
        h = _rms(x1, nfg_ref[...]).astype(BF16)
        ws = ws_ref[...]
        moe = jnp.zeros(x1.shape, F32)
        for col, (wg, wu, wd) in enumerate(((wg_lo, wu_lo, wd_lo), (wg_hi, wu_hi, wd_hi))):
            hg = jnp.dot(h, wg[0], preferred_element_type=F32)
            hu = jnp.dot(h, wu[0], preferred_element_type=F32)
            act = hg * (1.0 / (1.0 + jnp.exp(-hg))) * hu * ws[:, col:col + 1]
            moe = moe + jnp.dot(act.astype(BF16), wd[0], preferred_element_type=F32)
        y_ref[...] = _rms(x1 + moe, fg_ref[...])

    @pl.when(nv_ref[t] == 0)
    def _():
        y_ref[...] = jnp.zeros_like(y_ref)


def _moe_grouped(x1_slots, plan, wg_bf, wu_bf, wd_bf, nf_g, final_g, tm):
    n_steps = plan.n_valid.shape[0]
    lo_spec = lambda shape: pl.BlockSpec(shape, lambda t, elo, ehi, nv: (elo[t], 0, 0))
    hi_spec = lambda shape: pl.BlockSpec(shape, lambda t, elo, ehi, nv: (ehi[t], 0, 0))
    up_shape, down_shape = (1, D_MODEL, D_EXPERT), (1, D_EXPERT, D_MODEL)
    const = lambda a: pl.BlockSpec(a.shape, lambda t, elo, ehi, nv: (0, 0))
    row_spec = lambda width: pl.BlockSpec((tm, width), lambda t, elo, ehi, nv: (t, 0))
    return pl.pallas_call(
        _moe_grouped_kernel,
        grid_spec=pltpu.PrefetchScalarGridSpec(
            num_scalar_prefetch=3,
            grid=(n_steps,),
            in_specs=[row_spec(D_MODEL), row_spec(2),
                      lo_spec(up_shape), lo_spec(up_shape), lo_spec(down_shape),
                      hi_spec(up_shape), hi_spec(up_shape), hi_spec(down_shape),
                      const(nf_g), const(final_g)],
            out_specs=row_spec(D_MODEL)),
        out_shape=jax.ShapeDtypeStruct(x1_slots.shape, F32),
        compiler_params=pltpu.CompilerParams(
            dimension_semantics=("arbitrary",), vmem_limit_bytes=VMEM_LIMIT),
    )(plan.e_lo, plan.e_hi, plan.n_valid, x1_slots, plan.w_slots, wg_bf, wu_bf, wd_bf,
      wg_bf, wu_bf, wd_bf, nf_g, final_g)


def _spatial_operands(w_s, b_s, t_chunk):
    reps = CHUNK // t_chunk
    idx = jnp.arange(CHUNK)
    same = (idx[:, None] // t_chunk) == (idx[None, :] // t_chunk)
    causal = (idx[None, :] % t_chunk) <= (idx[:, None] % t_chunk)
    wm = jnp.tile(w_s[:, :t_chunk, :t_chunk], (1, reps, reps)) * (same & causal)
    wm_cat = wm.transpose(1, 0, 2).reshape(CHUNK, MLP_GROUPS * CHUNK).astype(BF16)
    bias = jnp.repeat(jnp.tile(b_s[:, :t_chunk], (1, reps)).T, MLP_CH, axis=1)
    return wm_cat, bias


def kernel(x_prompt, x_sample, cache_k, cache_v, page_table, norm_attn_g, w_in, sgu_g, w_spatial,
           b_spatial, out_g_attn, out_g_mlp, w_out, norm_ffn_g, w_group, b_group, w_router, b_router,
           w_gate, w_up, w_down, final_g):
    depth = w_in.shape[0]
    assert depth == 1, "single decoder layer"
    batch, seq, _ = x_prompt.shape
    n_seq, t_new, _ = x_sample.shape
    assert seq % MOBA_BLOCK == 0 and seq % CHUNK == 0 and CHUNK % t_new == 0
    assert MOBA_BLOCK % cache_k.shape[2] == 0

    row2 = lambda a: a.reshape(1, -1)
    w_in_bf = w_in[0].astype(BF16)
    w_out_bf = w_out[0].astype(BF16)
    wg_bf, wu_bf, wd_bf = w_gate[0].astype(BF16), w_up[0].astype(BF16), w_down[0].astype(BF16)
    n_logits = MOE_GROUPS + N_EXPERTS
    w_r = jnp.concatenate(
        [w_group[0], w_router[0].transpose(1, 0, 2).reshape(D_MODEL, N_EXPERTS),
         jnp.zeros((D_MODEL, LANES - n_logits), F32)], axis=1)
    b_r = jnp.concatenate(
        [b_group[0], b_router[0].reshape(-1), jnp.zeros((LANES - n_logits,), F32)]).reshape(1, LANES)
    w_r_hi = w_r.astype(BF16)
    w_r_hl = jnp.concatenate([w_r_hi, (w_r - w_r_hi.astype(F32)).astype(BF16)], axis=1)
    wkv_t_bf = w_in_bf[:, ATTN_WIDTH:3 * ATTN_WIDTH].T
    ck_t = jnp.transpose(cache_k[0], (0, 2, 3, 1))
    cv_t = jnp.transpose(cache_v[0], (0, 2, 3, 1))

    def project_in(x, t_chunk, tm, seq_transposed):
        wm_cat, bias_full = _spatial_operands(w_spatial[0], b_spatial[0], t_chunk)
        return _inproj(x, row2(norm_attn_g[0]), w_in_bf, wkv_t_bf, row2(sgu_g[0]), wm_cat, bias_full,
                       row2(out_g_mlp[0]), tm, seq_transposed)

    def project_out(att_n, gmn, x, tm, grouped):
        return _outproj(att_n, gmn, x, w_out_bf, row2(norm_ffn_g[0]), w_r_hl, b_r, tm, grouped)

    og_attn = row2(out_g_attn[0])

    xp = x_prompt.reshape(batch * seq, D_MODEL)
    xp, wg_bf, wu_bf, wd_bf = lax.optimization_barrier((xp, wg_bf, wu_bf, wd_bf))
    qp, kp_t, vp_t, gmn_p, _ = project_in(xp, CHUNK, TM_PROJ_PROMPT, seq)
    att_p = _attn_prompt(qp, kp_t, vp_t, og_attn, batch, seq)
    x1_p, bucket_p, wpair_p = project_out(att_p, gmn_p, xp, TM_PROJ_PROMPT, True)
    plan = _moe_plan(bucket_p[0], wpair_p[:2], TM_MOE_PROMPT)
    x1_slots = _sc_gather_rows(x1_p, plan.slot_row)

    xs = x_sample.reshape(n_seq * t_new, D_MODEL)
    qs, ks, vs, gmn_s, gvs = project_in(xs, t_new, TM_PROJ_SAMPLE, None)
    seq_shape = (n_seq, t_new, ATTN_WIDTH)
    qs3, ks3, vs3 = qs.reshape(seq_shape), ks.reshape(seq_shape), vs.reshape(seq_shape)
    half = n_seq // 2
    attend_half = lambda q_half, lo: _attn_sample(q_half, ks3[lo:lo + half], vs3[lo:lo + half],
                                                  og_attn, ck_t, cv_t, page_table[lo:lo + half])
    att_s0 = attend_half(qs3[:half], 0)
    x1_slots, att_s0 = lax.optimization_barrier((x1_slots, att_s0))
    y_slots = _moe_grouped(x1_slots, plan, wg_bf, wu_bf, wd_bf, row2(norm_ffn_g[0]),
                           row2(final_g), TM_MOE_PROMPT)
    y_slots, q_half1 = lax.optimization_barrier((y_slots, qs3[half:]))
    yp = _sc_gather_rows(y_slots, plan.token_slot)
    att_s1 = attend_half(q_half1, half)
    att_s = jnp.concatenate([att_s0, att_s1], axis=0).reshape(n_seq * t_new, ATTN_WIDTH)
    x1_s, h2_s, comb_s = project_out(att_s, gmn_s, xs, TM_PROJ_SAMPLE, False)
    ys = _moe(h2_s, comb_s, wg_bf, wu_bf, wd_bf, x1_s, row2(final_g), TM_MOE_SAMPLE)

    heads = (ATTN_HEADS, HEAD_DIM)
    rows_last = lambda a_t: a_t.reshape(batch, *heads, seq).transpose(0, 3, 1, 2)[None]
    return (yp.reshape(batch, seq, D_MODEL),
            ys.reshape(n_seq, t_new, D_MODEL),
            rows_last(kp_t),
            rows_last(vp_t),
            ks.reshape(depth, n_seq, t_new, *heads),
            vs.reshape(depth, n_seq, t_new, *heads),
            gvs.reshape(depth, n_seq, t_new, MLP_WIDTH))
```

```python
import functools
import math
from typing import NamedTuple

import jax
import jax.numpy as jnp
from jax import lax
from jax.experimental import pallas as pl
from jax.experimental.pallas import tpu as pltpu
from jax.experimental.pallas import tpu_sc as plsc

D_MODEL = 1024
ATTN_HEADS = 8
HEAD_DIM = 64
ATTN_WIDTH = ATTN_HEADS * HEAD_DIM
MOBA_BLOCK = 256
MOBA_TOPK = 3
QUERY_BLOCK = 128
MLP_GROUPS = 8
MLP_CH = 64
MLP_WIDTH = MLP_GROUPS * MLP_CH
CHUNK = 128
IN_WIDTH = 3 * ATTN_WIDTH + 2 * MLP_WIDTH
MOE_GROUPS = 4
EXPERTS_PER_GROUP = 4
N_EXPERTS = MOE_GROUPS * EXPERTS_PER_GROUP
D_EXPERT = D_MODEL // 2
EPS = 1e-6
PAIR_LO = (0, 0, 0, 1, 1, 2)
PAIR_HI = (1, 2, 3, 2, 3, 3)
PAIRS_PER_GROUP = len(PAIR_LO)
N_BUCKETS = MOE_GROUPS * PAIRS_PER_GROUP

LANES = 128
BF16_ROWS = 16
VMEM_LIMIT = 56 * 1024 * 1024

TM_PROJ_PROMPT = 512
TM_PROJ_SAMPLE = 128
TM_MOE_PROMPT = 256
TM_MOE_SAMPLE = 512

F32 = jnp.float32
BF16 = jnp.bfloat16
NEG_INF = float("-inf")
_NT = (((1,), (1,)), ((), ()))


def _rms(x, g):
    return x * lax.rsqrt(jnp.mean(x * x, axis=-1, keepdims=True) + EPS) * g


def _gelu(x):
    return 0.5 * x * (1.0 + jnp.tanh(0.7978845608028654 * (x + 0.044715 * (x * x * x))))


def _top_rank_select(gate, n_past, n_keep, axis):
    nb = gate.shape[axis]
    n_idx = lax.broadcasted_iota(jnp.int32, gate.shape, axis)
    rank = jnp.zeros(gate.shape, jnp.int32)
    for m in range(nb):
        gm = gate[:, m:m + 1] if axis == 1 else gate[m:m + 1, :]
        beats = jnp.where(gm > gate, 1, jnp.where(gm == gate, jnp.where(m < n_idx, 1, 0), 0))
        rank = rank + jnp.where(m < n_past, beats, 0)
    return jnp.where(n_idx < n_past, rank, n_keep) < n_keep


def _inproj_kernel(x_ref, g_ref, w_ref, wkv_t_ref, sgu_ref, wm_ref, bias_ref, og_ref,
                   q_ref, k_ref, v_ref, gmn_ref, vgn_ref, *, n_chunks, kv_transposed):
    h = _rms(x_ref[...], g_ref[...]).astype(BF16)

    def proj(lo, width):
        return jnp.dot(h, w_ref[:, lo:lo + width], preferred_element_type=F32)

    q_ref[...] = proj(0, ATTN_WIDTH)
    if kv_transposed:
        kv_t = lax.dot_general(wkv_t_ref[...], h, _NT, preferred_element_type=F32)
        k_ref[0] = kv_t[:ATTN_WIDTH]
        v_ref[0] = kv_t[ATTN_WIDTH:]
    else:
        k_ref[...] = proj(ATTN_WIDTH, ATTN_WIDTH)
        v_ref[...] = proj(2 * ATTN_WIDTH, ATTN_WIDTH)
    gu = _gelu(proj(3 * ATTN_WIDTH, MLP_WIDTH))
    vgn = _rms(_gelu(proj(3 * ATTN_WIDTH + MLP_WIDTH, MLP_WIDTH)), sgu_ref[...])
    vgn_ref[...] = vgn

    lane_grp = lax.broadcasted_iota(jnp.int32, (CHUNK, MLP_WIDTH), 1) // MLP_CH
    for c in range(n_chunks):
        rows = slice(c * CHUNK, (c + 1) * CHUNK)
        vc = vgn[rows].astype(BF16)
        vbd = jnp.concatenate(
            [jnp.where(lane_grp == g, vc, jnp.zeros_like(vc)) for g in range(MLP_GROUPS)], axis=0)
        mixed = jnp.dot(wm_ref[...], vbd, preferred_element_type=F32) + bias_ref[...]
        gmn_ref[rows, :] = _rms(gu[rows] * mixed, og_ref[...]).astype(BF16)


def _inproj(x, g, w_bf, wkv_t_bf, sgu_g, wm_cat, bias_full, og_mlp, tm, seq_transposed=None):
    rows = x.shape[0]
    row_spec = lambda width: pl.BlockSpec((tm, width), lambda i: (i, 0))
    full = lambda a: pl.BlockSpec(a.shape, lambda i: (0,) * a.ndim)
    out = jax.ShapeDtypeStruct((rows, ATTN_WIDTH), F32)
    kv_spec, kv_out = row_spec(ATTN_WIDTH), out
    if seq_transposed is not None:
        tiles = seq_transposed // tm
        kv_spec = pl.BlockSpec((1, ATTN_WIDTH, tm), lambda i: (i // tiles, 0, i % tiles))
        kv_out = jax.ShapeDtypeStruct((rows // seq_transposed, ATTN_WIDTH, seq_transposed), F32)
    return pl.pallas_call(
        functools.partial(_inproj_kernel, n_chunks=tm // CHUNK,
                          kv_transposed=seq_transposed is not None),
        grid=(rows // tm,),
        in_specs=[row_spec(D_MODEL), full(g), full(w_bf), full(wkv_t_bf), full(sgu_g), full(wm_cat),
                  full(bias_full), full(og_mlp)],
        out_specs=[row_spec(ATTN_WIDTH), kv_spec, kv_spec, row_spec(ATTN_WIDTH),
                   row_spec(ATTN_WIDTH)],
        out_shape=[out, kv_out, kv_out, jax.ShapeDtypeStruct((rows, MLP_WIDTH), BF16), out],
        compiler_params=pltpu.CompilerParams(
            dimension_semantics=("arbitrary",), vmem_limit_bytes=VMEM_LIMIT),
    )(x, g, w_bf, wkv_t_bf, sgu_g, wm_cat, bias_full, og_mlp)


V_ROWS = HEAD_DIM + BF16_ROWS
LOG2_E = 1.4426950408889634
MASKED = -1e30
Q_TILE = MOBA_BLOCK
COL_R, COL_J, COL_SEL = 0, 2, 8


def _bf16_split(x):
    mantissa, exponent = math.frexp(x)
    high = math.ldexp(round(mantissa * 256.0) / 256.0, exponent)
    return high, x - high


def _attn_prompt_kernel(q_ref, k_ref, v_ref, og_ref, o_ref, ka_scr, vt_scr, kmean_scr, qa_scr,
                        m_scr, alpha_scr, acc_scr, s_scr, *, n_blocks):
    cur = pl.program_id(1)

    @pl.when(cur == 0)
    def _():
        key = lax.broadcasted_iota(jnp.int32, (MOBA_BLOCK, HEAD_DIM), 0)
        col = lax.broadcasted_iota(jnp.int32, (MOBA_BLOCK, HEAD_DIM), 1)
        ones = jnp.ones((BF16_ROWS, MOBA_BLOCK), BF16)
        for n in range(n_blocks):
            keys = slice(n * MOBA_BLOCK, (n + 1) * MOBA_BLOCK)
            kb = k_ref[0, :, keys].T
            kmean_scr[n:n + 1, :] = jnp.sum(kb, axis=0, keepdims=True) * (1.0 / MOBA_BLOCK)
            vt = v_ref[0, :, keys]
            extra = jnp.where(col < COL_J, key,
                              jnp.where(col < COL_SEL, n, jnp.where(col == COL_SEL + n, 1, 0)))
            extra = extra.astype(F32).astype(BF16)
            for h in range(ATTN_HEADS):
                lanes = slice(h * HEAD_DIM, (h + 1) * HEAD_DIM)
                ka_scr[h, n] = jnp.concatenate([kb[:, lanes].astype(BF16), extra], axis=1)
                vt_scr[h, n, :HEAD_DIM, :] = vt[lanes, :].astype(BF16)
                vt_scr[h, n, HEAD_DIM:, :] = ones

    q_t = q_ref[...].T
    row = lax.broadcasted_iota(jnp.int32, (HEAD_DIM, Q_TILE), 0)
    blk = lax.broadcasted_iota(jnp.int32, (n_blocks, Q_TILE), 0)
    for h in range(ATTN_HEADS):
        lanes = slice(h * HEAD_DIM, (h + 1) * HEAD_DIM)
        hi, lo = _bf16_split(LOG2_E * 2.0 ** (-8.0 * (h + 1) / ATTN_HEADS))
        qh_t = q_t[lanes, :]
        gate = jnp.dot(kmean_scr[:, lanes], qh_t, precision=lax.Precision.HIGHEST,
                       preferred_element_type=F32)
        keep = _top_rank_select(gate, cur, MOBA_TOPK, 0) | (blk >= cur)
        sel_rows = jnp.where(keep, 0.0, MASKED)
        alibi = jnp.where(row == COL_R, hi,
                          jnp.where(row == COL_R + 1, lo,
                                    jnp.where(row == COL_J, hi * MOBA_BLOCK,
                                              jnp.where(row == COL_J + 1, lo * MOBA_BLOCK, 0.0))))
        extra = alibi + jnp.concatenate(
            [jnp.zeros((COL_SEL, Q_TILE), F32), sel_rows,
             jnp.zeros((HEAD_DIM - COL_SEL - n_blocks, Q_TILE), F32)], axis=0)
        qa_scr[h] = jnp.concatenate([qh_t * (LOG2_E * HEAD_DIM ** -0.5), extra], axis=0).astype(BF16)

    def attend_block(j, causal, first):
        for h in range(ATTN_HEADS):
            s = jnp.dot(ka_scr[h, j], qa_scr[h], preferred_element_type=F32)
            if causal is not None:
                s = jnp.where(causal, s, NEG_INF)
            s_scr[h] = s
            m_blk = jnp.max(s, axis=0, keepdims=True)
            if first:
                m_scr[h:h + 1, :] = m_blk
            else:
                m_old = m_scr[h:h + 1, :]
                m_new = jnp.maximum(m_old, m_blk)
                alpha_scr[h:h + 1, :] = jnp.exp2(m_old - m_new)
                m_scr[h:h + 1, :] = m_new
        for h in range(ATTN_HEADS):
            p = jnp.exp2(s_scr[h] - m_scr[h:h + 1, :]).astype(BF16)
            pv = jnp.dot(vt_scr[h, j], p, preferred_element_type=F32)
            acc_scr[h] = pv if first else alpha_scr[h:h + 1, :] * acc_scr[h] + pv

    causal = (lax.broadcasted_iota(jnp.int32, (MOBA_BLOCK, Q_TILE), 0)
              <= lax.broadcasted_iota(jnp.int32, (MOBA_BLOCK, Q_TILE), 1))
    attend_block(cur, causal, True)

    def past_block(j, carry):
        attend_block(j, None, False)
        return carry

    lax.fori_loop(0, cur, past_block, 0)

    outs = []
    for h in range(ATTN_HEADS):
        acc = acc_scr[h]
        outs.append(acc[:HEAD_DIM] / acc[HEAD_DIM:HEAD_DIM + 1])
    o_ref[...] = _rms(jnp.concatenate(outs, axis=0).T, og_ref[...]).astype(BF16)


def _attn_prompt(q, k_t, v_t, og_attn, batch, seq):
    n_blocks = seq // MOBA_BLOCK
    assert COL_SEL + n_blocks <= HEAD_DIM and n_blocks % 8 == 0
    q_spec = pl.BlockSpec((Q_TILE, ATTN_WIDTH), lambda b, i: (b * n_blocks + i, 0))
    return pl.pallas_call(
        functools.partial(_attn_prompt_kernel, n_blocks=n_blocks),
        grid=(batch, n_blocks),
        in_specs=[q_spec,
                  pl.BlockSpec((1, ATTN_WIDTH, seq), lambda b, i: (b, 0, 0)),
                  pl.BlockSpec((1, ATTN_WIDTH, seq), lambda b, i: (b, 0, 0)),
                  pl.BlockSpec(og_attn.shape, lambda b, i: (0, 0))],
        out_specs=q_spec,
        out_shape=jax.ShapeDtypeStruct(q.shape, BF16),
        scratch_shapes=[pltpu.VMEM((ATTN_HEADS, n_blocks, MOBA_BLOCK, 2 * HEAD_DIM), BF16),
                        pltpu.VMEM((ATTN_HEADS, n_blocks, V_ROWS, MOBA_BLOCK), BF16),
                        pltpu.VMEM((n_blocks, ATTN_WIDTH), F32),
                        pltpu.VMEM((ATTN_HEADS, 2 * HEAD_DIM, Q_TILE), BF16),
                        pltpu.VMEM((ATTN_HEADS, Q_TILE), F32),
                        pltpu.VMEM((ATTN_HEADS, Q_TILE), F32),
                        pltpu.VMEM((ATTN_HEADS, V_ROWS, Q_TILE), F32),
                        pltpu.VMEM((ATTN_HEADS, MOBA_BLOCK, Q_TILE), F32)],
        compiler_params=pltpu.CompilerParams(
            dimension_semantics=("arbitrary", "arbitrary"), vmem_limit_bytes=VMEM_LIMIT),
    )(q, k_t, v_t, og_attn)


SEQS_PER_STEP = 2


def _attn_sample_kernel(pt_ref, q_ref, kn_ref, vn_ref, og_ref, *refs, n_pages, page, t_new):
    del pt_ref
    o_ref = refs[2 * SEQS_PER_STEP * n_pages]
    for i in range(SEQS_PER_STEP):
        kp = refs[i * n_pages:(i + 1) * n_pages]
        vp = refs[(SEQS_PER_STEP + i) * n_pages:(SEQS_PER_STEP + i + 1) * n_pages]
        att = _attend_one_sample(q_ref[i], kn_ref[i], vn_ref[i], kp, vp, page, t_new)
        o_ref[i] = _rms(att, og_ref[...])


def _attend_one_sample(q, kn, vn, kp, vp, page, t_new):
    n_pages = len(kp)
    n_cols = t_new * ATTN_HEADS
    pages_per_block = MOBA_BLOCK // page
    n_past = n_pages // pages_per_block
    past_len = n_pages * page

    lane_h = lax.broadcasted_iota(jnp.int32, (n_cols, ATTN_WIDTH), 1) // HEAD_DIM
    row = lax.broadcasted_iota(jnp.int32, (n_cols, 1), 0)
    row_h = row % ATTN_HEADS
    row_t = row // ATTN_HEADS
    own_head = lane_h == row_h
    qrep = jnp.concatenate(
        [jnp.broadcast_to(q[t:t + 1, :], (ATTN_HEADS, ATTN_WIDTH)) for t in range(t_new)], axis=0)
    qbd = jnp.where(own_head, qrep, 0.0)
    qbd_s = (qbd * (HEAD_DIM ** -0.5)).astype(BF16)
    slope = jnp.exp2(-8.0 * (row_h + 1).astype(F32) / ATTN_HEADS)

    def block_pages(refs_, n):
        return [refs_[i][0].reshape(ATTN_WIDTH, page)
                for i in range(n * pages_per_block, (n + 1) * pages_per_block)]

    key_off = lax.broadcasted_iota(jnp.int32, (1, MOBA_BLOCK), 1)
    k_sums, s_past = [], []
    for n in range(n_past):
        pages = block_pages(kp, n)
        k_sums.append(jnp.sum(sum(pages), axis=1, keepdims=True))
        kb_t = jnp.concatenate(pages, axis=1).astype(BF16)
        dist = (past_len + row_t) - (n * MOBA_BLOCK + key_off)
        s_past.append(jnp.dot(qbd_s, kb_t, preferred_element_type=F32) - slope * dist.astype(F32))
    kmean = jnp.concatenate(k_sums, axis=1) * (1.0 / MOBA_BLOCK)
    gate = jnp.dot(qbd, kmean, precision=lax.Precision.HIGHEST, preferred_element_type=F32)
    sel = _top_rank_select(gate, n_past, MOBA_TOPK, 1)

    s_own = lax.dot_general(qbd_s, kn.astype(BF16), _NT, preferred_element_type=F32)
    dist = row_t - lax.broadcasted_iota(jnp.int32, (1, t_new), 1)
    s_own = jnp.where(dist >= 0, s_own - slope * dist.astype(F32), NEG_INF)
    m = jnp.max(s_own, axis=-1, keepdims=True)
    for n in range(n_past):
        s_past[n] = jnp.where(sel[:, n:n + 1], s_past[n], NEG_INF)
        m = jnp.maximum(m, jnp.max(s_past[n], axis=-1, keepdims=True))

    p = jnp.exp(s_own - m)
    l = jnp.sum(p, axis=-1, keepdims=True)
    acc = sum(p[:, t:t + 1] * vn[t:t + 1, :] for t in range(t_new))
    for n in range(n_past):
        vb_t = jnp.concatenate(block_pages(vp, n), axis=1).astype(BF16)
        p = jnp.exp(s_past[n] - m)
        l = l + jnp.sum(p, axis=-1, keepdims=True)
        acc = acc + lax.dot_general(p.astype(BF16), vb_t, _NT, preferred_element_type=F32)

    out = jnp.where(own_head, acc / l, 0.0)
    return jnp.sum(out.reshape(t_new, ATTN_HEADS, ATTN_WIDTH), axis=1)


def _attn_sample(q, kn, vn, og_attn, cache_kt, cache_vt, page_table):
    n_seq, t_new, _ = q.shape
    n_pages = page_table.shape[1]
    page = cache_kt.shape[-1]
    assert n_seq % SEQS_PER_STEP == 0
    seq_spec = pl.BlockSpec((SEQS_PER_STEP, t_new, ATTN_WIDTH), lambda i, pt: (i, 0, 0))

    def page_spec(s, p):
        return pl.BlockSpec(
            (1, ATTN_HEADS, HEAD_DIM, page),
            lambda i, pt: (pt[(i * SEQS_PER_STEP + s) * n_pages + p], 0, 0, 0))

    page_specs = [page_spec(s, p) for s in range(SEQS_PER_STEP) for p in range(n_pages)]
    n_refs = len(page_specs)
    return pl.pallas_call(
        functools.partial(_attn_sample_kernel, n_pages=n_pages, page=page, t_new=t_new),
        grid_spec=pltpu.PrefetchScalarGridSpec(
            num_scalar_prefetch=1,
            grid=(n_seq // SEQS_PER_STEP,),
            in_specs=([seq_spec] * 3 + [pl.BlockSpec(og_attn.shape, lambda i, pt: (0, 0))]
                      + page_specs * 2),
            out_specs=seq_spec),
        out_shape=jax.ShapeDtypeStruct(q.shape, F32),
        compiler_params=pltpu.CompilerParams(
            dimension_semantics=("arbitrary",), vmem_limit_bytes=VMEM_LIMIT),
    )(page_table.reshape(-1), q, kn, vn, og_attn, *([cache_kt] * n_refs), *([cache_vt] * n_refs))


def _outproj_kernel(att_ref, gmn_ref, x_ref, wo_ref, nfg_ref, wr_ref, br_ref, x1_ref, *route_refs,
                    grouped):
    mix = (jnp.dot(att_ref[...].astype(BF16), wo_ref[:ATTN_WIDTH, :], preferred_element_type=F32)
           + jnp.dot(gmn_ref[...], wo_ref[ATTN_WIDTH:, :], preferred_element_type=F32))
    x1 = x_ref[...] + mix
    x1_ref[...] = x1
    h2 = _rms(x1, nfg_ref[...])
    h2_hi = h2.astype(BF16)

    h2_lo = (h2 - h2_hi.astype(F32)).astype(BF16)
    hi_dot = jnp.dot(h2_hi, wr_ref[...], preferred_element_type=F32)
    lo_dot = jnp.dot(h2_lo, wr_ref[:, :LANES], preferred_element_type=F32)
    logits = hi_dot[:, :LANES] + hi_dot[:, LANES:] + lo_dot + br_ref[...]
    lt = logits.T
    tm = lt.shape[1]
    row4 = lax.broadcasted_iota(jnp.int32, (MOE_GROUPS, tm), 0)

    def first_argmax(v):
        vmax = jnp.max(v, axis=0, keepdims=True)
        idx = jnp.min(jnp.where(v == vmax, row4, MOE_GROUPS), axis=0, keepdims=True)
        return vmax, idx

    glog = lt[:MOE_GROUPS]
    ge = jnp.exp(glog - jnp.max(glog, axis=0, keepdims=True))
    gprob = ge / jnp.sum(ge, axis=0, keepdims=True)
    p_g, g_idx = first_argmax(gprob)
    elog = lt[MOE_GROUPS:MOE_GROUPS + EXPERTS_PER_GROUP]
    for g in range(1, MOE_GROUPS):
        lo = MOE_GROUPS + g * EXPERTS_PER_GROUP
        elog = jnp.where(g_idx == g, lt[lo:lo + EXPERTS_PER_GROUP], elog)
    l1, i1 = first_argmax(elog)
    l2, i2 = first_argmax(jnp.where(row4 == i1, NEG_INF, elog))
    e2 = jnp.exp(l2 - l1)
    denom = 1.0 + e2
    w1 = (1.0 / denom) * p_g
    w2 = (e2 / denom) * p_g
    if not grouped:
        h2_ref, comb_ref = route_refs
        h2_ref[...] = h2_hi
        lane_row = lax.broadcasted_iota(jnp.int32, (LANES, tm), 0)
        base = g_idx * EXPERTS_PER_GROUP
        comb_t = (jnp.where(lane_row == base + i1, w1, 0.0)
                  + jnp.where(lane_row == base + i2, w2, 0.0))
        comb_ref[...] = comb_t.T
        return

    bucket_ref, wpair_ref = route_refs
    e_lo = jnp.minimum(i1, i2)
    e_hi = jnp.maximum(i1, i2)
    pair = jnp.where(e_lo == 0, e_hi - 1, jnp.where(e_lo == 1, e_hi + 1, PAIRS_PER_GROUP - 1))
    row8 = lax.broadcasted_iota(jnp.int32, (8, tm), 0)
    bucket_ref[...] = jnp.where(row8 == 0, g_idx * PAIRS_PER_GROUP + pair, 0)
    w_lo = jnp.where(i1 < i2, w1, w2)
    w_hi = jnp.where(i1 < i2, w2, w1)
    wpair_ref[...] = jnp.where(row8 == 0, w_lo, jnp.where(row8 == 1, w_hi, 0.0))


def _outproj(att_n, gmn, x, wo_bf, nf_g, w_r, b_r, tm, grouped):
    rows = x.shape[0]
    row_spec = lambda width: pl.BlockSpec((tm, width), lambda i: (i, 0))
    full = lambda a: pl.BlockSpec(a.shape, lambda i: (0,) * a.ndim)
    lane_spec = pl.BlockSpec((8, tm), lambda i: (0, i))
    if grouped:
        route_specs = [lane_spec, lane_spec]
        route_shapes = [jax.ShapeDtypeStruct((8, rows), jnp.int32),
                        jax.ShapeDtypeStruct((8, rows), F32)]
    else:
        route_specs = [row_spec(D_MODEL), row_spec(LANES)]
        route_shapes = [jax.ShapeDtypeStruct((rows, D_MODEL), BF16),
                        jax.ShapeDtypeStruct((rows, LANES), F32)]
    return pl.pallas_call(
        functools.partial(_outproj_kernel, grouped=grouped),
        grid=(rows // tm,),
        in_specs=[row_spec(ATTN_WIDTH), row_spec(MLP_WIDTH), row_spec(D_MODEL), full(wo_bf),
                  full(nf_g), full(w_r), full(b_r)],
        out_specs=[row_spec(D_MODEL)] + route_specs,
        out_shape=[jax.ShapeDtypeStruct((rows, D_MODEL), F32)] + route_shapes,
        compiler_params=pltpu.CompilerParams(
            dimension_semantics=("arbitrary",), vmem_limit_bytes=VMEM_LIMIT),
    )(att_n, gmn, x, wo_bf, nf_g, w_r, b_r)


def _moe_kernel(h_ref, comb_ref, wg_ref, wu_ref, wd_ref, x1_ref, fg_ref, y_ref, acc_ref):
    e = pl.program_id(1)

    @pl.when(e == 0)
    def _():
        acc_ref[...] = jnp.zeros_like(acc_ref)

    h = h_ref[...]
    hg = jnp.dot(h, wg_ref[0], preferred_element_type=F32)
    hu = jnp.dot(h, wu_ref[0], preferred_element_type=F32)
    comb = comb_ref[...]
    lane = lax.broadcasted_iota(jnp.int32, comb.shape, 1)
    c = jnp.sum(jnp.where(lane == e, comb, 0.0), axis=-1, keepdims=True)
    act = hg * (1.0 / (1.0 + jnp.exp(-hg))) * hu * c
    acc_ref[...] += jnp.dot(act.astype(BF16), wd_ref[0], preferred_element_type=F32)

    @pl.when(e == N_EXPERTS - 1)
    def _():
        y_ref[...] = _rms(x1_ref[...] + acc_ref[...], fg_ref[...])


def _moe(h2, comb, wg_bf, wu_bf, wd_bf, x1, final_g, tm):
    rows = h2.shape[0]
    row_spec = lambda width: pl.BlockSpec((tm, width), lambda i, e: (i, 0))
    return pl.pallas_call(
        _moe_kernel,
        grid=(rows // tm, N_EXPERTS),
        in_specs=[row_spec(D_MODEL), row_spec(LANES),
                  pl.BlockSpec((1, D_MODEL, D_EXPERT), lambda i, e: (e, 0, 0)),
                  pl.BlockSpec((1, D_MODEL, D_EXPERT), lambda i, e: (e, 0, 0)),
                  pl.BlockSpec((1, D_EXPERT, D_MODEL), lambda i, e: (e, 0, 0)),
                  row_spec(D_MODEL),
                  pl.BlockSpec(final_g.shape, lambda i, e: (0, 0))],
        out_specs=row_spec(D_MODEL),
        out_shape=jax.ShapeDtypeStruct((rows, D_MODEL), F32),
        scratch_shapes=[pltpu.VMEM((tm, D_MODEL), F32)],
        compiler_params=pltpu.CompilerParams(
            dimension_semantics=("arbitrary", "arbitrary"), vmem_limit_bytes=VMEM_LIMIT),
    )(h2, comb, wg_bf, wu_bf, wd_bf, x1, final_g)


SC_CORES = 2
SC_SUBCORES = 16
SC_WINDOW = 32


def _sc_gather_rows(x, idx):
    n = idx.shape[0]
    width = x.shape[1]
    assert n % SC_WINDOW == 0
    mesh = plsc.VectorSubcoreMesh(core_axis_name="core", subcore_axis_name="subcore",
                                  num_cores=SC_CORES, num_subcores=SC_SUBCORES)

    @pl.kernel(out_type=jax.ShapeDtypeStruct((n, width), x.dtype), mesh=mesh)
    def gather_kernel(x_hbm, idx_hbm, out_hbm):
        def body(idx_vmem, out_vmem):
            pltpu.sync_copy(x_hbm.at[idx_vmem.at[0, pl.ds(0, SC_WINDOW)]], out_vmem)

        pltpu.emit_pipeline(
            body,
            grid=(n // SC_WINDOW,),
            in_specs=[pl.BlockSpec((1, LANES), lambda i: (i, 0))],
            out_specs=[pl.BlockSpec((SC_WINDOW, width), lambda i: (i, 0))],
            core_axis_name=("core", "subcore"),
            dimension_semantics=(pltpu.PARALLEL,),
        )(idx_hbm, out_hbm)

    idx_rows = jnp.pad(idx.reshape(n // SC_WINDOW, SC_WINDOW), ((0, 0), (0, LANES - SC_WINDOW)))
    return gather_kernel(x, idx_rows)


class MoePlan(NamedTuple):
    slot_row: jax.Array
    token_slot: jax.Array
    e_lo: jax.Array
    e_hi: jax.Array
    n_valid: jax.Array
    w_slots: jax.Array


def _moe_plan(bucket, wpair, tm):
    rows = bucket.shape[0]
    n_tiles_max = rows // tm + N_BUCKETS
    i32 = jnp.int32
    b_ids = jnp.arange(N_BUCKETS, dtype=i32)[:, None]

    def lookup(table, keys):
        return jnp.sum(jnp.where(keys[None, :] == b_ids, table[:, None], 0), axis=0)

    _, order, wlo_sorted, whi_sorted = lax.sort(
        (bucket, jnp.arange(rows, dtype=i32), wpair[0], wpair[1]), num_keys=1, is_stable=True)
    position = jnp.argsort(order).astype(i32)
    counts = jnp.sum((bucket[None, :] == b_ids).astype(i32), axis=1)
    starts = jnp.cumsum(counts) - counts
    tiles_b = (counts + tm - 1) // tm
    tile_end = jnp.cumsum(tiles_b)
    tile_start = tile_end - tiles_b
    n_tiles = tile_end[-1]
    token_slot = position + lookup(tile_start * tm - starts, bucket)
    t = jnp.arange(n_tiles_max, dtype=i32)
    tb = jnp.sum((jnp.minimum(t, n_tiles - 1)[None, :] >= tile_end[:, None]).astype(i32), axis=0)
    local = (t - lookup(tile_start, tb)) * tm
    n_valid = jnp.where(t < n_tiles, jnp.clip(lookup(counts, tb) - local, 0, tm), 0)
    slot = local[:, None] + jnp.arange(tm, dtype=i32)[None, :]
    src = ((lookup(starts, tb)[:, None] + slot) % rows).reshape(-1)
    assert rows < 2 ** 24
    sorted_cols = jnp.stack([order.astype(F32), wlo_sorted, whi_sorted], axis=1)
    slot_cols = sorted_cols[src]
    group = tb // PAIRS_PER_GROUP
    pair = tb % PAIRS_PER_GROUP
    table_pad = (0,) * (N_BUCKETS - PAIRS_PER_GROUP)
    e_lo = group * EXPERTS_PER_GROUP + lookup(jnp.asarray(PAIR_LO + table_pad, i32), pair)
    e_hi = group * EXPERTS_PER_GROUP + lookup(jnp.asarray(PAIR_HI + table_pad, i32), pair)
    return MoePlan(slot_cols[:, 0].astype(i32), token_slot.astype(i32), e_lo.astype(i32),
                   e_hi.astype(i32), n_valid.astype(i32), slot_cols[:, 1:])


def _moe_grouped_kernel(elo_ref, ehi_ref, nv_ref, x1_ref, ws_ref, wg_lo, wu_lo, wd_lo,
                        wg_hi, wu_hi, wd_hi, nfg_ref, fg_ref, y_ref):
    del elo_ref, ehi_ref
    t = pl.program_id(0)

    @pl.when(nv_ref[t] > 0)
    def _():
        x1 = x1_ref[...]
        h = _rms(x1, nfg_ref[...]).astype(BF16)
        ws = ws_ref[...]
        moe = jnp.zeros(x1.shape, F32)
        for col, (wg, wu, wd) in enumerate(((wg_lo, wu_lo, wd_lo), (wg_hi, wu_hi, wd_hi))):
            hg = jnp.dot(h, wg[0], preferred_element_type=F32)
            hu = jnp.dot(h, wu[0], preferred_element_type=F32)
            act = hg * (1.0 / (1.0 + jnp.exp(-hg))) * hu * ws[:, col:col + 1]
            moe = moe + jnp.dot(act.astype(BF16), wd[0], preferred_element_type=F32)
        y_ref[...] = _rms(x1 + moe, fg_ref[...])

    @pl.when(nv_ref[t] == 0)
    def _():
        y_ref[...] = jnp.zeros_like(y_ref)


def _moe_grouped(x1_slots, plan, wg_bf, wu_bf, wd_bf, nf_g, final_g, tm):
    n_steps = plan.n_valid.shape[0]
    lo_spec = lambda shape: pl.BlockSpec(shape, lambda t, elo, ehi, nv: (elo[t], 0, 0))
    hi_spec = lambda shape: pl.BlockSpec(shape, lambda t, elo, ehi, nv: (ehi[t], 0, 0))
    up_shape, down_shape = (1, D_MODEL, D_EXPERT), (1, D_EXPERT, D_MODEL)
    const = lambda a: pl.BlockSpec(a.shape, lambda t, elo, ehi, nv: (0, 0))
    row_spec = lambda width: pl.BlockSpec((tm, width), lambda t, elo, ehi, nv: (t, 0))
    return pl.pallas_call(
        _moe_grouped_kernel,
        grid_spec=pltpu.PrefetchScalarGridSpec(
            num_scalar_prefetch=3,
            grid=(n_steps,),
            in_specs=[row_spec(D_MODEL), row_spec(2),
                      lo_spec(up_shape), lo_spec(up_shape), lo_spec(down_shape),
                      hi_spec(up_shape), hi_spec(up_shape), hi_spec(down_shape),
                      const(nf_g), const(final_g)],
            out_specs=row_spec(D_MODEL)),
        out_shape=jax.ShapeDtypeStruct(x1_slots.shape, F32),
        compiler_params=pltpu.CompilerParams(
            dimension_semantics=("arbitrary",), vmem_limit_bytes=VMEM_LIMIT),
    )(plan.e_lo, plan.e_hi, plan.n_valid, x1_slots, plan.w_slots, wg_bf, wu_bf, wd_bf,
      wg_bf, wu_bf, wd_bf, nf_g, final_g)


def _spatial_operands(w_s, b_s, t_chunk):
    reps = CHUNK // t_chunk
    idx = jnp.arange(CHUNK)
    same = (idx[:, None] // t_chunk) == (idx[None, :] // t_chunk)
    causal = (idx[None, :] % t_chunk) <= (idx[:, None] % t_chunk)
    wm = jnp.tile(w_s[:, :t_chunk, :t_chunk], (1, reps, reps)) * (same & causal)
    wm_cat = wm.transpose(1, 0, 2).reshape(CHUNK, MLP_GROUPS * CHUNK).astype(BF16)
    bias = jnp.repeat(jnp.tile(b_s[:, :t_chunk], (1, reps)).T, MLP_CH, axis=1)
    return wm_cat, bias


def kernel(x_prompt, x_sample, cache_k, cache_v, page_table, norm_attn_g, w_in, sgu_g, w_spatial,
           b_spatial, out_g_attn, out_g_mlp, w_out, norm_ffn_g, w_group, b_group, w_router, b_router,
           w_gate, w_up, w_down, final_g):
    depth = w_in.shape[0]
    assert depth == 1, "single decoder layer"
    batch, seq, _ = x_prompt.shape
    n_seq, t_new, _ = x_sample.shape
    assert seq % MOBA_BLOCK == 0 and seq % CHUNK == 0 and CHUNK % t_new == 0
    assert MOBA_BLOCK % cache_k.shape[2] == 0

    row2 = lambda a: a.reshape(1, -1)
    w_in_bf = w_in[0].astype(BF16)
    w_out_bf = w_out[0].astype(BF16)
    wg_bf, wu_bf, wd_bf = w_gate[0].astype(BF16), w_up[0].astype(BF16), w_down[0].astype(BF16)
    n_logits = MOE_GROUPS + N_EXPERTS
    w_r = jnp.concatenate(
        [w_group[0], w_router[0].transpose(1, 0, 2).reshape(D_MODEL, N_EXPERTS),
         jnp.zeros((D_MODEL, LANES - n_logits), F32)], axis=1)
    b_r = jnp.concatenate(
        [b_group[0], b_router[0].reshape(-1), jnp.zeros((LANES - n_logits,), F32)]).reshape(1, LANES)
    w_r_hi = w_r.astype(BF16)
    w_r_hl = jnp.concatenate([w_r_hi, (w_r - w_r_hi.astype(F32)).astype(BF16)], axis=1)
    wkv_t_bf = w_in_bf[:, ATTN_WIDTH:3 * ATTN_WIDTH].T
    ck_t = jnp.transpose(cache_k[0], (0, 2, 3, 1))
    cv_t = jnp.transpose(cache_v[0], (0, 2, 3, 1))

    def project_in(x, t_chunk, tm, seq_transposed):
        wm_cat, bias_full = _spatial_operands(w_spatial[0], b_spatial[0], t_chunk)
        return _inproj(x, row2(norm_attn_g[0]), w_in_bf, wkv_t_bf, row2(sgu_g[0]), wm_cat, bias_full,
                       row2(out_g_mlp[0]), tm, seq_transposed)

    def project_out(att_n, gmn, x, tm, grouped):
        return _outproj(att_n, gmn, x, w_out_bf, row2(norm_ffn_g[0]), w_r_hl, b_r, tm, grouped)

    og_attn = row2(out_g_attn[0])

    xp = x_prompt.reshape(batch * seq, D_MODEL)
    xp, wg_bf, wu_bf, wd_bf = lax.optimization_barrier((xp, wg_bf, wu_bf, wd_bf))
    qp, kp_t, vp_t, gmn_p, _ = project_in(xp, CHUNK, TM_PROJ_PROMPT, seq)
    att_p = _attn_prompt(qp, kp_t, vp_t, og_attn, batch, seq)
    x1_p, bucket_p, wpair_p = project_out(att_p, gmn_p, xp, TM_PROJ_PROMPT, True)
    plan = _moe_plan(bucket_p[0], wpair_p[:2], TM_MOE_PROMPT)
    x1_slots = _sc_gather_rows(x1_p, plan.slot_row)

    xs = x_sample.reshape(n_seq * t_new, D_MODEL)
    qs, ks, vs, gmn_s, gvs = project_in(xs, t_new, TM_PROJ_SAMPLE, None)
    seq_shape = (n_seq, t_new, ATTN_WIDTH)
    qs3, ks3, vs3 = qs.reshape(seq_shape), ks.reshape(seq_shape), vs.reshape(seq_shape)
    half = n_seq // 2
    attend_half = lambda q_half, lo: _attn_sample(q_half, ks3[lo:lo + half], vs3[lo:lo + half],
                                                  og_attn, ck_t, cv_t, page_table[lo:lo + half])
    att_s0 = attend_half(qs3[:half], 0)
    x1_slots, att_s0 = lax.optimization_barrier((x1_slots, att_s0))
    y_slots = _moe_grouped(x1_slots, plan, wg_bf, wu_bf, wd_bf, row2(norm_ffn_g[0]),
                           row2(final_g), TM_MOE_PROMPT)
    y_slots, q_half1 = lax.optimization_barrier((y_slots, qs3[half:]))
    yp = _sc_gather_rows(y_slots, plan.token_slot)
    att_s1 = attend_half(q_half1, half)
    att_s = jnp.concatenate([att_s0, att_s1], axis=0).reshape(n_seq * t_new, ATTN_WIDTH)
    x1_s, h2_s, comb_s = project_out(att_s, gmn_s, xs, TM_PROJ_SAMPLE, False)
    ys = _moe(h2_s, comb_s, wg_bf, wu_bf, wd_bf, x1_s, row2(final_g), TM_MOE_SAMPLE)

    heads = (ATTN_HEADS, HEAD_DIM)
    rows_last = lambda a_t: a_t.reshape(batch, *heads, seq).transpose(0, 3, 1, 2)[None]
    return (yp.reshape(batch, seq, D_MODEL),
            ys.reshape(n_seq, t_new, D_MODEL),
            rows_last(kp_t),
            rows_last(vp_t),
            ks.reshape(depth, n_seq, t_new, *heads),
            vs.reshape(depth, n_seq, t_new, *heads),
            gvs.reshape(depth, n_seq, t_new, MLP_WIDTH))
```

```python
import functools
import math
from typing import NamedTuple

import jax
import jax.numpy as jnp
from jax import lax
from jax.experimental import pallas as pl
from jax.experimental.pallas import tpu as pltpu
from jax.experimental.pallas import tpu_sc as plsc

D_MODEL = 1024
ATTN_HEADS = 8
HEAD_DIM = 64
ATTN_WIDTH = ATTN_HEADS * HEAD_DIM
MOBA_BLOCK = 256
MOBA_TOPK = 3
QUERY_BLOCK = 128
MLP_GROUPS = 8
MLP_CH = 64
MLP_WIDTH = MLP_GROUPS * MLP_CH
CHUNK = 128
IN_WIDTH = 3 * ATTN_WIDTH + 2 * MLP_WIDTH
MOE_GROUPS = 4
EXPERTS_PER_GROUP = 4
N_EXPERTS = MOE_GROUPS * EXPERTS_PER_GROUP
D_EXPERT = D_MODEL // 2
EPS = 1e-6
PAIR_LO = (0, 0, 0, 1, 1, 2)
PAIR_HI = (1, 2, 3, 2, 3, 3)
PAIRS_PER_GROUP = len(PAIR_LO)
N_BUCKETS = MOE_GROUPS * PAIRS_PER_GROUP

LANES = 128
BF16_ROWS = 16
VMEM_LIMIT = 56 * 1024 * 1024

TM_PROJ_PROMPT = 512
TM_PROJ_SAMPLE = 128
TM_MOE_PROMPT = 256
TM_MOE_SAMPLE = 512

F32 = jnp.float32
BF16 = jnp.bfloat16
NEG_INF = float("-inf")
_NT = (((1,), (1,)), ((), ()))


def _rms(x, g):
    return x * lax.rsqrt(jnp.mean(x * x, axis=-1, keepdims=True) + EPS) * g


def _gelu(x):
    return 0.5 * x * (1.0 + jnp.tanh(0.7978845608028654 * (x + 0.044715 * (x * x * x))))


def _top_rank_select(gate, n_past, n_keep, axis):
    nb = gate.shape[axis]
    n_idx = lax.broadcasted_iota(jnp.int32, gate.shape, axis)
    rank = jnp.zeros(gate.shape, jnp.int32)
    for m in range(nb):
        gm = gate[:, m:m + 1] if axis == 1 else gate[m:m + 1, :]
        beats = jnp.where(gm > gate, 1, jnp.where(gm == gate, jnp.where(m < n_idx, 1, 0), 0))
        rank = rank + jnp.where(m < n_past, beats, 0)
    return jnp.where(n_idx < n_past, rank, n_keep) < n_keep


def _inproj_kernel(x_ref, g_ref, w_ref, wkv_t_ref, sgu_ref, wm_ref, bias_ref, og_ref, *refs,
                   n_chunks, kv_transposed, n_cast):
    cast_in, (q_ref, k_ref, v_ref, gmn_ref, vgn_ref), cast_out = (
        refs[:n_cast], refs[n_cast:n_cast + 5], refs[n_cast + 5:])
    for src_ref, dst_ref in zip(cast_in, cast_out):
        dst_ref[...] = src_ref[...].astype(BF16)

    h = _rms(x_ref[...], g_ref[...]).astype(BF16)

    def proj(lo, width):
        return jnp.dot(h, w_ref[:, lo:lo + width], preferred_element_type=F32)

    q_ref[...] = proj(0, ATTN_WIDTH)
    if kv_transposed:
        kv_t = lax.dot_general(wkv_t_ref[...], h, _NT, preferred_element_type=F32)
        k_ref[0] = kv_t[:ATTN_WIDTH]
        v_ref[0] = kv_t[ATTN_WIDTH:]
    else:
        k_ref[...] = proj(ATTN_WIDTH, ATTN_WIDTH)
        v_ref[...] = proj(2 * ATTN_WIDTH, ATTN_WIDTH)
    gu = _gelu(proj(3 * ATTN_WIDTH, MLP_WIDTH))
    vgn = _rms(_gelu(proj(3 * ATTN_WIDTH + MLP_WIDTH, MLP_WIDTH)), sgu_ref[...])
    vgn_ref[...] = vgn

    lane_grp = lax.broadcasted_iota(jnp.int32, (CHUNK, MLP_WIDTH), 1) // MLP_CH
    for c in range(n_chunks):
        rows = slice(c * CHUNK, (c + 1) * CHUNK)
        vc = vgn[rows].astype(BF16)
        vbd = jnp.concatenate(
            [jnp.where(lane_grp == g, vc, jnp.zeros_like(vc)) for g in range(MLP_GROUPS)], axis=0)
        mixed = jnp.dot(wm_ref[...], vbd, preferred_element_type=F32) + bias_ref[...]
        gmn_ref[rows, :] = _rms(gu[rows] * mixed, og_ref[...]).astype(BF16)


def _inproj(x, g, w_bf, wkv_t_bf, sgu_g, wm_cat, bias_full, og_mlp, tm, seq_transposed=None,
            cast_along=()):
    rows = x.shape[0]
    steps = rows // tm
    row_spec = lambda width: pl.BlockSpec((tm, width), lambda i: (i, 0))
    full = lambda a: pl.BlockSpec(a.shape, lambda i: (0,) * a.ndim)
    out = jax.ShapeDtypeStruct((rows, ATTN_WIDTH), F32)
    kv_spec, kv_out = row_spec(ATTN_WIDTH), out
    if seq_transposed is not None:
        tiles = seq_transposed // tm
        kv_spec = pl.BlockSpec((1, ATTN_WIDTH, tm), lambda i: (i // tiles, 0, i % tiles))
        kv_out = jax.ShapeDtypeStruct((rows // seq_transposed, ATTN_WIDTH, seq_transposed), F32)

    def slab_spec(a):
        per = steps // a.shape[0]
        assert per * a.shape[0] == steps and a.shape[1] % per == 0
        return pl.BlockSpec((1, a.shape[1] // per, a.shape[2]), lambda i: (i // per, i % per, 0))

    cast_specs = [slab_spec(a) for a in cast_along]
    return pl.pallas_call(
        functools.partial(_inproj_kernel, n_chunks=tm // CHUNK,
                          kv_transposed=seq_transposed is not None, n_cast=len(cast_along)),
        grid=(steps,),
        in_specs=[row_spec(D_MODEL), full(g), full(w_bf), full(wkv_t_bf), full(sgu_g), full(wm_cat),
                  full(bias_full), full(og_mlp)] + cast_specs,
        out_specs=[row_spec(ATTN_WIDTH), kv_spec, kv_spec, row_spec(ATTN_WIDTH),
                   row_spec(ATTN_WIDTH)] + cast_specs,
        out_shape=[out, kv_out, kv_out, jax.ShapeDtypeStruct((rows, MLP_WIDTH), BF16), out]
        + [jax.ShapeDtypeStruct(a.shape, BF16) for a in cast_along],
        compiler_params=pltpu.CompilerParams(
            dimension_semantics=("arbitrary",), vmem_limit_bytes=VMEM_LIMIT),
    )(x, g, w_bf, wkv_t_bf, sgu_g, wm_cat, bias_full, og_mlp, *cast_along)


V_ROWS = HEAD_DIM + BF16_ROWS
LOG2_E = 1.4426950408889634
MASKED = -1e30
Q_TILE = MOBA_BLOCK
COL_R, COL_J, COL_SEL = 0, 2, 8


def _bf16_split(x):
    mantissa, exponent = math.frexp(x)
    high = math.ldexp(round(mantissa * 256.0) / 256.0, exponent)
    return high, x - high


def _attn_prompt_kernel(q_ref, k_ref, v_ref, og_ref, o_ref, ka_scr, vt_scr, kmean_scr, qa_scr,
                        m_scr, alpha_scr, acc_scr, s_scr, *, n_blocks):
    cur = pl.program_id(1)

    @pl.when(cur == 0)
    def _():
        key = lax.broadcasted_iota(jnp.int32, (MOBA_BLOCK, HEAD_DIM), 0)
        col = lax.broadcasted_iota(jnp.int32, (MOBA_BLOCK, HEAD_DIM), 1)
        ones = jnp.ones((BF16_ROWS, MOBA_BLOCK), BF16)
        for n in range(n_blocks):
            keys = slice(n * MOBA_BLOCK, (n + 1) * MOBA_BLOCK)
            kb = k_ref[0, :, keys].T
            kmean_scr[n:n + 1, :] = jnp.sum(kb, axis=0, keepdims=True) * (1.0 / MOBA_BLOCK)
            vt = v_ref[0, :, keys]
            extra = jnp.where(col < COL_J, key,
                              jnp.where(col < COL_SEL, n, jnp.where(col == COL_SEL + n, 1, 0)))
            extra = extra.astype(F32).astype(BF16)
            for h in range(ATTN_HEADS):
                lanes = slice(h * HEAD_DIM, (h + 1) * HEAD_DIM)
                ka_scr[h, n] = jnp.concatenate([kb[:, lanes].astype(BF16), extra], axis=1)
                vt_scr[h, n, :HEAD_DIM, :] = vt[lanes, :].astype(BF16)
                vt_scr[h, n, HEAD_DIM:, :] = ones

    q_t = q_ref[...].T
    row = lax.broadcasted_iota(jnp.int32, (HEAD_DIM, Q_TILE), 0)
    blk = lax.broadcasted_iota(jnp.int32, (n_blocks, Q_TILE), 0)
    for h in range(ATTN_HEADS):
        lanes = slice(h * HEAD_DIM, (h + 1) * HEAD_DIM)
        hi, lo = _bf16_split(LOG2_E * 2.0 ** (-8.0 * (h + 1) / ATTN_HEADS))
        qh_t = q_t[lanes, :]
        gate = jnp.dot(kmean_scr[:, lanes], qh_t, precision=lax.Precision.HIGHEST,
                       preferred_element_type=F32)
        keep = _top_rank_select(gate, cur, MOBA_TOPK, 0) | (blk >= cur)
        sel_rows = jnp.where(keep, 0.0, MASKED)
        alibi = jnp.where(row == COL_R, hi,
                          jnp.where(row == COL_R + 1, lo,
                                    jnp.where(row == COL_J, hi * MOBA_BLOCK,
                                              jnp.where(row == COL_J + 1, lo * MOBA_BLOCK, 0.0))))
        extra = alibi + jnp.concatenate(
            [jnp.zeros((COL_SEL, Q_TILE), F32), sel_rows,
             jnp.zeros((HEAD_DIM - COL_SEL - n_blocks, Q_TILE), F32)], axis=0)
        qa_scr[h] = jnp.concatenate([qh_t * (LOG2_E * HEAD_DIM ** -0.5), extra], axis=0).astype(BF16)

    def attend_block(j, causal, first):
        for h in range(ATTN_HEADS):
            s = jnp.dot(ka_scr[h, j], qa_scr[h], preferred_element_type=F32)
            if causal is not None:
                s = jnp.where(causal, s, NEG_INF)
            s_scr[h] = s
            m_blk = jnp.max(s, axis=0, keepdims=True)
            if first:
                m_scr[h:h + 1, :] = m_blk
            else:
                m_old = m_scr[h:h + 1, :]
                m_new = jnp.maximum(m_old, m_blk)
                alpha_scr[h:h + 1, :] = jnp.exp2(m_old - m_new)
                m_scr[h:h + 1, :] = m_new
        for h in range(ATTN_HEADS):
            p = jnp.exp2(s_scr[h] - m_scr[h:h + 1, :]).astype(BF16)
            pv = jnp.dot(vt_scr[h, j], p, preferred_element_type=F32)
            acc_scr[h] = pv if first else alpha_scr[h:h + 1, :] * acc_scr[h] + pv

    causal = (lax.broadcasted_iota(jnp.int32, (MOBA_BLOCK, Q_TILE), 0)
              <= lax.broadcasted_iota(jnp.int32, (MOBA_BLOCK, Q_TILE), 1))
    attend_block(cur, causal, True)

    def past_block(j, carry):
        attend_block(j, None, False)
        return carry

    lax.fori_loop(0, cur, past_block, 0)

    outs = []
    for h in range(ATTN_HEADS):
        acc = acc_scr[h]
        outs.append(acc[:HEAD_DIM] / acc[HEAD_DIM:HEAD_DIM + 1])
    o_ref[...] = _rms(jnp.concatenate(outs, axis=0).T, og_ref[...]).astype(BF16)


def _attn_prompt(q, k_t, v_t, og_attn, batch, seq):
    n_blocks = seq // MOBA_BLOCK
    assert COL_SEL + n_blocks <= HEAD_DIM and n_blocks % 8 == 0
    q_spec = pl.BlockSpec((Q_TILE, ATTN_WIDTH), lambda b, i: (b * n_blocks + i, 0))
    return pl.pallas_call(
        functools.partial(_attn_prompt_kernel, n_blocks=n_blocks),
        grid=(batch, n_blocks),
        in_specs=[q_spec,
                  pl.BlockSpec((1, ATTN_WIDTH, seq), lambda b, i: (b, 0, 0)),
                  pl.BlockSpec((1, ATTN_WIDTH, seq), lambda b, i: (b, 0, 0)),
                  pl.BlockSpec(og_attn.shape, lambda b, i: (0, 0))],
        out_specs=q_spec,
        out_shape=jax.ShapeDtypeStruct(q.shape, BF16),
        scratch_shapes=[pltpu.VMEM((ATTN_HEADS, n_blocks, MOBA_BLOCK, 2 * HEAD_DIM), BF16),
                        pltpu.VMEM((ATTN_HEADS, n_blocks, V_ROWS, MOBA_BLOCK), BF16),
                        pltpu.VMEM((n_blocks, ATTN_WIDTH), F32),
                        pltpu.VMEM((ATTN_HEADS, 2 * HEAD_DIM, Q_TILE), BF16),
                        pltpu.VMEM((ATTN_HEADS, Q_TILE), F32),
                        pltpu.VMEM((ATTN_HEADS, Q_TILE), F32),
                        pltpu.VMEM((ATTN_HEADS, V_ROWS, Q_TILE), F32),
                        pltpu.VMEM((ATTN_HEADS, MOBA_BLOCK, Q_TILE), F32)],
        compiler_params=pltpu.CompilerParams(
            dimension_semantics=("arbitrary", "arbitrary"), vmem_limit_bytes=VMEM_LIMIT),
    )(q, k_t, v_t, og_attn)


SEQS_PER_STEP = 2


def _attn_sample_kernel(pt_ref, q_ref, kn_ref, vn_ref, og_ref, *refs, n_pages, page, t_new):
    del pt_ref
    o_ref = refs[2 * SEQS_PER_STEP * n_pages]
    for i in range(SEQS_PER_STEP):
        kp = refs[i * n_pages:(i + 1) * n_pages]
        vp = refs[(SEQS_PER_STEP + i) * n_pages:(SEQS_PER_STEP + i + 1) * n_pages]
        att = _attend_one_sample(q_ref[i], kn_ref[i], vn_ref[i], kp, vp, page, t_new)
        o_ref[i] = _rms(att, og_ref[...])


def _attend_one_sample(q, kn, vn, kp, vp, page, t_new):
    n_pages = len(kp)
    n_cols = t_new * ATTN_HEADS
    pages_per_block = MOBA_BLOCK // page
    n_past = n_pages // pages_per_block
    past_len = n_pages * page

    lane_h = lax.broadcasted_iota(jnp.int32, (n_cols, ATTN_WIDTH), 1) // HEAD_DIM
    row = lax.broadcasted_iota(jnp.int32, (n_cols, 1), 0)
    row_h = row % ATTN_HEADS
    row_t = row // ATTN_HEADS
    own_head = lane_h == row_h
    qrep = jnp.concatenate(
        [jnp.broadcast_to(q[t:t + 1, :], (ATTN_HEADS, ATTN_WIDTH)) for t in range(t_new)], axis=0)
    qbd = jnp.where(own_head, qrep, 0.0)
    qbd_s = (qbd * (HEAD_DIM ** -0.5)).astype(BF16)
    slope = jnp.exp2(-8.0 * (row_h + 1).astype(F32) / ATTN_HEADS)

    def block_pages(refs_, n):
        return [refs_[i][0].reshape(ATTN_WIDTH, page)
                for i in range(n * pages_per_block, (n + 1) * pages_per_block)]

    key_off = lax.broadcasted_iota(jnp.int32, (1, MOBA_BLOCK), 1)
    k_sums, s_past = [], []
    for n in range(n_past):
        pages = block_pages(kp, n)
        k_sums.append(jnp.sum(sum(pages), axis=1, keepdims=True))
        kb_t = jnp.concatenate(pages, axis=1).astype(BF16)
        dist = (past_len + row_t) - (n * MOBA_BLOCK + key_off)
        s_past.append(jnp.dot(qbd_s, kb_t, preferred_element_type=F32) - slope * dist.astype(F32))
    kmean = jnp.concatenate(k_sums, axis=1) * (1.0 / MOBA_BLOCK)
    gate = jnp.dot(qbd, kmean, precision=lax.Precision.HIGHEST, preferred_element_type=F32)
    sel = _top_rank_select(gate, n_past, MOBA_TOPK, 1)

    s_own = lax.dot_general(qbd_s, kn.astype(BF16), _NT, preferred_element_type=F32)
    dist = row_t - lax.broadcasted_iota(jnp.int32, (1, t_new), 1)
    s_own = jnp.where(dist >= 0, s_own - slope * dist.astype(F32), NEG_INF)
    m = jnp.max(s_own, axis=-1, keepdims=True)
    for n in range(n_past):
        s_past[n] = jnp.where(sel[:, n:n + 1], s_past[n], NEG_INF)
        m = jnp.maximum(m, jnp.max(s_past[n], axis=-1, keepdims=True))

    p = jnp.exp(s_own - m)
    l = jnp.sum(p, axis=-1, keepdims=True)
    acc = sum(p[:, t:t + 1] * vn[t:t + 1, :] for t in range(t_new))
    for n in range(n_past):
        vb_t = jnp.concatenate(block_pages(vp, n), axis=1).astype(BF16)
        p = jnp.exp(s_past[n] - m)
        l = l + jnp.sum(p, axis=-1, keepdims=True)
        acc = acc + lax.dot_general(p.astype(BF16), vb_t, _NT, preferred_element_type=F32)

    out = jnp.where(own_head, acc / l, 0.0)
    return jnp.sum(out.reshape(t_new, ATTN_HEADS, ATTN_WIDTH), axis=1)


def _attn_sample(q, kn, vn, og_attn, cache_kt, cache_vt, page_table):
    n_seq, t_new, _ = q.shape
    n_pages = page_table.shape[1]
    page = cache_kt.shape[-1]
    assert n_seq % SEQS_PER_STEP == 0
    seq_spec = pl.BlockSpec((SEQS_PER_STEP, t_new, ATTN_WIDTH), lambda i, pt: (i, 0, 0))

    def page_spec(s, p):
        return pl.BlockSpec(
            (1, ATTN_HEADS, HEAD_DIM, page),
            lambda i, pt: (pt[(i * SEQS_PER_STEP + s) * n_pages + p], 0, 0, 0))

    page_specs = [page_spec(s, p) for s in range(SEQS_PER_STEP) for p in range(n_pages)]
    n_refs = len(page_specs)
    return pl.pallas_call(
        functools.partial(_attn_sample_kernel, n_pages=n_pages, page=page, t_new=t_new),
        grid_spec=pltpu.PrefetchScalarGridSpec(
            num_scalar_prefetch=1,
            grid=(n_seq // SEQS_PER_STEP,),
            in_specs=([seq_spec] * 3 + [pl.BlockSpec(og_attn.shape, lambda i, pt: (0, 0))]
                      + page_specs * 2),
            out_specs=seq_spec),
        out_shape=jax.ShapeDtypeStruct(q.shape, F32),
        compiler_params=pltpu.CompilerParams(
            dimension_semantics=("arbitrary",), vmem_limit_bytes=VMEM_LIMIT),
    )(page_table.reshape(-1), q, kn, vn, og_attn, *([cache_kt] * n_refs), *([cache_vt] * n_refs))


def _outproj_kernel(att_ref, gmn_ref, x_ref, wo_ref, nfg_ref, wr_ref, br_ref, x1_ref, *route_refs,
                    grouped):
    mix = (jnp.dot(att_ref[...].astype(BF16), wo_ref[:ATTN_WIDTH, :], preferred_element_type=F32)
           + jnp.dot(gmn_ref[...], wo_ref[ATTN_WIDTH:, :], preferred_element_type=F32))
    x1 = x_ref[...] + mix
    x1_ref[...] = x1
    h2 = _rms(x1, nfg_ref[...])
    h2_hi = h2.astype(BF16)

    h2_lo = (h2 - h2_hi.astype(F32)).astype(BF16)
    hi_dot = jnp.dot(h2_hi, wr_ref[...], preferred_element_type=F32)
    lo_dot = jnp.dot(h2_lo, wr_ref[:, :LANES], preferred_element_type=F32)
    logits = hi_dot[:, :LANES] + hi_dot[:, LANES:] + lo_dot + br_ref[...]
    lt = logits.T
    tm = lt.shape[1]
    row4 = lax.broadcasted_iota(jnp.int32, (MOE_GROUPS, tm), 0)

    def first_argmax(v):
        vmax = jnp.max(v, axis=0, keepdims=True)
        idx = jnp.min(jnp.where(v == vmax, row4, MOE_GROUPS), axis=0, keepdims=True)
        return vmax, idx

    glog = lt[:MOE_GROUPS]
    ge = jnp.exp(glog - jnp.max(glog, axis=0, keepdims=True))
    gprob = ge / jnp.sum(ge, axis=0, keepdims=True)
    p_g, g_idx = first_argmax(gprob)
    elog = lt[MOE_GROUPS:MOE_GROUPS + EXPERTS_PER_GROUP]
    for g in range(1, MOE_GROUPS):
        lo = MOE_GROUPS + g * EXPERTS_PER_GROUP
        elog = jnp.where(g_idx == g, lt[lo:lo + EXPERTS_PER_GROUP], elog)
    l1, i1 = first_argmax(elog)
    l2, i2 = first_argmax(jnp.where(row4 == i1, NEG_INF, elog))
    e2 = jnp.exp(l2 - l1)
    denom = 1.0 + e2
    w1 = (1.0 / denom) * p_g
    w2 = (e2 / denom) * p_g
    if not grouped:
        h2_ref, comb_ref = route_refs
        h2_ref[...] = h2_hi
        lane_row = lax.broadcasted_iota(jnp.int32, (LANES, tm), 0)
        base = g_idx * EXPERTS_PER_GROUP
        comb_t = (jnp.where(lane_row == base + i1, w1, 0.0)
                  + jnp.where(lane_row == base + i2, w2, 0.0))
        comb_ref[...] = comb_t.T
        return

    bucket_ref, wpair_ref = route_refs
    e_lo = jnp.minimum(i1, i2)
    e_hi = jnp.maximum(i1, i2)
    pair = jnp.where(e_lo == 0, e_hi - 1, jnp.where(e_lo == 1, e_hi + 1, PAIRS_PER_GROUP - 1))
    row8 = lax.broadcasted_iota(jnp.int32, (8, tm), 0)
    bucket_ref[...] = jnp.where(row8 == 0, g_idx * PAIRS_PER_GROUP + pair, 0)
    w_lo = jnp.where(i1 < i2, w1, w2)
    w_hi = jnp.where(i1 < i2, w2, w1)
    wpair_ref[...] = jnp.where(row8 == 0, w_lo, jnp.where(row8 == 1, w_hi, 0.0))


def _outproj(att_n, gmn, x, wo_bf, nf_g, w_r, b_r, tm, grouped):
    rows = x.shape[0]
    row_spec = lambda width: pl.BlockSpec((tm, width), lambda i: (i, 0))
    full = lambda a: pl.BlockSpec(a.shape, lambda i: (0,) * a.ndim)
    lane_spec = pl.BlockSpec((8, tm), lambda i: (0, i))
    if grouped:
        route_specs = [lane_spec, lane_spec]
        route_shapes = [jax.ShapeDtypeStruct((8, rows), jnp.int32),
                        jax.ShapeDtypeStruct((8, rows), F32)]
    else:
        route_specs = [row_spec(D_MODEL), row_spec(LANES)]
        route_shapes = [jax.ShapeDtypeStruct((rows, D_MODEL), BF16),
                        jax.ShapeDtypeStruct((rows, LANES), F32)]
    return pl.pallas_call(
        functools.partial(_outproj_kernel, grouped=grouped),
        grid=(rows // tm,),
        in_specs=[row_spec(ATTN_WIDTH), row_spec(MLP_WIDTH), row_spec(D_MODEL), full(wo_bf),
                  full(nf_g), full(w_r), full(b_r)],
        out_specs=[row_spec(D_MODEL)] + route_specs,
        out_shape=[jax.ShapeDtypeStruct((rows, D_MODEL), F32)] + route_shapes,
        compiler_params=pltpu.CompilerParams(
            dimension_semantics=("arbitrary",), vmem_limit_bytes=VMEM_LIMIT),
    )(att_n, gmn, x, wo_bf, nf_g, w_r, b_r)


def _moe_kernel(h_ref, comb_ref, wg_ref, wu_ref, wd_ref, x1_ref, fg_ref, y_ref, acc_ref):
    e = pl.program_id(1)

    @pl.when(e == 0)
    def _():
        acc_ref[...] = jnp.zeros_like(acc_ref)

    h = h_ref[...]
    hg = jnp.dot(h, wg_ref[0], preferred_element_type=F32)
    hu = jnp.dot(h, wu_ref[0], preferred_element_type=F32)
    comb = comb_ref[...]
    lane = lax.broadcasted_iota(jnp.int32, comb.shape, 1)
    c = jnp.sum(jnp.where(lane == e, comb, 0.0), axis=-1, keepdims=True)
    act = hg * (1.0 / (1.0 + jnp.exp(-hg))) * hu * c
    acc_ref[...] += jnp.dot(act.astype(BF16), wd_ref[0], preferred_element_type=F32)

    @pl.when(e == N_EXPERTS - 1)
    def _():
        y_ref[...] = _rms(x1_ref[...] + acc_ref[...], fg_ref[...])


def _moe(h2, comb, wg_bf, wu_bf, wd_bf, x1, final_g, tm):
    rows = h2.shape[0]
    row_spec = lambda width: pl.BlockSpec((tm, width), lambda i, e: (i, 0))
    return pl.pallas_call(
        _moe_kernel,
        grid=(rows // tm, N_EXPERTS),
        in_specs=[row_spec(D_MODEL), row_spec(LANES),
                  pl.BlockSpec((1, D_MODEL, D_EXPERT), lambda i, e: (e, 0, 0)),
                  pl.BlockSpec((1, D_MODEL, D_EXPERT), lambda i, e: (e, 0, 0)),
                  pl.BlockSpec((1, D_EXPERT, D_MODEL), lambda i, e: (e, 0, 0)),
                  row_spec(D_MODEL),
                  pl.BlockSpec(final_g.shape, lambda i, e: (0, 0))],
        out_specs=row_spec(D_MODEL),
        out_shape=jax.ShapeDtypeStruct((rows, D_MODEL), F32),
        scratch_shapes=[pltpu.VMEM((tm, D_MODEL), F32)],
        compiler_params=pltpu.CompilerParams(
            dimension_semantics=("arbitrary", "arbitrary"), vmem_limit_bytes=VMEM_LIMIT),
    )(h2, comb, wg_bf, wu_bf, wd_bf, x1, final_g)


SC_CORES = 2
SC_SUBCORES = 16
SC_WINDOW = 32


def _sc_gather_rows(x, idx):
    n = idx.shape[0]
    width = x.shape[1]
    assert n % SC_WINDOW == 0
    mesh = plsc.VectorSubcoreMesh(core_axis_name="core", subcore_axis_name="subcore",
                                  num_cores=SC_CORES, num_subcores=SC_SUBCORES)

    @pl.kernel(out_type=jax.ShapeDtypeStruct((n, width), x.dtype), mesh=mesh)
    def gather_kernel(x_hbm, idx_hbm, out_hbm):
        def body(idx_vmem, out_vmem):
            pltpu.sync_copy(x_hbm.at[idx_vmem.at[0, pl.ds(0, SC_WINDOW)]], out_vmem)

        pltpu.emit_pipeline(
            body,
            grid=(n // SC_WINDOW,),
            in_specs=[pl.BlockSpec((1, LANES), lambda i: (i, 0))],
            out_specs=[pl.BlockSpec((SC_WINDOW, width), lambda i: (i, 0))],
            core_axis_name=("core", "subcore"),
            dimension_semantics=(pltpu.PARALLEL,),
        )(idx_hbm, out_hbm)

    idx_rows = jnp.pad(idx.reshape(n // SC_WINDOW, SC_WINDOW), ((0, 0), (0, LANES - SC_WINDOW)))
    return gather_kernel(x, idx_rows)


class MoePlan(NamedTuple):
    slot_row: jax.Array
    token_slot: jax.Array
    e_lo: jax.Array
    e_hi: jax.Array
    n_valid: jax.Array
    w_slots: jax.Array


def _moe_plan(bucket, wpair, tm):
    rows = bucket.shape[0]
    n_tiles_max = rows // tm + N_BUCKETS
    i32 = jnp.int32
    b_ids = jnp.arange(N_BUCKETS, dtype=i32)[:, None]

    def lookup(table, keys):
        return jnp.sum(jnp.where(keys[None, :] == b_ids, table[:, None], 0), axis=0)

    _, order, wlo_sorted, whi_sorted = lax.sort(
        (bucket, jnp.arange(rows, dtype=i32), wpair[0], wpair[1]), num_keys=1, is_stable=True)
    position = jnp.argsort(order).astype(i32)
    counts = jnp.sum((bucket[None, :] == b_ids).astype(i32), axis=1)
    starts = jnp.cumsum(counts) - counts
    tiles_b = (counts + tm - 1) // tm
    tile_end = jnp.cumsum(tiles_b)
    tile_start = tile_end - tiles_b
    n_tiles = tile_end[-1]
    token_slot = position + lookup(tile_start * tm - starts, bucket)
    t = jnp.arange(n_tiles_max, dtype=i32)
    tb = jnp.sum((jnp.minimum(t, n_tiles - 1)[None, :] >= tile_end[:, None]).astype(i32), axis=0)
    local = (t - lookup(tile_start, tb)) * tm
    n_valid = jnp.where(t < n_tiles, jnp.clip(lookup(counts, tb) - local, 0, tm), 0)
    slot = local[:, None] + jnp.arange(tm, dtype=i32)[None, :]
    src = ((lookup(starts, tb)[:, None] + slot) % rows).reshape(-1)
    assert rows < 2 ** 24
    sorted_cols = jnp.stack([order.astype(F32), wlo_sorted, whi_sorted], axis=1)
    slot_cols = sorted_cols[src]
    group = tb // PAIRS_PER_GROUP
    pair = tb % PAIRS_PER_GROUP
    table_pad = (0,) * (N_BUCKETS - PAIRS_PER_GROUP)
    e_lo = group * EXPERTS_PER_GROUP + lookup(jnp.asarray(PAIR_LO + table_pad, i32), pair)
    e_hi = group * EXPERTS_PER_GROUP + lookup(jnp.asarray(PAIR_HI + table_pad, i32), pair)
    return MoePlan(slot_cols[:, 0].astype(i32), token_slot.astype(i32), e_lo.astype(i32),
                   e_hi.astype(i32), n_valid.astype(i32), slot_cols[:, 1:])


def _moe_grouped_kernel(elo_ref, ehi_ref, nv_ref, x1_ref, ws_ref, wg_lo, wu_lo, wd_lo,
                        wg_hi, wu_hi, wd_hi, nfg_ref, fg_ref, y_ref):
    del elo_ref, ehi_ref
    t = pl.program_id(0)

    @pl.when(nv_ref[t] > 0)
    def _():
        x1 = x1_ref[...]
        h = _rms(x1, nfg_ref[...]).astype(BF16)
        ws = ws_ref[...]
        moe = jnp.zeros(x1.shape, F32)
        for col, (wg, wu, wd) in enumerate(((wg_lo, wu_lo, wd_lo), (wg_hi, wu_hi, wd_hi))):
            hg = jnp.dot(h, wg[0], preferred_element_type=F32)
            hu = jnp.dot(h, wu[0], preferred_element_type=F32)
            act = hg * (1.0 / (1.0 + jnp.exp(-hg))) * hu * ws[:, col:col + 1]
            moe = moe + jnp.dot(act.astype(BF16), wd[0], preferred_element_type=F32)
        y_ref[...] = _rms(x1 + moe, fg_ref[...])

    @pl.when(nv_ref[t] == 0)
    def _():
        y_ref[...] = jnp.zeros_like(y_ref)


def _moe_grouped(x1_slots, plan, wg_bf, wu_bf, wd_bf, nf_g, final_g, tm):
    n_steps = plan.n_valid.shape[0]
    lo_spec = lambda shape: pl.BlockSpec(shape, lambda t, elo, ehi, nv: (elo[t], 0, 0))
    hi_spec = lambda shape: pl.BlockSpec(shape, lambda t, elo, ehi, nv: (ehi[t], 0, 0))
    up_shape, down_shape = (1, D_MODEL, D_EXPERT), (1, D_EXPERT, D_MODEL)
    const = lambda a: pl.BlockSpec(a.shape, lambda t, elo, ehi, nv: (0, 0))
    row_spec = lambda width: pl.BlockSpec((tm, width), lambda t, elo, ehi, nv: (t, 0))
    return pl.pallas_call(
        _moe_grouped_kernel,
        grid_spec=pltpu.PrefetchScalarGridSpec(
            num_scalar_prefetch=3,
            grid=(n_steps,),
            in_specs=[row_spec(D_MODEL), row_spec(2),
                      lo_spec(up_shape), lo_spec(up_shape), lo_spec(down_shape),
                      hi_spec(up_shape), hi_spec(up_shape), hi_spec(down_shape),
                      const(nf_g), const(final_g)],
            out_specs=row_spec(D_MODEL)),
        out_shape=jax.ShapeDtypeStruct(x1_slots.shape, F32),
        compiler_params=pltpu.CompilerParams(
            dimension_semantics=("arbitrary",), vmem_limit_bytes=VMEM_LIMIT),
    )(plan.e_lo, plan.e_hi, plan.n_valid, x1_slots, plan.w_slots, wg_bf, wu_bf, wd_bf,
      wg_bf, wu_bf, wd_bf, nf_g, final_g)


def _spatial_operands(w_s, b_s, t_chunk):
    reps = CHUNK // t_chunk
    idx = jnp.arange(CHUNK)
    same = (idx[:, None] // t_chunk) == (idx[None, :] // t_chunk)
    causal = (idx[None, :] % t_chunk) <= (idx[:, None] % t_chunk)
    wm = jnp.tile(w_s[:, :t_chunk, :t_chunk], (1, reps, reps)) * (same & causal)
    wm_cat = wm.transpose(1, 0, 2).reshape(CHUNK, MLP_GROUPS * CHUNK).astype(BF16)
    bias = jnp.repeat(jnp.tile(b_s[:, :t_chunk], (1, reps)).T, MLP_CH, axis=1)
    return wm_cat, bias


def kernel(x_prompt, x_sample, cache_k, cache_v, page_table, norm_attn_g, w_in, sgu_g, w_spatial,
           b_spatial, out_g_attn, out_g_mlp, w_out, norm_ffn_g, w_group, b_group, w_router, b_router,
           w_gate, w_up, w_down, final_g):
    depth = w_in.shape[0]
    assert depth == 1, "single decoder layer"
    batch, seq, _ = x_prompt.shape
    n_seq, t_new, _ = x_sample.shape
    assert seq % MOBA_BLOCK == 0 and seq % CHUNK == 0 and CHUNK % t_new == 0
    assert MOBA_BLOCK % cache_k.shape[2] == 0

    row2 = lambda a: a.reshape(1, -1)
    w_in_bf = w_in[0].astype(BF16)
    w_out_bf = w_out[0].astype(BF16)
    n_logits = MOE_GROUPS + N_EXPERTS
    w_r = jnp.concatenate(
        [w_group[0], w_router[0].transpose(1, 0, 2).reshape(D_MODEL, N_EXPERTS),
         jnp.zeros((D_MODEL, LANES - n_logits), F32)], axis=1)
    b_r = jnp.concatenate(
        [b_group[0], b_router[0].reshape(-1), jnp.zeros((LANES - n_logits,), F32)]).reshape(1, LANES)
    w_r_hi = w_r.astype(BF16)
    w_r_hl = jnp.concatenate([w_r_hi, (w_r - w_r_hi.astype(F32)).astype(BF16)], axis=1)
    wkv_t_bf = w_in_bf[:, ATTN_WIDTH:3 * ATTN_WIDTH].T
    ck_t = jnp.transpose(cache_k[0], (0, 2, 3, 1))
    cv_t = jnp.transpose(cache_v[0], (0, 2, 3, 1))

    def project_in(x, t_chunk, tm, seq_transposed, cast_along=()):
        wm_cat, bias_full = _spatial_operands(w_spatial[0], b_spatial[0], t_chunk)
        return _inproj(x, row2(norm_attn_g[0]), w_in_bf, wkv_t_bf, row2(sgu_g[0]), wm_cat, bias_full,
                       row2(out_g_mlp[0]), tm, seq_transposed, cast_along)

    def project_out(att_n, gmn, x, tm, grouped):
        return _outproj(att_n, gmn, x, w_out_bf, row2(norm_ffn_g[0]), w_r_hl, b_r, tm, grouped)

    og_attn = row2(out_g_attn[0])

    xp = x_prompt.reshape(batch * seq, D_MODEL)
    qp, kp_t, vp_t, gmn_p, _, wg_bf, wu_bf, wd_bf = project_in(
        xp, CHUNK, TM_PROJ_PROMPT, seq, (w_gate[0], w_up[0], w_down[0]))
    att_p = _attn_prompt(qp, kp_t, vp_t, og_attn, batch, seq)
    x1_p, bucket_p, wpair_p = project_out(att_p, gmn_p, xp, TM_PROJ_PROMPT, True)
    plan = _moe_plan(bucket_p[0], wpair_p[:2], TM_MOE_PROMPT)
    x1_slots = _sc_gather_rows(x1_p, plan.slot_row)

    xs = x_sample.reshape(n_seq * t_new, D_MODEL)
    qs, ks, vs, gmn_s, gvs = project_in(xs, t_new, TM_PROJ_SAMPLE, None)
    seq_shape = (n_seq, t_new, ATTN_WIDTH)
    qs3, ks3, vs3 = qs.reshape(seq_shape), ks.reshape(seq_shape), vs.reshape(seq_shape)
    half = n_seq // 2
    attend_half = lambda q_half, lo: _attn_sample(q_half, ks3[lo:lo + half], vs3[lo:lo + half],
                                                  og_attn, ck_t, cv_t, page_table[lo:lo + half])
    att_s0 = attend_half(qs3[:half], 0)
    x1_slots, att_s0 = lax.optimization_barrier((x1_slots, att_s0))
    y_slots = _moe_grouped(x1_slots, plan, wg_bf, wu_bf, wd_bf, row2(norm_ffn_g[0]),
                           row2(final_g), TM_MOE_PROMPT)
    y_slots, q_half1 = lax.optimization_barrier((y_slots, qs3[half:]))
    yp = _sc_gather_rows(y_slots, plan.token_slot)
    att_s1 = attend_half(q_half1, half)
    att_s = jnp.concatenate([att_s0, att_s1], axis=0).reshape(n_seq * t_new, ATTN_WIDTH)
    x1_s, h2_s, comb_s = project_out(att_s, gmn_s, xs, TM_PROJ_SAMPLE, False)
    ys = _moe(h2_s, comb_s, wg_bf, wu_bf, wd_bf, x1_s, row2(final_g), TM_MOE_SAMPLE)

    heads = (ATTN_HEADS, HEAD_DIM)
    rows_last = lambda a_t: a_t.reshape(batch, *heads, seq).transpose(0, 3, 1, 2)[None]
    return (yp.reshape(batch, seq, D_MODEL),
            ys.reshape(n_seq, t_new, D_MODEL),
            rows_last(kp_t),
            rows_last(vp_t),
            ks.reshape(depth, n_seq, t_new, *heads),
            vs.reshape(depth, n_seq, t_new, *heads),
            gvs.reshape(depth, n_seq, t_new, MLP_WIDTH))
```

```python
import functools
import math
from typing import NamedTuple

import jax
import jax.numpy as jnp
from jax import lax
from jax.experimental import pallas as pl
from jax.experimental.pallas import tpu as pltpu
from jax.experimental.pallas import tpu_sc as plsc

D_MODEL = 1024
ATTN_HEADS = 8
HEAD_DIM = 64
ATTN_WIDTH = ATTN_HEADS * HEAD_DIM
MOBA_BLOCK = 256
MOBA_TOPK = 3
QUERY_BLOCK = 128
MLP_GROUPS = 8
MLP_CH = 64
MLP_WIDTH = MLP_GROUPS * MLP_CH
CHUNK = 128
IN_WIDTH = 3 * ATTN_WIDTH + 2 * MLP_WIDTH
MOE_GROUPS = 4
EXPERTS_PER_GROUP = 4
N_EXPERTS = MOE_GROUPS * EXPERTS_PER_GROUP
D_EXPERT = D_MODEL // 2
EPS = 1e-6
PAIR_LO = (0, 0, 0, 1, 1, 2)
PAIR_HI = (1, 2, 3, 2, 3, 3)
PAIRS_PER_GROUP = len(PAIR_LO)
N_BUCKETS = MOE_GROUPS * PAIRS_PER_GROUP

LANES = 128
BF16_ROWS = 16
VMEM_LIMIT = 56 * 1024 * 1024

TM_PROJ_PROMPT = 512
TM_PROJ_SAMPLE = 128
TM_MOE_PROMPT = 256
TM_MOE_SAMPLE = 512

F32 = jnp.float32
BF16 = jnp.bfloat16
NEG_INF = float("-inf")
_NT = (((1,), (1,)), ((), ()))


def _rms(x, g):
    return x * lax.rsqrt(jnp.mean(x * x, axis=-1, keepdims=True) + EPS) * g


def _gelu(x):
    return 0.5 * x * (1.0 + jnp.tanh(0.7978845608028654 * (x + 0.044715 * (x * x * x))))


def _top_rank_select(gate, n_past, n_keep, axis):
    nb = gate.shape[axis]
    n_idx = lax.broadcasted_iota(jnp.int32, gate.shape, axis)
    rank = jnp.zeros(gate.shape, jnp.int32)
    for m in range(nb):
        gm = gate[:, m:m + 1] if axis == 1 else gate[m:m + 1, :]
        beats = jnp.where(gm > gate, 1, jnp.where(gm == gate, jnp.where(m < n_idx, 1, 0), 0))
        rank = rank + jnp.where(m < n_past, beats, 0)
    return jnp.where(n_idx < n_past, rank, n_keep) < n_keep


def _inproj_kernel(x_ref, g_ref, w_ref, wkv_t_ref, sgu_ref, wm_ref, bias_ref, og_ref, *refs,
                   n_chunks, kv_transposed, n_cast):
    cast_in, (q_ref, k_ref, v_ref, gmn_ref, vgn_ref), cast_out = (
        refs[:n_cast], refs[n_cast:n_cast + 5], refs[n_cast + 5:])
    for src_ref, dst_ref in zip(cast_in, cast_out):
        dst_ref[...] = src_ref[...].astype(BF16)

    h = _rms(x_ref[...], g_ref[...]).astype(BF16)

    def proj(lo, width):
        return jnp.dot(h, w_ref[:, lo:lo + width], preferred_element_type=F32)

    vgn = _rms(_gelu(proj(3 * ATTN_WIDTH + MLP_WIDTH, MLP_WIDTH)), sgu_ref[...])
    vgn_ref[...] = vgn
    gu = _gelu(proj(3 * ATTN_WIDTH, MLP_WIDTH))

    lane_grp = lax.broadcasted_iota(jnp.int32, (CHUNK, MLP_WIDTH), 1) // MLP_CH
    mixed = []
    for c in range(n_chunks):
        vc = vgn[c * CHUNK:(c + 1) * CHUNK].astype(BF16)
        vbd = jnp.concatenate(
            [jnp.where(lane_grp == g, vc, jnp.zeros_like(vc)) for g in range(MLP_GROUPS)], axis=0)
        mixed.append(jnp.dot(wm_ref[...], vbd, preferred_element_type=F32) + bias_ref[...])

    q_ref[...] = proj(0, ATTN_WIDTH)
    for c in range(n_chunks):
        rows = slice(c * CHUNK, (c + 1) * CHUNK)
        gmn_ref[rows, :] = _rms(gu[rows] * mixed[c], og_ref[...]).astype(BF16)
    if kv_transposed:
        kv_t = lax.dot_general(wkv_t_ref[...], h, _NT, preferred_element_type=F32)
        k_ref[0] = kv_t[:ATTN_WIDTH]
        v_ref[0] = kv_t[ATTN_WIDTH:]
    else:
        k_ref[...] = proj(ATTN_WIDTH, ATTN_WIDTH)
        v_ref[...] = proj(2 * ATTN_WIDTH, ATTN_WIDTH)


def _inproj(x, g, w_bf, wkv_t_bf, sgu_g, wm_cat, bias_full, og_mlp, tm, seq_transposed=None,
            cast_along=()):
    rows = x.shape[0]
    steps = rows // tm
    row_spec = lambda width: pl.BlockSpec((tm, width), lambda i: (i, 0))
    full = lambda a: pl.BlockSpec(a.shape, lambda i: (0,) * a.ndim)
    out = jax.ShapeDtypeStruct((rows, ATTN_WIDTH), F32)
    kv_spec, kv_out = row_spec(ATTN_WIDTH), out
    if seq_transposed is not None:
        tiles = seq_transposed // tm
        kv_spec = pl.BlockSpec((1, ATTN_WIDTH, tm), lambda i: (i // tiles, 0, i % tiles))
        kv_out = jax.ShapeDtypeStruct((rows // seq_transposed, ATTN_WIDTH, seq_transposed), F32)

    def slab_spec(a):
        per = steps // a.shape[0]
        assert per * a.shape[0] == steps and a.shape[1] % per == 0
        return pl.BlockSpec((1, a.shape[1] // per, a.shape[2]), lambda i: (i // per, i % per, 0))

    cast_specs = [slab_spec(a) for a in cast_along]
    return pl.pallas_call(
        functools.partial(_inproj_kernel, n_chunks=tm // CHUNK,
                          kv_transposed=seq_transposed is not None, n_cast=len(cast_along)),
        grid=(steps,),
        in_specs=[row_spec(D_MODEL), full(g), full(w_bf), full(wkv_t_bf), full(sgu_g), full(wm_cat),
                  full(bias_full), full(og_mlp)] + cast_specs,
        out_specs=[row_spec(ATTN_WIDTH), kv_spec, kv_spec, row_spec(ATTN_WIDTH),
                   row_spec(ATTN_WIDTH)] + cast_specs,
        out_shape=[out, kv_out, kv_out, jax.ShapeDtypeStruct((rows, MLP_WIDTH), BF16), out]
        + [jax.ShapeDtypeStruct(a.shape, BF16) for a in cast_along],
        compiler_params=pltpu.CompilerParams(
            dimension_semantics=("arbitrary",), vmem_limit_bytes=VMEM_LIMIT),
    )(x, g, w_bf, wkv_t_bf, sgu_g, wm_cat, bias_full, og_mlp, *cast_along)


V_ROWS = HEAD_DIM + BF16_ROWS
LOG2_E = 1.4426950408889634
MASKED = -1e30
Q_TILE = MOBA_BLOCK
COL_R, COL_J, COL_SEL = 0, 2, 8


def _bf16_split(x):
    mantissa, exponent = math.frexp(x)
    high = math.ldexp(round(mantissa * 256.0) / 256.0, exponent)
    return high, x - high


def _attn_prompt_kernel(q_ref, k_ref, v_ref, og_ref, o_ref, ka_scr, vt_scr, kmean_scr, qa_scr,
                        m_scr, alpha_scr, acc_scr, s_scr, *, n_blocks):
    cur = pl.program_id(1)

    @pl.when(cur == 0)
    def _():
        key = lax.broadcasted_iota(jnp.int32, (MOBA_BLOCK, HEAD_DIM), 0)
        col = lax.broadcasted_iota(jnp.int32, (MOBA_BLOCK, HEAD_DIM), 1)
        ones = jnp.ones((BF16_ROWS, MOBA_BLOCK), BF16)
        for n in range(n_blocks):
            keys = slice(n * MOBA_BLOCK, (n + 1) * MOBA_BLOCK)
            kb = k_ref[0, :, keys].T
            kmean_scr[n:n + 1, :] = jnp.sum(kb, axis=0, keepdims=True) * (1.0 / MOBA_BLOCK)
            vt = v_ref[0, :, keys]
            extra = jnp.where(col < COL_J, key,
                              jnp.where(col < COL_SEL, n, jnp.where(col == COL_SEL + n, 1, 0)))
            extra = extra.astype(F32).astype(BF16)
            for h in range(ATTN_HEADS):
                lanes = slice(h * HEAD_DIM, (h + 1) * HEAD_DIM)
                ka_scr[h, n] = jnp.concatenate([kb[:, lanes].astype(BF16), extra], axis=1)
                vt_scr[h, n, :HEAD_DIM, :] = vt[lanes, :].astype(BF16)
                vt_scr[h, n, HEAD_DIM:, :] = ones

    q_t = q_ref[...].T
    row = lax.broadcasted_iota(jnp.int32, (HEAD_DIM, Q_TILE), 0)
    blk = lax.broadcasted_iota(jnp.int32, (n_blocks, Q_TILE), 0)
    for h in range(ATTN_HEADS):
        lanes = slice(h * HEAD_DIM, (h + 1) * HEAD_DIM)
        hi, lo = _bf16_split(LOG2_E * 2.0 ** (-8.0 * (h + 1) / ATTN_HEADS))
        qh_t = q_t[lanes, :]
        gate = jnp.dot(kmean_scr[:, lanes], qh_t, precision=lax.Precision.HIGHEST,
                       preferred_element_type=F32)
        keep = _top_rank_select(gate, cur, MOBA_TOPK, 0) | (blk >= cur)
        sel_rows = jnp.where(keep, 0.0, MASKED)
        alibi = jnp.where(row == COL_R, hi,
                          jnp.where(row == COL_R + 1, lo,
                                    jnp.where(row == COL_J, hi * MOBA_BLOCK,
                                              jnp.where(row == COL_J + 1, lo * MOBA_BLOCK, 0.0))))
        extra = alibi + jnp.concatenate(
            [jnp.zeros((COL_SEL, Q_TILE), F32), sel_rows,
             jnp.zeros((HEAD_DIM - COL_SEL - n_blocks, Q_TILE), F32)], axis=0)
        qa_scr[h] = jnp.concatenate([qh_t * (LOG2_E * HEAD_DIM ** -0.5), extra], axis=0).astype(BF16)

    def attend_block(j, causal, first):
        for h in range(ATTN_HEADS):
            s = jnp.dot(ka_scr[h, j], qa_scr[h], preferred_element_type=F32)
            if causal is not None:
                s = jnp.where(causal, s, NEG_INF)
            s_scr[h] = s
            m_blk = jnp.max(s, axis=0, keepdims=True)
            if first:
                m_scr[h:h + 1, :] = m_blk
            else:
                m_old = m_scr[h:h + 1, :]
                m_new = jnp.maximum(m_old, m_blk)
                alpha_scr[h:h + 1, :] = jnp.exp2(m_old - m_new)
                m_scr[h:h + 1, :] = m_new
        for h in range(ATTN_HEADS):
            p = jnp.exp2(s_scr[h] - m_scr[h:h + 1, :]).astype(BF16)
            pv = jnp.dot(vt_scr[h, j], p, preferred_element_type=F32)
            acc_scr[h] = pv if first else alpha_scr[h:h + 1, :] * acc_scr[h] + pv

    causal = (lax.broadcasted_iota(jnp.int32, (MOBA_BLOCK, Q_TILE), 0)
              <= lax.broadcasted_iota(jnp.int32, (MOBA_BLOCK, Q_TILE), 1))
    attend_block(cur, causal, True)

    def past_block(j, carry):
        attend_block(j, None, False)
        return carry

    lax.fori_loop(0, cur, past_block, 0)

    outs = []
    for h in range(ATTN_HEADS):
        acc = acc_scr[h]
        outs.append(acc[:HEAD_DIM] / acc[HEAD_DIM:HEAD_DIM + 1])
    o_ref[...] = _rms(jnp.concatenate(outs, axis=0).T, og_ref[...]).astype(BF16)


def _attn_prompt(q, k_t, v_t, og_attn, batch, seq):
    n_blocks = seq // MOBA_BLOCK
    assert COL_SEL + n_blocks <= HEAD_DIM and n_blocks % 8 == 0
    q_spec = pl.BlockSpec((Q_TILE, ATTN_WIDTH), lambda b, i: (b * n_blocks + i, 0))
    return pl.pallas_call(
        functools.partial(_attn_prompt_kernel, n_blocks=n_blocks),
        grid=(batch, n_blocks),
        in_specs=[q_spec,
                  pl.BlockSpec((1, ATTN_WIDTH, seq), lambda b, i: (b, 0, 0)),
                  pl.BlockSpec((1, ATTN_WIDTH, seq), lambda b, i: (b, 0, 0)),
                  pl.BlockSpec(og_attn.shape, lambda b, i: (0, 0))],
        out_specs=q_spec,
        out_shape=jax.ShapeDtypeStruct(q.shape, BF16),
        scratch_shapes=[pltpu.VMEM((ATTN_HEADS, n_blocks, MOBA_BLOCK, 2 * HEAD_DIM), BF16),
                        pltpu.VMEM((ATTN_HEADS, n_blocks, V_ROWS, MOBA_BLOCK), BF16),
                        pltpu.VMEM((n_blocks, ATTN_WIDTH), F32),
                        pltpu.VMEM((ATTN_HEADS, 2 * HEAD_DIM, Q_TILE), BF16),
                        pltpu.VMEM((ATTN_HEADS, Q_TILE), F32),
                        pltpu.VMEM((ATTN_HEADS, Q_TILE), F32),
                        pltpu.VMEM((ATTN_HEADS, V_ROWS, Q_TILE), F32),
                        pltpu.VMEM((ATTN_HEADS, MOBA_BLOCK, Q_TILE), F32)],
        compiler_params=pltpu.CompilerParams(
            dimension_semantics=("arbitrary", "arbitrary"), vmem_limit_bytes=VMEM_LIMIT),
    )(q, k_t, v_t, og_attn)


SEQS_PER_STEP = 2


def _attn_sample_kernel(pt_ref, q_ref, kn_ref, vn_ref, og_ref, *refs, n_pages, page, t_new):
    del pt_ref
    o_ref = refs[2 * SEQS_PER_STEP * n_pages]
    for i in range(SEQS_PER_STEP):
        kp = refs[i * n_pages:(i + 1) * n_pages]
        vp = refs[(SEQS_PER_STEP + i) * n_pages:(SEQS_PER_STEP + i + 1) * n_pages]
        rows = slice(i * t_new, (i + 1) * t_new)
        att = _attend_one_sample(q_ref[rows, :], kn_ref[rows, :], vn_ref[rows, :], kp, vp, page,
                                 t_new)
        o_ref[rows, :] = _rms(att, og_ref[...])


def _attend_one_sample(q, kn, vn, kp, vp, page, t_new):
    n_pages = len(kp)
    n_cols = t_new * ATTN_HEADS
    pages_per_block = MOBA_BLOCK // page
    n_past = n_pages // pages_per_block
    past_len = n_pages * page

    lane_h = lax.broadcasted_iota(jnp.int32, (n_cols, ATTN_WIDTH), 1) // HEAD_DIM
    row = lax.broadcasted_iota(jnp.int32, (n_cols, 1), 0)
    row_h = row % ATTN_HEADS
    row_t = row // ATTN_HEADS
    own_head = lane_h == row_h
    qrep = jnp.concatenate(
        [jnp.broadcast_to(q[t:t + 1, :], (ATTN_HEADS, ATTN_WIDTH)) for t in range(t_new)], axis=0)
    qbd = jnp.where(own_head, qrep, 0.0)
    qbd_s = (qbd * (HEAD_DIM ** -0.5)).astype(BF16)
    slope = jnp.exp2(-8.0 * (row_h + 1).astype(F32) / ATTN_HEADS)

    def block_pages(refs_, n):
        return [refs_[i][0].reshape(ATTN_WIDTH, page)
                for i in range(n * pages_per_block, (n + 1) * pages_per_block)]

    key_off = lax.broadcasted_iota(jnp.int32, (1, MOBA_BLOCK), 1)
    k_sums, s_past = [], []
    for n in range(n_past):
        pages = block_pages(kp, n)
        k_sums.append(jnp.sum(sum(pages), axis=1, keepdims=True))
        kb_t = jnp.concatenate(pages, axis=1).astype(BF16)
        dist = (past_len + row_t) - (n * MOBA_BLOCK + key_off)
        s_past.append(jnp.dot(qbd_s, kb_t, preferred_element_type=F32) - slope * dist.astype(F32))
    kmean = jnp.concatenate(k_sums, axis=1) * (1.0 / MOBA_BLOCK)
    gate = jnp.dot(qbd, kmean, precision=lax.Precision.HIGHEST, preferred_element_type=F32)
    sel = _top_rank_select(gate, n_past, MOBA_TOPK, 1)

    s_own = lax.dot_general(qbd_s, kn.astype(BF16), _NT, preferred_element_type=F32)
    dist = row_t - lax.broadcasted_iota(jnp.int32, (1, t_new), 1)
    s_own = jnp.where(dist >= 0, s_own - slope * dist.astype(F32), NEG_INF)
    m = jnp.max(s_own, axis=-1, keepdims=True)
    for n in range(n_past):
        s_past[n] = jnp.where(sel[:, n:n + 1], s_past[n], NEG_INF)
        m = jnp.maximum(m, jnp.max(s_past[n], axis=-1, keepdims=True))

    p = jnp.exp(s_own - m)
    l = jnp.sum(p, axis=-1, keepdims=True)
    acc = sum(p[:, t:t + 1] * vn[t:t + 1, :] for t in range(t_new))
    for n in range(n_past):
        vb_t = jnp.concatenate(block_pages(vp, n), axis=1).astype(BF16)
        p = jnp.exp(s_past[n] - m)
        l = l + jnp.sum(p, axis=-1, keepdims=True)
        acc = acc + lax.dot_general(p.astype(BF16), vb_t, _NT, preferred_element_type=F32)

    out = jnp.where(own_head, acc / l, 0.0)
    return jnp.sum(out.reshape(t_new, ATTN_HEADS, ATTN_WIDTH), axis=1)


def _attn_sample(q, kn, vn, og_attn, cache_kt, cache_vt, page_table, t_new, seq_lo, n_seq):
    n_pages = page_table.shape[1]
    page = cache_kt.shape[-1]
    assert n_seq % SEQS_PER_STEP == 0 and seq_lo % SEQS_PER_STEP == 0
    step_rows = SEQS_PER_STEP * t_new
    step_lo = seq_lo // SEQS_PER_STEP
    in_rows = pl.BlockSpec((step_rows, ATTN_WIDTH), lambda i, pt: (step_lo + i, 0))
    out_rows = pl.BlockSpec((step_rows, ATTN_WIDTH), lambda i, pt: (i, 0))

    def page_spec(s, p):
        return pl.BlockSpec(
            (1, ATTN_HEADS, HEAD_DIM, page),
            lambda i, pt: (pt[(seq_lo + i * SEQS_PER_STEP + s) * n_pages + p], 0, 0, 0))

    page_specs = [page_spec(s, p) for s in range(SEQS_PER_STEP) for p in range(n_pages)]
    n_refs = len(page_specs)
    return pl.pallas_call(
        functools.partial(_attn_sample_kernel, n_pages=n_pages, page=page, t_new=t_new),
        grid_spec=pltpu.PrefetchScalarGridSpec(
            num_scalar_prefetch=1,
            grid=(n_seq // SEQS_PER_STEP,),
            in_specs=([in_rows] * 3 + [pl.BlockSpec(og_attn.shape, lambda i, pt: (0, 0))]
                      + page_specs * 2),
            out_specs=out_rows),
        out_shape=jax.ShapeDtypeStruct((n_seq * t_new, ATTN_WIDTH), F32),
        compiler_params=pltpu.CompilerParams(
            dimension_semantics=("arbitrary",), vmem_limit_bytes=VMEM_LIMIT),
    )(page_table.reshape(-1), q, kn, vn, og_attn, *([cache_kt] * n_refs), *([cache_vt] * n_refs))


def _outproj_kernel(att_ref, gmn_ref, x_ref, wo_ref, nfg_ref, wr_ref, br_ref, x1_ref, *route_refs,
                    grouped):
    mix = (jnp.dot(att_ref[...].astype(BF16), wo_ref[:ATTN_WIDTH, :], preferred_element_type=F32)
           + jnp.dot(gmn_ref[...], wo_ref[ATTN_WIDTH:, :], preferred_element_type=F32))
    x1 = x_ref[...] + mix
    x1_ref[...] = x1
    h2 = _rms(x1, nfg_ref[...])
    h2_hi = h2.astype(BF16)

    h2_lo = (h2 - h2_hi.astype(F32)).astype(BF16)
    hi_dot = jnp.dot(h2_hi, wr_ref[...], preferred_element_type=F32)
    lo_dot = jnp.dot(h2_lo, wr_ref[:, :LANES], preferred_element_type=F32)
    logits = hi_dot[:, :LANES] + hi_dot[:, LANES:] + lo_dot + br_ref[...]
    lt = logits.T
    tm = lt.shape[1]
    row4 = lax.broadcasted_iota(jnp.int32, (MOE_GROUPS, tm), 0)

    def first_argmax(v):
        vmax = jnp.max(v, axis=0, keepdims=True)
        idx = jnp.min(jnp.where(v == vmax, row4, MOE_GROUPS), axis=0, keepdims=True)
        return vmax, idx

    glog = lt[:MOE_GROUPS]
    ge = jnp.exp(glog - jnp.max(glog, axis=0, keepdims=True))
    gprob = ge / jnp.sum(ge, axis=0, keepdims=True)
    p_g, g_idx = first_argmax(gprob)
    elog = lt[MOE_GROUPS:MOE_GROUPS + EXPERTS_PER_GROUP]
    for g in range(1, MOE_GROUPS):
        lo = MOE_GROUPS + g * EXPERTS_PER_GROUP
        elog = jnp.where(g_idx == g, lt[lo:lo + EXPERTS_PER_GROUP], elog)
    l1, i1 = first_argmax(elog)
    l2, i2 = first_argmax(jnp.where(row4 == i1, NEG_INF, elog))
    e2 = jnp.exp(l2 - l1)
    denom = 1.0 + e2
    w1 = (1.0 / denom) * p_g
    w2 = (e2 / denom) * p_g
    if not grouped:
        h2_ref, comb_ref = route_refs
        h2_ref[...] = h2_hi
        lane_row = lax.broadcasted_iota(jnp.int32, (LANES, tm), 0)
        base = g_idx * EXPERTS_PER_GROUP
        comb_t = (jnp.where(lane_row == base + i1, w1, 0.0)
                  + jnp.where(lane_row == base + i2, w2, 0.0))
        comb_ref[...] = comb_t.T
        return

    bucket_ref, wpair_ref = route_refs
    e_lo = jnp.minimum(i1, i2)
    e_hi = jnp.maximum(i1, i2)
    pair = jnp.where(e_lo == 0, e_hi - 1, jnp.where(e_lo == 1, e_hi + 1, PAIRS_PER_GROUP - 1))
    row8 = lax.broadcasted_iota(jnp.int32, (8, tm), 0)
    bucket_ref[...] = jnp.where(row8 == 0, g_idx * PAIRS_PER_GROUP + pair, 0)
    w_lo = jnp.where(i1 < i2, w1, w2)
    w_hi = jnp.where(i1 < i2, w2, w1)
    wpair_ref[...] = jnp.where(row8 == 0, w_lo, jnp.where(row8 == 1, w_hi, 0.0))


def _outproj(att_n, gmn, x, wo_bf, nf_g, w_r, b_r, tm, grouped):
    rows = x.shape[0]
    row_spec = lambda width: pl.BlockSpec((tm, width), lambda i: (i, 0))
    full = lambda a: pl.BlockSpec(a.shape, lambda i: (0,) * a.ndim)
    lane_spec = pl.BlockSpec((8, tm), lambda i: (0, i))
    if grouped:
        route_specs = [lane_spec, lane_spec]
        route_shapes = [jax.ShapeDtypeStruct((8, rows), jnp.int32),
                        jax.ShapeDtypeStruct((8, rows), F32)]
    else:
        route_specs = [row_spec(D_MODEL), row_spec(LANES)]
        route_shapes = [jax.ShapeDtypeStruct((rows, D_MODEL), BF16),
                        jax.ShapeDtypeStruct((rows, LANES), F32)]
    return pl.pallas_call(
        functools.partial(_outproj_kernel, grouped=grouped),
        grid=(rows // tm,),
        in_specs=[row_spec(ATTN_WIDTH), row_spec(MLP_WIDTH), row_spec(D_MODEL), full(wo_bf),
                  full(nf_g), full(w_r), full(b_r)],
        out_specs=[row_spec(D_MODEL)] + route_specs,
        out_shape=[jax.ShapeDtypeStruct((rows, D_MODEL), F32)] + route_shapes,
        compiler_params=pltpu.CompilerParams(
            dimension_semantics=("arbitrary",), vmem_limit_bytes=VMEM_LIMIT),
    )(att_n, gmn, x, wo_bf, nf_g, w_r, b_r)


def _moe_kernel(h_ref, comb_ref, wg_ref, wu_ref, wd_ref, x1_ref, fg_ref, y_ref, acc_ref):
    e = pl.program_id(1)

    @pl.when(e == 0)
    def _():
        acc_ref[...] = jnp.zeros_like(acc_ref)

    h = h_ref[...]
    hg = jnp.dot(h, wg_ref[0], preferred_element_type=F32)
    hu = jnp.dot(h, wu_ref[0], preferred_element_type=F32)
    comb = comb_ref[...]
    lane = lax.broadcasted_iota(jnp.int32, comb.shape, 1)
    c = jnp.sum(jnp.where(lane == e, comb, 0.0), axis=-1, keepdims=True)
    act = hg * (1.0 / (1.0 + jnp.exp(-hg))) * hu * c
    acc_ref[...] += jnp.dot(act.astype(BF16), wd_ref[0], preferred_element_type=F32)

    @pl.when(e == N_EXPERTS - 1)
    def _():
        y_ref[...] = _rms(x1_ref[...] + acc_ref[...], fg_ref[...])


def _moe(h2, comb, wg_bf, wu_bf, wd_bf, x1, final_g, tm):
    rows = h2.shape[0]
    row_spec = lambda width: pl.BlockSpec((tm, width), lambda i, e: (i, 0))
    return pl.pallas_call(
        _moe_kernel,
        grid=(rows // tm, N_EXPERTS),
        in_specs=[row_spec(D_MODEL), row_spec(LANES),
                  pl.BlockSpec((1, D_MODEL, D_EXPERT), lambda i, e: (e, 0, 0)),
                  pl.BlockSpec((1, D_MODEL, D_EXPERT), lambda i, e: (e, 0, 0)),
                  pl.BlockSpec((1, D_EXPERT, D_MODEL), lambda i, e: (e, 0, 0)),
                  row_spec(D_MODEL),
                  pl.BlockSpec(final_g.shape, lambda i, e: (0, 0))],
        out_specs=row_spec(D_MODEL),
        out_shape=jax.ShapeDtypeStruct((rows, D_MODEL), F32),
        scratch_shapes=[pltpu.VMEM((tm, D_MODEL), F32)],
        compiler_params=pltpu.CompilerParams(
            dimension_semantics=("arbitrary", "arbitrary"), vmem_limit_bytes=VMEM_LIMIT),
    )(h2, comb, wg_bf, wu_bf, wd_bf, x1, final_g)


SC_CORES = 2
SC_SUBCORES = 16
SC_WINDOW = 32


def _sc_gather_rows(x, idx):
    n = idx.shape[0]
    width = x.shape[1]
    assert n % SC_WINDOW == 0
    mesh = plsc.VectorSubcoreMesh(core_axis_name="core", subcore_axis_name="subcore",
                                  num_cores=SC_CORES, num_subcores=SC_SUBCORES)

    @pl.kernel(out_type=jax.ShapeDtypeStruct((n, width), x.dtype), mesh=mesh)
    def gather_kernel(x_hbm, idx_hbm, out_hbm):
        def body(idx_vmem, out_vmem):
            pltpu.sync_copy(x_hbm.at[idx_vmem.at[0, pl.ds(0, SC_WINDOW)]], out_vmem)

        pltpu.emit_pipeline(
            body,
            grid=(n // SC_WINDOW,),
            in_specs=[pl.BlockSpec((1, LANES), lambda i: (i, 0))],
            out_specs=[pl.BlockSpec((SC_WINDOW, width), lambda i: (i, 0))],
            core_axis_name=("core", "subcore"),
            dimension_semantics=(pltpu.PARALLEL,),
        )(idx_hbm, out_hbm)

    idx_rows = jnp.pad(idx.reshape(n // SC_WINDOW, SC_WINDOW), ((0, 0), (0, LANES - SC_WINDOW)))
    return gather_kernel(x, idx_rows)


class MoePlan(NamedTuple):
    slot_row: jax.Array
    token_slot: jax.Array
    e_lo: jax.Array
    e_hi: jax.Array
    n_valid: jax.Array
    w_slots: jax.Array


def _moe_plan(bucket, wpair, tm):
    rows = bucket.shape[0]
    n_tiles_max = rows // tm + N_BUCKETS
    i32 = jnp.int32
    b_ids = jnp.arange(N_BUCKETS, dtype=i32)[:, None]

    def lookup(table, keys):
        return jnp.sum(jnp.where(keys[None, :] == b_ids, table[:, None], 0), axis=0)

    _, order, wlo_sorted, whi_sorted = lax.sort(
        (bucket, jnp.arange(rows, dtype=i32), wpair[0], wpair[1]), num_keys=1, is_stable=True)
    position = jnp.argsort(order).astype(i32)
    counts = jnp.sum((bucket[None, :] == b_ids).astype(i32), axis=1)
    starts = jnp.cumsum(counts) - counts
    tiles_b = (counts + tm - 1) // tm
    tile_end = jnp.cumsum(tiles_b)
    tile_start = tile_end - tiles_b
    n_tiles = tile_end[-1]
    token_slot = position + lookup(tile_start * tm - starts, bucket)
    t = jnp.arange(n_tiles_max, dtype=i32)
    tb = jnp.sum((jnp.minimum(t, n_tiles - 1)[None, :] >= tile_end[:, None]).astype(i32), axis=0)
    local = (t - lookup(tile_start, tb)) * tm
    n_valid = jnp.where(t < n_tiles, jnp.clip(lookup(counts, tb) - local, 0, tm), 0)
    slot = local[:, None] + jnp.arange(tm, dtype=i32)[None, :]
    src = ((lookup(starts, tb)[:, None] + slot) % rows).reshape(-1)
    assert rows < 2 ** 24
    sorted_cols = jnp.stack([order.astype(F32), wlo_sorted, whi_sorted], axis=1)
    slot_cols = sorted_cols[src]
    group = tb // PAIRS_PER_GROUP
    pair = tb % PAIRS_PER_GROUP
    table_pad = (0,) * (N_BUCKETS - PAIRS_PER_GROUP)
    e_lo = group * EXPERTS_PER_GROUP + lookup(jnp.asarray(PAIR_LO + table_pad, i32), pair)
    e_hi = group * EXPERTS_PER_GROUP + lookup(jnp.asarray(PAIR_HI + table_pad, i32), pair)
    return MoePlan(slot_cols[:, 0].astype(i32), token_slot.astype(i32), e_lo.astype(i32),
                   e_hi.astype(i32), n_valid.astype(i32), slot_cols[:, 1:])


def _moe_grouped_kernel(elo_ref, ehi_ref, nv_ref, x1_ref, ws_ref, wg_lo, wu_lo, wd_lo,
                        wg_hi, wu_hi, wd_hi, nfg_ref, fg_ref, y_ref):
    del elo_ref, ehi_ref
    t = pl.program_id(0)

    @pl.when(nv_ref[t] > 0)
    def _():
        x1 = x1_ref[...]
        h = _rms(x1, nfg_ref[...]).astype(BF16)
        ws = ws_ref[...]
        moe = jnp.zeros(x1.shape, F32)
        for col, (wg, wu, wd) in enumerate(((wg_lo, wu_lo, wd_lo), (wg_hi, wu_hi, wd_hi))):
            hg = jnp.dot(h, wg[0], preferred_element_type=F32)
            hu = jnp.dot(h, wu[0], preferred_element_type=F32)
            act = hg * (1.0 / (1.0 + jnp.exp(-hg))) * hu * ws[:, col:col + 1]
            moe = moe + jnp.dot(act.astype(BF16), wd[0], preferred_element_type=F32)
        y_ref[...] = _rms(x1 + moe, fg_ref[...])

    @pl.when(nv_ref[t] == 0)
    def _():
        y_ref[...] = jnp.zeros_like(y_ref)


def _moe_grouped(x1_slots, plan, wg_bf, wu_bf, wd_bf, nf_g, final_g, tm):
    n_steps = plan.n_valid.shape[0]
    lo_spec = lambda shape: pl.BlockSpec(shape, lambda t, elo, ehi, nv: (elo[t], 0, 0))
    hi_spec = lambda shape: pl.BlockSpec(shape, lambda t, elo, ehi, nv: (ehi[t], 0, 0))
    up_shape, down_shape = (1, D_MODEL, D_EXPERT), (1, D_EXPERT, D_MODEL)
    const = lambda a: pl.BlockSpec(a.shape, lambda t, elo, ehi, nv: (0, 0))
    row_spec = lambda width: pl.BlockSpec((tm, width), lambda t, elo, ehi, nv: (t, 0))
    return pl.pallas_call(
        _moe_grouped_kernel,
        grid_spec=pltpu.PrefetchScalarGridSpec(
            num_scalar_prefetch=3,
            grid=(n_steps,),
            in_specs=[row_spec(D_MODEL), row_spec(2),
                      lo_spec(up_shape), lo_spec(up_shape), lo_spec(down_shape),
                      hi_spec(up_shape), hi_spec(up_shape), hi_spec(down_shape),
                      const(nf_g), const(final_g)],
            out_specs=row_spec(D_MODEL)),
        out_shape=jax.ShapeDtypeStruct(x1_slots.shape, F32),
        compiler_params=pltpu.CompilerParams(
            dimension_semantics=("arbitrary",), vmem_limit_bytes=VMEM_LIMIT),
    )(plan.e_lo, plan.e_hi, plan.n_valid, x1_slots, plan.w_slots, wg_bf, wu_bf, wd_bf,
      wg_bf, wu_bf, wd_bf, nf_g, final_g)


def _spatial_operands(w_s, b_s, t_chunk):
    reps = CHUNK // t_chunk
    idx = jnp.arange(CHUNK)
    same = (idx[:, None] // t_chunk) == (idx[None, :] // t_chunk)
    causal = (idx[None, :] % t_chunk) <= (idx[:, None] % t_chunk)
    wm = jnp.tile(w_s[:, :t_chunk, :t_chunk], (1, reps, reps)) * (same & causal)
    wm_cat = wm.transpose(1, 0, 2).reshape(CHUNK, MLP_GROUPS * CHUNK).astype(BF16)
    bias = jnp.repeat(jnp.tile(b_s[:, :t_chunk], (1, reps)).T, MLP_CH, axis=1)
    return wm_cat, bias


def kernel(x_prompt, x_sample, cache_k, cache_v, page_table, norm_attn_g, w_in, sgu_g, w_spatial,
           b_spatial, out_g_attn, out_g_mlp, w_out, norm_ffn_g, w_group, b_group, w_router, b_router,
           w_gate, w_up, w_down, final_g):
    depth = w_in.shape[0]
    assert depth == 1, "single decoder layer"
    batch, seq, _ = x_prompt.shape
    n_seq, t_new, _ = x_sample.shape
    assert seq % MOBA_BLOCK == 0 and seq % CHUNK == 0 and CHUNK % t_new == 0
    assert MOBA_BLOCK % cache_k.shape[2] == 0

    row2 = lambda a: a.reshape(1, -1)
    w_in_bf = w_in[0].astype(BF16)
    w_out_bf = w_out[0].astype(BF16)
    n_logits = MOE_GROUPS + N_EXPERTS
    w_r = jnp.concatenate(
        [w_group[0], w_router[0].transpose(1, 0, 2).reshape(D_MODEL, N_EXPERTS),
         jnp.zeros((D_MODEL, LANES - n_logits), F32)], axis=1)
    b_r = jnp.concatenate(
        [b_group[0], b_router[0].reshape(-1), jnp.zeros((LANES - n_logits,), F32)]).reshape(1, LANES)
    w_r_hi = w_r.astype(BF16)
    w_r_hl = jnp.concatenate([w_r_hi, (w_r - w_r_hi.astype(F32)).astype(BF16)], axis=1)
    wkv_t_bf = w_in_bf[:, ATTN_WIDTH:3 * ATTN_WIDTH].T
    ck_t = jnp.transpose(cache_k[0], (0, 2, 3, 1))
    cv_t = jnp.transpose(cache_v[0], (0, 2, 3, 1))

    def project_in(x, t_chunk, tm, seq_transposed, cast_along=()):
        wm_cat, bias_full = _spatial_operands(w_spatial[0], b_spatial[0], t_chunk)
        return _inproj(x, row2(norm_attn_g[0]), w_in_bf, wkv_t_bf, row2(sgu_g[0]), wm_cat, bias_full,
                       row2(out_g_mlp[0]), tm, seq_transposed, cast_along)

    def project_out(att_n, gmn, x, tm, grouped):
        return _outproj(att_n, gmn, x, w_out_bf, row2(norm_ffn_g[0]), w_r_hl, b_r, tm, grouped)

    og_attn = row2(out_g_attn[0])

    xp = x_prompt.reshape(batch * seq, D_MODEL)
    qp, kp_t, vp_t, gmn_p, _, wg_bf, wu_bf, wd_bf = project_in(
        xp, CHUNK, TM_PROJ_PROMPT, seq, (w_gate[0], w_up[0], w_down[0]))
    att_p = _attn_prompt(qp, kp_t, vp_t, og_attn, batch, seq)
    x1_p, bucket_p, wpair_p = project_out(att_p, gmn_p, xp, TM_PROJ_PROMPT, True)
    plan = _moe_plan(bucket_p[0], wpair_p[:2], TM_MOE_PROMPT)
    x1_slots = _sc_gather_rows(x1_p, plan.slot_row)

    xs = x_sample.reshape(n_seq * t_new, D_MODEL)
    qs, ks, vs, gmn_s, gvs = project_in(xs, t_new, TM_PROJ_SAMPLE, None)
    half = n_seq // 2
    attend_half = lambda q_all, lo: _attn_sample(q_all, ks, vs, og_attn, ck_t, cv_t, page_table,
                                                 t_new, lo, half)
    att_s0 = attend_half(qs, 0)
    x1_slots, att_s0 = lax.optimization_barrier((x1_slots, att_s0))
    y_slots = _moe_grouped(x1_slots, plan, wg_bf, wu_bf, wd_bf, row2(norm_ffn_g[0]),
                           row2(final_g), TM_MOE_PROMPT)
    y_slots, qs_late = lax.optimization_barrier((y_slots, qs))
    yp = _sc_gather_rows(y_slots, plan.token_slot)
    att_s1 = attend_half(qs_late, half)
    att_s = jnp.concatenate([att_s0, att_s1], axis=0)
    x1_s, h2_s, comb_s = project_out(att_s, gmn_s, xs, TM_PROJ_SAMPLE, False)
    ys = _moe(h2_s, comb_s, wg_bf, wu_bf, wd_bf, x1_s, row2(final_g), TM_MOE_SAMPLE)

    heads = (ATTN_HEADS, HEAD_DIM)
    rows_last = lambda a_t: a_t.reshape(batch, *heads, seq).transpose(0, 3, 1, 2)[None]
    return (yp.reshape(batch, seq, D_MODEL),
            ys.reshape(n_seq, t_new, D_MODEL),
            rows_last(kp_t),
            rows_last(vp_t),
            ks.reshape(depth, n_seq, t_new, *heads),
            vs.reshape(depth, n_seq, t_new, *heads),
            gvs.reshape(depth, n_seq, t_new, MLP_WIDTH))
```

```python
import functools
import math
from typing import NamedTuple

import jax
import jax.numpy as jnp
from jax import lax
from jax.experimental import pallas as pl
from jax.experimental.pallas import tpu as pltpu
from jax.experimental.pallas import tpu_sc as plsc

D_MODEL = 1024
ATTN_HEADS = 8
HEAD_DIM = 64
ATTN_WIDTH = ATTN_HEADS * HEAD_DIM
MOBA_BLOCK = 256
MOBA_TOPK = 3
QUERY_BLOCK = 128
MLP_GROUPS = 8
MLP_CH = 64
MLP_WIDTH = MLP_GROUPS * MLP_CH
CHUNK = 128
IN_WIDTH = 3 * ATTN_WIDTH + 2 * MLP_WIDTH
MOE_GROUPS = 4
EXPERTS_PER_GROUP = 4
N_EXPERTS = MOE_GROUPS * EXPERTS_PER_GROUP
D_EXPERT = D_MODEL // 2
EPS = 1e-6
PAIR_LO = (0, 0, 0, 1, 1, 2)
PAIR_HI = (1, 2, 3, 2, 3, 3)
PAIRS_PER_GROUP = len(PAIR_LO)
N_BUCKETS = MOE_GROUPS * PAIRS_PER_GROUP

LANES = 128
BF16_ROWS = 16
VMEM_LIMIT = 56 * 1024 * 1024

TM_PROJ_PROMPT = 512
TM_PROJ_SAMPLE = 128
TM_MOE_PROMPT = 256
TM_MOE_SAMPLE = 512

F32 = jnp.float32
BF16 = jnp.bfloat16
NEG_INF = float("-inf")
_NT = (((1,), (1,)), ((), ()))


def _rms(x, g):
    return x * lax.rsqrt(jnp.mean(x * x, axis=-1, keepdims=True) + EPS) * g


def _gelu(x):
    return 0.5 * x * (1.0 + jnp.tanh(0.7978845608028654 * (x + 0.044715 * (x * x * x))))


def _top_rank_select(gate, n_past, n_keep, axis):
    nb = gate.shape[axis]
    n_idx = lax.broadcasted_iota(jnp.int32, gate.shape, axis)
    rank = jnp.zeros(gate.shape, jnp.int32)
    for m in range(nb):
        gm = gate[:, m:m + 1] if axis == 1 else gate[m:m + 1, :]
        beats = jnp.where(gm > gate, 1, jnp.where(gm == gate, jnp.where(m < n_idx, 1, 0), 0))
        rank = rank + jnp.where(m < n_past, beats, 0)
    return jnp.where(n_idx < n_past, rank, n_keep) < n_keep


def _inproj_kernel(x_ref, g_ref, w_ref, wkv_t_ref, sgu_ref, wm_ref, bias_ref, og_ref, *refs,
                   n_chunks, kv_transposed, n_cast, emit_vgn):
    n_main = 5 if emit_vgn else 4
    cast_in, main, cast_out = refs[:n_cast], refs[n_cast:n_cast + n_main], refs[n_cast + n_main:]
    q_ref, k_ref, v_ref, gmn_ref = main[:4]
    for src_ref, dst_ref in zip(cast_in, cast_out):
        dst_ref[...] = src_ref[...].astype(BF16)

    h = _rms(x_ref[...], g_ref[...]).astype(BF16)

    def proj(lo, width):
        return jnp.dot(h, w_ref[:, lo:lo + width], preferred_element_type=F32)

    vgn = _rms(_gelu(proj(3 * ATTN_WIDTH + MLP_WIDTH, MLP_WIDTH)), sgu_ref[...])
    if emit_vgn:
        main[4][...] = vgn
    gu = _gelu(proj(3 * ATTN_WIDTH, MLP_WIDTH))

    lane_grp = lax.broadcasted_iota(jnp.int32, (CHUNK, MLP_WIDTH), 1) // MLP_CH
    mixed = []
    for c in range(n_chunks):
        vc = vgn[c * CHUNK:(c + 1) * CHUNK].astype(BF16)
        vbd = jnp.concatenate(
            [jnp.where(lane_grp == g, vc, jnp.zeros_like(vc)) for g in range(MLP_GROUPS)], axis=0)
        mixed.append(jnp.dot(wm_ref[...], vbd, preferred_element_type=F32) + bias_ref[...])

    q_ref[...] = proj(0, ATTN_WIDTH)
    for c in range(n_chunks):
        rows = slice(c * CHUNK, (c + 1) * CHUNK)
        gmn_ref[rows, :] = _rms(gu[rows] * mixed[c], og_ref[...]).astype(BF16)
    if kv_transposed:
        kv_t = lax.dot_general(wkv_t_ref[...], h, _NT, preferred_element_type=F32)
        k_ref[0] = kv_t[:ATTN_WIDTH]
        v_ref[0] = kv_t[ATTN_WIDTH:]
    else:
        k_ref[...] = proj(ATTN_WIDTH, ATTN_WIDTH)
        v_ref[...] = proj(2 * ATTN_WIDTH, ATTN_WIDTH)


def _inproj(x, g, w_bf, wkv_t_bf, sgu_g, wm_cat, bias_full, og_mlp, tm, seq_transposed=None,
            cast_along=(), emit_vgn=True):
    rows = x.shape[0]
    steps = rows // tm
    row_spec = lambda width: pl.BlockSpec((tm, width), lambda i: (i, 0))
    full = lambda a: pl.BlockSpec(a.shape, lambda i: (0,) * a.ndim)
    out = jax.ShapeDtypeStruct((rows, ATTN_WIDTH), F32)
    kv_spec, kv_out = row_spec(ATTN_WIDTH), out
    if seq_transposed is not None:
        tiles = seq_transposed // tm
        kv_spec = pl.BlockSpec((1, ATTN_WIDTH, tm), lambda i: (i // tiles, 0, i % tiles))
        kv_out = jax.ShapeDtypeStruct((rows // seq_transposed, ATTN_WIDTH, seq_transposed), F32)

    def slab_spec(a):
        per = steps // a.shape[0]
        assert per * a.shape[0] == steps and a.shape[1] % per == 0
        return pl.BlockSpec((1, a.shape[1] // per, a.shape[2]), lambda i: (i // per, i % per, 0))

    cast_specs = [slab_spec(a) for a in cast_along]
    vgn_spec, vgn_out = ([row_spec(MLP_WIDTH)], [out]) if emit_vgn else ([], [])
    return pl.pallas_call(
        functools.partial(_inproj_kernel, n_chunks=tm // CHUNK,
                          kv_transposed=seq_transposed is not None, n_cast=len(cast_along),
                          emit_vgn=emit_vgn),
        grid=(steps,),
        in_specs=[row_spec(D_MODEL), full(g), full(w_bf), full(wkv_t_bf), full(sgu_g), full(wm_cat),
                  full(bias_full), full(og_mlp)] + cast_specs,
        out_specs=[row_spec(ATTN_WIDTH), kv_spec, kv_spec, row_spec(MLP_WIDTH)] + vgn_spec
        + cast_specs,
        out_shape=[out, kv_out, kv_out, jax.ShapeDtypeStruct((rows, MLP_WIDTH), BF16)] + vgn_out
        + [jax.ShapeDtypeStruct(a.shape, BF16) for a in cast_along],
        compiler_params=pltpu.CompilerParams(
            dimension_semantics=("arbitrary",), vmem_limit_bytes=VMEM_LIMIT),
    )(x, g, w_bf, wkv_t_bf, sgu_g, wm_cat, bias_full, og_mlp, *cast_along)


V_ROWS = HEAD_DIM + BF16_ROWS
LOG2_E = 1.4426950408889634
MASKED = -1e30
Q_TILE = MOBA_BLOCK
COL_R, COL_J, COL_SEL = 0, 2, 8


def _bf16_split(x):
    mantissa, exponent = math.frexp(x)
    high = math.ldexp(round(mantissa * 256.0) / 256.0, exponent)
    return high, x - high


def _attn_prompt_kernel(q_ref, k_ref, v_ref, og_ref, o_ref, ka_scr, vt_scr, kmean_scr, qa_scr,
                        m_scr, alpha_scr, acc_scr, s_scr, *, n_blocks):
    cur = pl.program_id(1)

    @pl.when(cur == 0)
    def _():
        key = lax.broadcasted_iota(jnp.int32, (MOBA_BLOCK, HEAD_DIM), 0)
        col = lax.broadcasted_iota(jnp.int32, (MOBA_BLOCK, HEAD_DIM), 1)
        ones = jnp.ones((BF16_ROWS, MOBA_BLOCK), BF16)
        for n in range(n_blocks):
            keys = slice(n * MOBA_BLOCK, (n + 1) * MOBA_BLOCK)
            kb = k_ref[0, :, keys].T
            kmean_scr[n:n + 1, :] = jnp.sum(kb, axis=0, keepdims=True) * (1.0 / MOBA_BLOCK)
            vt = v_ref[0, :, keys]
            extra = jnp.where(col < COL_J, key,
                              jnp.where(col < COL_SEL, n, jnp.where(col == COL_SEL + n, 1, 0)))
            extra = extra.astype(F32).astype(BF16)
            for h in range(ATTN_HEADS):
                lanes = slice(h * HEAD_DIM, (h + 1) * HEAD_DIM)
                ka_scr[h, n] = jnp.concatenate([kb[:, lanes].astype(BF16), extra], axis=1)
                vt_scr[h, n, :HEAD_DIM, :] = vt[lanes, :].astype(BF16)
                vt_scr[h, n, HEAD_DIM:, :] = ones

    q_t = q_ref[...].T
    row = lax.broadcasted_iota(jnp.int32, (HEAD_DIM, Q_TILE), 0)
    blk = lax.broadcasted_iota(jnp.int32, (n_blocks, Q_TILE), 0)
    for h in range(ATTN_HEADS):
        lanes = slice(h * HEAD_DIM, (h + 1) * HEAD_DIM)
        hi, lo = _bf16_split(LOG2_E * 2.0 ** (-8.0 * (h + 1) / ATTN_HEADS))
        qh_t = q_t[lanes, :]
        gate = jnp.dot(kmean_scr[:, lanes], qh_t, precision=lax.Precision.HIGHEST,
                       preferred_element_type=F32)
        keep = _top_rank_select(gate, cur, MOBA_TOPK, 0) | (blk >= cur)
        sel_rows = jnp.where(keep, 0.0, MASKED)
        alibi = jnp.where(row == COL_R, hi,
                          jnp.where(row == COL_R + 1, lo,
                                    jnp.where(row == COL_J, hi * MOBA_BLOCK,
                                              jnp.where(row == COL_J + 1, lo * MOBA_BLOCK, 0.0))))
        extra = alibi + jnp.concatenate(
            [jnp.zeros((COL_SEL, Q_TILE), F32), sel_rows,
             jnp.zeros((HEAD_DIM - COL_SEL - n_blocks, Q_TILE), F32)], axis=0)
        qa_scr[h] = jnp.concatenate([qh_t * (LOG2_E * HEAD_DIM ** -0.5), extra], axis=0).astype(BF16)

    def attend_block(j, causal, first):
        for h in range(ATTN_HEADS):
            s = jnp.dot(ka_scr[h, j], qa_scr[h], preferred_element_type=F32)
            if causal is not None:
                s = jnp.where(causal, s, NEG_INF)
            s_scr[h] = s
            m_blk = jnp.max(s, axis=0, keepdims=True)
            if first:
                m_scr[h:h + 1, :] = m_blk
            else:
                m_old = m_scr[h:h + 1, :]
                m_new = jnp.maximum(m_old, m_blk)
                alpha_scr[h:h + 1, :] = jnp.exp2(m_old - m_new)
                m_scr[h:h + 1, :] = m_new
        for h in range(ATTN_HEADS):
            p = jnp.exp2(s_scr[h] - m_scr[h:h + 1, :]).astype(BF16)
            pv = jnp.dot(vt_scr[h, j], p, preferred_element_type=F32)
            acc_scr[h] = pv if first else alpha_scr[h:h + 1, :] * acc_scr[h] + pv

    causal = (lax.broadcasted_iota(jnp.int32, (MOBA_BLOCK, Q_TILE), 0)
              <= lax.broadcasted_iota(jnp.int32, (MOBA_BLOCK, Q_TILE), 1))
    attend_block(cur, causal, True)

    def past_block(j, carry):
        attend_block(j, None, False)
        return carry

    lax.fori_loop(0, cur, past_block, 0)

    outs = []
    for h in range(ATTN_HEADS):
        acc = acc_scr[h]
        outs.append(acc[:HEAD_DIM] / acc[HEAD_DIM:HEAD_DIM + 1])
    o_ref[...] = _rms(jnp.concatenate(outs, axis=0).T, og_ref[...]).astype(BF16)


def _attn_prompt(q, k_t, v_t, og_attn, batch, seq):
    n_blocks = seq // MOBA_BLOCK
    assert COL_SEL + n_blocks <= HEAD_DIM and n_blocks % 8 == 0
    q_spec = pl.BlockSpec((Q_TILE, ATTN_WIDTH), lambda b, i: (b * n_blocks + i, 0))
    return pl.pallas_call(
        functools.partial(_attn_prompt_kernel, n_blocks=n_blocks),
        grid=(batch, n_blocks),
        in_specs=[q_spec,
                  pl.BlockSpec((1, ATTN_WIDTH, seq), lambda b, i: (b, 0, 0)),
                  pl.BlockSpec((1, ATTN_WIDTH, seq), lambda b, i: (b, 0, 0)),
                  pl.BlockSpec(og_attn.shape, lambda b, i: (0, 0))],
        out_specs=q_spec,
        out_shape=jax.ShapeDtypeStruct(q.shape, BF16),
        scratch_shapes=[pltpu.VMEM((ATTN_HEADS, n_blocks, MOBA_BLOCK, 2 * HEAD_DIM), BF16),
                        pltpu.VMEM((ATTN_HEADS, n_blocks, V_ROWS, MOBA_BLOCK), BF16),
                        pltpu.VMEM((n_blocks, ATTN_WIDTH), F32),
                        pltpu.VMEM((ATTN_HEADS, 2 * HEAD_DIM, Q_TILE), BF16),
                        pltpu.VMEM((ATTN_HEADS, Q_TILE), F32),
                        pltpu.VMEM((ATTN_HEADS, Q_TILE), F32),
                        pltpu.VMEM((ATTN_HEADS, V_ROWS, Q_TILE), F32),
                        pltpu.VMEM((ATTN_HEADS, MOBA_BLOCK, Q_TILE), F32)],
        compiler_params=pltpu.CompilerParams(
            dimension_semantics=("arbitrary", "arbitrary"), vmem_limit_bytes=VMEM_LIMIT),
    )(q, k_t, v_t, og_attn)


SEQS_PER_STEP = 2


def _attn_sample_kernel(pt_ref, q_ref, kn_ref, vn_ref, og_ref, ck_hbm, cv_hbm, o_ref,
                        kbuf, vbuf, sem_k, sem_v, *, n_pages, page, t_new, seq_lo, n_steps):
    step = pl.program_id(0)
    slot = step % 2
    pages_per_step = SEQS_PER_STEP * n_pages

    def fetch(step_, slot_):
        first = (seq_lo + step_ * SEQS_PER_STEP) * n_pages
        for j in range(pages_per_step):
            page_id = pt_ref[first + j]
            pltpu.make_async_copy(ck_hbm.at[page_id], kbuf.at[slot_, j], sem_k.at[slot_]).start()
            pltpu.make_async_copy(cv_hbm.at[page_id], vbuf.at[slot_, j], sem_v.at[slot_]).start()

    @pl.when(step == 0)
    def _():
        fetch(0, 0)

    @pl.when(step + 1 < n_steps)
    def _():
        fetch(step + 1, 1 - slot)

    pltpu.make_async_copy(ck_hbm.at[pl.ds(0, pages_per_step)], kbuf.at[slot], sem_k.at[slot]).wait()
    pltpu.make_async_copy(cv_hbm.at[pl.ds(0, pages_per_step)], vbuf.at[slot], sem_v.at[slot]).wait()

    for i in range(SEQS_PER_STEP):
        kp = [kbuf.at[slot, pl.ds(i * n_pages + p, 1)] for p in range(n_pages)]
        vp = [vbuf.at[slot, pl.ds(i * n_pages + p, 1)] for p in range(n_pages)]
        rows = slice(i * t_new, (i + 1) * t_new)
        att = _attend_one_sample(q_ref[rows, :], kn_ref[rows, :], vn_ref[rows, :], kp, vp, page,
                                 t_new)
        o_ref[rows, :] = _rms(att, og_ref[...])


def _attend_one_sample(q, kn, vn, kp, vp, page, t_new):
    n_pages = len(kp)
    n_cols = t_new * ATTN_HEADS
    pages_per_block = MOBA_BLOCK // page
    n_past = n_pages // pages_per_block
    past_len = n_pages * page

    lane_h = lax.broadcasted_iota(jnp.int32, (n_cols, ATTN_WIDTH), 1) // HEAD_DIM
    row = lax.broadcasted_iota(jnp.int32, (n_cols, 1), 0)
    row_h = row % ATTN_HEADS
    row_t = row // ATTN_HEADS
    own_head = lane_h == row_h
    qrep = jnp.concatenate(
        [jnp.broadcast_to(q[t:t + 1, :], (ATTN_HEADS, ATTN_WIDTH)) for t in range(t_new)], axis=0)
    qbd = jnp.where(own_head, qrep, 0.0)
    qbd_s = (qbd * (HEAD_DIM ** -0.5)).astype(BF16)
    slope = jnp.exp2(-8.0 * (row_h + 1).astype(F32) / ATTN_HEADS)

    def block_pages(refs_, n):
        return [refs_[i][0].reshape(ATTN_WIDTH, page)
                for i in range(n * pages_per_block, (n + 1) * pages_per_block)]

    key_off = lax.broadcasted_iota(jnp.int32, (1, MOBA_BLOCK), 1)
    k_sums, s_past = [], []
    for n in range(n_past):
        pages = block_pages(kp, n)
        k_sums.append(jnp.sum(sum(pages), axis=1, keepdims=True))
        kb_t = jnp.concatenate(pages, axis=1).astype(BF16)
        dist = (past_len + row_t) - (n * MOBA_BLOCK + key_off)
        s_past.append(jnp.dot(qbd_s, kb_t, preferred_element_type=F32) - slope * dist.astype(F32))
    kmean = jnp.concatenate(k_sums, axis=1) * (1.0 / MOBA_BLOCK)
    gate = jnp.dot(qbd, kmean, precision=lax.Precision.HIGHEST, preferred_element_type=F32)
    sel = _top_rank_select(gate, n_past, MOBA_TOPK, 1)

    s_own = lax.dot_general(qbd_s, kn.astype(BF16), _NT, preferred_element_type=F32)
    dist = row_t - lax.broadcasted_iota(jnp.int32, (1, t_new), 1)
    s_own = jnp.where(dist >= 0, s_own - slope * dist.astype(F32), NEG_INF)
    m = jnp.max(s_own, axis=-1, keepdims=True)
    for n in range(n_past):
        s_past[n] = jnp.where(sel[:, n:n + 1], s_past[n], NEG_INF)
        m = jnp.maximum(m, jnp.max(s_past[n], axis=-1, keepdims=True))

    p = jnp.exp(s_own - m)
    l = jnp.sum(p, axis=-1, keepdims=True)
    acc = sum(p[:, t:t + 1] * vn[t:t + 1, :] for t in range(t_new))
    for n in range(n_past):
        vb_t = jnp.concatenate(block_pages(vp, n), axis=1).astype(BF16)
        p = jnp.exp(s_past[n] - m)
        l = l + jnp.sum(p, axis=-1, keepdims=True)
        acc = acc + lax.dot_general(p.astype(BF16), vb_t, _NT, preferred_element_type=F32)

    out = jnp.where(own_head, acc / l, 0.0)
    return jnp.sum(out.reshape(t_new, ATTN_HEADS, ATTN_WIDTH), axis=1)


def _attn_sample(q, kn, vn, og_attn, cache_kt, cache_vt, page_table, t_new, seq_lo, n_seq):
    n_pages = page_table.shape[1]
    page = cache_kt.shape[-1]
    assert n_seq % SEQS_PER_STEP == 0 and seq_lo % SEQS_PER_STEP == 0
    step_rows = SEQS_PER_STEP * t_new
    step_lo = seq_lo // SEQS_PER_STEP
    in_rows = pl.BlockSpec((step_rows, ATTN_WIDTH), lambda i, pt: (step_lo + i, 0))
    out_rows = pl.BlockSpec((step_rows, ATTN_WIDTH), lambda i, pt: (i, 0))

    n_steps = n_seq // SEQS_PER_STEP
    page_buf = pltpu.VMEM((2, SEQS_PER_STEP * n_pages, ATTN_HEADS, HEAD_DIM, page), F32)
    return pl.pallas_call(
        functools.partial(_attn_sample_kernel, n_pages=n_pages, page=page, t_new=t_new,
                          seq_lo=seq_lo, n_steps=n_steps),
        grid_spec=pltpu.PrefetchScalarGridSpec(
            num_scalar_prefetch=1,
            grid=(n_steps,),
            in_specs=([in_rows] * 3 + [pl.BlockSpec(og_attn.shape, lambda i, pt: (0, 0)),
                                       pl.BlockSpec(memory_space=pl.ANY),
                                       pl.BlockSpec(memory_space=pl.ANY)]),
            out_specs=out_rows,
            scratch_shapes=[page_buf, page_buf, pltpu.SemaphoreType.DMA((2,)),
                            pltpu.SemaphoreType.DMA((2,))]),
        out_shape=jax.ShapeDtypeStruct((n_seq * t_new, ATTN_WIDTH), F32),
        compiler_params=pltpu.CompilerParams(
            dimension_semantics=("arbitrary",), vmem_limit_bytes=VMEM_LIMIT),
    )(page_table.reshape(-1), q, kn, vn, og_attn, cache_kt, cache_vt)


def _outproj_kernel(att_ref, gmn_ref, x_ref, wo_ref, nfg_ref, wr_ref, br_ref, x1_ref, *route_refs,
                    grouped):
    mix = (jnp.dot(att_ref[...].astype(BF16), wo_ref[:ATTN_WIDTH, :], preferred_element_type=F32)
           + jnp.dot(gmn_ref[...], wo_ref[ATTN_WIDTH:, :], preferred_element_type=F32))
    x1 = x_ref[...] + mix
    x1_ref[...] = x1
    h2 = _rms(x1, nfg_ref[...])
    h2_hi = h2.astype(BF16)

    h2_lo = (h2 - h2_hi.astype(F32)).astype(BF16)
    hi_dot = jnp.dot(h2_hi, wr_ref[...], preferred_element_type=F32)
    lo_dot = jnp.dot(h2_lo, wr_ref[:, :LANES], preferred_element_type=F32)
    logits = hi_dot[:, :LANES] + hi_dot[:, LANES:] + lo_dot + br_ref[...]
    lt = logits.T
    tm = lt.shape[1]
    row4 = lax.broadcasted_iota(jnp.int32, (MOE_GROUPS, tm), 0)

    def first_argmax(v):
        vmax = jnp.max(v, axis=0, keepdims=True)
        idx = jnp.min(jnp.where(v == vmax, row4, MOE_GROUPS), axis=0, keepdims=True)
        return vmax, idx

    glog = lt[:MOE_GROUPS]
    ge = jnp.exp(glog - jnp.max(glog, axis=0, keepdims=True))
    gprob = ge / jnp.sum(ge, axis=0, keepdims=True)
    p_g, g_idx = first_argmax(gprob)
    elog = lt[MOE_GROUPS:MOE_GROUPS + EXPERTS_PER_GROUP]
    for g in range(1, MOE_GROUPS):
        lo = MOE_GROUPS + g * EXPERTS_PER_GROUP
        elog = jnp.where(g_idx == g, lt[lo:lo + EXPERTS_PER_GROUP], elog)
    l1, i1 = first_argmax(elog)
    l2, i2 = first_argmax(jnp.where(row4 == i1, NEG_INF, elog))
    e2 = jnp.exp(l2 - l1)
    denom = 1.0 + e2
    w1 = (1.0 / denom) * p_g
    w2 = (e2 / denom) * p_g
    if not grouped:
        h2_ref, comb_ref = route_refs
        h2_ref[...] = h2_hi
        lane_row = lax.broadcasted_iota(jnp.int32, (LANES, tm), 0)
        base = g_idx * EXPERTS_PER_GROUP
        comb_t = (jnp.where(lane_row == base + i1, w1, 0.0)
                  + jnp.where(lane_row == base + i2, w2, 0.0))
        comb_ref[...] = comb_t.T
        return

    bucket_ref, wpair_ref = route_refs
    e_lo = jnp.minimum(i1, i2)
    e_hi = jnp.maximum(i1, i2)
    pair = jnp.where(e_lo == 0, e_hi - 1, jnp.where(e_lo == 1, e_hi + 1, PAIRS_PER_GROUP - 1))
    row8 = lax.broadcasted_iota(jnp.int32, (8, tm), 0)
    bucket_ref[...] = jnp.where(row8 == 0, g_idx * PAIRS_PER_GROUP + pair, 0)
    w_lo = jnp.where(i1 < i2, w1, w2)
    w_hi = jnp.where(i1 < i2, w2, w1)
    wpair_ref[...] = jnp.where(row8 == 0, w_lo, jnp.where(row8 == 1, w_hi, 0.0))


def _outproj(att_n, gmn, x, wo_bf, nf_g, w_r, b_r, tm, grouped):
    rows = x.shape[0]
    row_spec = lambda width: pl.BlockSpec((tm, width), lambda i: (i, 0))
    full = lambda a: pl.BlockSpec(a.shape, lambda i: (0,) * a.ndim)
    lane_spec = pl.BlockSpec((8, tm), lambda i: (0, i))
    if grouped:
        route_specs = [lane_spec, lane_spec]
        route_shapes = [jax.ShapeDtypeStruct((8, rows), jnp.int32),
                        jax.ShapeDtypeStruct((8, rows), F32)]
    else:
        route_specs = [row_spec(D_MODEL), row_spec(LANES)]
        route_shapes = [jax.ShapeDtypeStruct((rows, D_MODEL), BF16),
                        jax.ShapeDtypeStruct((rows, LANES), F32)]
    return pl.pallas_call(
        functools.partial(_outproj_kernel, grouped=grouped),
        grid=(rows // tm,),
        in_specs=[row_spec(ATTN_WIDTH), row_spec(MLP_WIDTH), row_spec(D_MODEL), full(wo_bf),
                  full(nf_g), full(w_r), full(b_r)],
        out_specs=[row_spec(D_MODEL)] + route_specs,
        out_shape=[jax.ShapeDtypeStruct((rows, D_MODEL), F32)] + route_shapes,
        compiler_params=pltpu.CompilerParams(
            dimension_semantics=("arbitrary",), vmem_limit_bytes=VMEM_LIMIT),
    )(att_n, gmn, x, wo_bf, nf_g, w_r, b_r)


def _moe_kernel(h_ref, comb_ref, wg_ref, wu_ref, wd_ref, x1_ref, fg_ref, y_ref, acc_ref):
    e = pl.program_id(1)

    @pl.when(e == 0)
    def _():
        acc_ref[...] = jnp.zeros_like(acc_ref)

    h = h_ref[...]
    hg = jnp.dot(h, wg_ref[0], preferred_element_type=F32)
    hu = jnp.dot(h, wu_ref[0], preferred_element_type=F32)
    comb = comb_ref[...]
    lane = lax.broadcasted_iota(jnp.int32, comb.shape, 1)
    c = jnp.sum(jnp.where(lane == e, comb, 0.0), axis=-1, keepdims=True)
    act = hg * (1.0 / (1.0 + jnp.exp(-hg))) * hu * c
    acc_ref[...] += jnp.dot(act.astype(BF16), wd_ref[0], preferred_element_type=F32)

    @pl.when(e == N_EXPERTS - 1)
    def _():
        y_ref[...] = _rms(x1_ref[...] + acc_ref[...], fg_ref[...])


def _moe(h2, comb, wg_bf, wu_bf, wd_bf, x1, final_g, tm):
    rows = h2.shape[0]
    row_spec = lambda width: pl.BlockSpec((tm, width), lambda i, e: (i, 0))
    return pl.pallas_call(
        _moe_kernel,
        grid=(rows // tm, N_EXPERTS),
        in_specs=[row_spec(D_MODEL), row_spec(LANES),
                  pl.BlockSpec((1, D_MODEL, D_EXPERT), lambda i, e: (e, 0, 0)),
                  pl.BlockSpec((1, D_MODEL, D_EXPERT), lambda i, e: (e, 0, 0)),
                  pl.BlockSpec((1, D_EXPERT, D_MODEL), lambda i, e: (e, 0, 0)),
                  row_spec(D_MODEL),
                  pl.BlockSpec(final_g.shape, lambda i, e: (0, 0))],
        out_specs=row_spec(D_MODEL),
        out_shape=jax.ShapeDtypeStruct((rows, D_MODEL), F32),
        scratch_shapes=[pltpu.VMEM((tm, D_MODEL), F32)],
        compiler_params=pltpu.CompilerParams(
            dimension_semantics=("arbitrary", "arbitrary"), vmem_limit_bytes=VMEM_LIMIT),
    )(h2, comb, wg_bf, wu_bf, wd_bf, x1, final_g)


SC_CORES = 2
SC_SUBCORES = 16
SC_WINDOW = 32


def _sc_gather_rows(x, idx):
    n = idx.shape[0]
    width = x.shape[1]
    assert n % SC_WINDOW == 0
    mesh = plsc.VectorSubcoreMesh(core_axis_name="core", subcore_axis_name="subcore",
                                  num_cores=SC_CORES, num_subcores=SC_SUBCORES)

    @pl.kernel(out_type=jax.ShapeDtypeStruct((n, width), x.dtype), mesh=mesh)
    def gather_kernel(x_hbm, idx_hbm, out_hbm):
        def body(idx_vmem, out_vmem):
            pltpu.sync_copy(x_hbm.at[idx_vmem.at[0, pl.ds(0, SC_WINDOW)]], out_vmem)

        pltpu.emit_pipeline(
            body,
            grid=(n // SC_WINDOW,),
            in_specs=[pl.BlockSpec((1, LANES), lambda i: (i, 0))],
            out_specs=[pl.BlockSpec((SC_WINDOW, width), lambda i: (i, 0))],
            core_axis_name=("core", "subcore"),
            dimension_semantics=(pltpu.PARALLEL,),
        )(idx_hbm, out_hbm)

    idx_rows = jnp.pad(idx.reshape(n // SC_WINDOW, SC_WINDOW), ((0, 0), (0, LANES - SC_WINDOW)))
    return gather_kernel(x, idx_rows)


class MoePlan(NamedTuple):
    slot_row: jax.Array
    token_slot: jax.Array
    e_lo: jax.Array
    e_hi: jax.Array
    n_valid: jax.Array
    w_slots: jax.Array


def _moe_plan(bucket, wpair, tm):
    rows = bucket.shape[0]
    n_tiles_max = rows // tm + N_BUCKETS
    i32 = jnp.int32
    b_ids = jnp.arange(N_BUCKETS, dtype=i32)[:, None]

    def lookup(table, keys):
        return jnp.sum(jnp.where(keys[None, :] == b_ids, table[:, None], 0), axis=0)

    _, order, wlo_sorted, whi_sorted = lax.sort(
        (bucket, jnp.arange(rows, dtype=i32), wpair[0], wpair[1]), num_keys=1, is_stable=True)
    position = jnp.argsort(order).astype(i32)
    counts = jnp.sum((bucket[None, :] == b_ids).astype(i32), axis=1)
    starts = jnp.cumsum(counts) - counts
    tiles_b = (counts + tm - 1) // tm
    tile_end = jnp.cumsum(tiles_b)
    tile_start = tile_end - tiles_b
    n_tiles = tile_end[-1]
    token_slot = position + lookup(tile_start * tm - starts, bucket)
    t = jnp.arange(n_tiles_max, dtype=i32)
    tb = jnp.sum((jnp.minimum(t, n_tiles - 1)[None, :] >= tile_end[:, None]).astype(i32), axis=0)
    local = (t - lookup(tile_start, tb)) * tm
    n_valid = jnp.where(t < n_tiles, jnp.clip(lookup(counts, tb) - local, 0, tm), 0)
    slot = local[:, None] + jnp.arange(tm, dtype=i32)[None, :]
    src = ((lookup(starts, tb)[:, None] + slot) % rows).reshape(-1)
    assert rows < 2 ** 24
    sorted_cols = jnp.stack([order.astype(F32), wlo_sorted, whi_sorted], axis=1)
    slot_cols = sorted_cols[src]
    group = tb // PAIRS_PER_GROUP
    pair = tb % PAIRS_PER_GROUP
    table_pad = (0,) * (N_BUCKETS - PAIRS_PER_GROUP)
    e_lo = group * EXPERTS_PER_GROUP + lookup(jnp.asarray(PAIR_LO + table_pad, i32), pair)
    e_hi = group * EXPERTS_PER_GROUP + lookup(jnp.asarray(PAIR_HI + table_pad, i32), pair)
    return MoePlan(slot_cols[:, 0].astype(i32), token_slot.astype(i32), e_lo.astype(i32),
                   e_hi.astype(i32), n_valid.astype(i32), slot_cols[:, 1:])


def _moe_grouped_kernel(elo_ref, ehi_ref, nv_ref, x1_ref, ws_ref, wg_lo, wu_lo, wd_lo,
                        wg_hi, wu_hi, wd_hi, nfg_ref, fg_ref, y_ref):
    del elo_ref, ehi_ref
    t = pl.program_id(0)

    @pl.when(nv_ref[t] > 0)
    def _():
        x1 = x1_ref[...]
        h = _rms(x1, nfg_ref[...]).astype(BF16)
        ws = ws_ref[...]
        moe = jnp.zeros(x1.shape, F32)
        for col, (wg, wu, wd) in enumerate(((wg_lo, wu_lo, wd_lo), (wg_hi, wu_hi, wd_hi))):
            hg = jnp.dot(h, wg[0], preferred_element_type=F32)
            hu = jnp.dot(h, wu[0], preferred_element_type=F32)
            act = hg * (1.0 / (1.0 + jnp.exp(-hg))) * hu * ws[:, col:col + 1]
            moe = moe + jnp.dot(act.astype(BF16), wd[0], preferred_element_type=F32)
        y_ref[...] = _rms(x1 + moe, fg_ref[...])

    @pl.when(nv_ref[t] == 0)
    def _():
        y_ref[...] = jnp.zeros_like(y_ref)


def _moe_grouped(x1_slots, plan, wg_bf, wu_bf, wd_bf, nf_g, final_g, tm):
    n_steps = plan.n_valid.shape[0]
    lo_spec = lambda shape: pl.BlockSpec(shape, lambda t, elo, ehi, nv: (elo[t], 0, 0))
    hi_spec = lambda shape: pl.BlockSpec(shape, lambda t, elo, ehi, nv: (ehi[t], 0, 0))
    up_shape, down_shape = (1, D_MODEL, D_EXPERT), (1, D_EXPERT, D_MODEL)
    const = lambda a: pl.BlockSpec(a.shape, lambda t, elo, ehi, nv: (0, 0))
    row_spec = lambda width: pl.BlockSpec((tm, width), lambda t, elo, ehi, nv: (t, 0))
    return pl.pallas_call(
        _moe_grouped_kernel,
        grid_spec=pltpu.PrefetchScalarGridSpec(
            num_scalar_prefetch=3,
            grid=(n_steps,),
            in_specs=[row_spec(D_MODEL), row_spec(2),
                      lo_spec(up_shape), lo_spec(up_shape), lo_spec(down_shape),
                      hi_spec(up_shape), hi_spec(up_shape), hi_spec(down_shape),
                      const(nf_g), const(final_g)],
            out_specs=row_spec(D_MODEL)),
        out_shape=jax.ShapeDtypeStruct(x1_slots.shape, F32),
        compiler_params=pltpu.CompilerParams(
            dimension_semantics=("arbitrary",), vmem_limit_bytes=VMEM_LIMIT),
    )(plan.e_lo, plan.e_hi, plan.n_valid, x1_slots, plan.w_slots, wg_bf, wu_bf, wd_bf,
      wg_bf, wu_bf, wd_bf, nf_g, final_g)


def _spatial_operands(w_s, b_s, t_chunk):
    reps = CHUNK // t_chunk
    idx = jnp.arange(CHUNK)
    same = (idx[:, None] // t_chunk) == (idx[None, :] // t_chunk)
    causal = (idx[None, :] % t_chunk) <= (idx[:, None] % t_chunk)
    wm = jnp.tile(w_s[:, :t_chunk, :t_chunk], (1, reps, reps)) * (same & causal)
    wm_cat = wm.transpose(1, 0, 2).reshape(CHUNK, MLP_GROUPS * CHUNK).astype(BF16)
    bias = jnp.repeat(jnp.tile(b_s[:, :t_chunk], (1, reps)).T, MLP_CH, axis=1)
    return wm_cat, bias


def kernel(x_prompt, x_sample, cache_k, cache_v, page_table, norm_attn_g, w_in, sgu_g, w_spatial,
           b_spatial, out_g_attn, out_g_mlp, w_out, norm_ffn_g, w_group, b_group, w_router, b_router,
           w_gate, w_up, w_down, final_g):
    depth = w_in.shape[0]
    assert depth == 1, "single decoder layer"
    batch, seq, _ = x_prompt.shape
    n_seq, t_new, _ = x_sample.shape
    assert seq % MOBA_BLOCK == 0 and seq % CHUNK == 0 and CHUNK % t_new == 0
    assert MOBA_BLOCK % cache_k.shape[2] == 0

    row2 = lambda a: a.reshape(1, -1)
    w_in_bf = w_in[0].astype(BF16)
    w_out_bf = w_out[0].astype(BF16)
    n_logits = MOE_GROUPS + N_EXPERTS
    w_r = jnp.concatenate(
        [w_group[0], w_router[0].transpose(1, 0, 2).reshape(D_MODEL, N_EXPERTS),
         jnp.zeros((D_MODEL, LANES - n_logits), F32)], axis=1)
    b_r = jnp.concatenate(
        [b_group[0], b_router[0].reshape(-1), jnp.zeros((LANES - n_logits,), F32)]).reshape(1, LANES)
    w_r_hi = w_r.astype(BF16)
    w_r_hl = jnp.concatenate([w_r_hi, (w_r - w_r_hi.astype(F32)).astype(BF16)], axis=1)
    wkv_t_bf = w_in_bf[:, ATTN_WIDTH:3 * ATTN_WIDTH].T
    ck_t = jnp.transpose(cache_k[0], (0, 2, 3, 1))
    cv_t = jnp.transpose(cache_v[0], (0, 2, 3, 1))

    def project_in(x, t_chunk, tm, seq_transposed, cast_along=(), emit_vgn=True):
        wm_cat, bias_full = _spatial_operands(w_spatial[0], b_spatial[0], t_chunk)
        return _inproj(x, row2(norm_attn_g[0]), w_in_bf, wkv_t_bf, row2(sgu_g[0]), wm_cat, bias_full,
                       row2(out_g_mlp[0]), tm, seq_transposed, cast_along, emit_vgn)

    def project_out(att_n, gmn, x, tm, grouped):
        return _outproj(att_n, gmn, x, w_out_bf, row2(norm_ffn_g[0]), w_r_hl, b_r, tm, grouped)

    og_attn = row2(out_g_attn[0])

    xp = x_prompt.reshape(batch * seq, D_MODEL)
    qp, kp_t, vp_t, gmn_p, wg_bf, wu_bf, wd_bf = project_in(
        xp, CHUNK, TM_PROJ_PROMPT, seq, (w_gate[0], w_up[0], w_down[0]), emit_vgn=False)
    att_p = _attn_prompt(qp, kp_t, vp_t, og_attn, batch, seq)
    x1_p, bucket_p, wpair_p = project_out(att_p, gmn_p, xp, TM_PROJ_PROMPT, True)
    plan = _moe_plan(bucket_p[0], wpair_p[:2], TM_MOE_PROMPT)
    x1_slots = _sc_gather_rows(x1_p, plan.slot_row)

    xs = x_sample.reshape(n_seq * t_new, D_MODEL)
    qs, ks, vs, gmn_s, gvs = project_in(xs, t_new, TM_PROJ_SAMPLE, None)
    half = n_seq // 2
    attend_half = lambda q_all, lo: _attn_sample(q_all, ks, vs, og_attn, ck_t, cv_t, page_table,
                                                 t_new, lo, half)
    att_s0 = attend_half(qs, 0)
    x1_slots, att_s0 = lax.optimization_barrier((x1_slots, att_s0))
    y_slots = _moe_grouped(x1_slots, plan, wg_bf, wu_bf, wd_bf, row2(norm_ffn_g[0]),
                           row2(final_g), TM_MOE_PROMPT)
    y_slots, qs_late = lax.optimization_barrier((y_slots, qs))
    yp = _sc_gather_rows(y_slots, plan.token_slot)
    att_s1 = attend_half(qs_late, half)
    att_s = jnp.concatenate([att_s0, att_s1], axis=0)
    x1_s, h2_s, comb_s = project_out(att_s, gmn_s, xs, TM_PROJ_SAMPLE, False)
    ys = _moe(h2_s, comb_s, wg_bf, wu_bf, wd_bf, x1_s, row2(final_g), TM_MOE_SAMPLE)

    heads = (ATTN_HEADS, HEAD_DIM)
    rows_last = lambda a_t: a_t.reshape(batch, *heads, seq).transpose(0, 3, 1, 2)[None]
    return (yp.reshape(batch, seq, D_MODEL),
            ys.reshape(n_seq, t_new, D_MODEL),
            rows_last(kp_t),
            rows_last(vp_t),
            ks.reshape(depth, n_seq, t_new, *heads),
            vs.reshape(depth, n_seq, t_new, *heads),
            gvs.reshape(depth, n_seq, t_new, MLP_WIDTH))
```

```python
import functools
import math
from typing import NamedTuple

import jax
import jax.numpy as jnp
from jax import lax
from jax.experimental import pallas as pl
from jax.experimental.pallas import tpu as pltpu
from jax.experimental.pallas import tpu_sc as plsc

D_MODEL = 1024
ATTN_HEADS = 8
HEAD_DIM = 64
ATTN_WIDTH = ATTN_HEADS * HEAD_DIM
MOBA_BLOCK = 256
MOBA_TOPK = 3
QUERY_BLOCK = 128
MLP_GROUPS = 8
MLP_CH = 64
MLP_WIDTH = MLP_GROUPS * MLP_CH
CHUNK = 128
IN_WIDTH = 3 * ATTN_WIDTH + 2 * MLP_WIDTH
MOE_GROUPS = 4
EXPERTS_PER_GROUP = 4
N_EXPERTS = MOE_GROUPS * EXPERTS_PER_GROUP
D_EXPERT = D_MODEL // 2
EPS = 1e-6
PAIR_LO = (0, 0, 0, 1, 1, 2)
PAIR_HI = (1, 2, 3, 2, 3, 3)
PAIRS_PER_GROUP = len(PAIR_LO)
N_BUCKETS = MOE_GROUPS * PAIRS_PER_GROUP

LANES = 128
BF16_ROWS = 16
VMEM_LIMIT = 56 * 1024 * 1024

TM_PROJ_PROMPT = 512
TM_PROJ_SAMPLE = 128
TM_MOE_PROMPT = 256
TM_MOE_SAMPLE = 512

F32 = jnp.float32
BF16 = jnp.bfloat16
NEG_INF = float("-inf")
_NT = (((1,), (1,)), ((), ()))


def _rms(x, g):
    return x * lax.rsqrt(jnp.mean(x * x, axis=-1, keepdims=True) + EPS) * g


def _gelu(x):
    return 0.5 * x * (1.0 + jnp.tanh(0.7978845608028654 * (x + 0.044715 * (x * x * x))))


def _top_rank_select(gate, n_past, n_keep, axis):
    nb = gate.shape[axis]
    n_idx = lax.broadcasted_iota(jnp.int32, gate.shape, axis)
    rank = jnp.zeros(gate.shape, jnp.int32)
    for m in range(nb):
        gm = gate[:, m:m + 1] if axis == 1 else gate[m:m + 1, :]
        beats = jnp.where(gm > gate, 1, jnp.where(gm == gate, jnp.where(m < n_idx, 1, 0), 0))
        rank = rank + jnp.where(m < n_past, beats, 0)
    return jnp.where(n_idx < n_past, rank, n_keep) < n_keep


def _inproj_kernel(x_ref, g_ref, w_ref, wkv_t_ref, sgu_ref, wm_ref, bias_ref, og_ref, *refs,
                   n_chunks, kv_transposed, n_cast, emit_vgn):
    n_main = 5 if emit_vgn else 4
    cast_in, main, cast_out = refs[:n_cast], refs[n_cast:n_cast + n_main], refs[n_cast + n_main:]
    q_ref, k_ref, v_ref, gmn_ref = main[:4]
    for src_ref, dst_ref in zip(cast_in, cast_out):
        dst_ref[...] = src_ref[...].astype(BF16)

    h = _rms(x_ref[...], g_ref[...]).astype(BF16)

    def proj(lo, width):
        return jnp.dot(h, w_ref[:, lo:lo + width], preferred_element_type=F32)

    vgn = _rms(_gelu(proj(3 * ATTN_WIDTH + MLP_WIDTH, MLP_WIDTH)), sgu_ref[...])
    if emit_vgn:
        main[4][...] = vgn
    gu = _gelu(proj(3 * ATTN_WIDTH, MLP_WIDTH))

    lane_grp = lax.broadcasted_iota(jnp.int32, (CHUNK, MLP_WIDTH), 1) // MLP_CH
    mixed = []
    for c in range(n_chunks):
        vc = vgn[c * CHUNK:(c + 1) * CHUNK].astype(BF16)
        vbd = jnp.concatenate(
            [jnp.where(lane_grp == g, vc, jnp.zeros_like(vc)) for g in range(MLP_GROUPS)], axis=0)
        mixed.append(jnp.dot(wm_ref[...], vbd, preferred_element_type=F32) + bias_ref[...])

    q_ref[...] = proj(0, ATTN_WIDTH)
    for c in range(n_chunks):
        rows = slice(c * CHUNK, (c + 1) * CHUNK)
        gmn_ref[rows, :] = _rms(gu[rows] * mixed[c], og_ref[...]).astype(BF16)
    if kv_transposed:
        kv_t = lax.dot_general(wkv_t_ref[...], h, _NT, preferred_element_type=F32)
        k_ref[0] = kv_t[:ATTN_WIDTH]
        v_ref[0] = kv_t[ATTN_WIDTH:]
    else:
        k_ref[...] = proj(ATTN_WIDTH, ATTN_WIDTH)
        v_ref[...] = proj(2 * ATTN_WIDTH, ATTN_WIDTH)


def _inproj(x, g, w_bf, wkv_t_bf, sgu_g, wm_cat, bias_full, og_mlp, tm, seq_transposed=None,
            cast_along=(), emit_vgn=True):
    rows = x.shape[0]
    steps = rows // tm
    row_spec = lambda width: pl.BlockSpec((tm, width), lambda i: (i, 0))
    full = lambda a: pl.BlockSpec(a.shape, lambda i: (0,) * a.ndim)
    out = jax.ShapeDtypeStruct((rows, ATTN_WIDTH), F32)
    kv_spec, kv_out = row_spec(ATTN_WIDTH), out
    if seq_transposed is not None:
        tiles = seq_transposed // tm
        kv_spec = pl.BlockSpec((1, ATTN_WIDTH, tm), lambda i: (i // tiles, 0, i % tiles))
        kv_out = jax.ShapeDtypeStruct((rows // seq_transposed, ATTN_WIDTH, seq_transposed), F32)

    def slab_spec(a):
        per = steps // a.shape[0]
        assert per * a.shape[0] == steps and a.shape[1] % per == 0
        return pl.BlockSpec((1, a.shape[1] // per, a.shape[2]), lambda i: (i // per, i % per, 0))

    cast_specs = [slab_spec(a) for a in cast_along]
    vgn_spec, vgn_out = ([row_spec(MLP_WIDTH)], [out]) if emit_vgn else ([], [])
    return pl.pallas_call(
        functools.partial(_inproj_kernel, n_chunks=tm // CHUNK,
                          kv_transposed=seq_transposed is not None, n_cast=len(cast_along),
                          emit_vgn=emit_vgn),
        grid=(steps,),
        in_specs=[row_spec(D_MODEL), full(g), full(w_bf), full(wkv_t_bf), full(sgu_g), full(wm_cat),
                  full(bias_full), full(og_mlp)] + cast_specs,
        out_specs=[row_spec(ATTN_WIDTH), kv_spec, kv_spec, row_spec(MLP_WIDTH)] + vgn_spec
        + cast_specs,
        out_shape=[out, kv_out, kv_out, jax.ShapeDtypeStruct((rows, MLP_WIDTH), BF16)] + vgn_out
        + [jax.ShapeDtypeStruct(a.shape, BF16) for a in cast_along],
        compiler_params=pltpu.CompilerParams(
            dimension_semantics=("arbitrary",), vmem_limit_bytes=VMEM_LIMIT),
    )(x, g, w_bf, wkv_t_bf, sgu_g, wm_cat, bias_full, og_mlp, *cast_along)


V_ROWS = HEAD_DIM + BF16_ROWS
LOG2_E = 1.4426950408889634
MASKED = -1e30
Q_TILE = MOBA_BLOCK
COL_R, COL_J, COL_SEL = 0, 2, 8


def _bf16_split(x):
    mantissa, exponent = math.frexp(x)
    high = math.ldexp(round(mantissa * 256.0) / 256.0, exponent)
    return high, x - high


def _attn_prompt_kernel(q_ref, k_ref, v_ref, og_ref, o_ref, ka_scr, vt_scr, kmean_scr, qa_scr,
                        m_scr, alpha_scr, acc_scr, s_scr, *, n_blocks):
    cur = pl.program_id(1)

    @pl.when(cur == 0)
    def _():
        key = lax.broadcasted_iota(jnp.int32, (MOBA_BLOCK, HEAD_DIM), 0)
        col = lax.broadcasted_iota(jnp.int32, (MOBA_BLOCK, HEAD_DIM), 1)
        ones = jnp.ones((BF16_ROWS, MOBA_BLOCK), BF16)
        for n in range(n_blocks):
            keys = slice(n * MOBA_BLOCK, (n + 1) * MOBA_BLOCK)
            kb = k_ref[0, :, keys].T
            kmean_scr[n:n + 1, :] = jnp.sum(kb, axis=0, keepdims=True) * (1.0 / MOBA_BLOCK)
            vt = v_ref[0, :, keys]
            extra = jnp.where(col < COL_J, key,
                              jnp.where(col < COL_SEL, n, jnp.where(col == COL_SEL + n, 1, 0)))
            extra = extra.astype(F32).astype(BF16)
            for h in range(ATTN_HEADS):
                lanes = slice(h * HEAD_DIM, (h + 1) * HEAD_DIM)
                ka_scr[h, n] = jnp.concatenate([kb[:, lanes].astype(BF16), extra], axis=1)
                vt_scr[h, n, :HEAD_DIM, :] = vt[lanes, :].astype(BF16)
                vt_scr[h, n, HEAD_DIM:, :] = ones

    q_t = q_ref[...].T
    row = lax.broadcasted_iota(jnp.int32, (HEAD_DIM, Q_TILE), 0)
    blk = lax.broadcasted_iota(jnp.int32, (n_blocks, Q_TILE), 0)
    for h in range(ATTN_HEADS):
        lanes = slice(h * HEAD_DIM, (h + 1) * HEAD_DIM)
        hi, lo = _bf16_split(LOG2_E * 2.0 ** (-8.0 * (h + 1) / ATTN_HEADS))
        qh_t = q_t[lanes, :]
        gate = jnp.dot(kmean_scr[:, lanes], qh_t, precision=lax.Precision.HIGHEST,
                       preferred_element_type=F32)
        keep = _top_rank_select(gate, cur, MOBA_TOPK, 0) | (blk >= cur)
        sel_rows = jnp.where(keep, 0.0, MASKED)
        alibi = jnp.where(row == COL_R, hi,
                          jnp.where(row == COL_R + 1, lo,
                                    jnp.where(row == COL_J, hi * MOBA_BLOCK,
                                              jnp.where(row == COL_J + 1, lo * MOBA_BLOCK, 0.0))))
        extra = alibi + jnp.concatenate(
            [jnp.zeros((COL_SEL, Q_TILE), F32), sel_rows,
             jnp.zeros((HEAD_DIM - COL_SEL - n_blocks, Q_TILE), F32)], axis=0)
        qa_scr[h] = jnp.concatenate([qh_t * (LOG2_E * HEAD_DIM ** -0.5), extra], axis=0).astype(BF16)

    def attend_block(j, causal, first):
        for h in range(ATTN_HEADS):
            s = jnp.dot(ka_scr[h, j], qa_scr[h], preferred_element_type=F32)
            if causal is not None:
                s = jnp.where(causal, s, NEG_INF)
            s_scr[h] = s
            m_blk = jnp.max(s, axis=0, keepdims=True)
            if first:
                m_scr[h:h + 1, :] = m_blk
            else:
                m_old = m_scr[h:h + 1, :]
                m_new = jnp.maximum(m_old, m_blk)
                alpha_scr[h:h + 1, :] = jnp.exp2(m_old - m_new)
                m_scr[h:h + 1, :] = m_new
        for h in range(ATTN_HEADS):
            p = jnp.exp2(s_scr[h] - m_scr[h:h + 1, :]).astype(BF16)
            pv = jnp.dot(vt_scr[h, j], p, preferred_element_type=F32)
            acc_scr[h] = pv if first else alpha_scr[h:h + 1, :] * acc_scr[h] + pv

    causal = (lax.broadcasted_iota(jnp.int32, (MOBA_BLOCK, Q_TILE), 0)
              <= lax.broadcasted_iota(jnp.int32, (MOBA_BLOCK, Q_TILE), 1))
    attend_block(cur, causal, True)

    def past_block(j, carry):
        attend_block(j, None, False)
        return carry

    lax.fori_loop(0, cur, past_block, 0)

    outs = []
    for h in range(ATTN_HEADS):
        acc = acc_scr[h]
        outs.append(acc[:HEAD_DIM] / acc[HEAD_DIM:HEAD_DIM + 1])
    o_ref[...] = _rms(jnp.concatenate(outs, axis=0).T, og_ref[...]).astype(BF16)


def _attn_prompt_scratch(n_blocks):
    return [pltpu.VMEM((ATTN_HEADS, n_blocks, MOBA_BLOCK, 2 * HEAD_DIM), BF16),
            pltpu.VMEM((ATTN_HEADS, n_blocks, V_ROWS, MOBA_BLOCK), BF16),
            pltpu.VMEM((n_blocks, ATTN_WIDTH), F32),
            pltpu.VMEM((ATTN_HEADS, 2 * HEAD_DIM, Q_TILE), BF16),
            pltpu.VMEM((ATTN_HEADS, Q_TILE), F32),
            pltpu.VMEM((ATTN_HEADS, Q_TILE), F32),
            pltpu.VMEM((ATTN_HEADS, V_ROWS, Q_TILE), F32),
            pltpu.VMEM((ATTN_HEADS, MOBA_BLOCK, Q_TILE), F32)]


SEQS_PER_STEP = 2


def _sample_pair(step, between, pt_ref, q_ref, kn_ref, vn_ref, og_ref, ck_hbm, cv_hbm, o_ref,
                 kbuf, vbuf, sem_k, sem_v, *, n_pages, page, t_new, seq_lo, n_active):
    def fetch(seq, slot):
        for p in range(n_pages):
            page_id = pt_ref[seq * n_pages + p]
            pltpu.make_async_copy(ck_hbm.at[page_id], kbuf.at[slot, p], sem_k.at[slot]).start()
            pltpu.make_async_copy(cv_hbm.at[page_id], vbuf.at[slot, p], sem_v.at[slot]).start()

    def attend(slot):
        pltpu.make_async_copy(ck_hbm.at[pl.ds(0, n_pages)], kbuf.at[slot], sem_k.at[slot]).wait()
        pltpu.make_async_copy(cv_hbm.at[pl.ds(0, n_pages)], vbuf.at[slot], sem_v.at[slot]).wait()
        kp = [kbuf.at[slot, pl.ds(p, 1)] for p in range(n_pages)]
        vp = [vbuf.at[slot, pl.ds(p, 1)] for p in range(n_pages)]
        rows = slice(slot * t_new, (slot + 1) * t_new)
        att = _attend_one_sample(q_ref[rows, :], kn_ref[rows, :], vn_ref[rows, :], kp, vp, page,
                                 t_new)
        o_ref[rows, :] = _rms(att, og_ref[...])

    first = seq_lo + SEQS_PER_STEP * step
    active = step < n_active
    more = step + 1 < n_active

    @pl.when(step == 0)
    def _():
        fetch(seq_lo, 0)
        fetch(seq_lo + 1, 1)

    for slot in range(SEQS_PER_STEP):
        pl.when(active)(functools.partial(attend, slot))
        pl.when(more)(functools.partial(fetch, first + SEQS_PER_STEP + slot, slot))
        if slot == 0:
            between()


def _attn_sample_kernel(pt_ref, *refs, **params):
    _sample_pair(pl.program_id(0), lambda: None, pt_ref, *refs, **params)


N_PROMPT_SCRATCH = 8


def _attn_fused_kernel(pt_ref, q_ref, k_ref, v_ref, og_ref, qs_ref, kn_ref, vn_ref, ck_hbm, cv_hbm,
                       o_ref, os_ref, *scratch, n_blocks, sample):
    step = pl.program_id(0) * n_blocks + pl.program_id(1)
    prompt_tile = functools.partial(_attn_prompt_kernel, q_ref, k_ref, v_ref, og_ref, o_ref,
                                    *scratch[:N_PROMPT_SCRATCH], n_blocks=n_blocks)
    _sample_pair(step, prompt_tile, pt_ref, qs_ref, kn_ref, vn_ref, og_ref, ck_hbm, cv_hbm, os_ref,
                 *scratch[N_PROMPT_SCRATCH:], **sample)


def _attend_one_sample(q, kn, vn, kp, vp, page, t_new):
    n_pages = len(kp)
    n_cols = t_new * ATTN_HEADS
    pages_per_block = MOBA_BLOCK // page
    n_past = n_pages // pages_per_block
    past_len = n_pages * page

    lane_h = lax.broadcasted_iota(jnp.int32, (n_cols, ATTN_WIDTH), 1) // HEAD_DIM
    row = lax.broadcasted_iota(jnp.int32, (n_cols, 1), 0)
    row_h = row % ATTN_HEADS
    row_t = row // ATTN_HEADS
    own_head = lane_h == row_h
    qrep = jnp.concatenate(
        [jnp.broadcast_to(q[t:t + 1, :], (ATTN_HEADS, ATTN_WIDTH)) for t in range(t_new)], axis=0)
    qbd = jnp.where(own_head, qrep, 0.0)
    qbd_s = (qbd * (HEAD_DIM ** -0.5)).astype(BF16)
    slope = jnp.exp2(-8.0 * (row_h + 1).astype(F32) / ATTN_HEADS)

    def block_pages(refs_, n):
        return [refs_[i][0].reshape(ATTN_WIDTH, page)
                for i in range(n * pages_per_block, (n + 1) * pages_per_block)]

    key_off = lax.broadcasted_iota(jnp.int32, (1, MOBA_BLOCK), 1)
    k_sums, s_past = [], []
    for n in range(n_past):
        pages = block_pages(kp, n)
        k_sums.append(jnp.sum(sum(pages), axis=1, keepdims=True))
        kb_t = jnp.concatenate(pages, axis=1).astype(BF16)
        dist = (past_len + row_t) - (n * MOBA_BLOCK + key_off)
        s_past.append(jnp.dot(qbd_s, kb_t, preferred_element_type=F32) - slope * dist.astype(F32))
    kmean = jnp.concatenate(k_sums, axis=1) * (1.0 / MOBA_BLOCK)
    gate = jnp.dot(qbd, kmean, precision=lax.Precision.HIGHEST, preferred_element_type=F32)
    sel = _top_rank_select(gate, n_past, MOBA_TOPK, 1)

    s_own = lax.dot_general(qbd_s, kn.astype(BF16), _NT, preferred_element_type=F32)
    dist = row_t - lax.broadcasted_iota(jnp.int32, (1, t_new), 1)
    s_own = jnp.where(dist >= 0, s_own - slope * dist.astype(F32), NEG_INF)
    m = jnp.max(s_own, axis=-1, keepdims=True)
    for n in range(n_past):
        s_past[n] = jnp.where(sel[:, n:n + 1], s_past[n], NEG_INF)
        m = jnp.maximum(m, jnp.max(s_past[n], axis=-1, keepdims=True))

    p = jnp.exp(s_own - m)
    l = jnp.sum(p, axis=-1, keepdims=True)
    acc = sum(p[:, t:t + 1] * vn[t:t + 1, :] for t in range(t_new))
    for n in range(n_past):
        vb_t = jnp.concatenate(block_pages(vp, n), axis=1).astype(BF16)
        p = jnp.exp(s_past[n] - m)
        l = l + jnp.sum(p, axis=-1, keepdims=True)
        acc = acc + lax.dot_general(p.astype(BF16), vb_t, _NT, preferred_element_type=F32)

    out = jnp.where(own_head, acc / l, 0.0)
    return jnp.sum(out.reshape(t_new, ATTN_HEADS, ATTN_WIDTH), axis=1)


def _attn_sample(q, kn, vn, og_attn, cache_kt, cache_vt, page_table, t_new, seq_lo, n_seq):
    n_pages = page_table.shape[1]
    page = cache_kt.shape[-1]
    assert n_seq % SEQS_PER_STEP == 0 and seq_lo % SEQS_PER_STEP == 0
    step_rows = SEQS_PER_STEP * t_new
    step_lo = seq_lo // SEQS_PER_STEP
    in_rows = pl.BlockSpec((step_rows, ATTN_WIDTH), lambda i, pt: (step_lo + i, 0))
    out_rows = pl.BlockSpec((step_rows, ATTN_WIDTH), lambda i, pt: (i, 0))

    n_steps = n_seq // SEQS_PER_STEP
    page_buf = pltpu.VMEM((SEQS_PER_STEP, n_pages, ATTN_HEADS, HEAD_DIM, page), F32)
    return pl.pallas_call(
        functools.partial(_attn_sample_kernel, n_pages=n_pages, page=page, t_new=t_new,
                          seq_lo=seq_lo, n_active=n_steps),
        grid_spec=pltpu.PrefetchScalarGridSpec(
            num_scalar_prefetch=1,
            grid=(n_steps,),
            in_specs=([in_rows] * 3 + [pl.BlockSpec(og_attn.shape, lambda i, pt: (0, 0)),
                                       pl.BlockSpec(memory_space=pl.ANY),
                                       pl.BlockSpec(memory_space=pl.ANY)]),
            out_specs=out_rows,
            scratch_shapes=[page_buf, page_buf, pltpu.SemaphoreType.DMA((2,)),
                            pltpu.SemaphoreType.DMA((2,))]),
        out_shape=jax.ShapeDtypeStruct((n_seq * t_new, ATTN_WIDTH), F32),
        compiler_params=pltpu.CompilerParams(
            dimension_semantics=("arbitrary",), vmem_limit_bytes=VMEM_LIMIT),
    )(page_table.reshape(-1), q, kn, vn, og_attn, cache_kt, cache_vt)


def _attn_prompt_and_samples(q, k_t, v_t, og_attn, batch, seq, qs, kn, vn, cache_kt, cache_vt,
                             page_table, t_new, n_seq):
    n_blocks = seq // MOBA_BLOCK
    assert COL_SEL + n_blocks <= HEAD_DIM and n_blocks % 8 == 0
    n_pages = page_table.shape[1]
    page = cache_kt.shape[-1]
    n_active = n_seq // SEQS_PER_STEP
    assert n_seq % SEQS_PER_STEP == 0 and 1 <= n_active <= batch * n_blocks
    step_rows = SEQS_PER_STEP * t_new
    q_spec = pl.BlockSpec((Q_TILE, ATTN_WIDTH), lambda b, i, pt: (b * n_blocks + i, 0))
    kv_spec = pl.BlockSpec((1, ATTN_WIDTH, seq), lambda b, i, pt: (b, 0, 0))
    s_rows = pl.BlockSpec((step_rows, ATTN_WIDTH),
                          lambda b, i, pt: (jnp.minimum(b * n_blocks + i, n_active - 1), 0))
    page_buf = pltpu.VMEM((SEQS_PER_STEP, n_pages, ATTN_HEADS, HEAD_DIM, page), F32)
    sample = dict(n_pages=n_pages, page=page, t_new=t_new, seq_lo=0, n_active=n_active)
    return pl.pallas_call(
        functools.partial(_attn_fused_kernel, n_blocks=n_blocks, sample=sample),
        grid_spec=pltpu.PrefetchScalarGridSpec(
            num_scalar_prefetch=1,
            grid=(batch, n_blocks),
            in_specs=[q_spec, kv_spec, kv_spec, pl.BlockSpec(og_attn.shape, lambda b, i, pt: (0, 0)),
                      s_rows, s_rows, s_rows,
                      pl.BlockSpec(memory_space=pl.ANY), pl.BlockSpec(memory_space=pl.ANY)],
            out_specs=[q_spec, s_rows],
            scratch_shapes=_attn_prompt_scratch(n_blocks) + [
                page_buf, page_buf, pltpu.SemaphoreType.DMA((2,)), pltpu.SemaphoreType.DMA((2,))]),
        out_shape=[jax.ShapeDtypeStruct(q.shape, BF16),
                   jax.ShapeDtypeStruct((n_seq * t_new, ATTN_WIDTH), F32)],
        compiler_params=pltpu.CompilerParams(
            dimension_semantics=("arbitrary", "arbitrary"), vmem_limit_bytes=VMEM_LIMIT),
    )(page_table.reshape(-1), q, k_t, v_t, og_attn, qs, kn, vn, cache_kt, cache_vt)


def _outproj_kernel(att_ref, gmn_ref, x_ref, wo_ref, nfg_ref, wr_ref, br_ref, x1_ref, *route_refs,
                    grouped):
    mix = (jnp.dot(att_ref[...].astype(BF16), wo_ref[:ATTN_WIDTH, :], preferred_element_type=F32)
           + jnp.dot(gmn_ref[...], wo_ref[ATTN_WIDTH:, :], preferred_element_type=F32))
    x1 = x_ref[...] + mix
    x1_ref[...] = x1
    h2 = _rms(x1, nfg_ref[...])
    h2_hi = h2.astype(BF16)

    h2_lo = (h2 - h2_hi.astype(F32)).astype(BF16)
    hi_dot = jnp.dot(h2_hi, wr_ref[...], preferred_element_type=F32)
    lo_dot = jnp.dot(h2_lo, wr_ref[:, :LANES], preferred_element_type=F32)
    logits = hi_dot[:, :LANES] + hi_dot[:, LANES:] + lo_dot + br_ref[...]
    lt = logits.T
    tm = lt.shape[1]
    row4 = lax.broadcasted_iota(jnp.int32, (MOE_GROUPS, tm), 0)

    def first_argmax(v):
        vmax = jnp.max(v, axis=0, keepdims=True)
        idx = jnp.min(jnp.where(v == vmax, row4, MOE_GROUPS), axis=0, keepdims=True)
        return vmax, idx

    glog = lt[:MOE_GROUPS]
    ge = jnp.exp(glog - jnp.max(glog, axis=0, keepdims=True))
    gprob = ge / jnp.sum(ge, axis=0, keepdims=True)
    p_g, g_idx = first_argmax(gprob)
    elog = lt[MOE_GROUPS:MOE_GROUPS + EXPERTS_PER_GROUP]
    for g in range(1, MOE_GROUPS):
        lo = MOE_GROUPS + g * EXPERTS_PER_GROUP
        elog = jnp.where(g_idx == g, lt[lo:lo + EXPERTS_PER_GROUP], elog)
    l1, i1 = first_argmax(elog)
    l2, i2 = first_argmax(jnp.where(row4 == i1, NEG_INF, elog))
    e2 = jnp.exp(l2 - l1)
    denom = 1.0 + e2
    w1 = (1.0 / denom) * p_g
    w2 = (e2 / denom) * p_g
    if not grouped:
        h2_ref, comb_ref = route_refs
        h2_ref[...] = h2_hi
        lane_row = lax.broadcasted_iota(jnp.int32, (LANES, tm), 0)
        base = g_idx * EXPERTS_PER_GROUP
        comb_t = (jnp.where(lane_row == base + i1, w1, 0.0)
                  + jnp.where(lane_row == base + i2, w2, 0.0))
        comb_ref[...] = comb_t.T
        return

    bucket_ref, wpair_ref = route_refs
    e_lo = jnp.minimum(i1, i2)
    e_hi = jnp.maximum(i1, i2)
    pair = jnp.where(e_lo == 0, e_hi - 1, jnp.where(e_lo == 1, e_hi + 1, PAIRS_PER_GROUP - 1))
    row8 = lax.broadcasted_iota(jnp.int32, (8, tm), 0)
    bucket_ref[...] = jnp.where(row8 == 0, g_idx * PAIRS_PER_GROUP + pair, 0)
    w_lo = jnp.where(i1 < i2, w1, w2)
    w_hi = jnp.where(i1 < i2, w2, w1)
    wpair_ref[...] = jnp.where(row8 == 0, w_lo, jnp.where(row8 == 1, w_hi, 0.0))


def _outproj(att_n, gmn, x, wo_bf, nf_g, w_r, b_r, tm, grouped):
    rows = x.shape[0]
    row_spec = lambda width: pl.BlockSpec((tm, width), lambda i: (i, 0))
    full = lambda a: pl.BlockSpec(a.shape, lambda i: (0,) * a.ndim)
    lane_spec = pl.BlockSpec((8, tm), lambda i: (0, i))
    if grouped:
        route_specs = [lane_spec, lane_spec]
        route_shapes = [jax.ShapeDtypeStruct((8, rows), jnp.int32),
                        jax.ShapeDtypeStruct((8, rows), F32)]
    else:
        route_specs = [row_spec(D_MODEL), row_spec(LANES)]
        route_shapes = [jax.ShapeDtypeStruct((rows, D_MODEL), BF16),
                        jax.ShapeDtypeStruct((rows, LANES), F32)]
    return pl.pallas_call(
        functools.partial(_outproj_kernel, grouped=grouped),
        grid=(rows // tm,),
        in_specs=[row_spec(ATTN_WIDTH), row_spec(MLP_WIDTH), row_spec(D_MODEL), full(wo_bf),
                  full(nf_g), full(w_r), full(b_r)],
        out_specs=[row_spec(D_MODEL)] + route_specs,
        out_shape=[jax.ShapeDtypeStruct((rows, D_MODEL), F32)] + route_shapes,
        compiler_params=pltpu.CompilerParams(
            dimension_semantics=("arbitrary",), vmem_limit_bytes=VMEM_LIMIT),
    )(att_n, gmn, x, wo_bf, nf_g, w_r, b_r)


def _moe_kernel(h_ref, comb_ref, wg_ref, wu_ref, wd_ref, x1_ref, fg_ref, y_ref, acc_ref):
    e = pl.program_id(1)

    @pl.when(e == 0)
    def _():
        acc_ref[...] = jnp.zeros_like(acc_ref)

    h = h_ref[...]
    hg = jnp.dot(h, wg_ref[0], preferred_element_type=F32)
    hu = jnp.dot(h, wu_ref[0], preferred_element_type=F32)
    comb = comb_ref[...]
    lane = lax.broadcasted_iota(jnp.int32, comb.shape, 1)
    c = jnp.sum(jnp.where(lane == e, comb, 0.0), axis=-1, keepdims=True)
    act = hg * (1.0 / (1.0 + jnp.exp(-hg))) * hu * c
    acc_ref[...] += jnp.dot(act.astype(BF16), wd_ref[0], preferred_element_type=F32)

    @pl.when(e == N_EXPERTS - 1)
    def _():
        y_ref[...] = _rms(x1_ref[...] + acc_ref[...], fg_ref[...])


def _moe(h2, comb, wg_bf, wu_bf, wd_bf, x1, final_g, tm):
    rows = h2.shape[0]
    row_spec = lambda width: pl.BlockSpec((tm, width), lambda i, e: (i, 0))
    return pl.pallas_call(
        _moe_kernel,
        grid=(rows // tm, N_EXPERTS),
        in_specs=[row_spec(D_MODEL), row_spec(LANES),
                  pl.BlockSpec((1, D_MODEL, D_EXPERT), lambda i, e: (e, 0, 0)),
                  pl.BlockSpec((1, D_MODEL, D_EXPERT), lambda i, e: (e, 0, 0)),
                  pl.BlockSpec((1, D_EXPERT, D_MODEL), lambda i, e: (e, 0, 0)),
                  row_spec(D_MODEL),
                  pl.BlockSpec(final_g.shape, lambda i, e: (0, 0))],
        out_specs=row_spec(D_MODEL),
        out_shape=jax.ShapeDtypeStruct((rows, D_MODEL), F32),
        scratch_shapes=[pltpu.VMEM((tm, D_MODEL), F32)],
        compiler_params=pltpu.CompilerParams(
            dimension_semantics=("arbitrary", "arbitrary"), vmem_limit_bytes=VMEM_LIMIT),
    )(h2, comb, wg_bf, wu_bf, wd_bf, x1, final_g)


SC_CORES = 2
SC_SUBCORES = 16
SC_WINDOW = 32


def _sc_gather_rows(x, idx):
    n = idx.shape[0]
    width = x.shape[1]
    assert n % SC_WINDOW == 0
    mesh = plsc.VectorSubcoreMesh(core_axis_name="core", subcore_axis_name="subcore",
                                  num_cores=SC_CORES, num_subcores=SC_SUBCORES)

    @pl.kernel(out_type=jax.ShapeDtypeStruct((n, width), x.dtype), mesh=mesh)
    def gather_kernel(x_hbm, idx_hbm, out_hbm):
        def body(idx_vmem, out_vmem):
            pltpu.sync_copy(x_hbm.at[idx_vmem.at[0, pl.ds(0, SC_WINDOW)]], out_vmem)

        pltpu.emit_pipeline(
            body,
            grid=(n // SC_WINDOW,),
            in_specs=[pl.BlockSpec((1, LANES), lambda i: (i, 0))],
            out_specs=[pl.BlockSpec((SC_WINDOW, width), lambda i: (i, 0))],
            core_axis_name=("core", "subcore"),
            dimension_semantics=(pltpu.PARALLEL,),
        )(idx_hbm, out_hbm)

    idx_rows = jnp.pad(idx.reshape(n // SC_WINDOW, SC_WINDOW), ((0, 0), (0, LANES - SC_WINDOW)))
    return gather_kernel(x, idx_rows)


class MoePlan(NamedTuple):
    slot_row: jax.Array
    token_slot: jax.Array
    e_lo: jax.Array
    e_hi: jax.Array
    n_valid: jax.Array
    w_slots: jax.Array


def _moe_plan(bucket, wpair, tm):
    rows = bucket.shape[0]
    n_tiles_max = rows // tm + N_BUCKETS
    i32 = jnp.int32
    b_ids = jnp.arange(N_BUCKETS, dtype=i32)[:, None]

    def lookup(table, keys):
        return jnp.sum(jnp.where(keys[None, :] == b_ids, table[:, None], 0), axis=0)

    _, order, wlo_sorted, whi_sorted = lax.sort(
        (bucket, jnp.arange(rows, dtype=i32), wpair[0], wpair[1]), num_keys=1, is_stable=True)
    position = jnp.argsort(order).astype(i32)
    counts = jnp.sum((bucket[None, :] == b_ids).astype(i32), axis=1)
    starts = jnp.cumsum(counts) - counts
    tiles_b = (counts + tm - 1) // tm
    tile_end = jnp.cumsum(tiles_b)
    tile_start = tile_end - tiles_b
    n_tiles = tile_end[-1]
    token_slot = position + lookup(tile_start * tm - starts, bucket)
    t = jnp.arange(n_tiles_max, dtype=i32)
    tb = jnp.sum((jnp.minimum(t, n_tiles - 1)[None, :] >= tile_end[:, None]).astype(i32), axis=0)
    local = (t - lookup(tile_start, tb)) * tm
    n_valid = jnp.where(t < n_tiles, jnp.clip(lookup(counts, tb) - local, 0, tm), 0)
    slot = local[:, None] + jnp.arange(tm, dtype=i32)[None, :]
    src = ((lookup(starts, tb)[:, None] + slot) % rows).reshape(-1)
    assert rows < 2 ** 24
    sorted_cols = jnp.stack([order.astype(F32), wlo_sorted, whi_sorted], axis=1)
    slot_cols = sorted_cols[src]
    group = tb // PAIRS_PER_GROUP
    pair = tb % PAIRS_PER_GROUP
    table_pad = (0,) * (N_BUCKETS - PAIRS_PER_GROUP)
    e_lo = group * EXPERTS_PER_GROUP + lookup(jnp.asarray(PAIR_LO + table_pad, i32), pair)
    e_hi = group * EXPERTS_PER_GROUP + lookup(jnp.asarray(PAIR_HI + table_pad, i32), pair)
    return MoePlan(slot_cols[:, 0].astype(i32), token_slot.astype(i32), e_lo.astype(i32),
                   e_hi.astype(i32), n_valid.astype(i32), slot_cols[:, 1:])


def _moe_grouped_kernel(elo_ref, ehi_ref, nv_ref, x1_ref, ws_ref, wg_lo, wu_lo, wd_lo,
                        wg_hi, wu_hi, wd_hi, nfg_ref, fg_ref, y_ref):
    del elo_ref, ehi_ref
    t = pl.program_id(0)

    @pl.when(nv_ref[t] > 0)
    def _():
        x1 = x1_ref[...]
        h = _rms(x1, nfg_ref[...]).astype(BF16)
        ws = ws_ref[...]
        moe = jnp.zeros(x1.shape, F32)
        for col, (wg, wu, wd) in enumerate(((wg_lo, wu_lo, wd_lo), (wg_hi, wu_hi, wd_hi))):
            hg = jnp.dot(h, wg[0], preferred_element_type=F32)
            hu = jnp.dot(h, wu[0], preferred_element_type=F32)
            act = hg * (1.0 / (1.0 + jnp.exp(-hg))) * hu * ws[:, col:col + 1]
            moe = moe + jnp.dot(act.astype(BF16), wd[0], preferred_element_type=F32)
        y_ref[...] = _rms(x1 + moe, fg_ref[...])

    @pl.when(nv_ref[t] == 0)
    def _():
        y_ref[...] = jnp.zeros_like(y_ref)


def _moe_grouped(x1_slots, plan, wg_bf, wu_bf, wd_bf, nf_g, final_g, tm):
    n_steps = plan.n_valid.shape[0]
    lo_spec = lambda shape: pl.BlockSpec(shape, lambda t, elo, ehi, nv: (elo[t], 0, 0))
    hi_spec = lambda shape: pl.BlockSpec(shape, lambda t, elo, ehi, nv: (ehi[t], 0, 0))
    up_shape, down_shape = (1, D_MODEL, D_EXPERT), (1, D_EXPERT, D_MODEL)
    const = lambda a: pl.BlockSpec(a.shape, lambda t, elo, ehi, nv: (0, 0))
    row_spec = lambda width: pl.BlockSpec((tm, width), lambda t, elo, ehi, nv: (t, 0))
    return pl.pallas_call(
        _moe_grouped_kernel,
        grid_spec=pltpu.PrefetchScalarGridSpec(
            num_scalar_prefetch=3,
            grid=(n_steps,),
            in_specs=[row_spec(D_MODEL), row_spec(2),
                      lo_spec(up_shape), lo_spec(up_shape), lo_spec(down_shape),
                      hi_spec(up_shape), hi_spec(up_shape), hi_spec(down_shape),
                      const(nf_g), const(final_g)],
            out_specs=row_spec(D_MODEL)),
        out_shape=jax.ShapeDtypeStruct(x1_slots.shape, F32),
        compiler_params=pltpu.CompilerParams(
            dimension_semantics=("arbitrary",), vmem_limit_bytes=VMEM_LIMIT),
    )(plan.e_lo, plan.e_hi, plan.n_valid, x1_slots, plan.w_slots, wg_bf, wu_bf, wd_bf,
      wg_bf, wu_bf, wd_bf, nf_g, final_g)


def _spatial_operands(w_s, b_s, t_chunk):
    reps = CHUNK // t_chunk
    idx = jnp.arange(CHUNK)
    same = (idx[:, None] // t_chunk) == (idx[None, :] // t_chunk)
    causal = (idx[None, :] % t_chunk) <= (idx[:, None] % t_chunk)
    wm = jnp.tile(w_s[:, :t_chunk, :t_chunk], (1, reps, reps)) * (same & causal)
    wm_cat = wm.transpose(1, 0, 2).reshape(CHUNK, MLP_GROUPS * CHUNK).astype(BF16)
    bias = jnp.repeat(jnp.tile(b_s[:, :t_chunk], (1, reps)).T, MLP_CH, axis=1)
    return wm_cat, bias


def kernel(x_prompt, x_sample, cache_k, cache_v, page_table, norm_attn_g, w_in, sgu_g, w_spatial,
           b_spatial, out_g_attn, out_g_mlp, w_out, norm_ffn_g, w_group, b_group, w_router, b_router,
           w_gate, w_up, w_down, final_g):
    depth = w_in.shape[0]
    assert depth == 1, "single decoder layer"
    batch, seq, _ = x_prompt.shape
    n_seq, t_new, _ = x_sample.shape
    assert seq % MOBA_BLOCK == 0 and seq % CHUNK == 0 and CHUNK % t_new == 0
    assert MOBA_BLOCK % cache_k.shape[2] == 0

    row2 = lambda a: a.reshape(1, -1)
    w_in_bf = w_in[0].astype(BF16)
    w_out_bf = w_out[0].astype(BF16)
    n_logits = MOE_GROUPS + N_EXPERTS
    w_r = jnp.concatenate(
        [w_group[0], w_router[0].transpose(1, 0, 2).reshape(D_MODEL, N_EXPERTS),
         jnp.zeros((D_MODEL, LANES - n_logits), F32)], axis=1)
    b_r = jnp.concatenate(
        [b_group[0], b_router[0].reshape(-1), jnp.zeros((LANES - n_logits,), F32)]).reshape(1, LANES)
    w_r_hi = w_r.astype(BF16)
    w_r_hl = jnp.concatenate([w_r_hi, (w_r - w_r_hi.astype(F32)).astype(BF16)], axis=1)
    wkv_t_bf = w_in_bf[:, ATTN_WIDTH:3 * ATTN_WIDTH].T
    ck_t = jnp.transpose(cache_k[0], (0, 2, 3, 1))
    cv_t = jnp.transpose(cache_v[0], (0, 2, 3, 1))

    def project_in(x, t_chunk, tm, seq_transposed, cast_along=(), emit_vgn=True):
        wm_cat, bias_full = _spatial_operands(w_spatial[0], b_spatial[0], t_chunk)
        return _inproj(x, row2(norm_attn_g[0]), w_in_bf, wkv_t_bf, row2(sgu_g[0]), wm_cat, bias_full,
                       row2(out_g_mlp[0]), tm, seq_transposed, cast_along, emit_vgn)

    def project_out(att_n, gmn, x, tm, grouped):
        return _outproj(att_n, gmn, x, w_out_bf, row2(norm_ffn_g[0]), w_r_hl, b_r, tm, grouped)

    og_attn = row2(out_g_attn[0])

    xp = x_prompt.reshape(batch * seq, D_MODEL)
    qp, kp_t, vp_t, gmn_p, wg_bf, wu_bf, wd_bf = project_in(
        xp, CHUNK, TM_PROJ_PROMPT, seq, (w_gate[0], w_up[0], w_down[0]), emit_vgn=False)
    xs = x_sample.reshape(n_seq * t_new, D_MODEL)
    qs, ks, vs, gmn_s, gvs = project_in(xs, t_new, TM_PROJ_SAMPLE, None)

    n_fused = n_seq // 2
    n_cover = (n_seq - n_fused) // 2
    att_p, att_s0 = _attn_prompt_and_samples(qp, kp_t, vp_t, og_attn, batch, seq, qs, ks, vs,
                                             ck_t, cv_t, page_table, t_new, n_fused)
    x1_p, bucket_p, wpair_p = project_out(att_p, gmn_p, xp, TM_PROJ_PROMPT, True)
    plan = _moe_plan(bucket_p[0], wpair_p[:2], TM_MOE_PROMPT)
    x1_slots = _sc_gather_rows(x1_p, plan.slot_row)
    attend_rest = lambda q_all, lo: _attn_sample(q_all, ks, vs, og_attn, ck_t, cv_t, page_table,
                                                 t_new, lo, n_cover)
    att_s1 = attend_rest(qs, n_fused)
    x1_slots, att_s1 = lax.optimization_barrier((x1_slots, att_s1))
    y_slots = _moe_grouped(x1_slots, plan, wg_bf, wu_bf, wd_bf, row2(norm_ffn_g[0]),
                           row2(final_g), TM_MOE_PROMPT)
    y_slots, qs_late = lax.optimization_barrier((y_slots, qs))
    yp = _sc_gather_rows(y_slots, plan.token_slot)
    att_s2 = attend_rest(qs_late, n_fused + n_cover)
    att_s = jnp.concatenate([att_s0, att_s1, att_s2], axis=0)
    x1_s, h2_s, comb_s = project_out(att_s, gmn_s, xs, TM_PROJ_SAMPLE, False)
    ys = _moe(h2_s, comb_s, wg_bf, wu_bf, wd_bf, x1_s, row2(final_g), TM_MOE_SAMPLE)

    heads = (ATTN_HEADS, HEAD_DIM)
    rows_last = lambda a_t: a_t.reshape(batch, *heads, seq).transpose(0, 3, 1, 2)[None]
    return (yp.reshape(batch, seq, D_MODEL),
            ys.reshape(n_seq, t_new, D_MODEL),
            rows_last(kp_t),
            rows_last(vp_t),
            ks.reshape(depth, n_seq, t_new, *heads),
            vs.reshape(depth, n_seq, t_new, *heads),
            gvs.reshape(depth, n_seq, t_new, MLP_WIDTH))
```

```python
import functools
import math
from typing import NamedTuple

import jax
import jax.numpy as jnp
from jax import lax
from jax.experimental import pallas as pl
from jax.experimental.pallas import tpu as pltpu
from jax.experimental.pallas import tpu_sc as plsc

D_MODEL = 1024
ATTN_HEADS = 8
HEAD_DIM = 64
ATTN_WIDTH = ATTN_HEADS * HEAD_DIM
MOBA_BLOCK = 256
MOBA_TOPK = 3
QUERY_BLOCK = 128
MLP_GROUPS = 8
MLP_CH = 64
MLP_WIDTH = MLP_GROUPS * MLP_CH
CHUNK = 128
IN_WIDTH = 3 * ATTN_WIDTH + 2 * MLP_WIDTH
MOE_GROUPS = 4
EXPERTS_PER_GROUP = 4
N_EXPERTS = MOE_GROUPS * EXPERTS_PER_GROUP
D_EXPERT = D_MODEL // 2
EPS = 1e-6
PAIR_LO = (0, 0, 0, 1, 1, 2)
PAIR_HI = (1, 2, 3, 2, 3, 3)
PAIRS_PER_GROUP = len(PAIR_LO)
N_BUCKETS = MOE_GROUPS * PAIRS_PER_GROUP

LANES = 128
BF16_ROWS = 16
VMEM_LIMIT = 56 * 1024 * 1024

TM_PROJ_PROMPT = 512
TM_PROJ_SAMPLE = 128
TM_MOE_PROMPT = 256
TM_MOE_SAMPLE = 512

F32 = jnp.float32
BF16 = jnp.bfloat16
NEG_INF = float("-inf")
_NT = (((1,), (1,)), ((), ()))


def _rms(x, g):
    return x * lax.rsqrt(jnp.mean(x * x, axis=-1, keepdims=True) + EPS) * g


def _gelu(x):
    return 0.5 * x * (1.0 + jnp.tanh(0.7978845608028654 * (x + 0.044715 * (x * x * x))))


def _top_rank_select(gate, n_past, n_keep, axis):
    nb = gate.shape[axis]
    n_idx = lax.broadcasted_iota(jnp.int32, gate.shape, axis)
    rank = jnp.zeros(gate.shape, jnp.int32)
    for m in range(nb):
        gm = gate[:, m:m + 1] if axis == 1 else gate[m:m + 1, :]
        beats = jnp.where(gm > gate, 1, jnp.where(gm == gate, jnp.where(m < n_idx, 1, 0), 0))
        rank = rank + jnp.where(m < n_past, beats, 0)
    return jnp.where(n_idx < n_past, rank, n_keep) < n_keep


def _inproj_kernel(x_ref, g_ref, w_ref, wkv_t_ref, sgu_ref, wm_ref, bias_ref, og_ref, *refs,
                   n_chunks, kv_transposed, n_cast, emit_vgn):
    n_main = 5 if emit_vgn else 4
    cast_in, main, cast_out = refs[:n_cast], refs[n_cast:n_cast + n_main], refs[n_cast + n_main:]
    q_ref, k_ref, v_ref, gmn_ref = main[:4]
    for src_ref, dst_ref in zip(cast_in, cast_out):
        dst_ref[...] = src_ref[...].astype(BF16)

    h = _rms(x_ref[...], g_ref[...]).astype(BF16)

    def proj(lo, width):
        return jnp.dot(h, w_ref[:, lo:lo + width], preferred_element_type=F32)

    vgn = _rms(_gelu(proj(3 * ATTN_WIDTH + MLP_WIDTH, MLP_WIDTH)), sgu_ref[...])
    if emit_vgn:
        main[4][...] = vgn
    gu = _gelu(proj(3 * ATTN_WIDTH, MLP_WIDTH))

    lane_grp = lax.broadcasted_iota(jnp.int32, (CHUNK, MLP_WIDTH), 1) // MLP_CH
    mixed = []
    for c in range(n_chunks):
        vc = vgn[c * CHUNK:(c + 1) * CHUNK].astype(BF16)
        vbd = jnp.concatenate(
            [jnp.where(lane_grp == g, vc, jnp.zeros_like(vc)) for g in range(MLP_GROUPS)], axis=0)
        mixed.append(jnp.dot(wm_ref[...], vbd, preferred_element_type=F32) + bias_ref[...])

    q_ref[...] = proj(0, ATTN_WIDTH)
    for c in range(n_chunks):
        rows = slice(c * CHUNK, (c + 1) * CHUNK)
        gmn_ref[rows, :] = _rms(gu[rows] * mixed[c], og_ref[...]).astype(BF16)
    if kv_transposed:
        kv_t = lax.dot_general(wkv_t_ref[...], h, _NT, preferred_element_type=F32)
        k_ref[0] = kv_t[:ATTN_WIDTH]
        v_ref[0] = kv_t[ATTN_WIDTH:]
    else:
        k_ref[...] = proj(ATTN_WIDTH, ATTN_WIDTH)
        v_ref[...] = proj(2 * ATTN_WIDTH, ATTN_WIDTH)


def _inproj(x, g, w_bf, wkv_t_bf, sgu_g, wm_cat, bias_full, og_mlp, tm, seq_transposed=None,
            cast_along=(), emit_vgn=True):
    rows = x.shape[0]
    steps = rows // tm
    row_spec = lambda width: pl.BlockSpec((tm, width), lambda i: (i, 0))
    full = lambda a: pl.BlockSpec(a.shape, lambda i: (0,) * a.ndim)
    out = jax.ShapeDtypeStruct((rows, ATTN_WIDTH), F32)
    kv_spec, kv_out = row_spec(ATTN_WIDTH), out
    if seq_transposed is not None:
        tiles = seq_transposed // tm
        kv_spec = pl.BlockSpec((1, ATTN_WIDTH, tm), lambda i: (i // tiles, 0, i % tiles))
        kv_out = jax.ShapeDtypeStruct((rows // seq_transposed, ATTN_WIDTH, seq_transposed), F32)

    def slab_spec(a):
        per = steps // a.shape[0]
        assert per * a.shape[0] == steps and a.shape[1] % per == 0
        return pl.BlockSpec((1, a.shape[1] // per, a.shape[2]), lambda i: (i // per, i % per, 0))

    cast_specs = [slab_spec(a) for a in cast_along]
    vgn_spec, vgn_out = ([row_spec(MLP_WIDTH)], [out]) if emit_vgn else ([], [])
    return pl.pallas_call(
        functools.partial(_inproj_kernel, n_chunks=tm // CHUNK,
                          kv_transposed=seq_transposed is not None, n_cast=len(cast_along),
                          emit_vgn=emit_vgn),
        grid=(steps,),
        in_specs=[row_spec(D_MODEL), full(g), full(w_bf), full(wkv_t_bf), full(sgu_g), full(wm_cat),
                  full(bias_full), full(og_mlp)] + cast_specs,
        out_specs=[row_spec(ATTN_WIDTH), kv_spec, kv_spec, row_spec(MLP_WIDTH)] + vgn_spec
        + cast_specs,
        out_shape=[out, kv_out, kv_out, jax.ShapeDtypeStruct((rows, MLP_WIDTH), BF16)] + vgn_out
        + [jax.ShapeDtypeStruct(a.shape, BF16) for a in cast_along],
        compiler_params=pltpu.CompilerParams(
            dimension_semantics=("arbitrary",), vmem_limit_bytes=VMEM_LIMIT),
    )(x, g, w_bf, wkv_t_bf, sgu_g, wm_cat, bias_full, og_mlp, *cast_along)


V_ROWS = HEAD_DIM + BF16_ROWS
LOG2_E = 1.4426950408889634
MASKED = -1e30
Q_TILE = MOBA_BLOCK
COL_R, COL_J, COL_SEL = 0, 2, 8


def _bf16_split(x):
    mantissa, exponent = math.frexp(x)
    high = math.ldexp(round(mantissa * 256.0) / 256.0, exponent)
    return high, x - high


def _attn_prompt_kernel(q_ref, k_ref, v_ref, og_ref, o_ref, ka_scr, vt_scr, kmean_scr, qa_scr,
                        m_scr, alpha_scr, acc_scr, s_scr, *, n_blocks):
    cur = pl.program_id(1)

    @pl.when(cur == 0)
    def _():
        key = lax.broadcasted_iota(jnp.int32, (MOBA_BLOCK, HEAD_DIM), 0)
        col = lax.broadcasted_iota(jnp.int32, (MOBA_BLOCK, HEAD_DIM), 1)
        ones = jnp.ones((BF16_ROWS, MOBA_BLOCK), BF16)
        for n in range(n_blocks):
            keys = slice(n * MOBA_BLOCK, (n + 1) * MOBA_BLOCK)
            kb = k_ref[0, :, keys].T
            kmean_scr[n:n + 1, :] = jnp.sum(kb, axis=0, keepdims=True) * (1.0 / MOBA_BLOCK)
            vt = v_ref[0, :, keys]
            extra = jnp.where(col < COL_J, key,
                              jnp.where(col < COL_SEL, n, jnp.where(col == COL_SEL + n, 1, 0)))
            extra = extra.astype(F32).astype(BF16)
            for h in range(ATTN_HEADS):
                lanes = slice(h * HEAD_DIM, (h + 1) * HEAD_DIM)
                ka_scr[h, n] = jnp.concatenate([kb[:, lanes].astype(BF16), extra], axis=1)
                vt_scr[h, n, :HEAD_DIM, :] = vt[lanes, :].astype(BF16)
                vt_scr[h, n, HEAD_DIM:, :] = ones

    q_t = q_ref[...].T
    row = lax.broadcasted_iota(jnp.int32, (HEAD_DIM, Q_TILE), 0)
    blk = lax.broadcasted_iota(jnp.int32, (n_blocks, Q_TILE), 0)
    for h in range(ATTN_HEADS):
        lanes = slice(h * HEAD_DIM, (h + 1) * HEAD_DIM)
        hi, lo = _bf16_split(LOG2_E * 2.0 ** (-8.0 * (h + 1) / ATTN_HEADS))
        qh_t = q_t[lanes, :]
        gate = jnp.dot(kmean_scr[:, lanes], qh_t, precision=lax.Precision.HIGHEST,
                       preferred_element_type=F32)
        keep = _top_rank_select(gate, cur, MOBA_TOPK, 0) | (blk >= cur)
        sel_rows = jnp.where(keep, 0.0, MASKED)
        alibi = jnp.where(row == COL_R, hi,
                          jnp.where(row == COL_R + 1, lo,
                                    jnp.where(row == COL_J, hi * MOBA_BLOCK,
                                              jnp.where(row == COL_J + 1, lo * MOBA_BLOCK, 0.0))))
        extra = alibi + jnp.concatenate(
            [jnp.zeros((COL_SEL, Q_TILE), F32), sel_rows,
             jnp.zeros((HEAD_DIM - COL_SEL - n_blocks, Q_TILE), F32)], axis=0)
        qa_scr[h] = jnp.concatenate([qh_t * (LOG2_E * HEAD_DIM ** -0.5), extra], axis=0).astype(BF16)

    def attend_block(j, causal, first):
        for h in range(ATTN_HEADS):
            s = jnp.dot(ka_scr[h, j], qa_scr[h], preferred_element_type=F32)
            if causal is not None:
                s = jnp.where(causal, s, NEG_INF)
            s_scr[h] = s
            m_blk = jnp.max(s, axis=0, keepdims=True)
            if first:
                m_scr[h:h + 1, :] = m_blk
            else:
                m_old = m_scr[h:h + 1, :]
                m_new = jnp.maximum(m_old, m_blk)
                alpha_scr[h:h + 1, :] = jnp.exp2(m_old - m_new)
                m_scr[h:h + 1, :] = m_new
        for h in range(ATTN_HEADS):
            p = jnp.exp2(s_scr[h] - m_scr[h:h + 1, :]).astype(BF16)
            pv = jnp.dot(vt_scr[h, j], p, preferred_element_type=F32)
            acc_scr[h] = pv if first else alpha_scr[h:h + 1, :] * acc_scr[h] + pv

    causal = (lax.broadcasted_iota(jnp.int32, (MOBA_BLOCK, Q_TILE), 0)
              <= lax.broadcasted_iota(jnp.int32, (MOBA_BLOCK, Q_TILE), 1))
    attend_block(cur, causal, True)

    def past_block(j, carry):
        attend_block(j, None, False)
        return carry

    lax.fori_loop(0, cur, past_block, 0)

    outs = []
    for h in range(ATTN_HEADS):
        acc = acc_scr[h]
        outs.append(acc[:HEAD_DIM] / acc[HEAD_DIM:HEAD_DIM + 1])
    o_ref[...] = _rms(jnp.concatenate(outs, axis=0).T, og_ref[...]).astype(BF16)


def _attn_prompt_scratch(n_blocks):
    return [pltpu.VMEM((ATTN_HEADS, n_blocks, MOBA_BLOCK, 2 * HEAD_DIM), BF16),
            pltpu.VMEM((ATTN_HEADS, n_blocks, V_ROWS, MOBA_BLOCK), BF16),
            pltpu.VMEM((n_blocks, ATTN_WIDTH), F32),
            pltpu.VMEM((ATTN_HEADS, 2 * HEAD_DIM, Q_TILE), BF16),
            pltpu.VMEM((ATTN_HEADS, Q_TILE), F32),
            pltpu.VMEM((ATTN_HEADS, Q_TILE), F32),
            pltpu.VMEM((ATTN_HEADS, V_ROWS, Q_TILE), F32),
            pltpu.VMEM((ATTN_HEADS, MOBA_BLOCK, Q_TILE), F32)]


SEQS_PER_STEP = 2


def _sample_pair(step, between, pt_ref, q_ref, kn_ref, vn_ref, og_ref, ck_hbm, cv_hbm, o_ref,
                 kbuf, vbuf, sem_k, sem_v, *, n_pages, page, t_new, seq_lo, n_active):
    n_slots = kbuf.shape[0]
    n_seqs = SEQS_PER_STEP * n_active
    assert n_slots % SEQS_PER_STEP == 0 and n_slots <= n_seqs

    def fetch(k, slot):
        for p in range(n_pages):
            page_id = pt_ref[(seq_lo + k) * n_pages + p]
            pltpu.make_async_copy(ck_hbm.at[page_id], kbuf.at[slot, p], sem_k.at[slot]).start()
            pltpu.make_async_copy(cv_hbm.at[page_id], vbuf.at[slot, p], sem_v.at[slot]).start()

    def attend(i, slot):
        pltpu.make_async_copy(ck_hbm.at[pl.ds(0, n_pages)], kbuf.at[slot], sem_k.at[slot]).wait()
        pltpu.make_async_copy(cv_hbm.at[pl.ds(0, n_pages)], vbuf.at[slot], sem_v.at[slot]).wait()
        kp = [kbuf.at[slot, pl.ds(p, 1)] for p in range(n_pages)]
        vp = [vbuf.at[slot, pl.ds(p, 1)] for p in range(n_pages)]
        rows = slice(i * t_new, (i + 1) * t_new)
        att = _attend_one_sample(q_ref[rows, :], kn_ref[rows, :], vn_ref[rows, :], kp, vp, page,
                                 t_new)
        o_ref[rows, :] = _rms(att, og_ref[...])

    @pl.when(step == 0)
    def _():
        for k in range(n_slots):
            fetch(k, k)

    for i in range(SEQS_PER_STEP):
        k = SEQS_PER_STEP * step + i
        slot = k % n_slots if n_slots > SEQS_PER_STEP else i
        pl.when(step < n_active)(functools.partial(attend, i, slot))
        pl.when(k + n_slots < n_seqs)(functools.partial(fetch, k + n_slots, slot))
        if i == 0:
            between()


def _attn_sample_kernel(pt_ref, *refs, **params):
    _sample_pair(pl.program_id(0), lambda: None, pt_ref, *refs, **params)


N_PROMPT_SCRATCH = 8


def _attn_fused_kernel(pt_ref, q_ref, k_ref, v_ref, og_ref, qs_ref, kn_ref, vn_ref, ck_hbm, cv_hbm,
                       o_ref, os_ref, *scratch, n_blocks, sample):
    step = pl.program_id(0) * n_blocks + pl.program_id(1)
    prompt_tile = functools.partial(_attn_prompt_kernel, q_ref, k_ref, v_ref, og_ref, o_ref,
                                    *scratch[:N_PROMPT_SCRATCH], n_blocks=n_blocks)
    _sample_pair(step, prompt_tile, pt_ref, qs_ref, kn_ref, vn_ref, og_ref, ck_hbm, cv_hbm, os_ref,
                 *scratch[N_PROMPT_SCRATCH:], **sample)


def _attend_one_sample(q, kn, vn, kp, vp, page, t_new):
    n_pages = len(kp)
    n_cols = t_new * ATTN_HEADS
    pages_per_block = MOBA_BLOCK // page
    n_past = n_pages // pages_per_block
    past_len = n_pages * page

    lane_h = lax.broadcasted_iota(jnp.int32, (n_cols, ATTN_WIDTH), 1) // HEAD_DIM
    row = lax.broadcasted_iota(jnp.int32, (n_cols, 1), 0)
    row_h = row % ATTN_HEADS
    row_t = row // ATTN_HEADS
    own_head = lane_h == row_h
    qrep = jnp.concatenate(
        [jnp.broadcast_to(q[t:t + 1, :], (ATTN_HEADS, ATTN_WIDTH)) for t in range(t_new)], axis=0)
    qbd = jnp.where(own_head, qrep, 0.0)
    qbd_s = (qbd * (HEAD_DIM ** -0.5)).astype(BF16)
    slope = jnp.exp2(-8.0 * (row_h + 1).astype(F32) / ATTN_HEADS)

    def block_pages(refs_, n):
        return [refs_[i][0].reshape(ATTN_WIDTH, page)
                for i in range(n * pages_per_block, (n + 1) * pages_per_block)]

    key_off = lax.broadcasted_iota(jnp.int32, (1, MOBA_BLOCK), 1)
    k_sums, s_past = [], []
    for n in range(n_past):
        pages = block_pages(kp, n)
        k_sums.append(jnp.sum(sum(pages), axis=1, keepdims=True))
        kb_t = jnp.concatenate(pages, axis=1).astype(BF16)
        dist = (past_len + row_t) - (n * MOBA_BLOCK + key_off)
        s_past.append(jnp.dot(qbd_s, kb_t, preferred_element_type=F32) - slope * dist.astype(F32))
    kmean = jnp.concatenate(k_sums, axis=1) * (1.0 / MOBA_BLOCK)
    gate = jnp.dot(qbd, kmean, precision=lax.Precision.HIGHEST, preferred_element_type=F32)
    sel = _top_rank_select(gate, n_past, MOBA_TOPK, 1)

    s_own = lax.dot_general(qbd_s, kn.astype(BF16), _NT, preferred_element_type=F32)
    dist = row_t - lax.broadcasted_iota(jnp.int32, (1, t_new), 1)
    s_own = jnp.where(dist >= 0, s_own - slope * dist.astype(F32), NEG_INF)
    m = jnp.max(s_own, axis=-1, keepdims=True)
    for n in range(n_past):
        s_past[n] = jnp.where(sel[:, n:n + 1], s_past[n], NEG_INF)
        m = jnp.maximum(m, jnp.max(s_past[n], axis=-1, keepdims=True))

    p = jnp.exp(s_own - m)
    l = jnp.sum(p, axis=-1, keepdims=True)
    acc = sum(p[:, t:t + 1] * vn[t:t + 1, :] for t in range(t_new))
    for n in range(n_past):
        vb_t = jnp.concatenate(block_pages(vp, n), axis=1).astype(BF16)
        p = jnp.exp(s_past[n] - m)
        l = l + jnp.sum(p, axis=-1, keepdims=True)
        acc = acc + lax.dot_general(p.astype(BF16), vb_t, _NT, preferred_element_type=F32)

    out = jnp.where(own_head, acc / l, 0.0)
    return jnp.sum(out.reshape(t_new, ATTN_HEADS, ATTN_WIDTH), axis=1)


def _attn_sample(q, kn, vn, og_attn, cache_kt, cache_vt, page_table, t_new, seq_lo, n_seq):
    n_pages = page_table.shape[1]
    page = cache_kt.shape[-1]
    assert n_seq % SEQS_PER_STEP == 0 and seq_lo % SEQS_PER_STEP == 0
    step_rows = SEQS_PER_STEP * t_new
    step_lo = seq_lo // SEQS_PER_STEP
    in_rows = pl.BlockSpec((step_rows, ATTN_WIDTH), lambda i, pt: (step_lo + i, 0))
    out_rows = pl.BlockSpec((step_rows, ATTN_WIDTH), lambda i, pt: (i, 0))

    n_steps = n_seq // SEQS_PER_STEP
    n_slots = 2 * SEQS_PER_STEP
    page_buf = pltpu.VMEM((n_slots, n_pages, ATTN_HEADS, HEAD_DIM, page), F32)
    return pl.pallas_call(
        functools.partial(_attn_sample_kernel, n_pages=n_pages, page=page, t_new=t_new,
                          seq_lo=seq_lo, n_active=n_steps),
        grid_spec=pltpu.PrefetchScalarGridSpec(
            num_scalar_prefetch=1,
            grid=(n_steps,),
            in_specs=([in_rows] * 3 + [pl.BlockSpec(og_attn.shape, lambda i, pt: (0, 0)),
                                       pl.BlockSpec(memory_space=pl.ANY),
                                       pl.BlockSpec(memory_space=pl.ANY)]),
            out_specs=out_rows,
            scratch_shapes=[page_buf, page_buf, pltpu.SemaphoreType.DMA((n_slots,)),
                            pltpu.SemaphoreType.DMA((n_slots,))]),
        out_shape=jax.ShapeDtypeStruct((n_seq * t_new, ATTN_WIDTH), F32),
        compiler_params=pltpu.CompilerParams(
            dimension_semantics=("arbitrary",), vmem_limit_bytes=VMEM_LIMIT),
    )(page_table.reshape(-1), q, kn, vn, og_attn, cache_kt, cache_vt)


def _attn_prompt_and_samples(q, k_t, v_t, og_attn, batch, seq, qs, kn, vn, cache_kt, cache_vt,
                             page_table, t_new, n_seq):
    n_blocks = seq // MOBA_BLOCK
    assert COL_SEL + n_blocks <= HEAD_DIM and n_blocks % 8 == 0
    n_pages = page_table.shape[1]
    page = cache_kt.shape[-1]
    n_active = n_seq // SEQS_PER_STEP
    assert n_seq % SEQS_PER_STEP == 0 and 1 <= n_active <= batch * n_blocks
    step_rows = SEQS_PER_STEP * t_new
    q_spec = pl.BlockSpec((Q_TILE, ATTN_WIDTH), lambda b, i, pt: (b * n_blocks + i, 0))
    kv_spec = pl.BlockSpec((1, ATTN_WIDTH, seq), lambda b, i, pt: (b, 0, 0))
    s_rows = pl.BlockSpec((step_rows, ATTN_WIDTH),
                          lambda b, i, pt: (jnp.minimum(b * n_blocks + i, n_active - 1), 0))
    page_buf = pltpu.VMEM((SEQS_PER_STEP, n_pages, ATTN_HEADS, HEAD_DIM, page), F32)
    sample = dict(n_pages=n_pages, page=page, t_new=t_new, seq_lo=0, n_active=n_active)
    return pl.pallas_call(
        functools.partial(_attn_fused_kernel, n_blocks=n_blocks, sample=sample),
        grid_spec=pltpu.PrefetchScalarGridSpec(
            num_scalar_prefetch=1,
            grid=(batch, n_blocks),
            in_specs=[q_spec, kv_spec, kv_spec, pl.BlockSpec(og_attn.shape, lambda b, i, pt: (0, 0)),
                      s_rows, s_rows, s_rows,
                      pl.BlockSpec(memory_space=pl.ANY), pl.BlockSpec(memory_space=pl.ANY)],
            out_specs=[q_spec, s_rows],
            scratch_shapes=_attn_prompt_scratch(n_blocks) + [
                page_buf, page_buf, pltpu.SemaphoreType.DMA((2,)), pltpu.SemaphoreType.DMA((2,))]),
        out_shape=[jax.ShapeDtypeStruct(q.shape, BF16),
                   jax.ShapeDtypeStruct((n_seq * t_new, ATTN_WIDTH), F32)],
        compiler_params=pltpu.CompilerParams(
            dimension_semantics=("arbitrary", "arbitrary"), vmem_limit_bytes=VMEM_LIMIT),
    )(page_table.reshape(-1), q, k_t, v_t, og_attn, qs, kn, vn, cache_kt, cache_vt)


def _outproj_kernel(att_ref, gmn_ref, x_ref, wo_ref, nfg_ref, wr_ref, br_ref, x1_ref, *route_refs,
                    grouped):
    mix = (jnp.dot(att_ref[...].astype(BF16), wo_ref[:ATTN_WIDTH, :], preferred_element_type=F32)
           + jnp.dot(gmn_ref[...], wo_ref[ATTN_WIDTH:, :], preferred_element_type=F32))
    x1 = x_ref[...] + mix
    x1_ref[...] = x1
    h2 = _rms(x1, nfg_ref[...])
    h2_hi = h2.astype(BF16)

    h2_lo = (h2 - h2_hi.astype(F32)).astype(BF16)
    hi_dot = jnp.dot(h2_hi, wr_ref[...], preferred_element_type=F32)
    lo_dot = jnp.dot(h2_lo, wr_ref[:, :LANES], preferred_element_type=F32)
    logits = hi_dot[:, :LANES] + hi_dot[:, LANES:] + lo_dot + br_ref[...]
    lt = logits.T
    tm = lt.shape[1]
    row4 = lax.broadcasted_iota(jnp.int32, (MOE_GROUPS, tm), 0)

    def first_argmax(v):
        vmax = jnp.max(v, axis=0, keepdims=True)
        idx = jnp.min(jnp.where(v == vmax, row4, MOE_GROUPS), axis=0, keepdims=True)
        return vmax, idx

    glog = lt[:MOE_GROUPS]
    ge = jnp.exp(glog - jnp.max(glog, axis=0, keepdims=True))
    gprob = ge / jnp.sum(ge, axis=0, keepdims=True)
    p_g, g_idx = first_argmax(gprob)
    elog = lt[MOE_GROUPS:MOE_GROUPS + EXPERTS_PER_GROUP]
    for g in range(1, MOE_GROUPS):
        lo = MOE_GROUPS + g * EXPERTS_PER_GROUP
        elog = jnp.where(g_idx == g, lt[lo:lo + EXPERTS_PER_GROUP], elog)
    l1, i1 = first_argmax(elog)
    l2, i2 = first_argmax(jnp.where(row4 == i1, NEG_INF, elog))
    e2 = jnp.exp(l2 - l1)
    denom = 1.0 + e2
    w1 = (1.0 / denom) * p_g
    w2 = (e2 / denom) * p_g
    if not grouped:
        h2_ref, comb_ref = route_refs
        h2_ref[...] = h2_hi
        lane_row = lax.broadcasted_iota(jnp.int32, (LANES, tm), 0)
        base = g_idx * EXPERTS_PER_GROUP
        comb_t = (jnp.where(lane_row == base + i1, w1, 0.0)
                  + jnp.where(lane_row == base + i2, w2, 0.0))
        comb_ref[...] = comb_t.T
        return

    bucket_ref, wpair_ref = route_refs
    e_lo = jnp.minimum(i1, i2)
    e_hi = jnp.maximum(i1, i2)
    pair = jnp.where(e_lo == 0, e_hi - 1, jnp.where(e_lo == 1, e_hi + 1, PAIRS_PER_GROUP - 1))
    row8 = lax.broadcasted_iota(jnp.int32, (8, tm), 0)
    bucket_ref[...] = jnp.where(row8 == 0, g_idx * PAIRS_PER_GROUP + pair, 0)
    w_lo = jnp.where(i1 < i2, w1, w2)
    w_hi = jnp.where(i1 < i2, w2, w1)
    wpair_ref[...] = jnp.where(row8 == 0, w_lo, jnp.where(row8 == 1, w_hi, 0.0))


def _outproj(att_n, gmn, x, wo_bf, nf_g, w_r, b_r, tm, grouped):
    rows = x.shape[0]
    row_spec = lambda width: pl.BlockSpec((tm, width), lambda i: (i, 0))
    full = lambda a: pl.BlockSpec(a.shape, lambda i: (0,) * a.ndim)
    lane_spec = pl.BlockSpec((8, tm), lambda i: (0, i))
    if grouped:
        route_specs = [lane_spec, lane_spec]
        route_shapes = [jax.ShapeDtypeStruct((8, rows), jnp.int32),
                        jax.ShapeDtypeStruct((8, rows), F32)]
    else:
        route_specs = [row_spec(D_MODEL), row_spec(LANES)]
        route_shapes = [jax.ShapeDtypeStruct((rows, D_MODEL), BF16),
                        jax.ShapeDtypeStruct((rows, LANES), F32)]
    return pl.pallas_call(
        functools.partial(_outproj_kernel, grouped=grouped),
        grid=(rows // tm,),
        in_specs=[row_spec(ATTN_WIDTH), row_spec(MLP_WIDTH), row_spec(D_MODEL), full(wo_bf),
                  full(nf_g), full(w_r), full(b_r)],
        out_specs=[row_spec(D_MODEL)] + route_specs,
        out_shape=[jax.ShapeDtypeStruct((rows, D_MODEL), F32)] + route_shapes,
        compiler_params=pltpu.CompilerParams(
            dimension_semantics=("arbitrary",), vmem_limit_bytes=VMEM_LIMIT),
    )(att_n, gmn, x, wo_bf, nf_g, w_r, b_r)


def _moe_kernel(h_ref, comb_ref, wg_ref, wu_ref, wd_ref, x1_ref, fg_ref, y_ref, acc_ref):
    e = pl.program_id(1)

    @pl.when(e == 0)
    def _():
        acc_ref[...] = jnp.zeros_like(acc_ref)

    h = h_ref[...]
    hg = jnp.dot(h, wg_ref[0], preferred_element_type=F32)
    hu = jnp.dot(h, wu_ref[0], preferred_element_type=F32)
    comb = comb_ref[...]
    lane = lax.broadcasted_iota(jnp.int32, comb.shape, 1)
    c = jnp.sum(jnp.where(lane == e, comb, 0.0), axis=-1, keepdims=True)
    act = hg * (1.0 / (1.0 + jnp.exp(-hg))) * hu * c
    acc_ref[...] += jnp.dot(act.astype(BF16), wd_ref[0], preferred_element_type=F32)

    @pl.when(e == N_EXPERTS - 1)
    def _():
        y_ref[...] = _rms(x1_ref[...] + acc_ref[...], fg_ref[...])


def _moe(h2, comb, wg_bf, wu_bf, wd_bf, x1, final_g, tm):
    rows = h2.shape[0]
    row_spec = lambda width: pl.BlockSpec((tm, width), lambda i, e: (i, 0))
    return pl.pallas_call(
        _moe_kernel,
        grid=(rows // tm, N_EXPERTS),
        in_specs=[row_spec(D_MODEL), row_spec(LANES),
                  pl.BlockSpec((1, D_MODEL, D_EXPERT), lambda i, e: (e, 0, 0)),
                  pl.BlockSpec((1, D_MODEL, D_EXPERT), lambda i, e: (e, 0, 0)),
                  pl.BlockSpec((1, D_EXPERT, D_MODEL), lambda i, e: (e, 0, 0)),
                  row_spec(D_MODEL),
                  pl.BlockSpec(final_g.shape, lambda i, e: (0, 0))],
        out_specs=row_spec(D_MODEL),
        out_shape=jax.ShapeDtypeStruct((rows, D_MODEL), F32),
        scratch_shapes=[pltpu.VMEM((tm, D_MODEL), F32)],
        compiler_params=pltpu.CompilerParams(
            dimension_semantics=("arbitrary", "arbitrary"), vmem_limit_bytes=VMEM_LIMIT),
    )(h2, comb, wg_bf, wu_bf, wd_bf, x1, final_g)


SC_CORES = 2
SC_SUBCORES = 16
SC_WINDOW = 32


def _sc_gather_rows(x, idx):
    n = idx.shape[0]
    width = x.shape[1]
    assert n % SC_WINDOW == 0
    mesh = plsc.VectorSubcoreMesh(core_axis_name="core", subcore_axis_name="subcore",
                                  num_cores=SC_CORES, num_subcores=SC_SUBCORES)

    @pl.kernel(out_type=jax.ShapeDtypeStruct((n, width), x.dtype), mesh=mesh)
    def gather_kernel(x_hbm, idx_hbm, out_hbm):
        def body(idx_vmem, out_vmem):
            pltpu.sync_copy(x_hbm.at[idx_vmem.at[0, pl.ds(0, SC_WINDOW)]], out_vmem)

        pltpu.emit_pipeline(
            body,
            grid=(n // SC_WINDOW,),
            in_specs=[pl.BlockSpec((1, LANES), lambda i: (i, 0))],
            out_specs=[pl.BlockSpec((SC_WINDOW, width), lambda i: (i, 0))],
            core_axis_name=("core", "subcore"),
            dimension_semantics=(pltpu.PARALLEL,),
        )(idx_hbm, out_hbm)

    idx_rows = jnp.pad(idx.reshape(n // SC_WINDOW, SC_WINDOW), ((0, 0), (0, LANES - SC_WINDOW)))
    return gather_kernel(x, idx_rows)


class MoePlan(NamedTuple):
    slot_row: jax.Array
    token_slot: jax.Array
    e_lo: jax.Array
    e_hi: jax.Array
    n_valid: jax.Array
    w_slots: jax.Array


def _moe_plan(bucket, wpair, tm):
    rows = bucket.shape[0]
    n_tiles_max = rows // tm + N_BUCKETS
    i32 = jnp.int32
    b_ids = jnp.arange(N_BUCKETS, dtype=i32)[:, None]

    def lookup(table, keys):
        return jnp.sum(jnp.where(keys[None, :] == b_ids, table[:, None], 0), axis=0)

    _, order, wlo_sorted, whi_sorted = lax.sort(
        (bucket, jnp.arange(rows, dtype=i32), wpair[0], wpair[1]), num_keys=1, is_stable=True)
    position = jnp.argsort(order).astype(i32)
    counts = jnp.sum((bucket[None, :] == b_ids).astype(i32), axis=1)
    starts = jnp.cumsum(counts) - counts
    tiles_b = (counts + tm - 1) // tm
    tile_end = jnp.cumsum(tiles_b)
    tile_start = tile_end - tiles_b
    n_tiles = tile_end[-1]
    token_slot = position + lookup(tile_start * tm - starts, bucket)
    t = jnp.arange(n_tiles_max, dtype=i32)
    tb = jnp.sum((jnp.minimum(t, n_tiles - 1)[None, :] >= tile_end[:, None]).astype(i32), axis=0)
    local = (t - lookup(tile_start, tb)) * tm
    n_valid = jnp.where(t < n_tiles, jnp.clip(lookup(counts, tb) - local, 0, tm), 0)
    slot = local[:, None] + jnp.arange(tm, dtype=i32)[None, :]
    src = ((lookup(starts, tb)[:, None] + slot) % rows).reshape(-1)
    assert rows < 2 ** 24
    sorted_cols = jnp.stack([order.astype(F32), wlo_sorted, whi_sorted], axis=1)
    slot_cols = sorted_cols[src]
    group = tb // PAIRS_PER_GROUP
    pair = tb % PAIRS_PER_GROUP
    table_pad = (0,) * (N_BUCKETS - PAIRS_PER_GROUP)
    e_lo = group * EXPERTS_PER_GROUP + lookup(jnp.asarray(PAIR_LO + table_pad, i32), pair)
    e_hi = group * EXPERTS_PER_GROUP + lookup(jnp.asarray(PAIR_HI + table_pad, i32), pair)
    return MoePlan(slot_cols[:, 0].astype(i32), token_slot.astype(i32), e_lo.astype(i32),
                   e_hi.astype(i32), n_valid.astype(i32), slot_cols[:, 1:])


def _moe_grouped_kernel(elo_ref, ehi_ref, nv_ref, x1_ref, ws_ref, wg_lo, wu_lo, wd_lo,
                        wg_hi, wu_hi, wd_hi, nfg_ref, fg_ref, y_ref):
    del elo_ref, ehi_ref
    t = pl.program_id(0)

    @pl.when(nv_ref[t] > 0)
    def _():
        x1 = x1_ref[...]
        h = _rms(x1, nfg_ref[...]).astype(BF16)
        ws = ws_ref[...]
        moe = jnp.zeros(x1.shape, F32)
        for col, (wg, wu, wd) in enumerate(((wg_lo, wu_lo, wd_lo), (wg_hi, wu_hi, wd_hi))):
            hg = jnp.dot(h, wg[0], preferred_element_type=F32)
            hu = jnp.dot(h, wu[0], preferred_element_type=F32)
            act = hg * (1.0 / (1.0 + jnp.exp(-hg))) * hu * ws[:, col:col + 1]
            moe = moe + jnp.dot(act.astype(BF16), wd[0], preferred_element_type=F32)
        y_ref[...] = _rms(x1 + moe, fg_ref[...])

    @pl.when(nv_ref[t] == 0)
    def _():
        y_ref[...] = jnp.zeros_like(y_ref)


def _moe_grouped(x1_slots, plan, wg_bf, wu_bf, wd_bf, nf_g, final_g, tm):
    n_steps = plan.n_valid.shape[0]
    lo_spec = lambda shape: pl.BlockSpec(shape, lambda t, elo, ehi, nv: (elo[t], 0, 0))
    hi_spec = lambda shape: pl.BlockSpec(shape, lambda t, elo, ehi, nv: (ehi[t], 0, 0))
    up_shape, down_shape = (1, D_MODEL, D_EXPERT), (1, D_EXPERT, D_MODEL)
    const = lambda a: pl.BlockSpec(a.shape, lambda t, elo, ehi, nv: (0, 0))
    row_spec = lambda width: pl.BlockSpec((tm, width), lambda t, elo, ehi, nv: (t, 0))
    return pl.pallas_call(
        _moe_grouped_kernel,
        grid_spec=pltpu.PrefetchScalarGridSpec(
            num_scalar_prefetch=3,
            grid=(n_steps,),
            in_specs=[row_spec(D_MODEL), row_spec(2),
                      lo_spec(up_shape), lo_spec(up_shape), lo_spec(down_shape),
                      hi_spec(up_shape), hi_spec(up_shape), hi_spec(down_shape),
                      const(nf_g), const(final_g)],
            out_specs=row_spec(D_MODEL)),
        out_shape=jax.ShapeDtypeStruct(x1_slots.shape, F32),
        compiler_params=pltpu.CompilerParams(
            dimension_semantics=("arbitrary",), vmem_limit_bytes=VMEM_LIMIT),
    )(plan.e_lo, plan.e_hi, plan.n_valid, x1_slots, plan.w_slots, wg_bf, wu_bf, wd_bf,
      wg_bf, wu_bf, wd_bf, nf_g, final_g)


def _spatial_operands(w_s, b_s, t_chunk):
    reps = CHUNK // t_chunk
    idx = jnp.arange(CHUNK)
    same = (idx[:, None] // t_chunk) == (idx[None, :] // t_chunk)
    causal = (idx[None, :] % t_chunk) <= (idx[:, None] % t_chunk)
    wm = jnp.tile(w_s[:, :t_chunk, :t_chunk], (1, reps, reps)) * (same & causal)
    wm_cat = wm.transpose(1, 0, 2).reshape(CHUNK, MLP_GROUPS * CHUNK).astype(BF16)
    bias = jnp.repeat(jnp.tile(b_s[:, :t_chunk], (1, reps)).T, MLP_CH, axis=1)
    return wm_cat, bias


def kernel(x_prompt, x_sample, cache_k, cache_v, page_table, norm_attn_g, w_in, sgu_g, w_spatial,
           b_spatial, out_g_attn, out_g_mlp, w_out, norm_ffn_g, w_group, b_group, w_router, b_router,
           w_gate, w_up, w_down, final_g):
    depth = w_in.shape[0]
    assert depth == 1, "single decoder layer"
    batch, seq, _ = x_prompt.shape
    n_seq, t_new, _ = x_sample.shape
    assert seq % MOBA_BLOCK == 0 and seq % CHUNK == 0 and CHUNK % t_new == 0
    assert MOBA_BLOCK % cache_k.shape[2] == 0

    row2 = lambda a: a.reshape(1, -1)
    w_in_bf = w_in[0].astype(BF16)
    w_out_bf = w_out[0].astype(BF16)
    n_logits = MOE_GROUPS + N_EXPERTS
    w_r = jnp.concatenate(
        [w_group[0], w_router[0].transpose(1, 0, 2).reshape(D_MODEL, N_EXPERTS),
         jnp.zeros((D_MODEL, LANES - n_logits), F32)], axis=1)
    b_r = jnp.concatenate(
        [b_group[0], b_router[0].reshape(-1), jnp.zeros((LANES - n_logits,), F32)]).reshape(1, LANES)
    w_r_hi = w_r.astype(BF16)
    w_r_hl = jnp.concatenate([w_r_hi, (w_r - w_r_hi.astype(F32)).astype(BF16)], axis=1)
    wkv_t_bf = w_in_bf[:, ATTN_WIDTH:3 * ATTN_WIDTH].T
    ck_t = jnp.transpose(cache_k[0], (0, 2, 3, 1))
    cv_t = jnp.transpose(cache_v[0], (0, 2, 3, 1))

    def project_in(x, t_chunk, tm, seq_transposed, cast_along=(), emit_vgn=True):
        wm_cat, bias_full = _spatial_operands(w_spatial[0], b_spatial[0], t_chunk)
        return _inproj(x, row2(norm_attn_g[0]), w_in_bf, wkv_t_bf, row2(sgu_g[0]), wm_cat, bias_full,
                       row2(out_g_mlp[0]), tm, seq_transposed, cast_along, emit_vgn)

    def project_out(att_n, gmn, x, tm, grouped):
        return _outproj(att_n, gmn, x, w_out_bf, row2(norm_ffn_g[0]), w_r_hl, b_r, tm, grouped)

    og_attn = row2(out_g_attn[0])

    xp = x_prompt.reshape(batch * seq, D_MODEL)
    qp, kp_t, vp_t, gmn_p, wg_bf, wu_bf, wd_bf = project_in(
        xp, CHUNK, TM_PROJ_PROMPT, seq, (w_gate[0], w_up[0], w_down[0]), emit_vgn=False)
    xs = x_sample.reshape(n_seq * t_new, D_MODEL)
    qs, ks, vs, gmn_s, gvs = project_in(xs, t_new, TM_PROJ_SAMPLE, None)

    n_fused = n_seq // 2
    n_cover = (n_seq - n_fused) // 2
    att_p, att_s0 = _attn_prompt_and_samples(qp, kp_t, vp_t, og_attn, batch, seq, qs, ks, vs,
                                             ck_t, cv_t, page_table, t_new, n_fused)
    x1_p, bucket_p, wpair_p = project_out(att_p, gmn_p, xp, TM_PROJ_PROMPT, True)
    plan = _moe_plan(bucket_p[0], wpair_p[:2], TM_MOE_PROMPT)
    x1_slots = _sc_gather_rows(x1_p, plan.slot_row)
    attend_rest = lambda q_all, lo: _attn_sample(q_all, ks, vs, og_attn, ck_t, cv_t, page_table,
                                                 t_new, lo, n_cover)
    att_s1 = attend_rest(qs, n_fused)
    x1_slots, att_s1 = lax.optimization_barrier((x1_slots, att_s1))
    y_slots = _moe_grouped(x1_slots, plan, wg_bf, wu_bf, wd_bf, row2(norm_ffn_g[0]),
                           row2(final_g), TM_MOE_PROMPT)
    y_slots, qs_late = lax.optimization_barrier((y_slots, qs))
    yp = _sc_gather_rows(y_slots, plan.token_slot)
    att_s2 = attend_rest(qs_late, n_fused + n_cover)
    att_s = jnp.concatenate([att_s0, att_s1, att_s2], axis=0)
    x1_s, h2_s, comb_s = project_out(att_s, gmn_s, xs, TM_PROJ_SAMPLE, False)
    ys = _moe(h2_s, comb_s, wg_bf, wu_bf, wd_bf, x1_s, row2(final_g), TM_MOE_SAMPLE)

    heads = (ATTN_HEADS, HEAD_DIM)
    rows_last = lambda a_t: a_t.reshape(batch, *heads, seq).transpose(0, 3, 1, 2)[None]
    return (yp.reshape(batch, seq, D_MODEL),
            ys.reshape(n_seq, t_new, D_MODEL),
            rows_last(kp_t),
            rows_last(vp_t),
            ks.reshape(depth, n_seq, t_new, *heads),
            vs.reshape(depth, n_seq, t_new, *heads),
            gvs.reshape(depth, n_seq, t_new, MLP_WIDTH))
```

```python
import functools
import math
from typing import NamedTuple

import jax
import jax.numpy as jnp
from jax import lax
from jax.experimental import pallas as pl
from jax.experimental.pallas import tpu as pltpu
from jax.experimental.pallas import tpu_sc as plsc

D_MODEL = 1024
ATTN_HEADS = 8
HEAD_DIM = 64
ATTN_WIDTH = ATTN_HEADS * HEAD_DIM
MOBA_BLOCK = 256
MOBA_TOPK = 3
MLP_GROUPS = 8
MLP_CH = 64
MLP_WIDTH = MLP_GROUPS * MLP_CH
CHUNK = 128
MOE_GROUPS = 4
EXPERTS_PER_GROUP = 4
N_EXPERTS = MOE_GROUPS * EXPERTS_PER_GROUP
D_EXPERT = D_MODEL // 2
EPS = 1e-6
PAIR_LO = (0, 0, 0, 1, 1, 2)
PAIR_HI = (1, 2, 3, 2, 3, 3)
PAIRS_PER_GROUP = len(PAIR_LO)
N_BUCKETS = MOE_GROUPS * PAIRS_PER_GROUP

LANES = 128
BF16_ROWS = 16
VMEM_LIMIT = 56 * 1024 * 1024

TM_PROJ_PROMPT = 512
TM_PROJ_SAMPLE = 128
TM_MOE_PROMPT = 256
TM_MOE_SAMPLE = 512

F32 = jnp.float32
BF16 = jnp.bfloat16
NEG_INF = float("-inf")
_NT = (((1,), (1,)), ((), ()))


def _rms(x, g):
    return x * lax.rsqrt(jnp.mean(x * x, axis=-1, keepdims=True) + EPS) * g


def _gelu(x):
    return 0.5 * x * (1.0 + jnp.tanh(0.7978845608028654 * (x + 0.044715 * (x * x * x))))


def _top_rank_select(gate, n_past, n_keep, axis):
    nb = gate.shape[axis]
    n_idx = lax.broadcasted_iota(jnp.int32, gate.shape, axis)
    rank = jnp.zeros(gate.shape, jnp.int32)
    for m in range(nb):
        gm = gate[:, m:m + 1] if axis == 1 else gate[m:m + 1, :]
        beats = jnp.where(gm > gate, 1, jnp.where(gm == gate, jnp.where(m < n_idx, 1, 0), 0))
        rank = rank + jnp.where(m < n_past, beats, 0)
    return jnp.where(n_idx < n_past, rank, n_keep) < n_keep


def _inproj_kernel(x_ref, g_ref, w_ref, wkv_t_ref, sgu_ref, wm_ref, bias_ref, og_ref, *refs,
                   n_chunks, kv_transposed, n_cast, emit_vgn):
    n_main = 5 if emit_vgn else 4
    cast_in, main, cast_out = refs[:n_cast], refs[n_cast:n_cast + n_main], refs[n_cast + n_main:]
    q_ref, k_ref, v_ref, gmn_ref = main[:4]
    for src_ref, dst_ref in zip(cast_in, cast_out):
        dst_ref[...] = src_ref[...].astype(BF16)

    h = _rms(x_ref[...], g_ref[...]).astype(BF16)

    def proj(lo, width):
        return jnp.dot(h, w_ref[:, lo:lo + width], preferred_element_type=F32)

    vgn = _rms(_gelu(proj(3 * ATTN_WIDTH + MLP_WIDTH, MLP_WIDTH)), sgu_ref[...])
    if emit_vgn:
        main[4][...] = vgn
    gu = _gelu(proj(3 * ATTN_WIDTH, MLP_WIDTH))

    lane_grp = lax.broadcasted_iota(jnp.int32, (CHUNK, MLP_WIDTH), 1) // MLP_CH
    mixed = []
    for c in range(n_chunks):
        vc = vgn[c * CHUNK:(c + 1) * CHUNK].astype(BF16)
        vbd = jnp.concatenate(
            [jnp.where(lane_grp == g, vc, jnp.zeros_like(vc)) for g in range(MLP_GROUPS)], axis=0)
        mixed.append(jnp.dot(wm_ref[...], vbd, preferred_element_type=F32) + bias_ref[...])

    q_ref[...] = proj(0, ATTN_WIDTH)
    for c in range(n_chunks):
        rows = slice(c * CHUNK, (c + 1) * CHUNK)
        gmn_ref[rows, :] = _rms(gu[rows] * mixed[c], og_ref[...]).astype(BF16)
    if kv_transposed:
        kv_t = lax.dot_general(wkv_t_ref[...], h, _NT, preferred_element_type=F32)
        k_ref[0] = kv_t[:ATTN_WIDTH]
        v_ref[0] = kv_t[ATTN_WIDTH:]
    else:
        k_ref[...] = proj(ATTN_WIDTH, ATTN_WIDTH)
        v_ref[...] = proj(2 * ATTN_WIDTH, ATTN_WIDTH)


def _inproj(x, g, w_bf, wkv_t_bf, sgu_g, wm_cat, bias_full, og_mlp, tm, seq_transposed=None,
            cast_along=(), emit_vgn=True):
    rows = x.shape[0]
    steps = rows // tm
    row_spec = lambda width: pl.BlockSpec((tm, width), lambda i: (i, 0))
    full = lambda a: pl.BlockSpec(a.shape, lambda i: (0,) * a.ndim)
    out = jax.ShapeDtypeStruct((rows, ATTN_WIDTH), F32)
    kv_spec, kv_out = row_spec(ATTN_WIDTH), out
    if seq_transposed is not None:
        tiles = seq_transposed // tm
        kv_spec = pl.BlockSpec((1, ATTN_WIDTH, tm), lambda i: (i // tiles, 0, i % tiles))
        kv_out = jax.ShapeDtypeStruct((rows // seq_transposed, ATTN_WIDTH, seq_transposed), F32)

    def slab_spec(a):
        per = steps // a.shape[0]
        assert per * a.shape[0] == steps and a.shape[1] % per == 0
        return pl.BlockSpec((1, a.shape[1] // per, a.shape[2]), lambda i: (i // per, i % per, 0))

    cast_specs = [slab_spec(a) for a in cast_along]
    vgn_spec, vgn_out = ([row_spec(MLP_WIDTH)], [out]) if emit_vgn else ([], [])
    return pl.pallas_call(
        functools.partial(_inproj_kernel, n_chunks=tm // CHUNK,
                          kv_transposed=seq_transposed is not None, n_cast=len(cast_along),
                          emit_vgn=emit_vgn),
        grid=(steps,),
        in_specs=[row_spec(D_MODEL), full(g), full(w_bf), full(wkv_t_bf), full(sgu_g), full(wm_cat),
                  full(bias_full), full(og_mlp)] + cast_specs,
        out_specs=[row_spec(ATTN_WIDTH), kv_spec, kv_spec, row_spec(MLP_WIDTH)] + vgn_spec
        + cast_specs,
        out_shape=[out, kv_out, kv_out, jax.ShapeDtypeStruct((rows, MLP_WIDTH), BF16)] + vgn_out
        + [jax.ShapeDtypeStruct(a.shape, BF16) for a in cast_along],
        compiler_params=pltpu.CompilerParams(
            dimension_semantics=("arbitrary",), vmem_limit_bytes=VMEM_LIMIT),
    )(x, g, w_bf, wkv_t_bf, sgu_g, wm_cat, bias_full, og_mlp, *cast_along)


V_ROWS = HEAD_DIM + BF16_ROWS
LOG2_E = 1.4426950408889634
MASKED = -1e30
Q_TILE = MOBA_BLOCK
COL_R, COL_J, COL_SEL = 0, 2, 8


def _bf16_split(x):
    mantissa, exponent = math.frexp(x)
    high = math.ldexp(round(mantissa * 256.0) / 256.0, exponent)
    return high, x - high


def _attn_prompt_kernel(q_ref, k_ref, v_ref, og_ref, o_ref, ka_scr, vt_scr, kmean_scr, qa_scr,
                        m_scr, alpha_scr, acc_scr, s_scr, *, n_blocks):
    cur = pl.program_id(1)

    @pl.when(cur == 0)
    def _():
        key = lax.broadcasted_iota(jnp.int32, (MOBA_BLOCK, HEAD_DIM), 0)
        col = lax.broadcasted_iota(jnp.int32, (MOBA_BLOCK, HEAD_DIM), 1)
        ones = jnp.ones((BF16_ROWS, MOBA_BLOCK), BF16)
        for n in range(n_blocks):
            keys = slice(n * MOBA_BLOCK, (n + 1) * MOBA_BLOCK)
            kb = k_ref[0, :, keys].T
            kmean_scr[n:n + 1, :] = jnp.sum(kb, axis=0, keepdims=True) * (1.0 / MOBA_BLOCK)
            vt = v_ref[0, :, keys]
            extra = jnp.where(col < COL_J, key,
                              jnp.where(col < COL_SEL, n, jnp.where(col == COL_SEL + n, 1, 0)))
            extra = extra.astype(F32).astype(BF16)
            for h in range(ATTN_HEADS):
                lanes = slice(h * HEAD_DIM, (h + 1) * HEAD_DIM)
                ka_scr[h, n] = jnp.concatenate([kb[:, lanes].astype(BF16), extra], axis=1)
                vt_scr[h, n, :HEAD_DIM, :] = vt[lanes, :].astype(BF16)
                vt_scr[h, n, HEAD_DIM:, :] = ones

    q_t = q_ref[...].T
    row = lax.broadcasted_iota(jnp.int32, (HEAD_DIM, Q_TILE), 0)
    blk = lax.broadcasted_iota(jnp.int32, (n_blocks, Q_TILE), 0)
    for h in range(ATTN_HEADS):
        lanes = slice(h * HEAD_DIM, (h + 1) * HEAD_DIM)
        hi, lo = _bf16_split(LOG2_E * 2.0 ** (-8.0 * (h + 1) / ATTN_HEADS))
        qh_t = q_t[lanes, :]
        gate = jnp.dot(kmean_scr[:, lanes], qh_t, precision=lax.Precision.HIGHEST,
                       preferred_element_type=F32)
        keep = _top_rank_select(gate, cur, MOBA_TOPK, 0) | (blk >= cur)
        sel_rows = jnp.where(keep, 0.0, MASKED)
        alibi = jnp.where(row == COL_R, hi,
                          jnp.where(row == COL_R + 1, lo,
                                    jnp.where(row == COL_J, hi * MOBA_BLOCK,
                                              jnp.where(row == COL_J + 1, lo * MOBA_BLOCK, 0.0))))
        extra = alibi + jnp.concatenate(
            [jnp.zeros((COL_SEL, Q_TILE), F32), sel_rows,
             jnp.zeros((HEAD_DIM - COL_SEL - n_blocks, Q_TILE), F32)], axis=0)
        qa_scr[h] = jnp.concatenate([qh_t * (LOG2_E * HEAD_DIM ** -0.5), extra], axis=0).astype(BF16)

    def attend_block(j, causal, first):
        for h in range(ATTN_HEADS):
            s = jnp.dot(ka_scr[h, j], qa_scr[h], preferred_element_type=F32)
            if causal is not None:
                s = jnp.where(causal, s, NEG_INF)
            s_scr[h] = s
            m_blk = jnp.max(s, axis=0, keepdims=True)
            if first:
                m_scr[h:h + 1, :] = m_blk
            else:
                m_old = m_scr[h:h + 1, :]
                m_new = jnp.maximum(m_old, m_blk)
                alpha_scr[h:h + 1, :] = jnp.exp2(m_old - m_new)
                m_scr[h:h + 1, :] = m_new
        for h in range(ATTN_HEADS):
            p = jnp.exp2(s_scr[h] - m_scr[h:h + 1, :]).astype(BF16)
            pv = jnp.dot(vt_scr[h, j], p, preferred_element_type=F32)
            acc_scr[h] = pv if first else alpha_scr[h:h + 1, :] * acc_scr[h] + pv

    causal = (lax.broadcasted_iota(jnp.int32, (MOBA_BLOCK, Q_TILE), 0)
              <= lax.broadcasted_iota(jnp.int32, (MOBA_BLOCK, Q_TILE), 1))
    attend_block(cur, causal, True)

    def past_block(j, carry):
        attend_block(j, None, False)
        return carry

    lax.fori_loop(0, cur, past_block, 0)

    outs = []
    for h in range(ATTN_HEADS):
        acc = acc_scr[h]
        outs.append(acc[:HEAD_DIM] / acc[HEAD_DIM:HEAD_DIM + 1])
    o_ref[...] = _rms(jnp.concatenate(outs, axis=0).T, og_ref[...]).astype(BF16)


def _attn_prompt(q, k_t, v_t, og_attn, batch, seq):
    n_blocks = seq // MOBA_BLOCK
    assert COL_SEL + n_blocks <= HEAD_DIM and n_blocks % 8 == 0
    q_spec = pl.BlockSpec((Q_TILE, ATTN_WIDTH), lambda b, i: (b * n_blocks + i, 0))
    return pl.pallas_call(
        functools.partial(_attn_prompt_kernel, n_blocks=n_blocks),
        grid=(batch, n_blocks),
        in_specs=[q_spec,
                  pl.BlockSpec((1, ATTN_WIDTH, seq), lambda b, i: (b, 0, 0)),
                  pl.BlockSpec((1, ATTN_WIDTH, seq), lambda b, i: (b, 0, 0)),
                  pl.BlockSpec(og_attn.shape, lambda b, i: (0, 0))],
        out_specs=q_spec,
        out_shape=jax.ShapeDtypeStruct(q.shape, BF16),
        scratch_shapes=[pltpu.VMEM((ATTN_HEADS, n_blocks, MOBA_BLOCK, 2 * HEAD_DIM), BF16),
                        pltpu.VMEM((ATTN_HEADS, n_blocks, V_ROWS, MOBA_BLOCK), BF16),
                        pltpu.VMEM((n_blocks, ATTN_WIDTH), F32),
                        pltpu.VMEM((ATTN_HEADS, 2 * HEAD_DIM, Q_TILE), BF16),
                        pltpu.VMEM((ATTN_HEADS, Q_TILE), F32),
                        pltpu.VMEM((ATTN_HEADS, Q_TILE), F32),
                        pltpu.VMEM((ATTN_HEADS, V_ROWS, Q_TILE), F32),
                        pltpu.VMEM((ATTN_HEADS, MOBA_BLOCK, Q_TILE), F32)],
        compiler_params=pltpu.CompilerParams(
            dimension_semantics=("arbitrary", "arbitrary"), vmem_limit_bytes=VMEM_LIMIT),
    )(q, k_t, v_t, og_attn)


SEQS_PER_STEP = 2


def _attn_sample_kernel(pt_ref, q_ref, kn_ref, vn_ref, og_ref, ck_hbm, cv_hbm, o_ref,
                        kbuf, vbuf, sem_k, sem_v, *, n_pages, page, t_new, seq_lo, n_steps):
    step = pl.program_id(0)
    slot = step % 2
    pages_per_step = SEQS_PER_STEP * n_pages

    def fetch(step_, slot_):
        first = (seq_lo + step_ * SEQS_PER_STEP) * n_pages
        for j in range(pages_per_step):
            page_id = pt_ref[first + j]
            pltpu.make_async_copy(ck_hbm.at[page_id], kbuf.at[slot_, j], sem_k.at[slot_]).start()
            pltpu.make_async_copy(cv_hbm.at[page_id], vbuf.at[slot_, j], sem_v.at[slot_]).start()

    @pl.when(step == 0)
    def _():
        fetch(0, 0)

    @pl.when(step + 1 < n_steps)
    def _():
        fetch(step + 1, 1 - slot)

    pltpu.make_async_copy(ck_hbm.at[pl.ds(0, pages_per_step)], kbuf.at[slot], sem_k.at[slot]).wait()
    pltpu.make_async_copy(cv_hbm.at[pl.ds(0, pages_per_step)], vbuf.at[slot], sem_v.at[slot]).wait()

    for i in range(SEQS_PER_STEP):
        kp = [kbuf.at[slot, pl.ds(i * n_pages + p, 1)] for p in range(n_pages)]
        vp = [vbuf.at[slot, pl.ds(i * n_pages + p, 1)] for p in range(n_pages)]
        rows = slice(i * t_new, (i + 1) * t_new)
        att = _attend_one_sample(q_ref[rows, :], kn_ref[rows, :], vn_ref[rows, :], kp, vp, page,
                                 t_new)
        o_ref[rows, :] = _rms(att, og_ref[...])


def _attend_one_sample(q, kn, vn, kp, vp, page, t_new):
    n_pages = len(kp)
    n_cols = t_new * ATTN_HEADS
    pages_per_block = MOBA_BLOCK // page
    n_past = n_pages // pages_per_block
    past_len = n_pages * page

    lane_h = lax.broadcasted_iota(jnp.int32, (n_cols, ATTN_WIDTH), 1) // HEAD_DIM
    row = lax.broadcasted_iota(jnp.int32, (n_cols, 1), 0)
    row_h = row % ATTN_HEADS
    row_t = row // ATTN_HEADS
    own_head = lane_h == row_h
    qrep = jnp.concatenate(
        [jnp.broadcast_to(q[t:t + 1, :], (ATTN_HEADS, ATTN_WIDTH)) for t in range(t_new)], axis=0)
    qbd = jnp.where(own_head, qrep, 0.0)
    qbd_s = (qbd * (HEAD_DIM ** -0.5)).astype(BF16)
    slope = jnp.exp2(-8.0 * (row_h + 1).astype(F32) / ATTN_HEADS)

    def block_pages(refs_, n):
        return [refs_[i][0].reshape(ATTN_WIDTH, page)
                for i in range(n * pages_per_block, (n + 1) * pages_per_block)]

    key_off = lax.broadcasted_iota(jnp.int32, (1, MOBA_BLOCK), 1)
    k_sums, s_past = [], []
    for n in range(n_past):
        pages = block_pages(kp, n)
        k_sums.append(jnp.sum(sum(pages), axis=1, keepdims=True))
        kb_t = jnp.concatenate(pages, axis=1).astype(BF16)
        dist = (past_len + row_t) - (n * MOBA_BLOCK + key_off)
        s_past.append(jnp.dot(qbd_s, kb_t, preferred_element_type=F32) - slope * dist.astype(F32))
    kmean = jnp.concatenate(k_sums, axis=1) * (1.0 / MOBA_BLOCK)
    gate = jnp.dot(qbd, kmean, precision=lax.Precision.HIGHEST, preferred_element_type=F32)
    sel = _top_rank_select(gate, n_past, MOBA_TOPK, 1)

    s_own = lax.dot_general(qbd_s, kn.astype(BF16), _NT, preferred_element_type=F32)
    dist = row_t - lax.broadcasted_iota(jnp.int32, (1, t_new), 1)
    s_own = jnp.where(dist >= 0, s_own - slope * dist.astype(F32), NEG_INF)
    m = jnp.max(s_own, axis=-1, keepdims=True)
    for n in range(n_past):
        s_past[n] = jnp.where(sel[:, n:n + 1], s_past[n], NEG_INF)
        m = jnp.maximum(m, jnp.max(s_past[n], axis=-1, keepdims=True))

    p = jnp.exp(s_own - m)
    l = jnp.sum(p, axis=-1, keepdims=True)
    acc = sum(p[:, t:t + 1] * vn[t:t + 1, :] for t in range(t_new))
    for n in range(n_past):
        vb_t = jnp.concatenate(block_pages(vp, n), axis=1).astype(BF16)
        p = jnp.exp(s_past[n] - m)
        l = l + jnp.sum(p, axis=-1, keepdims=True)
        acc = acc + lax.dot_general(p.astype(BF16), vb_t, _NT, preferred_element_type=F32)

    out = jnp.where(own_head, acc / l, 0.0)
    return jnp.sum(out.reshape(t_new, ATTN_HEADS, ATTN_WIDTH), axis=1)


def _attn_sample(q, kn, vn, og_attn, cache_kt, cache_vt, page_table, t_new, seq_lo, n_seq):
    n_pages = page_table.shape[1]
    page = cache_kt.shape[-1]
    assert n_seq % SEQS_PER_STEP == 0 and seq_lo % SEQS_PER_STEP == 0
    step_rows = SEQS_PER_STEP * t_new
    step_lo = seq_lo // SEQS_PER_STEP
    in_rows = pl.BlockSpec((step_rows, ATTN_WIDTH), lambda i, pt: (step_lo + i, 0))
    out_rows = pl.BlockSpec((step_rows, ATTN_WIDTH), lambda i, pt: (i, 0))

    n_steps = n_seq // SEQS_PER_STEP
    page_buf = pltpu.VMEM((2, SEQS_PER_STEP * n_pages, ATTN_HEADS, HEAD_DIM, page), F32)
    return pl.pallas_call(
        functools.partial(_attn_sample_kernel, n_pages=n_pages, page=page, t_new=t_new,
                          seq_lo=seq_lo, n_steps=n_steps),
        grid_spec=pltpu.PrefetchScalarGridSpec(
            num_scalar_prefetch=1,
            grid=(n_steps,),
            in_specs=([in_rows] * 3 + [pl.BlockSpec(og_attn.shape, lambda i, pt: (0, 0)),
                                       pl.BlockSpec(memory_space=pl.ANY),
                                       pl.BlockSpec(memory_space=pl.ANY)]),
            out_specs=out_rows,
            scratch_shapes=[page_buf, page_buf, pltpu.SemaphoreType.DMA((2,)),
                            pltpu.SemaphoreType.DMA((2,))]),
        out_shape=jax.ShapeDtypeStruct((n_seq * t_new, ATTN_WIDTH), F32),
        compiler_params=pltpu.CompilerParams(
            dimension_semantics=("arbitrary",), vmem_limit_bytes=VMEM_LIMIT),
    )(page_table.reshape(-1), q, kn, vn, og_attn, cache_kt, cache_vt)


def _outproj_kernel(att_ref, gmn_ref, x_ref, wo_ref, nfg_ref, wr_ref, br_ref, x1_ref, *route_refs,
                    grouped):
    mix = (jnp.dot(att_ref[...].astype(BF16), wo_ref[:ATTN_WIDTH, :], preferred_element_type=F32)
           + jnp.dot(gmn_ref[...], wo_ref[ATTN_WIDTH:, :], preferred_element_type=F32))
    x1 = x_ref[...] + mix
    x1_ref[...] = x1
    h2 = _rms(x1, nfg_ref[...])
    h2_hi = h2.astype(BF16)

    h2_lo = (h2 - h2_hi.astype(F32)).astype(BF16)
    hi_dot = jnp.dot(h2_hi, wr_ref[...], preferred_element_type=F32)
    lo_dot = jnp.dot(h2_lo, wr_ref[:, :LANES], preferred_element_type=F32)
    logits = hi_dot[:, :LANES] + hi_dot[:, LANES:] + lo_dot + br_ref[...]
    lt = logits.T
    tm = lt.shape[1]
    row4 = lax.broadcasted_iota(jnp.int32, (MOE_GROUPS, tm), 0)

    def first_argmax(v):
        vmax = jnp.max(v, axis=0, keepdims=True)
        idx = jnp.min(jnp.where(v == vmax, row4, MOE_GROUPS), axis=0, keepdims=True)
        return vmax, idx

    glog = lt[:MOE_GROUPS]
    ge = jnp.exp(glog - jnp.max(glog, axis=0, keepdims=True))
    gprob = ge / jnp.sum(ge, axis=0, keepdims=True)
    p_g, g_idx = first_argmax(gprob)
    elog = lt[MOE_GROUPS:MOE_GROUPS + EXPERTS_PER_GROUP]
    for g in range(1, MOE_GROUPS):
        lo = MOE_GROUPS + g * EXPERTS_PER_GROUP
        elog = jnp.where(g_idx == g, lt[lo:lo + EXPERTS_PER_GROUP], elog)
    l1, i1 = first_argmax(elog)
    l2, i2 = first_argmax(jnp.where(row4 == i1, NEG_INF, elog))
    e2 = jnp.exp(l2 - l1)
    denom = 1.0 + e2
    w1 = (1.0 / denom) * p_g
    w2 = (e2 / denom) * p_g
    if not grouped:
        h2_ref, comb_ref = route_refs
        h2_ref[...] = h2_hi
        lane_row = lax.broadcasted_iota(jnp.int32, (LANES, tm), 0)
        base = g_idx * EXPERTS_PER_GROUP
        comb_t = (jnp.where(lane_row == base + i1, w1, 0.0)
                  + jnp.where(lane_row == base + i2, w2, 0.0))
        comb_ref[...] = comb_t.T
        return

    bucket_ref, wpair_ref = route_refs
    e_lo = jnp.minimum(i1, i2)
    e_hi = jnp.maximum(i1, i2)
    pair = jnp.where(e_lo == 0, e_hi - 1, jnp.where(e_lo == 1, e_hi + 1, PAIRS_PER_GROUP - 1))
    row8 = lax.broadcasted_iota(jnp.int32, (8, tm), 0)
    bucket_ref[...] = jnp.where(row8 == 0, g_idx * PAIRS_PER_GROUP + pair, 0)
    w_lo = jnp.where(i1 < i2, w1, w2)
    w_hi = jnp.where(i1 < i2, w2, w1)
    wpair_ref[...] = jnp.where(row8 == 0, w_lo, jnp.where(row8 == 1, w_hi, 0.0))


def _outproj(att_n, gmn, x, wo_bf, nf_g, w_r, b_r, tm, grouped):
    rows = x.shape[0]
    row_spec = lambda width: pl.BlockSpec((tm, width), lambda i: (i, 0))
    full = lambda a: pl.BlockSpec(a.shape, lambda i: (0,) * a.ndim)
    lane_spec = pl.BlockSpec((8, tm), lambda i: (0, i))
    if grouped:
        route_specs = [lane_spec, lane_spec]
        route_shapes = [jax.ShapeDtypeStruct((8, rows), jnp.int32),
                        jax.ShapeDtypeStruct((8, rows), F32)]
    else:
        route_specs = [row_spec(D_MODEL), row_spec(LANES)]
        route_shapes = [jax.ShapeDtypeStruct((rows, D_MODEL), BF16),
                        jax.ShapeDtypeStruct((rows, LANES), F32)]
    return pl.pallas_call(
        functools.partial(_outproj_kernel, grouped=grouped),
        grid=(rows // tm,),
        in_specs=[row_spec(ATTN_WIDTH), row_spec(MLP_WIDTH), row_spec(D_MODEL), full(wo_bf),
                  full(nf_g), full(w_r), full(b_r)],
        out_specs=[row_spec(D_MODEL)] + route_specs,
        out_shape=[jax.ShapeDtypeStruct((rows, D_MODEL), F32)] + route_shapes,
        compiler_params=pltpu.CompilerParams(
            dimension_semantics=("arbitrary",), vmem_limit_bytes=VMEM_LIMIT),
    )(att_n, gmn, x, wo_bf, nf_g, w_r, b_r)


def _moe_kernel(h_ref, comb_ref, wg_ref, wu_ref, wd_ref, x1_ref, fg_ref, y_ref, acc_ref):
    e = pl.program_id(1)

    @pl.when(e == 0)
    def _():
        acc_ref[...] = jnp.zeros_like(acc_ref)

    h = h_ref[...]
    hg = jnp.dot(h, wg_ref[0], preferred_element_type=F32)
    hu = jnp.dot(h, wu_ref[0], preferred_element_type=F32)
    comb = comb_ref[...]
    lane = lax.broadcasted_iota(jnp.int32, comb.shape, 1)
    c = jnp.sum(jnp.where(lane == e, comb, 0.0), axis=-1, keepdims=True)
    act = hg * (1.0 / (1.0 + jnp.exp(-hg))) * hu * c
    acc_ref[...] += jnp.dot(act.astype(BF16), wd_ref[0], preferred_element_type=F32)

    @pl.when(e == N_EXPERTS - 1)
    def _():
        y_ref[...] = _rms(x1_ref[...] + acc_ref[...], fg_ref[...])


def _moe(h2, comb, wg_bf, wu_bf, wd_bf, x1, final_g, tm):
    rows = h2.shape[0]
    row_spec = lambda width: pl.BlockSpec((tm, width), lambda i, e: (i, 0))
    return pl.pallas_call(
        _moe_kernel,
        grid=(rows // tm, N_EXPERTS),
        in_specs=[row_spec(D_MODEL), row_spec(LANES),
                  pl.BlockSpec((1, D_MODEL, D_EXPERT), lambda i, e: (e, 0, 0)),
                  pl.BlockSpec((1, D_MODEL, D_EXPERT), lambda i, e: (e, 0, 0)),
                  pl.BlockSpec((1, D_EXPERT, D_MODEL), lambda i, e: (e, 0, 0)),
                  row_spec(D_MODEL),
                  pl.BlockSpec(final_g.shape, lambda i, e: (0, 0))],
        out_specs=row_spec(D_MODEL),
        out_shape=jax.ShapeDtypeStruct((rows, D_MODEL), F32),
        scratch_shapes=[pltpu.VMEM((tm, D_MODEL), F32)],
        compiler_params=pltpu.CompilerParams(
            dimension_semantics=("arbitrary", "arbitrary"), vmem_limit_bytes=VMEM_LIMIT),
    )(h2, comb, wg_bf, wu_bf, wd_bf, x1, final_g)


SC_CORES = 2
SC_SUBCORES = 16
SC_WINDOW = 32


def _sc_gather_rows(x, idx):
    n = idx.shape[0]
    width = x.shape[1]
    assert n % SC_WINDOW == 0
    mesh = plsc.VectorSubcoreMesh(core_axis_name="core", subcore_axis_name="subcore",
                                  num_cores=SC_CORES, num_subcores=SC_SUBCORES)

    @pl.kernel(out_type=jax.ShapeDtypeStruct((n, width), x.dtype), mesh=mesh)
    def gather_kernel(x_hbm, idx_hbm, out_hbm):
        def body(idx_vmem, out_vmem):
            pltpu.sync_copy(x_hbm.at[idx_vmem.at[0, pl.ds(0, SC_WINDOW)]], out_vmem)

        pltpu.emit_pipeline(
            body,
            grid=(n // SC_WINDOW,),
            in_specs=[pl.BlockSpec((1, LANES), lambda i: (i, 0))],
            out_specs=[pl.BlockSpec((SC_WINDOW, width), lambda i: (i, 0))],
            core_axis_name=("core", "subcore"),
            dimension_semantics=(pltpu.PARALLEL,),
        )(idx_hbm, out_hbm)

    idx_rows = jnp.pad(idx.reshape(n // SC_WINDOW, SC_WINDOW), ((0, 0), (0, LANES - SC_WINDOW)))
    return gather_kernel(x, idx_rows)


class MoePlan(NamedTuple):
    slot_row: jax.Array
    token_slot: jax.Array
    e_lo: jax.Array
    e_hi: jax.Array
    n_valid: jax.Array
    w_slots: jax.Array


def _moe_plan(bucket, wpair, tm):
    rows = bucket.shape[0]
    n_tiles_max = rows // tm + N_BUCKETS
    i32 = jnp.int32
    b_ids = jnp.arange(N_BUCKETS, dtype=i32)[:, None]

    def lookup(table, keys):
        return jnp.sum(jnp.where(keys[None, :] == b_ids, table[:, None], 0), axis=0)

    _, order, wlo_sorted, whi_sorted = lax.sort(
        (bucket, jnp.arange(rows, dtype=i32), wpair[0], wpair[1]), num_keys=1, is_stable=True)
    position = jnp.argsort(order).astype(i32)
    counts = jnp.sum((bucket[None, :] == b_ids).astype(i32), axis=1)
    starts = jnp.cumsum(counts) - counts
    tiles_b = (counts + tm - 1) // tm
    tile_end = jnp.cumsum(tiles_b)
    tile_start = tile_end - tiles_b
    n_tiles = tile_end[-1]
    token_slot = position + lookup(tile_start * tm - starts, bucket)
    t = jnp.arange(n_tiles_max, dtype=i32)
    tb = jnp.sum((jnp.minimum(t, n_tiles - 1)[None, :] >= tile_end[:, None]).astype(i32), axis=0)
    local = (t - lookup(tile_start, tb)) * tm
    n_valid = jnp.where(t < n_tiles, jnp.clip(lookup(counts, tb) - local, 0, tm), 0)
    slot = local[:, None] + jnp.arange(tm, dtype=i32)[None, :]
    src = ((lookup(starts, tb)[:, None] + slot) % rows).reshape(-1)
    assert rows < 2 ** 24
    sorted_cols = jnp.stack([order.astype(F32), wlo_sorted, whi_sorted], axis=1)
    slot_cols = sorted_cols[src]
    group = tb // PAIRS_PER_GROUP
    pair = tb % PAIRS_PER_GROUP
    table_pad = (0,) * (N_BUCKETS - PAIRS_PER_GROUP)
    e_lo = group * EXPERTS_PER_GROUP + lookup(jnp.asarray(PAIR_LO + table_pad, i32), pair)
    e_hi = group * EXPERTS_PER_GROUP + lookup(jnp.asarray(PAIR_HI + table_pad, i32), pair)
    return MoePlan(slot_cols[:, 0].astype(i32), token_slot.astype(i32), e_lo.astype(i32),
                   e_hi.astype(i32), n_valid.astype(i32), slot_cols[:, 1:])


def _moe_grouped_kernel(elo_ref, ehi_ref, nv_ref, x1_ref, ws_ref, wg_lo, wu_lo, wd_lo,
                        wg_hi, wu_hi, wd_hi, nfg_ref, fg_ref, y_ref):
    del elo_ref, ehi_ref
    n_valid = nv_ref[pl.program_id(0)]
    tm = x1_ref.shape[0]
    half = tm // 2

    def experts(rows):
        x1 = x1_ref[:rows, :]
        h = _rms(x1, nfg_ref[...]).astype(BF16)
        ws = ws_ref[:rows, :]
        moe = jnp.zeros(x1.shape, F32)
        for col, (wg, wu, wd) in enumerate(((wg_lo, wu_lo, wd_lo), (wg_hi, wu_hi, wd_hi))):
            hg = jnp.dot(h, wg[0], preferred_element_type=F32)
            hu = jnp.dot(h, wu[0], preferred_element_type=F32)
            act = hg * (1.0 / (1.0 + jnp.exp(-hg))) * hu * ws[:, col:col + 1]
            moe = moe + jnp.dot(act.astype(BF16), wd[0], preferred_element_type=F32)
        y_ref[:rows, :] = _rms(x1 + moe, fg_ref[...])

    @pl.when(n_valid > half)
    def _():
        experts(tm)

    @pl.when((n_valid > 0) & (n_valid <= half))
    def _():
        experts(half)
        y_ref[half:, :] = jnp.zeros((tm - half, y_ref.shape[1]), F32)

    @pl.when(n_valid == 0)
    def _():
        y_ref[...] = jnp.zeros_like(y_ref)


def _moe_grouped(x1_slots, plan, wg_bf, wu_bf, wd_bf, nf_g, final_g, tm):
    n_steps = plan.n_valid.shape[0]
    lo_spec = lambda shape: pl.BlockSpec(shape, lambda t, elo, ehi, nv: (elo[t], 0, 0))
    hi_spec = lambda shape: pl.BlockSpec(shape, lambda t, elo, ehi, nv: (ehi[t], 0, 0))
    up_shape, down_shape = (1, D_MODEL, D_EXPERT), (1, D_EXPERT, D_MODEL)
    const = lambda a: pl.BlockSpec(a.shape, lambda t, elo, ehi, nv: (0, 0))
    row_spec = lambda width: pl.BlockSpec((tm, width), lambda t, elo, ehi, nv: (t, 0))
    return pl.pallas_call(
        _moe_grouped_kernel,
        grid_spec=pltpu.PrefetchScalarGridSpec(
            num_scalar_prefetch=3,
            grid=(n_steps,),
            in_specs=[row_spec(D_MODEL), row_spec(2),
                      lo_spec(up_shape), lo_spec(up_shape), lo_spec(down_shape),
                      hi_spec(up_shape), hi_spec(up_shape), hi_spec(down_shape),
                      const(nf_g), const(final_g)],
            out_specs=row_spec(D_MODEL)),
        out_shape=jax.ShapeDtypeStruct(x1_slots.shape, F32),
        compiler_params=pltpu.CompilerParams(
            dimension_semantics=("arbitrary",), vmem_limit_bytes=VMEM_LIMIT),
    )(plan.e_lo, plan.e_hi, plan.n_valid, x1_slots, plan.w_slots, wg_bf, wu_bf, wd_bf,
      wg_bf, wu_bf, wd_bf, nf_g, final_g)


def _spatial_operands(w_s, b_s, t_chunk):
    reps = CHUNK // t_chunk
    idx = jnp.arange(CHUNK)
    same = (idx[:, None] // t_chunk) == (idx[None, :] // t_chunk)
    causal = (idx[None, :] % t_chunk) <= (idx[:, None] % t_chunk)
    wm = jnp.tile(w_s[:, :t_chunk, :t_chunk], (1, reps, reps)) * (same & causal)
    wm_cat = wm.transpose(1, 0, 2).reshape(CHUNK, MLP_GROUPS * CHUNK).astype(BF16)
    bias = jnp.repeat(jnp.tile(b_s[:, :t_chunk], (1, reps)).T, MLP_CH, axis=1)
    return wm_cat, bias


def kernel(x_prompt, x_sample, cache_k, cache_v, page_table, norm_attn_g, w_in, sgu_g, w_spatial,
           b_spatial, out_g_attn, out_g_mlp, w_out, norm_ffn_g, w_group, b_group, w_router, b_router,
           w_gate, w_up, w_down, final_g):
    depth = w_in.shape[0]
    assert depth == 1, "single decoder layer"
    batch, seq, _ = x_prompt.shape
    n_seq, t_new, _ = x_sample.shape
    assert seq % MOBA_BLOCK == 0 and seq % CHUNK == 0 and CHUNK % t_new == 0
    assert MOBA_BLOCK % cache_k.shape[2] == 0

    row2 = lambda a: a.reshape(1, -1)
    w_in_bf = w_in[0].astype(BF16)
    w_out_bf = w_out[0].astype(BF16)
    n_logits = MOE_GROUPS + N_EXPERTS
    w_r = jnp.concatenate(
        [w_group[0], w_router[0].transpose(1, 0, 2).reshape(D_MODEL, N_EXPERTS),
         jnp.zeros((D_MODEL, LANES - n_logits), F32)], axis=1)
    b_r = jnp.concatenate(
        [b_group[0], b_router[0].reshape(-1), jnp.zeros((LANES - n_logits,), F32)]).reshape(1, LANES)
    w_r_hi = w_r.astype(BF16)
    w_r_hl = jnp.concatenate([w_r_hi, (w_r - w_r_hi.astype(F32)).astype(BF16)], axis=1)
    wkv_t_bf = w_in_bf[:, ATTN_WIDTH:3 * ATTN_WIDTH].T
    ck_t = jnp.transpose(cache_k[0], (0, 2, 3, 1))
    cv_t = jnp.transpose(cache_v[0], (0, 2, 3, 1))

    def project_in(x, t_chunk, tm, seq_transposed, cast_along=(), emit_vgn=True):
        wm_cat, bias_full = _spatial_operands(w_spatial[0], b_spatial[0], t_chunk)
        return _inproj(x, row2(norm_attn_g[0]), w_in_bf, wkv_t_bf, row2(sgu_g[0]), wm_cat, bias_full,
                       row2(out_g_mlp[0]), tm, seq_transposed, cast_along, emit_vgn)

    def project_out(att_n, gmn, x, tm, grouped):
        return _outproj(att_n, gmn, x, w_out_bf, row2(norm_ffn_g[0]), w_r_hl, b_r, tm, grouped)

    og_attn = row2(out_g_attn[0])

    xp = x_prompt.reshape(batch * seq, D_MODEL)
    qp, kp_t, vp_t, gmn_p, wg_bf, wu_bf, wd_bf = project_in(
        xp, CHUNK, TM_PROJ_PROMPT, seq, (w_gate[0], w_up[0], w_down[0]), emit_vgn=False)
    att_p = _attn_prompt(qp, kp_t, vp_t, og_attn, batch, seq)
    x1_p, bucket_p, wpair_p = project_out(att_p, gmn_p, xp, TM_PROJ_PROMPT, True)
    plan = _moe_plan(bucket_p[0], wpair_p[:2], TM_MOE_PROMPT)
    x1_slots = _sc_gather_rows(x1_p, plan.slot_row)

    xs = x_sample.reshape(n_seq * t_new, D_MODEL)
    qs, ks, vs, gmn_s, gvs = project_in(xs, t_new, TM_PROJ_SAMPLE, None)
    half = n_seq // 2
    attend_half = lambda q_all, lo: _attn_sample(q_all, ks, vs, og_attn, ck_t, cv_t, page_table,
                                                 t_new, lo, half)
    att_s0 = attend_half(qs, 0)
    x1_slots, att_s0 = lax.optimization_barrier((x1_slots, att_s0))
    y_slots = _moe_grouped(x1_slots, plan, wg_bf, wu_bf, wd_bf, row2(norm_ffn_g[0]),
                           row2(final_g), TM_MOE_PROMPT)
    y_slots, qs_late = lax.optimization_barrier((y_slots, qs))
    yp = _sc_gather_rows(y_slots, plan.token_slot)
    att_s1 = attend_half(qs_late, half)
    att_s = jnp.concatenate([att_s0, att_s1], axis=0)
    x1_s, h2_s, comb_s = project_out(att_s, gmn_s, xs, TM_PROJ_SAMPLE, False)
    ys = _moe(h2_s, comb_s, wg_bf, wu_bf, wd_bf, x1_s, row2(final_g), TM_MOE_SAMPLE)

    heads = (ATTN_HEADS, HEAD_DIM)
    rows_last = lambda a_t: a_t.reshape(batch, *heads, seq).transpose(0, 3, 1, 2)[None]
    return (yp.reshape(batch, seq, D_MODEL),
            ys.reshape(n_seq, t_new, D_MODEL),
            rows_last(kp_t),
            rows_last(vp_t),
            ks.reshape(depth, n_seq, t_new, *heads),
            vs.reshape(depth, n_seq, t_new, *heads),
            gvs.reshape(depth, n_seq, t_new, MLP_WIDTH))
```

```python
import functools
import math
from typing import NamedTuple

import jax
import jax.numpy as jnp
from jax import lax
from jax.experimental import pallas as pl
from jax.experimental.pallas import tpu as pltpu
from jax.experimental.pallas import tpu_sc as plsc

D_MODEL = 1024
ATTN_HEADS = 8
HEAD_DIM = 64
ATTN_WIDTH = ATTN_HEADS * HEAD_DIM
MOBA_BLOCK = 256
MOBA_TOPK = 3
MLP_GROUPS = 8
MLP_CH = 64
MLP_WIDTH = MLP_GROUPS * MLP_CH
CHUNK = 128
MOE_GROUPS = 4
EXPERTS_PER_GROUP = 4
N_EXPERTS = MOE_GROUPS * EXPERTS_PER_GROUP
D_EXPERT = D_MODEL // 2
EPS = 1e-6
PAIR_LO = (0, 0, 0, 1, 1, 2)
PAIR_HI = (1, 2, 3, 2, 3, 3)
PAIRS_PER_GROUP = len(PAIR_LO)
N_BUCKETS = MOE_GROUPS * PAIRS_PER_GROUP

LANES = 128
BF16_ROWS = 16
VMEM_LIMIT = 56 * 1024 * 1024

TM_PROJ_PROMPT = 512
TM_PROJ_SAMPLE = 128
TM_MOE_PROMPT = 256
TM_MOE_SAMPLE = 512

F32 = jnp.float32
BF16 = jnp.bfloat16
NEG_INF = float("-inf")
_NT = (((1,), (1,)), ((), ()))


def _rms(x, g):
    return x * lax.rsqrt(jnp.mean(x * x, axis=-1, keepdims=True) + EPS) * g


def _gelu(x):
    return 0.5 * x * (1.0 + jnp.tanh(0.7978845608028654 * (x + 0.044715 * (x * x * x))))


def _top_rank_select(gate, n_past, n_keep, axis):
    nb = gate.shape[axis]
    n_idx = lax.broadcasted_iota(jnp.int32, gate.shape, axis)
    rank = jnp.zeros(gate.shape, jnp.int32)
    for m in range(nb):
        gm = gate[:, m:m + 1] if axis == 1 else gate[m:m + 1, :]
        beats = jnp.where(gm > gate, 1, jnp.where(gm == gate, jnp.where(m < n_idx, 1, 0), 0))
        rank = rank + jnp.where(m < n_past, beats, 0)
    return jnp.where(n_idx < n_past, rank, n_keep) < n_keep


def _inproj_kernel(x_ref, g_ref, w_ref, wkv_t_ref, sgu_ref, wm_ref, bias_ref, og_ref, *refs,
                   n_chunks, kv_transposed, n_cast, emit_vgn):
    n_main = 5 if emit_vgn else 4
    cast_in, main, cast_out = refs[:n_cast], refs[n_cast:n_cast + n_main], refs[n_cast + n_main:]
    q_ref, k_ref, v_ref, gmn_ref = main[:4]
    for src_ref, dst_ref in zip(cast_in, cast_out):
        dst_ref[...] = src_ref[...].astype(BF16)

    h = _rms(x_ref[...], g_ref[...]).astype(BF16)

    def proj(lo, width):
        return jnp.dot(h, w_ref[:, lo:lo + width], preferred_element_type=F32)

    vgn = _rms(_gelu(proj(3 * ATTN_WIDTH + MLP_WIDTH, MLP_WIDTH)), sgu_ref[...])
    if emit_vgn:
        main[4][...] = vgn
    gu = _gelu(proj(3 * ATTN_WIDTH, MLP_WIDTH))

    lane_grp = lax.broadcasted_iota(jnp.int32, (CHUNK, MLP_WIDTH), 1) // MLP_CH
    mixed = []
    for c in range(n_chunks):
        vc = vgn[c * CHUNK:(c + 1) * CHUNK].astype(BF16)
        vbd = jnp.concatenate(
            [jnp.where(lane_grp == g, vc, jnp.zeros_like(vc)) for g in range(MLP_GROUPS)], axis=0)
        mixed.append(jnp.dot(wm_ref[...], vbd, preferred_element_type=F32) + bias_ref[...])

    q_ref[...] = proj(0, ATTN_WIDTH)
    for c in range(n_chunks):
        rows = slice(c * CHUNK, (c + 1) * CHUNK)
        gmn_ref[rows, :] = _rms(gu[rows] * mixed[c], og_ref[...]).astype(BF16)
    if kv_transposed:
        kv_t = lax.dot_general(wkv_t_ref[...], h, _NT, preferred_element_type=F32)
        k_ref[0] = kv_t[:ATTN_WIDTH]
        v_ref[0] = kv_t[ATTN_WIDTH:]
    else:
        k_ref[...] = proj(ATTN_WIDTH, ATTN_WIDTH)
        v_ref[...] = proj(2 * ATTN_WIDTH, ATTN_WIDTH)


def _inproj(x, g, w_bf, wkv_t_bf, sgu_g, wm_cat, bias_full, og_mlp, tm, seq_transposed=None,
            cast_along=(), emit_vgn=True):
    rows = x.shape[0]
    steps = rows // tm
    row_spec = lambda width: pl.BlockSpec((tm, width), lambda i: (i, 0))
    full = lambda a: pl.BlockSpec(a.shape, lambda i: (0,) * a.ndim)
    out = jax.ShapeDtypeStruct((rows, ATTN_WIDTH), F32)
    kv_spec, kv_out = row_spec(ATTN_WIDTH), out
    if seq_transposed is not None:
        tiles = seq_transposed // tm
        kv_spec = pl.BlockSpec((1, ATTN_WIDTH, tm), lambda i: (i // tiles, 0, i % tiles))
        kv_out = jax.ShapeDtypeStruct((rows // seq_transposed, ATTN_WIDTH, seq_transposed), F32)

    def slab_spec(a):
        per = steps // a.shape[0]
        assert per * a.shape[0] == steps and a.shape[1] % per == 0
        return pl.BlockSpec((1, a.shape[1] // per, a.shape[2]), lambda i: (i // per, i % per, 0))

    cast_specs = [slab_spec(a) for a in cast_along]
    vgn_spec, vgn_out = ([row_spec(MLP_WIDTH)], [out]) if emit_vgn else ([], [])
    return pl.pallas_call(
        functools.partial(_inproj_kernel, n_chunks=tm // CHUNK,
                          kv_transposed=seq_transposed is not None, n_cast=len(cast_along),
                          emit_vgn=emit_vgn),
        grid=(steps,),
        in_specs=[row_spec(D_MODEL), full(g), full(w_bf), full(wkv_t_bf), full(sgu_g), full(wm_cat),
                  full(bias_full), full(og_mlp)] + cast_specs,
        out_specs=[row_spec(ATTN_WIDTH), kv_spec, kv_spec, row_spec(MLP_WIDTH)] + vgn_spec
        + cast_specs,
        out_shape=[out, kv_out, kv_out, jax.ShapeDtypeStruct((rows, MLP_WIDTH), BF16)] + vgn_out
        + [jax.ShapeDtypeStruct(a.shape, BF16) for a in cast_along],
        compiler_params=pltpu.CompilerParams(
            dimension_semantics=("arbitrary",), vmem_limit_bytes=VMEM_LIMIT),
    )(x, g, w_bf, wkv_t_bf, sgu_g, wm_cat, bias_full, og_mlp, *cast_along)


V_ROWS = HEAD_DIM + BF16_ROWS
LOG2_E = 1.4426950408889634
MASKED = -1e30
Q_TILE = MOBA_BLOCK
COL_R, COL_J, COL_SEL = 0, 2, 8


def _bf16_split(x):
    mantissa, exponent = math.frexp(x)
    high = math.ldexp(round(mantissa * 256.0) / 256.0, exponent)
    return high, x - high


def _attn_prompt_kernel(q_ref, k_ref, v_ref, og_ref, o_ref, ka_scr, vt_scr, kmean_scr, qa_scr,
                        m_scr, alpha_scr, acc_scr, s_scr, *, n_blocks):
    cur = pl.program_id(1)

    @pl.when(cur == 0)
    def _():
        key = lax.broadcasted_iota(jnp.int32, (MOBA_BLOCK, HEAD_DIM), 0)
        col = lax.broadcasted_iota(jnp.int32, (MOBA_BLOCK, HEAD_DIM), 1)
        ones = jnp.ones((BF16_ROWS, MOBA_BLOCK), BF16)
        for n in range(n_blocks):
            keys = slice(n * MOBA_BLOCK, (n + 1) * MOBA_BLOCK)
            kb = k_ref[0, :, keys].T
            kmean_scr[n:n + 1, :] = jnp.sum(kb, axis=0, keepdims=True) * (1.0 / MOBA_BLOCK)
            vt = v_ref[0, :, keys]
            extra = jnp.where(col < COL_J, key,
                              jnp.where(col < COL_SEL, n, jnp.where(col == COL_SEL + n, 1, 0)))
            extra = extra.astype(F32).astype(BF16)
            for h in range(ATTN_HEADS):
                lanes = slice(h * HEAD_DIM, (h + 1) * HEAD_DIM)
                ka_scr[h, n] = jnp.concatenate([kb[:, lanes].astype(BF16), extra], axis=1)
                vt_scr[h, n, :HEAD_DIM, :] = vt[lanes, :].astype(BF16)
                vt_scr[h, n, HEAD_DIM:, :] = ones

    q_t = q_ref[...].T
    row = lax.broadcasted_iota(jnp.int32, (HEAD_DIM, Q_TILE), 0)
    blk = lax.broadcasted_iota(jnp.int32, (n_blocks, Q_TILE), 0)
    for h in range(ATTN_HEADS):
        lanes = slice(h * HEAD_DIM, (h + 1) * HEAD_DIM)
        hi, lo = _bf16_split(LOG2_E * 2.0 ** (-8.0 * (h + 1) / ATTN_HEADS))
        qh_t = q_t[lanes, :]
        gate = jnp.dot(kmean_scr[:, lanes], qh_t, precision=lax.Precision.HIGHEST,
                       preferred_element_type=F32)
        keep = _top_rank_select(gate, cur, MOBA_TOPK, 0) | (blk >= cur)
        sel_rows = jnp.where(keep, 0.0, MASKED)
        alibi = jnp.where(row == COL_R, hi,
                          jnp.where(row == COL_R + 1, lo,
                                    jnp.where(row == COL_J, hi * MOBA_BLOCK,
                                              jnp.where(row == COL_J + 1, lo * MOBA_BLOCK, 0.0))))
        extra = alibi + jnp.concatenate(
            [jnp.zeros((COL_SEL, Q_TILE), F32), sel_rows,
             jnp.zeros((HEAD_DIM - COL_SEL - n_blocks, Q_TILE), F32)], axis=0)
        qa_scr[h] = jnp.concatenate([qh_t * (LOG2_E * HEAD_DIM ** -0.5), extra], axis=0).astype(BF16)

    def attend_block(j, causal, first):
        for h in range(ATTN_HEADS):
            s = jnp.dot(ka_scr[h, j], qa_scr[h], preferred_element_type=F32)
            if causal is not None:
                s = jnp.where(causal, s, NEG_INF)
            s_scr[h] = s
            m_blk = jnp.max(s, axis=0, keepdims=True)
            if first:
                m_scr[h:h + 1, :] = m_blk
            else:
                m_old = m_scr[h:h + 1, :]
                m_new = jnp.maximum(m_old, m_blk)
                alpha_scr[h:h + 1, :] = jnp.exp2(m_old - m_new)
                m_scr[h:h + 1, :] = m_new
        for h in range(ATTN_HEADS):
            p = jnp.exp2(s_scr[h] - m_scr[h:h + 1, :]).astype(BF16)
            pv = jnp.dot(vt_scr[h, j], p, preferred_element_type=F32)
            acc_scr[h] = pv if first else alpha_scr[h:h + 1, :] * acc_scr[h] + pv

    causal = (lax.broadcasted_iota(jnp.int32, (MOBA_BLOCK, Q_TILE), 0)
              <= lax.broadcasted_iota(jnp.int32, (MOBA_BLOCK, Q_TILE), 1))
    attend_block(cur, causal, True)

    def past_block(j, carry):
        attend_block(j, None, False)
        return carry

    lax.fori_loop(0, cur, past_block, 0)

    outs = []
    for h in range(ATTN_HEADS):
        acc = acc_scr[h]
        outs.append(acc[:HEAD_DIM] / acc[HEAD_DIM:HEAD_DIM + 1])
    o_ref[...] = _rms(jnp.concatenate(outs, axis=0).T, og_ref[...]).astype(BF16)


def _attn_prompt(q, k_t, v_t, og_attn, batch, seq):
    n_blocks = seq // MOBA_BLOCK
    assert COL_SEL + n_blocks <= HEAD_DIM and n_blocks % 8 == 0
    q_spec = pl.BlockSpec((Q_TILE, ATTN_WIDTH), lambda b, i: (b * n_blocks + i, 0))
    return pl.pallas_call(
        functools.partial(_attn_prompt_kernel, n_blocks=n_blocks),
        grid=(batch, n_blocks),
        in_specs=[q_spec,
                  pl.BlockSpec((1, ATTN_WIDTH, seq), lambda b, i: (b, 0, 0)),
                  pl.BlockSpec((1, ATTN_WIDTH, seq), lambda b, i: (b, 0, 0)),
                  pl.BlockSpec(og_attn.shape, lambda b, i: (0, 0))],
        out_specs=q_spec,
        out_shape=jax.ShapeDtypeStruct(q.shape, BF16),
        scratch_shapes=[pltpu.VMEM((ATTN_HEADS, n_blocks, MOBA_BLOCK, 2 * HEAD_DIM), BF16),
                        pltpu.VMEM((ATTN_HEADS, n_blocks, V_ROWS, MOBA_BLOCK), BF16),
                        pltpu.VMEM((n_blocks, ATTN_WIDTH), F32),
                        pltpu.VMEM((ATTN_HEADS, 2 * HEAD_DIM, Q_TILE), BF16),
                        pltpu.VMEM((ATTN_HEADS, Q_TILE), F32),
                        pltpu.VMEM((ATTN_HEADS, Q_TILE), F32),
                        pltpu.VMEM((ATTN_HEADS, V_ROWS, Q_TILE), F32),
                        pltpu.VMEM((ATTN_HEADS, MOBA_BLOCK, Q_TILE), F32)],
        compiler_params=pltpu.CompilerParams(
            dimension_semantics=("arbitrary", "arbitrary"), vmem_limit_bytes=VMEM_LIMIT),
    )(q, k_t, v_t, og_attn)


SEQS_PER_STEP = 2


def _attn_sample_kernel(pt_ref, q_ref, kn_ref, vn_ref, og_ref, ck_hbm, cv_hbm, o_ref,
                        kbuf, vbuf, sem_k, sem_v, *, n_pages, page, t_new, seq_lo, n_steps):
    step = pl.program_id(0)
    slot = step % 2
    pages_per_step = SEQS_PER_STEP * n_pages

    def fetch(step_, slot_):
        first = (seq_lo + step_ * SEQS_PER_STEP) * n_pages
        for j in range(pages_per_step):
            page_id = pt_ref[first + j]
            pltpu.make_async_copy(ck_hbm.at[page_id], kbuf.at[slot_, j], sem_k.at[slot_]).start()
            pltpu.make_async_copy(cv_hbm.at[page_id], vbuf.at[slot_, j], sem_v.at[slot_]).start()

    @pl.when(step == 0)
    def _():
        fetch(0, 0)

    @pl.when(step + 1 < n_steps)
    def _():
        fetch(step + 1, 1 - slot)

    pltpu.make_async_copy(ck_hbm.at[pl.ds(0, pages_per_step)], kbuf.at[slot], sem_k.at[slot]).wait()
    pltpu.make_async_copy(cv_hbm.at[pl.ds(0, pages_per_step)], vbuf.at[slot], sem_v.at[slot]).wait()

    for i in range(SEQS_PER_STEP):
        kp = [kbuf.at[slot, pl.ds(i * n_pages + p, 1)] for p in range(n_pages)]
        vp = [vbuf.at[slot, pl.ds(i * n_pages + p, 1)] for p in range(n_pages)]
        rows = slice(i * t_new, (i + 1) * t_new)
        att = _attend_one_sample(q_ref[rows, :], kn_ref[rows, :], vn_ref[rows, :], kp, vp, page,
                                 t_new)
        o_ref[rows, :] = _rms(att, og_ref[...])


def _attend_one_sample(q, kn, vn, kp, vp, page, t_new):
    n_pages = len(kp)
    n_cols = t_new * ATTN_HEADS
    pages_per_block = MOBA_BLOCK // page
    n_past = n_pages // pages_per_block
    past_len = n_pages * page

    lane_h = lax.broadcasted_iota(jnp.int32, (n_cols, ATTN_WIDTH), 1) // HEAD_DIM
    row = lax.broadcasted_iota(jnp.int32, (n_cols, 1), 0)
    row_h = row % ATTN_HEADS
    row_t = row // ATTN_HEADS
    own_head = lane_h == row_h
    qrep = jnp.concatenate(
        [jnp.broadcast_to(q[t:t + 1, :], (ATTN_HEADS, ATTN_WIDTH)) for t in range(t_new)], axis=0)
    qbd = jnp.where(own_head, qrep, 0.0)
    qbd_s = (qbd * (HEAD_DIM ** -0.5)).astype(BF16)
    slope = jnp.exp2(-8.0 * (row_h + 1).astype(F32) / ATTN_HEADS)

    def block_pages(refs_, n):
        return [refs_[i][0].reshape(ATTN_WIDTH, page)
                for i in range(n * pages_per_block, (n + 1) * pages_per_block)]

    key_off = lax.broadcasted_iota(jnp.int32, (1, MOBA_BLOCK), 1)
    k_sums, s_past = [], []
    for n in range(n_past):
        pages = block_pages(kp, n)
        k_sums.append(jnp.sum(sum(pages), axis=1, keepdims=True))
        kb_t = jnp.concatenate(pages, axis=1).astype(BF16)
        dist = (past_len + row_t) - (n * MOBA_BLOCK + key_off)
        s_past.append(jnp.dot(qbd_s, kb_t, preferred_element_type=F32) - slope * dist.astype(F32))
    kmean = jnp.concatenate(k_sums, axis=1) * (1.0 / MOBA_BLOCK)
    gate = jnp.dot(qbd, kmean, precision=lax.Precision.HIGHEST, preferred_element_type=F32)
    sel = _top_rank_select(gate, n_past, MOBA_TOPK, 1)

    s_own = lax.dot_general(qbd_s, kn.astype(BF16), _NT, preferred_element_type=F32)
    dist = row_t - lax.broadcasted_iota(jnp.int32, (1, t_new), 1)
    s_own = jnp.where(dist >= 0, s_own - slope * dist.astype(F32), NEG_INF)
    m = jnp.max(s_own, axis=-1, keepdims=True)
    for n in range(n_past):
        s_past[n] = jnp.where(sel[:, n:n + 1], s_past[n], NEG_INF)
        m = jnp.maximum(m, jnp.max(s_past[n], axis=-1, keepdims=True))

    p = jnp.exp(s_own - m)
    l = jnp.sum(p, axis=-1, keepdims=True)
    acc = sum(p[:, t:t + 1] * vn[t:t + 1, :] for t in range(t_new))
    for n in range(n_past):
        vb_t = jnp.concatenate(block_pages(vp, n), axis=1).astype(BF16)
        p = jnp.exp(s_past[n] - m)
        l = l + jnp.sum(p, axis=-1, keepdims=True)
        acc = acc + lax.dot_general(p.astype(BF16), vb_t, _NT, preferred_element_type=F32)

    out = jnp.where(own_head, acc / l, 0.0)
    return jnp.sum(out.reshape(t_new, ATTN_HEADS, ATTN_WIDTH), axis=1)


def _attn_sample(q, kn, vn, og_attn, cache_kt, cache_vt, page_table, t_new, seq_lo, n_seq):
    n_pages = page_table.shape[1]
    page = cache_kt.shape[-1]
    assert n_seq % SEQS_PER_STEP == 0 and seq_lo % SEQS_PER_STEP == 0
    step_rows = SEQS_PER_STEP * t_new
    step_lo = seq_lo // SEQS_PER_STEP
    in_rows = pl.BlockSpec((step_rows, ATTN_WIDTH), lambda i, pt: (step_lo + i, 0))
    out_rows = pl.BlockSpec((step_rows, ATTN_WIDTH), lambda i, pt: (i, 0))

    n_steps = n_seq // SEQS_PER_STEP
    page_buf = pltpu.VMEM((2, SEQS_PER_STEP * n_pages, ATTN_HEADS, HEAD_DIM, page), F32)
    return pl.pallas_call(
        functools.partial(_attn_sample_kernel, n_pages=n_pages, page=page, t_new=t_new,
                          seq_lo=seq_lo, n_steps=n_steps),
        grid_spec=pltpu.PrefetchScalarGridSpec(
            num_scalar_prefetch=1,
            grid=(n_steps,),
            in_specs=([in_rows] * 3 + [pl.BlockSpec(og_attn.shape, lambda i, pt: (0, 0)),
                                       pl.BlockSpec(memory_space=pl.ANY),
                                       pl.BlockSpec(memory_space=pl.ANY)]),
            out_specs=out_rows,
            scratch_shapes=[page_buf, page_buf, pltpu.SemaphoreType.DMA((2,)),
                            pltpu.SemaphoreType.DMA((2,))]),
        out_shape=jax.ShapeDtypeStruct((n_seq * t_new, ATTN_WIDTH), F32),
        compiler_params=pltpu.CompilerParams(
            dimension_semantics=("arbitrary",), vmem_limit_bytes=VMEM_LIMIT),
    )(page_table.reshape(-1), q, kn, vn, og_attn, cache_kt, cache_vt)


def _outproj_kernel(att_ref, gmn_ref, x_ref, wo_ref, nfg_ref, wr_ref, br_ref, x1_ref, *route_refs,
                    grouped):
    mix = (jnp.dot(att_ref[...].astype(BF16), wo_ref[:ATTN_WIDTH, :], preferred_element_type=F32)
           + jnp.dot(gmn_ref[...], wo_ref[ATTN_WIDTH:, :], preferred_element_type=F32))
    x1 = x_ref[...] + mix
    x1_ref[...] = x1
    h2 = _rms(x1, nfg_ref[...])
    h2_hi = h2.astype(BF16)

    h2_lo = (h2 - h2_hi.astype(F32)).astype(BF16)
    hi_dot = jnp.dot(h2_hi, wr_ref[...], preferred_element_type=F32)
    lo_dot = jnp.dot(h2_lo, wr_ref[:, :LANES], preferred_element_type=F32)
    logits = hi_dot[:, :LANES] + hi_dot[:, LANES:] + lo_dot + br_ref[...]
    lt = logits.T
    tm = lt.shape[1]
    row4 = lax.broadcasted_iota(jnp.int32, (MOE_GROUPS, tm), 0)

    def first_argmax(v):
        vmax = jnp.max(v, axis=0, keepdims=True)
        idx = jnp.min(jnp.where(v == vmax, row4, MOE_GROUPS), axis=0, keepdims=True)
        return vmax, idx

    glog = lt[:MOE_GROUPS]
    ge = jnp.exp(glog - jnp.max(glog, axis=0, keepdims=True))
    gprob = ge / jnp.sum(ge, axis=0, keepdims=True)
    p_g, g_idx = first_argmax(gprob)
    elog = lt[MOE_GROUPS:MOE_GROUPS + EXPERTS_PER_GROUP]
    for g in range(1, MOE_GROUPS):
        lo = MOE_GROUPS + g * EXPERTS_PER_GROUP
        elog = jnp.where(g_idx == g, lt[lo:lo + EXPERTS_PER_GROUP], elog)
    l1, i1 = first_argmax(elog)
    l2, i2 = first_argmax(jnp.where(row4 == i1, NEG_INF, elog))
    e2 = jnp.exp(l2 - l1)
    denom = 1.0 + e2
    w1 = (1.0 / denom) * p_g
    w2 = (e2 / denom) * p_g
    if not grouped:
        h2_ref, comb_ref = route_refs
        h2_ref[...] = h2_hi
        lane_row = lax.broadcasted_iota(jnp.int32, (LANES, tm), 0)
        base = g_idx * EXPERTS_PER_GROUP
        comb_t = (jnp.where(lane_row == base + i1, w1, 0.0)
                  + jnp.where(lane_row == base + i2, w2, 0.0))
        comb_ref[...] = comb_t.T
        return

    bucket_ref, wpair_ref = route_refs
    e_lo = jnp.minimum(i1, i2)
    e_hi = jnp.maximum(i1, i2)
    pair = jnp.where(e_lo == 0, e_hi - 1, jnp.where(e_lo == 1, e_hi + 1, PAIRS_PER_GROUP - 1))
    row8 = lax.broadcasted_iota(jnp.int32, (8, tm), 0)
    bucket_ref[...] = jnp.where(row8 == 0, g_idx * PAIRS_PER_GROUP + pair, 0)
    w_lo = jnp.where(i1 < i2, w1, w2)
    w_hi = jnp.where(i1 < i2, w2, w1)
    wpair_ref[...] = jnp.where(row8 == 0, w_lo, jnp.where(row8 == 1, w_hi, 0.0))


def _outproj(att_n, gmn, x, wo_bf, nf_g, w_r, b_r, tm, grouped):
    rows = x.shape[0]
    row_spec = lambda width: pl.BlockSpec((tm, width), lambda i: (i, 0))
    full = lambda a: pl.BlockSpec(a.shape, lambda i: (0,) * a.ndim)
    lane_spec = pl.BlockSpec((8, tm), lambda i: (0, i))
    if grouped:
        route_specs = [lane_spec, lane_spec]
        route_shapes = [jax.ShapeDtypeStruct((8, rows), jnp.int32),
                        jax.ShapeDtypeStruct((8, rows), F32)]
    else:
        route_specs = [row_spec(D_MODEL), row_spec(LANES)]
        route_shapes = [jax.ShapeDtypeStruct((rows, D_MODEL), BF16),
                        jax.ShapeDtypeStruct((rows, LANES), F32)]
    return pl.pallas_call(
        functools.partial(_outproj_kernel, grouped=grouped),
        grid=(rows // tm,),
        in_specs=[row_spec(ATTN_WIDTH), row_spec(MLP_WIDTH), row_spec(D_MODEL), full(wo_bf),
                  full(nf_g), full(w_r), full(b_r)],
        out_specs=[row_spec(D_MODEL)] + route_specs,
        out_shape=[jax.ShapeDtypeStruct((rows, D_MODEL), F32)] + route_shapes,
        compiler_params=pltpu.CompilerParams(
            dimension_semantics=("arbitrary",), vmem_limit_bytes=VMEM_LIMIT),
    )(att_n, gmn, x, wo_bf, nf_g, w_r, b_r)


def _moe_kernel(h_ref, comb_ref, wg_ref, wu_ref, wd_ref, x1_ref, fg_ref, y_ref, acc_ref):
    e = pl.program_id(1)

    @pl.when(e == 0)
    def _():
        acc_ref[...] = jnp.zeros_like(acc_ref)

    h = h_ref[...]
    hg = jnp.dot(h, wg_ref[0], preferred_element_type=F32)
    hu = jnp.dot(h, wu_ref[0], preferred_element_type=F32)
    comb = comb_ref[...]
    lane = lax.broadcasted_iota(jnp.int32, comb.shape, 1)
    c = jnp.sum(jnp.where(lane == e, comb, 0.0), axis=-1, keepdims=True)
    act = hg * (1.0 / (1.0 + jnp.exp(-hg))) * hu * c
    acc_ref[...] += jnp.dot(act.astype(BF16), wd_ref[0], preferred_element_type=F32)

    @pl.when(e == N_EXPERTS - 1)
    def _():
        y_ref[...] = _rms(x1_ref[...] + acc_ref[...], fg_ref[...])


def _moe(h2, comb, wg_bf, wu_bf, wd_bf, x1, final_g, tm):
    rows = h2.shape[0]
    row_spec = lambda width: pl.BlockSpec((tm, width), lambda i, e: (i, 0))
    return pl.pallas_call(
        _moe_kernel,
        grid=(rows // tm, N_EXPERTS),
        in_specs=[row_spec(D_MODEL), row_spec(LANES),
                  pl.BlockSpec((1, D_MODEL, D_EXPERT), lambda i, e: (e, 0, 0)),
                  pl.BlockSpec((1, D_MODEL, D_EXPERT), lambda i, e: (e, 0, 0)),
                  pl.BlockSpec((1, D_EXPERT, D_MODEL), lambda i, e: (e, 0, 0)),
                  row_spec(D_MODEL),
                  pl.BlockSpec(final_g.shape, lambda i, e: (0, 0))],
        out_specs=row_spec(D_MODEL),
        out_shape=jax.ShapeDtypeStruct((rows, D_MODEL), F32),
        scratch_shapes=[pltpu.VMEM((tm, D_MODEL), F32)],
        compiler_params=pltpu.CompilerParams(
            dimension_semantics=("arbitrary", "arbitrary"), vmem_limit_bytes=VMEM_LIMIT),
    )(h2, comb, wg_bf, wu_bf, wd_bf, x1, final_g)


SC_CORES = 2
SC_SUBCORES = 16
SC_WINDOW = 32


def _sc_gather_rows(x, idx):
    n = idx.shape[0]
    width = x.shape[1]
    assert n % SC_WINDOW == 0
    mesh = plsc.VectorSubcoreMesh(core_axis_name="core", subcore_axis_name="subcore",
                                  num_cores=SC_CORES, num_subcores=SC_SUBCORES)

    @pl.kernel(out_type=jax.ShapeDtypeStruct((n, width), x.dtype), mesh=mesh)
    def gather_kernel(x_hbm, idx_hbm, out_hbm):
        def body(idx_vmem, out_vmem):
            pltpu.sync_copy(x_hbm.at[idx_vmem.at[0, pl.ds(0, SC_WINDOW)]], out_vmem)

        pltpu.emit_pipeline(
            body,
            grid=(n // SC_WINDOW,),
            in_specs=[pl.BlockSpec((1, LANES), lambda i: (i, 0))],
            out_specs=[pl.BlockSpec((SC_WINDOW, width), lambda i: (i, 0))],
            core_axis_name=("core", "subcore"),
            dimension_semantics=(pltpu.PARALLEL,),
        )(idx_hbm, out_hbm)

    idx_rows = jnp.pad(idx.reshape(n // SC_WINDOW, SC_WINDOW), ((0, 0), (0, LANES - SC_WINDOW)))
    return gather_kernel(x, idx_rows)


class MoePlan(NamedTuple):
    slot_row: jax.Array
    token_slot: jax.Array
    e_lo: jax.Array
    e_hi: jax.Array
    n_valid: jax.Array
    tile_block: jax.Array
    slot_cols: jax.Array


def _moe_plan(bucket, wpair, tm):
    rows = bucket.shape[0]
    n_tiles_max = rows // tm + N_BUCKETS
    i32 = jnp.int32
    b_ids = jnp.arange(N_BUCKETS, dtype=i32)[:, None]

    def lookup(table, keys):
        return jnp.sum(jnp.where(keys[None, :] == b_ids, table[:, None], 0), axis=0)

    _, order, wlo_sorted, whi_sorted = lax.sort(
        (bucket, jnp.arange(rows, dtype=i32), wpair[0], wpair[1]), num_keys=1, is_stable=True)
    position = jnp.argsort(order).astype(i32)
    counts = jnp.sum((bucket[None, :] == b_ids).astype(i32), axis=1)
    starts = jnp.cumsum(counts) - counts
    tiles_b = (counts + tm - 1) // tm
    tile_end = jnp.cumsum(tiles_b)
    tile_start = tile_end - tiles_b
    n_tiles = tile_end[-1]
    token_slot = position + lookup(tile_start * tm - starts, bucket)
    t = jnp.arange(n_tiles_max, dtype=i32)
    tb = jnp.sum((jnp.minimum(t, n_tiles - 1)[None, :] >= tile_end[:, None]).astype(i32), axis=0)
    local = (t - lookup(tile_start, tb)) * tm
    n_valid = jnp.where(t < n_tiles, jnp.clip(lookup(counts, tb) - local, 0, tm), 0)
    slot = local[:, None] + jnp.arange(tm, dtype=i32)[None, :]
    src = ((lookup(starts, tb)[:, None] + slot) % rows).reshape(-1)
    assert rows < 2 ** 24
    sorted_cols = jnp.stack([order.astype(F32), wlo_sorted, whi_sorted], axis=1)
    slot_cols = sorted_cols[src]
    group = tb // PAIRS_PER_GROUP
    pair = tb % PAIRS_PER_GROUP
    table_pad = (0,) * (N_BUCKETS - PAIRS_PER_GROUP)
    e_lo = group * EXPERTS_PER_GROUP + lookup(jnp.asarray(PAIR_LO + table_pad, i32), pair)
    e_hi = group * EXPERTS_PER_GROUP + lookup(jnp.asarray(PAIR_HI + table_pad, i32), pair)
    return MoePlan(slot_cols[:, 0].astype(i32), token_slot.astype(i32), e_lo.astype(i32),
                   e_hi.astype(i32), n_valid.astype(i32), jnp.minimum(t, n_tiles - 1).astype(i32),
                   slot_cols)


def _moe_grouped_kernel(elo_ref, ehi_ref, nv_ref, blk_ref, x1_ref, cols_ref, wg_lo, wu_lo, wd_lo,
                        wg_hi, wu_hi, wd_hi, nfg_ref, fg_ref, y_ref):
    del elo_ref, ehi_ref, blk_ref

    @pl.when(nv_ref[pl.program_id(0)] > 0)
    def _():
        x1 = x1_ref[...]
        h = _rms(x1, nfg_ref[...]).astype(BF16)
        slot_cols = cols_ref[...]
        moe = jnp.zeros(x1.shape, F32)
        for col, (wg, wu, wd) in enumerate(((wg_lo, wu_lo, wd_lo), (wg_hi, wu_hi, wd_hi)), 1):
            hg = jnp.dot(h, wg[0], preferred_element_type=F32)
            hu = jnp.dot(h, wu[0], preferred_element_type=F32)
            act = hg * (1.0 / (1.0 + jnp.exp(-hg))) * hu * slot_cols[:, col:col + 1]
            moe = moe + jnp.dot(act.astype(BF16), wd[0], preferred_element_type=F32)
        y_ref[...] = _rms(x1 + moe, fg_ref[...])


def _moe_grouped(x1_slots, plan, wg_bf, wu_bf, wd_bf, nf_g, final_g, tm):
    n_steps = plan.n_valid.shape[0]
    lo_spec = lambda shape: pl.BlockSpec(shape, lambda t, elo, ehi, nv, blk: (elo[t], 0, 0))
    hi_spec = lambda shape: pl.BlockSpec(shape, lambda t, elo, ehi, nv, blk: (ehi[t], 0, 0))
    up_shape, down_shape = (1, D_MODEL, D_EXPERT), (1, D_EXPERT, D_MODEL)
    const = lambda a: pl.BlockSpec(a.shape, lambda t, elo, ehi, nv, blk: (0, 0))
    row_spec = lambda width: pl.BlockSpec((tm, width), lambda t, elo, ehi, nv, blk: (blk[t], 0))
    return pl.pallas_call(
        _moe_grouped_kernel,
        grid_spec=pltpu.PrefetchScalarGridSpec(
            num_scalar_prefetch=4,
            grid=(n_steps,),
            in_specs=[row_spec(D_MODEL), row_spec(plan.slot_cols.shape[1]),
                      lo_spec(up_shape), lo_spec(up_shape), lo_spec(down_shape),
                      hi_spec(up_shape), hi_spec(up_shape), hi_spec(down_shape),
                      const(nf_g), const(final_g)],
            out_specs=row_spec(D_MODEL)),
        out_shape=jax.ShapeDtypeStruct(x1_slots.shape, F32),
        compiler_params=pltpu.CompilerParams(
            dimension_semantics=("arbitrary",), vmem_limit_bytes=VMEM_LIMIT),
    )(plan.e_lo, plan.e_hi, plan.n_valid, plan.tile_block, x1_slots, plan.slot_cols,
      wg_bf, wu_bf, wd_bf,
      wg_bf, wu_bf, wd_bf, nf_g, final_g)


def _spatial_operands(w_s, b_s, t_chunk):
    reps = CHUNK // t_chunk
    idx = jnp.arange(CHUNK)
    same = (idx[:, None] // t_chunk) == (idx[None, :] // t_chunk)
    causal = (idx[None, :] % t_chunk) <= (idx[:, None] % t_chunk)
    wm = jnp.tile(w_s[:, :t_chunk, :t_chunk], (1, reps, reps)) * (same & causal)
    wm_cat = wm.transpose(1, 0, 2).reshape(CHUNK, MLP_GROUPS * CHUNK).astype(BF16)
    bias = jnp.repeat(jnp.tile(b_s[:, :t_chunk], (1, reps)).T, MLP_CH, axis=1)
    return wm_cat, bias


def kernel(x_prompt, x_sample, cache_k, cache_v, page_table, norm_attn_g, w_in, sgu_g, w_spatial,
           b_spatial, out_g_attn, out_g_mlp, w_out, norm_ffn_g, w_group, b_group, w_router, b_router,
           w_gate, w_up, w_down, final_g):
    depth = w_in.shape[0]
    assert depth == 1, "single decoder layer"
    batch, seq, _ = x_prompt.shape
    n_seq, t_new, _ = x_sample.shape
    assert seq % MOBA_BLOCK == 0 and seq % CHUNK == 0 and CHUNK % t_new == 0
    assert MOBA_BLOCK % cache_k.shape[2] == 0

    row2 = lambda a: a.reshape(1, -1)
    w_in_bf = w_in[0].astype(BF16)
    w_out_bf = w_out[0].astype(BF16)
    n_logits = MOE_GROUPS + N_EXPERTS
    w_r = jnp.concatenate(
        [w_group[0], w_router[0].transpose(1, 0, 2).reshape(D_MODEL, N_EXPERTS),
         jnp.zeros((D_MODEL, LANES - n_logits), F32)], axis=1)
    b_r = jnp.concatenate(
        [b_group[0], b_router[0].reshape(-1), jnp.zeros((LANES - n_logits,), F32)]).reshape(1, LANES)
    w_r_hi = w_r.astype(BF16)
    w_r_hl = jnp.concatenate([w_r_hi, (w_r - w_r_hi.astype(F32)).astype(BF16)], axis=1)
    wkv_t_bf = w_in[0][:, ATTN_WIDTH:3 * ATTN_WIDTH].T.astype(BF16)
    ck_t = jnp.transpose(cache_k[0], (0, 2, 3, 1))
    cv_t = jnp.transpose(cache_v[0], (0, 2, 3, 1))

    def project_in(x, t_chunk, tm, seq_transposed, cast_along=(), emit_vgn=True):
        wm_cat, bias_full = _spatial_operands(w_spatial[0], b_spatial[0], t_chunk)
        return _inproj(x, row2(norm_attn_g[0]), w_in_bf, wkv_t_bf, row2(sgu_g[0]), wm_cat, bias_full,
                       row2(out_g_mlp[0]), tm, seq_transposed, cast_along, emit_vgn)

    def project_out(att_n, gmn, x, tm, grouped):
        return _outproj(att_n, gmn, x, w_out_bf, row2(norm_ffn_g[0]), w_r_hl, b_r, tm, grouped)

    og_attn = row2(out_g_attn[0])

    xp = x_prompt.reshape(batch * seq, D_MODEL)
    qp, kp_t, vp_t, gmn_p, wg_bf, wu_bf, wd_bf = project_in(
        xp, CHUNK, TM_PROJ_PROMPT, seq, (w_gate[0], w_up[0], w_down[0]), emit_vgn=False)
    att_p = _attn_prompt(qp, kp_t, vp_t, og_attn, batch, seq)
    x1_p, bucket_p, wpair_p = project_out(att_p, gmn_p, xp, TM_PROJ_PROMPT, True)
    plan = _moe_plan(bucket_p[0], wpair_p[:2], TM_MOE_PROMPT)
    x1_slots = _sc_gather_rows(x1_p, plan.slot_row)

    xs = x_sample.reshape(n_seq * t_new, D_MODEL)
    qs, ks, vs, gmn_s, gvs = project_in(xs, t_new, TM_PROJ_SAMPLE, None)
    half = n_seq // 2
    attend_half = lambda q_all, lo: _attn_sample(q_all, ks, vs, og_attn, ck_t, cv_t, page_table,
                                                 t_new, lo, half)
    att_s0 = attend_half(qs, 0)
    x1_slots, att_s0 = lax.optimization_barrier((x1_slots, att_s0))
    y_slots = _moe_grouped(x1_slots, plan, wg_bf, wu_bf, wd_bf, row2(norm_ffn_g[0]),
                           row2(final_g), TM_MOE_PROMPT)
    y_slots, qs_late = lax.optimization_barrier((y_slots, qs))
    yp = _sc_gather_rows(y_slots, plan.token_slot)
    att_s1 = attend_half(qs_late, half)
    att_s = jnp.concatenate([att_s0, att_s1], axis=0)
    x1_s, h2_s, comb_s = project_out(att_s, gmn_s, xs, TM_PROJ_SAMPLE, False)
    ys = _moe(h2_s, comb_s, wg_bf, wu_bf, wd_bf, x1_s, row2(final_g), TM_MOE_SAMPLE)

    heads = (ATTN_HEADS, HEAD_DIM)
    rows_last = lambda a_t: a_t.reshape(batch, *heads, seq).transpose(0, 3, 1, 2)[None]
    return (yp.reshape(batch, seq, D_MODEL),
            ys.reshape(n_seq, t_new, D_MODEL),
            rows_last(kp_t),
            rows_last(vp_t),
            ks.reshape(depth, n_seq, t_new, *heads),
            vs.reshape(depth, n_seq, t_new, *heads),
            gvs.reshape(depth, n_seq, t_new, MLP_WIDTH))
```

```python
import functools
import math
from typing import NamedTuple

import jax
import jax.numpy as jnp
from jax import lax
from jax.experimental import pallas as pl
from jax.experimental.pallas import tpu as pltpu
from jax.experimental.pallas import tpu_sc as plsc

D_MODEL = 1024
ATTN_HEADS = 8
HEAD_DIM = 64
ATTN_WIDTH = ATTN_HEADS * HEAD_DIM
MOBA_BLOCK = 256
MOBA_TOPK = 3
MLP_GROUPS = 8
MLP_CH = 64
MLP_WIDTH = MLP_GROUPS * MLP_CH
CHUNK = 128
MOE_GROUPS = 4
EXPERTS_PER_GROUP = 4
N_EXPERTS = MOE_GROUPS * EXPERTS_PER_GROUP
D_EXPERT = D_MODEL // 2
EPS = 1e-6
PAIR_LO = (0, 0, 0, 1, 1, 2)
PAIR_HI = (1, 2, 3, 2, 3, 3)
PAIRS_PER_GROUP = len(PAIR_LO)
N_BUCKETS = MOE_GROUPS * PAIRS_PER_GROUP

LANES = 128
BF16_ROWS = 16
VMEM_LIMIT = 56 * 1024 * 1024

TM_PROJ_PROMPT = 512
TM_PROJ_SAMPLE = 128
TM_MOE_PROMPT = 256
TM_MOE_SAMPLE = 512

F32 = jnp.float32
BF16 = jnp.bfloat16
NEG_INF = float("-inf")
_NT = (((1,), (1,)), ((), ()))


def _rms(x, g):
    return x * lax.rsqrt(jnp.mean(x * x, axis=-1, keepdims=True) + EPS) * g


def _gelu(x):
    return 0.5 * x * (1.0 + jnp.tanh(0.7978845608028654 * (x + 0.044715 * (x * x * x))))


def _top_rank_select(gate, n_past, n_keep, axis):
    nb = gate.shape[axis]
    n_idx = lax.broadcasted_iota(jnp.int32, gate.shape, axis)
    rank = jnp.zeros(gate.shape, jnp.int32)
    for m in range(nb):
        gm = gate[:, m:m + 1] if axis == 1 else gate[m:m + 1, :]
        beats = jnp.where(gm > gate, 1, jnp.where(gm == gate, jnp.where(m < n_idx, 1, 0), 0))
        rank = rank + jnp.where(m < n_past, beats, 0)
    return jnp.where(n_idx < n_past, rank, n_keep) < n_keep


def _inproj_kernel(x_ref, g_ref, w_ref, wkv_t_ref, sgu_ref, wm_ref, bias_ref, og_ref, *refs,
                   n_chunks, kv_transposed, n_cast, emit_vgn):
    n_main = 5 if emit_vgn else 4
    cast_in, main, cast_out = refs[:n_cast], refs[n_cast:n_cast + n_main], refs[n_cast + n_main:]
    q_ref, k_ref, v_ref, gmn_ref = main[:4]
    for src_ref, dst_ref in zip(cast_in, cast_out):
        dst_ref[...] = src_ref[...].astype(BF16)

    h = _rms(x_ref[...], g_ref[...]).astype(BF16)

    def proj(lo, width):
        return jnp.dot(h, w_ref[:, lo:lo + width], preferred_element_type=F32)

    vgn = _rms(_gelu(proj(3 * ATTN_WIDTH + MLP_WIDTH, MLP_WIDTH)), sgu_ref[...])
    if emit_vgn:
        main[4][...] = vgn
    gu = _gelu(proj(3 * ATTN_WIDTH, MLP_WIDTH))

    lane_grp = lax.broadcasted_iota(jnp.int32, (CHUNK, MLP_WIDTH), 1) // MLP_CH
    mixed = []
    for c in range(n_chunks):
        vc = vgn[c * CHUNK:(c + 1) * CHUNK].astype(BF16)
        vbd = jnp.concatenate(
            [jnp.where(lane_grp == g, vc, jnp.zeros_like(vc)) for g in range(MLP_GROUPS)], axis=0)
        mixed.append(jnp.dot(wm_ref[...], vbd, preferred_element_type=F32) + bias_ref[...])

    q_ref[...] = proj(0, ATTN_WIDTH)
    for c in range(n_chunks):
        rows = slice(c * CHUNK, (c + 1) * CHUNK)
        gmn_ref[rows, :] = _rms(gu[rows] * mixed[c], og_ref[...]).astype(BF16)
    if kv_transposed:
        kv_t = lax.dot_general(wkv_t_ref[...], h, _NT, preferred_element_type=F32)
        k_ref[0] = kv_t[:ATTN_WIDTH]
        v_ref[0] = kv_t[ATTN_WIDTH:]
    else:
        k_ref[...] = proj(ATTN_WIDTH, ATTN_WIDTH)
        v_ref[...] = proj(2 * ATTN_WIDTH, ATTN_WIDTH)


def _inproj(x, g, w_bf, wkv_t_bf, sgu_g, wm_cat, bias_full, og_mlp, tm, seq_transposed=None,
            cast_along=(), emit_vgn=True):
    rows = x.shape[0]
    steps = rows // tm
    row_spec = lambda width: pl.BlockSpec((tm, width), lambda i: (i, 0))
    full = lambda a: pl.BlockSpec(a.shape, lambda i: (0,) * a.ndim)
    out = jax.ShapeDtypeStruct((rows, ATTN_WIDTH), F32)
    kv_spec, kv_out = row_spec(ATTN_WIDTH), out
    if seq_transposed is not None:
        tiles = seq_transposed // tm
        kv_spec = pl.BlockSpec((1, ATTN_WIDTH, tm), lambda i: (i // tiles, 0, i % tiles))
        kv_out = jax.ShapeDtypeStruct((rows // seq_transposed, ATTN_WIDTH, seq_transposed), F32)

    def slab_spec(a):
        per = steps // a.shape[0]
        assert per * a.shape[0] == steps and a.shape[1] % per == 0
        return pl.BlockSpec((1, a.shape[1] // per, a.shape[2]), lambda i: (i // per, i % per, 0))

    cast_specs = [slab_spec(a) for a in cast_along]
    vgn_spec, vgn_out = ([row_spec(MLP_WIDTH)], [out]) if emit_vgn else ([], [])
    return pl.pallas_call(
        functools.partial(_inproj_kernel, n_chunks=tm // CHUNK,
                          kv_transposed=seq_transposed is not None, n_cast=len(cast_along),
                          emit_vgn=emit_vgn),
        grid=(steps,),
        in_specs=[row_spec(D_MODEL), full(g), full(w_bf), full(wkv_t_bf), full(sgu_g), full(wm_cat),
                  full(bias_full), full(og_mlp)] + cast_specs,
        out_specs=[row_spec(ATTN_WIDTH), kv_spec, kv_spec, row_spec(MLP_WIDTH)] + vgn_spec
        + cast_specs,
        out_shape=[out, kv_out, kv_out, jax.ShapeDtypeStruct((rows, MLP_WIDTH), BF16)] + vgn_out
        + [jax.ShapeDtypeStruct(a.shape, BF16) for a in cast_along],
        compiler_params=pltpu.CompilerParams(
            dimension_semantics=("arbitrary",), vmem_limit_bytes=VMEM_LIMIT),
    )(x, g, w_bf, wkv_t_bf, sgu_g, wm_cat, bias_full, og_mlp, *cast_along)


V_ROWS = HEAD_DIM + BF16_ROWS
LOG2_E = 1.4426950408889634
MASKED = -1e30
Q_TILE = MOBA_BLOCK
COL_R, COL_J, COL_SEL = 0, 2, 8


def _bf16_split(x):
    mantissa, exponent = math.frexp(x)
    high = math.ldexp(round(mantissa * 256.0) / 256.0, exponent)
    return high, x - high


def _attn_prompt_kernel(q_ref, k_ref, v_ref, og_ref, o_ref, ka_scr, vt_scr, kmean_scr, qa_scr,
                        m_scr, alpha_scr, acc_scr, s_scr, *, n_blocks):
    cur = pl.program_id(1)

    @pl.when(cur == 0)
    def _():
        key = lax.broadcasted_iota(jnp.int32, (MOBA_BLOCK, HEAD_DIM), 0)
        col = lax.broadcasted_iota(jnp.int32, (MOBA_BLOCK, HEAD_DIM), 1)
        ones = jnp.ones((BF16_ROWS, MOBA_BLOCK), BF16)
        for n in range(n_blocks):
            keys = slice(n * MOBA_BLOCK, (n + 1) * MOBA_BLOCK)
            kb = k_ref[0, :, keys].T
            kmean_scr[n:n + 1, :] = jnp.sum(kb, axis=0, keepdims=True) * (1.0 / MOBA_BLOCK)
            vt = v_ref[0, :, keys]
            extra = jnp.where(col < COL_J, key,
                              jnp.where(col < COL_SEL, n, jnp.where(col == COL_SEL + n, 1, 0)))
            extra = extra.astype(F32).astype(BF16)
            for h in range(ATTN_HEADS):
                lanes = slice(h * HEAD_DIM, (h + 1) * HEAD_DIM)
                ka_scr[h, n] = jnp.concatenate([kb[:, lanes].astype(BF16), extra], axis=1)
                vt_scr[h, n, :HEAD_DIM, :] = vt[lanes, :].astype(BF16)
                vt_scr[h, n, HEAD_DIM:, :] = ones

    q_t = q_ref[...].T
    row = lax.broadcasted_iota(jnp.int32, (HEAD_DIM, Q_TILE), 0)
    blk = lax.broadcasted_iota(jnp.int32, (n_blocks, Q_TILE), 0)
    for h in range(ATTN_HEADS):
        lanes = slice(h * HEAD_DIM, (h + 1) * HEAD_DIM)
        hi, lo = _bf16_split(LOG2_E * 2.0 ** (-8.0 * (h + 1) / ATTN_HEADS))
        qh_t = q_t[lanes, :]
        gate = jnp.dot(kmean_scr[:, lanes], qh_t, precision=lax.Precision.HIGHEST,
                       preferred_element_type=F32)
        keep = _top_rank_select(gate, cur, MOBA_TOPK, 0) | (blk >= cur)
        sel_rows = jnp.where(keep, 0.0, MASKED)
        alibi = jnp.where(row == COL_R, hi,
                          jnp.where(row == COL_R + 1, lo,
                                    jnp.where(row == COL_J, hi * MOBA_BLOCK,
                                              jnp.where(row == COL_J + 1, lo * MOBA_BLOCK, 0.0))))
        extra = alibi + jnp.concatenate(
            [jnp.zeros((COL_SEL, Q_TILE), F32), sel_rows,
             jnp.zeros((HEAD_DIM - COL_SEL - n_blocks, Q_TILE), F32)], axis=0)
        qa_scr[h] = jnp.concatenate([qh_t * (LOG2_E * HEAD_DIM ** -0.5), extra], axis=0).astype(BF16)

    def attend_block(j, causal, first):
        for h in range(ATTN_HEADS):
            s = jnp.dot(ka_scr[h, j], qa_scr[h], preferred_element_type=F32)
            if causal is not None:
                s = jnp.where(causal, s, NEG_INF)
            s_scr[h] = s
            m_blk = jnp.max(s, axis=0, keepdims=True)
            if first:
                m_scr[h:h + 1, :] = m_blk
            else:
                m_old = m_scr[h:h + 1, :]
                m_new = jnp.maximum(m_old, m_blk)
                alpha_scr[h:h + 1, :] = jnp.exp2(m_old - m_new)
                m_scr[h:h + 1, :] = m_new
        for h in range(ATTN_HEADS):
            p = jnp.exp2(s_scr[h] - m_scr[h:h + 1, :]).astype(BF16)
            pv = jnp.dot(vt_scr[h, j], p, preferred_element_type=F32)
            acc_scr[h] = pv if first else alpha_scr[h:h + 1, :] * acc_scr[h] + pv

    causal = (lax.broadcasted_iota(jnp.int32, (MOBA_BLOCK, Q_TILE), 0)
              <= lax.broadcasted_iota(jnp.int32, (MOBA_BLOCK, Q_TILE), 1))
    attend_block(cur, causal, True)

    def past_block(j, carry):
        attend_block(j, None, False)
        return carry

    lax.fori_loop(0, cur, past_block, 0)

    outs = []
    for h in range(ATTN_HEADS):
        acc = acc_scr[h]
        outs.append(acc[:HEAD_DIM] / acc[HEAD_DIM:HEAD_DIM + 1])
    o_ref[...] = _rms(jnp.concatenate(outs, axis=0).T, og_ref[...]).astype(BF16)


def _attn_prompt(q, k_t, v_t, og_attn, batch, seq):
    n_blocks = seq // MOBA_BLOCK
    assert COL_SEL + n_blocks <= HEAD_DIM and n_blocks % 8 == 0
    q_spec = pl.BlockSpec((Q_TILE, ATTN_WIDTH), lambda b, i: (b * n_blocks + i, 0))
    return pl.pallas_call(
        functools.partial(_attn_prompt_kernel, n_blocks=n_blocks),
        grid=(batch, n_blocks),
        in_specs=[q_spec,
                  pl.BlockSpec((1, ATTN_WIDTH, seq), lambda b, i: (b, 0, 0)),
                  pl.BlockSpec((1, ATTN_WIDTH, seq), lambda b, i: (b, 0, 0)),
                  pl.BlockSpec(og_attn.shape, lambda b, i: (0, 0))],
        out_specs=q_spec,
        out_shape=jax.ShapeDtypeStruct(q.shape, BF16),
        scratch_shapes=[pltpu.VMEM((ATTN_HEADS, n_blocks, MOBA_BLOCK, 2 * HEAD_DIM), BF16),
                        pltpu.VMEM((ATTN_HEADS, n_blocks, V_ROWS, MOBA_BLOCK), BF16),
                        pltpu.VMEM((n_blocks, ATTN_WIDTH), F32),
                        pltpu.VMEM((ATTN_HEADS, 2 * HEAD_DIM, Q_TILE), BF16),
                        pltpu.VMEM((ATTN_HEADS, Q_TILE), F32),
                        pltpu.VMEM((ATTN_HEADS, Q_TILE), F32),
                        pltpu.VMEM((ATTN_HEADS, V_ROWS, Q_TILE), F32),
                        pltpu.VMEM((ATTN_HEADS, MOBA_BLOCK, Q_TILE), F32)],
        compiler_params=pltpu.CompilerParams(
            dimension_semantics=("arbitrary", "arbitrary"), vmem_limit_bytes=VMEM_LIMIT),
    )(q, k_t, v_t, og_attn)


SEQS_PER_STEP = 2


def _attn_sample_kernel(pt_ref, q_ref, kn_ref, vn_ref, og_ref, ck_hbm, cv_hbm, o_ref,
                        kbuf, vbuf, sem_k, sem_v, *, n_pages, page, t_new, seq_lo, n_steps):
    step = pl.program_id(0)
    slot = step % 2
    pages_per_step = SEQS_PER_STEP * n_pages

    def fetch(step_, slot_):
        first = (seq_lo + step_ * SEQS_PER_STEP) * n_pages
        for j in range(pages_per_step):
            page_id = pt_ref[first + j]
            pltpu.make_async_copy(ck_hbm.at[page_id], kbuf.at[slot_, j], sem_k.at[slot_]).start()
            pltpu.make_async_copy(cv_hbm.at[page_id], vbuf.at[slot_, j], sem_v.at[slot_]).start()

    @pl.when(step == 0)
    def _():
        fetch(0, 0)

    @pl.when(step + 1 < n_steps)
    def _():
        fetch(step + 1, 1 - slot)

    pltpu.make_async_copy(ck_hbm.at[pl.ds(0, pages_per_step)], kbuf.at[slot], sem_k.at[slot]).wait()
    pltpu.make_async_copy(cv_hbm.at[pl.ds(0, pages_per_step)], vbuf.at[slot], sem_v.at[slot]).wait()

    for i in range(SEQS_PER_STEP):
        kp = [kbuf.at[slot, pl.ds(i * n_pages + p, 1)] for p in range(n_pages)]
        vp = [vbuf.at[slot, pl.ds(i * n_pages + p, 1)] for p in range(n_pages)]
        rows = slice(i * t_new, (i + 1) * t_new)
        att = _attend_one_sample(q_ref[rows, :], kn_ref[rows, :], vn_ref[rows, :], kp, vp, page,
                                 t_new)
        o_ref[rows, :] = _rms(att, og_ref[...])


def _attend_one_sample(q, kn, vn, kp, vp, page, t_new):
    n_pages = len(kp)
    n_cols = t_new * ATTN_HEADS
    pages_per_block = MOBA_BLOCK // page
    n_past = n_pages // pages_per_block
    past_len = n_pages * page

    lane_h = lax.broadcasted_iota(jnp.int32, (n_cols, ATTN_WIDTH), 1) // HEAD_DIM
    row = lax.broadcasted_iota(jnp.int32, (n_cols, 1), 0)
    row_h = row % ATTN_HEADS
    row_t = row // ATTN_HEADS
    own_head = lane_h == row_h
    qrep = jnp.concatenate(
        [jnp.broadcast_to(q[t:t + 1, :], (ATTN_HEADS, ATTN_WIDTH)) for t in range(t_new)], axis=0)
    qbd = jnp.where(own_head, qrep, 0.0)
    qbd_s = (qbd * (HEAD_DIM ** -0.5)).astype(BF16)
    slope = jnp.exp2(-8.0 * (row_h + 1).astype(F32) / ATTN_HEADS)

    def block_pages(refs_, n):
        return [refs_[i][0].reshape(ATTN_WIDTH, page)
                for i in range(n * pages_per_block, (n + 1) * pages_per_block)]

    key_off = lax.broadcasted_iota(jnp.int32, (1, MOBA_BLOCK), 1)
    k_sums, s_past = [], []
    for n in range(n_past):
        pages = block_pages(kp, n)
        k_sums.append(jnp.sum(sum(pages), axis=1, keepdims=True))
        kb_t = jnp.concatenate(pages, axis=1).astype(BF16)
        dist = (past_len + row_t) - (n * MOBA_BLOCK + key_off)
        s_past.append(jnp.dot(qbd_s, kb_t, preferred_element_type=F32) - slope * dist.astype(F32))
    kmean = jnp.concatenate(k_sums, axis=1) * (1.0 / MOBA_BLOCK)
    gate = jnp.dot(qbd, kmean, precision=lax.Precision.HIGHEST, preferred_element_type=F32)
    sel = _top_rank_select(gate, n_past, MOBA_TOPK, 1)

    s_own = lax.dot_general(qbd_s, kn.astype(BF16), _NT, preferred_element_type=F32)
    dist = row_t - lax.broadcasted_iota(jnp.int32, (1, t_new), 1)
    s_own = jnp.where(dist >= 0, s_own - slope * dist.astype(F32), NEG_INF)
    m = jnp.max(s_own, axis=-1, keepdims=True)
    for n in range(n_past):
        s_past[n] = jnp.where(sel[:, n:n + 1], s_past[n], NEG_INF)
        m = jnp.maximum(m, jnp.max(s_past[n], axis=-1, keepdims=True))

    p = jnp.exp(s_own - m)
    l = jnp.sum(p, axis=-1, keepdims=True)
    acc = sum(p[:, t:t + 1] * vn[t:t + 1, :] for t in range(t_new))
    for n in range(n_past):
        vb_t = jnp.concatenate(block_pages(vp, n), axis=1).astype(BF16)
        p = jnp.exp(s_past[n] - m)
        l = l + jnp.sum(p, axis=-1, keepdims=True)
        acc = acc + lax.dot_general(p.astype(BF16), vb_t, _NT, preferred_element_type=F32)

    out = jnp.where(own_head, acc / l, 0.0)
    return jnp.sum(out.reshape(t_new, ATTN_HEADS, ATTN_WIDTH), axis=1)


def _attn_sample(q, kn, vn, og_attn, cache_kt, cache_vt, page_table, t_new, seq_lo, n_seq):
    n_pages = page_table.shape[1]
    page = cache_kt.shape[-1]
    assert n_seq % SEQS_PER_STEP == 0 and seq_lo % SEQS_PER_STEP == 0
    step_rows = SEQS_PER_STEP * t_new
    step_lo = seq_lo // SEQS_PER_STEP
    in_rows = pl.BlockSpec((step_rows, ATTN_WIDTH), lambda i, pt: (step_lo + i, 0))
    out_rows = pl.BlockSpec((step_rows, ATTN_WIDTH), lambda i, pt: (i, 0))

    n_steps = n_seq // SEQS_PER_STEP
    page_buf = pltpu.VMEM((2, SEQS_PER_STEP * n_pages, ATTN_HEADS, HEAD_DIM, page), F32)
    return pl.pallas_call(
        functools.partial(_attn_sample_kernel, n_pages=n_pages, page=page, t_new=t_new,
                          seq_lo=seq_lo, n_steps=n_steps),
        grid_spec=pltpu.PrefetchScalarGridSpec(
            num_scalar_prefetch=1,
            grid=(n_steps,),
            in_specs=([in_rows] * 3 + [pl.BlockSpec(og_attn.shape, lambda i, pt: (0, 0)),
                                       pl.BlockSpec(memory_space=pl.ANY),
                                       pl.BlockSpec(memory_space=pl.ANY)]),
            out_specs=out_rows,
            scratch_shapes=[page_buf, page_buf, pltpu.SemaphoreType.DMA((2,)),
                            pltpu.SemaphoreType.DMA((2,))]),
        out_shape=jax.ShapeDtypeStruct((n_seq * t_new, ATTN_WIDTH), F32),
        compiler_params=pltpu.CompilerParams(
            dimension_semantics=("arbitrary",), vmem_limit_bytes=VMEM_LIMIT),
    )(page_table.reshape(-1), q, kn, vn, og_attn, cache_kt, cache_vt)


def _outproj_kernel(att_ref, gmn_ref, x_ref, wo_ref, nfg_ref, wr_ref, br_ref, x1_ref, *route_refs,
                    grouped):
    mix = (jnp.dot(att_ref[...].astype(BF16), wo_ref[:ATTN_WIDTH, :], preferred_element_type=F32)
           + jnp.dot(gmn_ref[...], wo_ref[ATTN_WIDTH:, :], preferred_element_type=F32))
    x1 = x_ref[...] + mix
    x1_ref[...] = x1
    h2 = _rms(x1, nfg_ref[...])
    h2_hi = h2.astype(BF16)

    h2_lo = (h2 - h2_hi.astype(F32)).astype(BF16)
    hi_dot = jnp.dot(h2_hi, wr_ref[...], preferred_element_type=F32)
    lo_dot = jnp.dot(h2_lo, wr_ref[:, :LANES], preferred_element_type=F32)
    logits = hi_dot[:, :LANES] + hi_dot[:, LANES:] + lo_dot + br_ref[...]
    lt = logits.T
    tm = lt.shape[1]
    row4 = lax.broadcasted_iota(jnp.int32, (MOE_GROUPS, tm), 0)

    def first_argmax(v):
        vmax = jnp.max(v, axis=0, keepdims=True)
        idx = jnp.min(jnp.where(v == vmax, row4, MOE_GROUPS), axis=0, keepdims=True)
        return vmax, idx

    glog = lt[:MOE_GROUPS]
    ge = jnp.exp(glog - jnp.max(glog, axis=0, keepdims=True))
    gprob = ge / jnp.sum(ge, axis=0, keepdims=True)
    p_g, g_idx = first_argmax(gprob)
    elog = lt[MOE_GROUPS:MOE_GROUPS + EXPERTS_PER_GROUP]
    for g in range(1, MOE_GROUPS):
        lo = MOE_GROUPS + g * EXPERTS_PER_GROUP
        elog = jnp.where(g_idx == g, lt[lo:lo + EXPERTS_PER_GROUP], elog)
    l1, i1 = first_argmax(elog)
    l2, i2 = first_argmax(jnp.where(row4 == i1, NEG_INF, elog))
    e2 = jnp.exp(l2 - l1)
    denom = 1.0 + e2
    w1 = (1.0 / denom) * p_g
    w2 = (e2 / denom) * p_g
    if not grouped:
        h2_ref, comb_ref = route_refs
        h2_ref[...] = h2_hi
        lane_row = lax.broadcasted_iota(jnp.int32, (LANES, tm), 0)
        base = g_idx * EXPERTS_PER_GROUP
        comb_t = (jnp.where(lane_row == base + i1, w1, 0.0)
                  + jnp.where(lane_row == base + i2, w2, 0.0))
        comb_ref[...] = comb_t.T
        return

    bucket_ref, wpair_ref = route_refs
    e_lo = jnp.minimum(i1, i2)
    e_hi = jnp.maximum(i1, i2)
    pair = jnp.where(e_lo == 0, e_hi - 1, jnp.where(e_lo == 1, e_hi + 1, PAIRS_PER_GROUP - 1))
    row8 = lax.broadcasted_iota(jnp.int32, (8, tm), 0)
    bucket_ref[...] = jnp.where(row8 == 0, g_idx * PAIRS_PER_GROUP + pair, 0)
    w_lo = jnp.where(i1 < i2, w1, w2)
    w_hi = jnp.where(i1 < i2, w2, w1)
    wpair_ref[...] = jnp.where(row8 == 0, w_lo, jnp.where(row8 == 1, w_hi, 0.0))


def _outproj(att_n, gmn, x, wo_bf, nf_g, w_r, b_r, tm, grouped):
    rows = x.shape[0]
    row_spec = lambda width: pl.BlockSpec((tm, width), lambda i: (i, 0))
    full = lambda a: pl.BlockSpec(a.shape, lambda i: (0,) * a.ndim)
    lane_spec = pl.BlockSpec((8, tm), lambda i: (0, i))
    if grouped:
        route_specs = [lane_spec, lane_spec]
        route_shapes = [jax.ShapeDtypeStruct((8, rows), jnp.int32),
                        jax.ShapeDtypeStruct((8, rows), F32)]
    else:
        route_specs = [row_spec(D_MODEL), row_spec(LANES)]
        route_shapes = [jax.ShapeDtypeStruct((rows, D_MODEL), BF16),
                        jax.ShapeDtypeStruct((rows, LANES), F32)]
    return pl.pallas_call(
        functools.partial(_outproj_kernel, grouped=grouped),
        grid=(rows // tm,),
        in_specs=[row_spec(ATTN_WIDTH), row_spec(MLP_WIDTH), row_spec(D_MODEL), full(wo_bf),
                  full(nf_g), full(w_r), full(b_r)],
        out_specs=[row_spec(D_MODEL)] + route_specs,
        out_shape=[jax.ShapeDtypeStruct((rows, D_MODEL), F32)] + route_shapes,
        compiler_params=pltpu.CompilerParams(
            dimension_semantics=("arbitrary",), vmem_limit_bytes=VMEM_LIMIT),
    )(att_n, gmn, x, wo_bf, nf_g, w_r, b_r)


def _moe_kernel(h_ref, comb_ref, wg_ref, wu_ref, wd_ref, x1_ref, fg_ref, y_ref, acc_ref):
    e = pl.program_id(1)

    @pl.when(e == 0)
    def _():
        acc_ref[...] = jnp.zeros_like(acc_ref)

    h = h_ref[...]
    hg = jnp.dot(h, wg_ref[0], preferred_element_type=F32)
    hu = jnp.dot(h, wu_ref[0], preferred_element_type=F32)
    comb = comb_ref[...]
    lane = lax.broadcasted_iota(jnp.int32, comb.shape, 1)
    c = jnp.sum(jnp.where(lane == e, comb, 0.0), axis=-1, keepdims=True)
    act = hg * (1.0 / (1.0 + jnp.exp(-hg))) * hu * c
    acc_ref[...] += jnp.dot(act.astype(BF16), wd_ref[0], preferred_element_type=F32)

    @pl.when(e == N_EXPERTS - 1)
    def _():
        y_ref[...] = _rms(x1_ref[...] + acc_ref[...], fg_ref[...])


def _moe(h2, comb, wg_bf, wu_bf, wd_bf, x1, final_g, tm):
    rows = h2.shape[0]
    row_spec = lambda width: pl.BlockSpec((tm, width), lambda i, e: (i, 0))
    return pl.pallas_call(
        _moe_kernel,
        grid=(rows // tm, N_EXPERTS),
        in_specs=[row_spec(D_MODEL), row_spec(LANES),
                  pl.BlockSpec((1, D_MODEL, D_EXPERT), lambda i, e: (e, 0, 0)),
                  pl.BlockSpec((1, D_MODEL, D_EXPERT), lambda i, e: (e, 0, 0)),
                  pl.BlockSpec((1, D_EXPERT, D_MODEL), lambda i, e: (e, 0, 0)),
                  row_spec(D_MODEL),
                  pl.BlockSpec(final_g.shape, lambda i, e: (0, 0))],
        out_specs=row_spec(D_MODEL),
        out_shape=jax.ShapeDtypeStruct((rows, D_MODEL), F32),
        scratch_shapes=[pltpu.VMEM((tm, D_MODEL), F32)],
        compiler_params=pltpu.CompilerParams(
            dimension_semantics=("arbitrary", "arbitrary"), vmem_limit_bytes=VMEM_LIMIT),
    )(h2, comb, wg_bf, wu_bf, wd_bf, x1, final_g)


SC_CORES = 2
SC_SUBCORES = 16
SC_WINDOW = 32


def _sc_gather_rows(x, idx):
    n = idx.shape[0]
    width = x.shape[1]
    assert n % SC_WINDOW == 0
    mesh = plsc.VectorSubcoreMesh(core_axis_name="core", subcore_axis_name="subcore",
                                  num_cores=SC_CORES, num_subcores=SC_SUBCORES)

    @pl.kernel(out_type=jax.ShapeDtypeStruct((n, width), x.dtype), mesh=mesh)
    def gather_kernel(x_hbm, idx_hbm, out_hbm):
        def body(idx_vmem, out_vmem):
            pltpu.sync_copy(x_hbm.at[idx_vmem.at[0, pl.ds(0, SC_WINDOW)]], out_vmem)

        pltpu.emit_pipeline(
            body,
            grid=(n // SC_WINDOW,),
            in_specs=[pl.BlockSpec((1, LANES), lambda i: (i, 0))],
            out_specs=[pl.BlockSpec((SC_WINDOW, width), lambda i: (i, 0))],
            core_axis_name=("core", "subcore"),
            dimension_semantics=(pltpu.PARALLEL,),
        )(idx_hbm, out_hbm)

    idx_rows = jnp.pad(idx.reshape(n // SC_WINDOW, SC_WINDOW), ((0, 0), (0, LANES - SC_WINDOW)))
    return gather_kernel(x, idx_rows)


class MoePlan(NamedTuple):
    slot_row: jax.Array
    token_slot: jax.Array
    e_lo: jax.Array
    e_hi: jax.Array
    n_valid: jax.Array
    tile_block: jax.Array
    weight_plan: jax.Array
    slot_cols: jax.Array


def _moe_plan(bucket, wpair, tm):
    rows = bucket.shape[0]
    n_tiles_max = rows // tm + N_BUCKETS
    i32 = jnp.int32
    b_ids = jnp.arange(N_BUCKETS, dtype=i32)[:, None]

    def lookup(table, keys):
        return jnp.sum(jnp.where(keys[None, :] == b_ids, table[:, None], 0), axis=0)

    _, order, wlo_sorted, whi_sorted = lax.sort(
        (bucket, jnp.arange(rows, dtype=i32), wpair[0], wpair[1]), num_keys=1, is_stable=True)
    position = jnp.argsort(order).astype(i32)
    counts = jnp.sum((bucket[None, :] == b_ids).astype(i32), axis=1)
    starts = jnp.cumsum(counts) - counts
    tiles_b = (counts + tm - 1) // tm
    tile_end = jnp.cumsum(tiles_b)
    tile_start = tile_end - tiles_b
    n_tiles = tile_end[-1]
    token_slot = position + lookup(tile_start * tm - starts, bucket)
    t = jnp.arange(n_tiles_max, dtype=i32)
    tb = jnp.sum((jnp.minimum(t, n_tiles - 1)[None, :] >= tile_end[:, None]).astype(i32), axis=0)
    local = (t - lookup(tile_start, tb)) * tm
    n_valid = jnp.where(t < n_tiles, jnp.clip(lookup(counts, tb) - local, 0, tm), 0)
    slot = local[:, None] + jnp.arange(tm, dtype=i32)[None, :]
    src = ((lookup(starts, tb)[:, None] + slot) % rows).reshape(-1)
    assert rows < 2 ** 24
    sorted_cols = jnp.stack([order.astype(F32), wlo_sorted, whi_sorted], axis=1)
    slot_cols = sorted_cols[src]
    table_pad = (0,) * (N_BUCKETS - PAIRS_PER_GROUP)

    def bucket_experts(b):
        group, pair = b // PAIRS_PER_GROUP, b % PAIRS_PER_GROUP
        return (group * EXPERTS_PER_GROUP + lookup(jnp.asarray(PAIR_LO + table_pad, i32), pair),
                group * EXPERTS_PER_GROUP + lookup(jnp.asarray(PAIR_HI + table_pad, i32), pair))

    e_lo, e_hi = bucket_experts(tb)
    used = counts > 0
    b_row = jnp.arange(N_BUCKETS, dtype=i32)
    later_used = (b_row[None, :] > b_row[:, None]) & used[None, :]
    next_used = jnp.min(jnp.where(later_used, b_row[None, :], N_BUCKETS), axis=1)
    w_slot = lookup(jnp.cumsum(used.astype(i32)) - 1, tb) % 2
    first = (t == lookup(tile_start, tb)) & (t < n_tiles)
    next_t = lookup(next_used, tb)
    next_lo, next_hi = bucket_experts(jnp.minimum(next_t, N_BUCKETS - 1))
    weight_plan = jnp.concatenate([w_slot, first.astype(i32), (first & (next_t < N_BUCKETS)).astype(i32),
                                   next_lo, next_hi]).astype(i32)
    return MoePlan(slot_cols[:, 0].astype(i32), token_slot.astype(i32), e_lo.astype(i32),
                   e_hi.astype(i32), n_valid.astype(i32), jnp.minimum(t, n_tiles - 1).astype(i32),
                   weight_plan, slot_cols)


def _moe_grouped_kernel(elo_ref, ehi_ref, nv_ref, blk_ref, wp_ref, x1_ref, cols_ref, wg_hbm, wu_hbm,
                        wd_hbm, nfg_ref, fg_ref, y_ref, wg_buf, wu_buf, wd_buf, sem, *, n_steps):
    del blk_ref
    t = pl.program_id(0)
    slot, first, fetch_next, next_lo, next_hi = (wp_ref[k * n_steps + t] for k in range(5))

    def fetch(e_lo, e_hi, s):
        for src, buf in ((wg_hbm, wg_buf), (wu_hbm, wu_buf), (wd_hbm, wd_buf)):
            pltpu.make_async_copy(src.at[e_lo], buf.at[s, 0], sem.at[s]).start()
            pltpu.make_async_copy(src.at[e_hi], buf.at[s, 1], sem.at[s]).start()

    @pl.when(t == 0)
    def _():
        fetch(elo_ref[0], ehi_ref[0], slot)

    @pl.when(first == 1)
    def _():
        for src, buf in ((wg_hbm, wg_buf), (wu_hbm, wu_buf), (wd_hbm, wd_buf)):
            pltpu.make_async_copy(src.at[pl.ds(0, 2)], buf.at[slot], sem.at[slot]).wait()

        @pl.when(fetch_next == 1)
        def _():
            fetch(next_lo, next_hi, 1 - slot)

    @pl.when(nv_ref[t] > 0)
    def _():
        x1 = x1_ref[...]
        h = _rms(x1, nfg_ref[...]).astype(BF16)
        slot_cols = cols_ref[...]
        moe = jnp.zeros(x1.shape, F32)
        for e in range(2):
            hg = jnp.dot(h, wg_buf[slot, e], preferred_element_type=F32)
            hu = jnp.dot(h, wu_buf[slot, e], preferred_element_type=F32)
            act = hg * (1.0 / (1.0 + jnp.exp(-hg))) * hu * slot_cols[:, e + 1:e + 2]
            moe = moe + jnp.dot(act.astype(BF16), wd_buf[slot, e], preferred_element_type=F32)
        y_ref[...] = _rms(x1 + moe, fg_ref[...])


def _moe_grouped(x1_slots, plan, wg_bf, wu_bf, wd_bf, nf_g, final_g, tm):
    n_steps = plan.n_valid.shape[0]
    const = lambda a: pl.BlockSpec(a.shape, lambda t, elo, ehi, nv, blk, wp: (0, 0))
    row_spec = lambda width: pl.BlockSpec((tm, width), lambda t, elo, ehi, nv, blk, wp: (blk[t], 0))
    hbm = pl.BlockSpec(memory_space=pl.ANY)
    pair_buf = lambda w: pltpu.VMEM((2, 2) + w.shape[1:], w.dtype)
    return pl.pallas_call(
        functools.partial(_moe_grouped_kernel, n_steps=n_steps),
        grid_spec=pltpu.PrefetchScalarGridSpec(
            num_scalar_prefetch=5,
            grid=(n_steps,),
            in_specs=[row_spec(D_MODEL), row_spec(plan.slot_cols.shape[1]), hbm, hbm, hbm,
                      const(nf_g), const(final_g)],
            out_specs=row_spec(D_MODEL),
            scratch_shapes=[pair_buf(wg_bf), pair_buf(wu_bf), pair_buf(wd_bf),
                            pltpu.SemaphoreType.DMA((2,))]),
        out_shape=jax.ShapeDtypeStruct(x1_slots.shape, F32),
        compiler_params=pltpu.CompilerParams(
            dimension_semantics=("arbitrary",), vmem_limit_bytes=VMEM_LIMIT),
    )(plan.e_lo, plan.e_hi, plan.n_valid, plan.tile_block, plan.weight_plan, x1_slots,
      plan.slot_cols, wg_bf, wu_bf, wd_bf, nf_g, final_g)


def _spatial_operands(w_s, b_s, t_chunk):
    reps = CHUNK // t_chunk
    idx = jnp.arange(CHUNK)
    same = (idx[:, None] // t_chunk) == (idx[None, :] // t_chunk)
    causal = (idx[None, :] % t_chunk) <= (idx[:, None] % t_chunk)
    wm = jnp.tile(w_s[:, :t_chunk, :t_chunk], (1, reps, reps)) * (same & causal)
    wm_cat = wm.transpose(1, 0, 2).reshape(CHUNK, MLP_GROUPS * CHUNK).astype(BF16)
    bias = jnp.repeat(jnp.tile(b_s[:, :t_chunk], (1, reps)).T, MLP_CH, axis=1)
    return wm_cat, bias


def kernel(x_prompt, x_sample, cache_k, cache_v, page_table, norm_attn_g, w_in, sgu_g, w_spatial,
           b_spatial, out_g_attn, out_g_mlp, w_out, norm_ffn_g, w_group, b_group, w_router, b_router,
           w_gate, w_up, w_down, final_g):
    depth = w_in.shape[0]
    assert depth == 1, "single decoder layer"
    batch, seq, _ = x_prompt.shape
    n_seq, t_new, _ = x_sample.shape
    assert seq % MOBA_BLOCK == 0 and seq % CHUNK == 0 and CHUNK % t_new == 0
    assert MOBA_BLOCK % cache_k.shape[2] == 0

    row2 = lambda a: a.reshape(1, -1)
    w_in_bf = w_in[0].astype(BF16)
    w_out_bf = w_out[0].astype(BF16)
    n_logits = MOE_GROUPS + N_EXPERTS
    w_r = jnp.concatenate(
        [w_group[0], w_router[0].transpose(1, 0, 2).reshape(D_MODEL, N_EXPERTS),
         jnp.zeros((D_MODEL, LANES - n_logits), F32)], axis=1)
    b_r = jnp.concatenate(
        [b_group[0], b_router[0].reshape(-1), jnp.zeros((LANES - n_logits,), F32)]).reshape(1, LANES)
    w_r_hi = w_r.astype(BF16)
    w_r_hl = jnp.concatenate([w_r_hi, (w_r - w_r_hi.astype(F32)).astype(BF16)], axis=1)
    wkv_t_bf = w_in[0][:, ATTN_WIDTH:3 * ATTN_WIDTH].T.astype(BF16)
    ck_t = jnp.transpose(cache_k[0], (0, 2, 3, 1))
    cv_t = jnp.transpose(cache_v[0], (0, 2, 3, 1))

    def project_in(x, t_chunk, tm, seq_transposed, cast_along=(), emit_vgn=True):
        wm_cat, bias_full = _spatial_operands(w_spatial[0], b_spatial[0], t_chunk)
        return _inproj(x, row2(norm_attn_g[0]), w_in_bf, wkv_t_bf, row2(sgu_g[0]), wm_cat, bias_full,
                       row2(out_g_mlp[0]), tm, seq_transposed, cast_along, emit_vgn)

    def project_out(att_n, gmn, x, tm, grouped):
        return _outproj(att_n, gmn, x, w_out_bf, row2(norm_ffn_g[0]), w_r_hl, b_r, tm, grouped)

    og_attn = row2(out_g_attn[0])

    xp = x_prompt.reshape(batch * seq, D_MODEL)
    qp, kp_t, vp_t, gmn_p, wg_bf, wu_bf, wd_bf = project_in(
        xp, CHUNK, TM_PROJ_PROMPT, seq, (w_gate[0], w_up[0], w_down[0]), emit_vgn=False)
    att_p = _attn_prompt(qp, kp_t, vp_t, og_attn, batch, seq)
    x1_p, bucket_p, wpair_p = project_out(att_p, gmn_p, xp, TM_PROJ_PROMPT, True)
    plan = _moe_plan(bucket_p[0], wpair_p[:2], TM_MOE_PROMPT)
    x1_slots = _sc_gather_rows(x1_p, plan.slot_row)

    xs = x_sample.reshape(n_seq * t_new, D_MODEL)
    qs, ks, vs, gmn_s, gvs = project_in(xs, t_new, TM_PROJ_SAMPLE, None)
    half = n_seq // 2
    attend_half = lambda q_all, lo: _attn_sample(q_all, ks, vs, og_attn, ck_t, cv_t, page_table,
                                                 t_new, lo, half)
    att_s0 = attend_half(qs, 0)
    x1_slots, att_s0 = lax.optimization_barrier((x1_slots, att_s0))
    y_slots = _moe_grouped(x1_slots, plan, wg_bf, wu_bf, wd_bf, row2(norm_ffn_g[0]),
                           row2(final_g), TM_MOE_PROMPT)
    y_slots, qs_late = lax.optimization_barrier((y_slots, qs))
    yp = _sc_gather_rows(y_slots, plan.token_slot)
    att_s1 = attend_half(qs_late, half)
    att_s = jnp.concatenate([att_s0, att_s1], axis=0)
    x1_s, h2_s, comb_s = project_out(att_s, gmn_s, xs, TM_PROJ_SAMPLE, False)
    ys = _moe(h2_s, comb_s, wg_bf, wu_bf, wd_bf, x1_s, row2(final_g), TM_MOE_SAMPLE)

    heads = (ATTN_HEADS, HEAD_DIM)
    rows_last = lambda a_t: a_t.reshape(batch, *heads, seq).transpose(0, 3, 1, 2)[None]
    return (yp.reshape(batch, seq, D_MODEL),
            ys.reshape(n_seq, t_new, D_MODEL),
            rows_last(kp_t),
            rows_last(vp_t),
            ks.reshape(depth, n_seq, t_new, *heads),
            vs.reshape(depth, n_seq, t_new, *heads),
            gvs.reshape(depth, n_seq, t_new, MLP_WIDTH))
```

```python
import functools
import math
from typing import NamedTuple

import jax
import jax.numpy as jnp
from jax import lax
from jax.experimental import pallas as pl
from jax.experimental.pallas import tpu as pltpu
from jax.experimental.pallas import tpu_sc as plsc

D_MODEL = 1024
ATTN_HEADS = 8
HEAD_DIM = 64
ATTN_WIDTH = ATTN_HEADS * HEAD_DIM
MOBA_BLOCK = 256
MOBA_TOPK = 3
MLP_GROUPS = 8
MLP_CH = 64
MLP_WIDTH = MLP_GROUPS * MLP_CH
CHUNK = 128
MOE_GROUPS = 4
EXPERTS_PER_GROUP = 4
N_EXPERTS = MOE_GROUPS * EXPERTS_PER_GROUP
D_EXPERT = D_MODEL // 2
EPS = 1e-6
PAIR_LO = (0, 0, 0, 1, 1, 2)
PAIR_HI = (1, 2, 3, 2, 3, 3)
PAIRS_PER_GROUP = len(PAIR_LO)
N_BUCKETS = MOE_GROUPS * PAIRS_PER_GROUP

LANES = 128
BF16_ROWS = 16
VMEM_LIMIT = 56 * 1024 * 1024

TM_PROJ_PROMPT = 512
TM_PROJ_SAMPLE = 128
TM_MOE_PROMPT = 256
TM_MOE_SAMPLE = 512

F32 = jnp.float32
BF16 = jnp.bfloat16
NEG_INF = float("-inf")
_NT = (((1,), (1,)), ((), ()))


def _rms(x, g):
    return x * lax.rsqrt(jnp.mean(x * x, axis=-1, keepdims=True) + EPS) * g


def _gelu(x):
    return 0.5 * x * (1.0 + jnp.tanh(0.7978845608028654 * (x + 0.044715 * (x * x * x))))


def _top_rank_select(gate, n_past, n_keep, axis):
    nb = gate.shape[axis]
    n_idx = lax.broadcasted_iota(jnp.int32, gate.shape, axis)
    rank = jnp.zeros(gate.shape, jnp.int32)
    for m in range(nb):
        gm = gate[:, m:m + 1] if axis == 1 else gate[m:m + 1, :]
        beats = jnp.where(gm > gate, 1, jnp.where(gm == gate, jnp.where(m < n_idx, 1, 0), 0))
        rank = rank + jnp.where(m < n_past, beats, 0)
    return jnp.where(n_idx < n_past, rank, n_keep) < n_keep


def _inproj_kernel(x_ref, g_ref, w_ref, wkv_t_ref, sgu_ref, wm_ref, bias_ref, og_ref, *refs,
                   n_chunks, kv_transposed, n_cast, emit_vgn):
    n_main = 5 if emit_vgn else 4
    cast_in, main, cast_out = refs[:n_cast], refs[n_cast:n_cast + n_main], refs[n_cast + n_main:]
    q_ref, k_ref, v_ref, gmn_ref = main[:4]
    for src_ref, dst_ref in zip(cast_in, cast_out):
        dst_ref[...] = src_ref[...].astype(BF16)

    h = _rms(x_ref[...], g_ref[...]).astype(BF16)

    def proj(lo, width):
        return jnp.dot(h, w_ref[:, lo:lo + width], preferred_element_type=F32)

    vgn = _rms(_gelu(proj(3 * ATTN_WIDTH + MLP_WIDTH, MLP_WIDTH)), sgu_ref[...])
    if emit_vgn:
        main[4][...] = vgn
    gu = _gelu(proj(3 * ATTN_WIDTH, MLP_WIDTH))

    lane_grp = lax.broadcasted_iota(jnp.int32, (CHUNK, MLP_WIDTH), 1) // MLP_CH
    mixed = []
    for c in range(n_chunks):
        vc = vgn[c * CHUNK:(c + 1) * CHUNK].astype(BF16)
        vbd = jnp.concatenate(
            [jnp.where(lane_grp == g, vc, jnp.zeros_like(vc)) for g in range(MLP_GROUPS)], axis=0)
        mixed.append(jnp.dot(wm_ref[...], vbd, preferred_element_type=F32) + bias_ref[...])

    q_ref[...] = proj(0, ATTN_WIDTH)
    for c in range(n_chunks):
        rows = slice(c * CHUNK, (c + 1) * CHUNK)
        gmn_ref[rows, :] = _rms(gu[rows] * mixed[c], og_ref[...]).astype(BF16)
    if kv_transposed:
        kv_t = lax.dot_general(wkv_t_ref[...], h, _NT, preferred_element_type=F32)
        k_ref[0] = kv_t[:ATTN_WIDTH]
        v_ref[0] = kv_t[ATTN_WIDTH:]
    else:
        k_ref[...] = proj(ATTN_WIDTH, ATTN_WIDTH)
        v_ref[...] = proj(2 * ATTN_WIDTH, ATTN_WIDTH)


def _inproj(x, g, w_bf, wkv_t_bf, sgu_g, wm_cat, bias_full, og_mlp, tm, seq_transposed=None,
            cast_along=(), emit_vgn=True):
    rows = x.shape[0]
    steps = rows // tm
    row_spec = lambda width: pl.BlockSpec((tm, width), lambda i: (i, 0))
    full = lambda a: pl.BlockSpec(a.shape, lambda i: (0,) * a.ndim)
    out = jax.ShapeDtypeStruct((rows, ATTN_WIDTH), F32)
    kv_spec, kv_out = row_spec(ATTN_WIDTH), out
    if seq_transposed is not None:
        tiles = seq_transposed // tm
        kv_spec = pl.BlockSpec((1, ATTN_WIDTH, tm), lambda i: (i // tiles, 0, i % tiles))
        kv_out = jax.ShapeDtypeStruct((rows // seq_transposed, ATTN_WIDTH, seq_transposed), F32)

    def slab_spec(a):
        per = steps // a.shape[0]
        assert per * a.shape[0] == steps and a.shape[1] % per == 0
        return pl.BlockSpec((1, a.shape[1] // per, a.shape[2]), lambda i: (i // per, i % per, 0))

    cast_specs = [slab_spec(a) for a in cast_along]
    vgn_spec, vgn_out = ([row_spec(MLP_WIDTH)], [out]) if emit_vgn else ([], [])
    return pl.pallas_call(
        functools.partial(_inproj_kernel, n_chunks=tm // CHUNK,
                          kv_transposed=seq_transposed is not None, n_cast=len(cast_along),
                          emit_vgn=emit_vgn),
        grid=(steps,),
        in_specs=[row_spec(D_MODEL), full(g), full(w_bf), full(wkv_t_bf), full(sgu_g), full(wm_cat),
                  full(bias_full), full(og_mlp)] + cast_specs,
        out_specs=[row_spec(ATTN_WIDTH), kv_spec, kv_spec, row_spec(MLP_WIDTH)] + vgn_spec
        + cast_specs,
        out_shape=[out, kv_out, kv_out, jax.ShapeDtypeStruct((rows, MLP_WIDTH), BF16)] + vgn_out
        + [jax.ShapeDtypeStruct(a.shape, BF16) for a in cast_along],
        compiler_params=pltpu.CompilerParams(
            dimension_semantics=("arbitrary",), vmem_limit_bytes=VMEM_LIMIT),
    )(x, g, w_bf, wkv_t_bf, sgu_g, wm_cat, bias_full, og_mlp, *cast_along)


V_ROWS = HEAD_DIM + BF16_ROWS
LOG2_E = 1.4426950408889634
MASKED = -1e30
Q_TILE = MOBA_BLOCK
COL_R, COL_J, COL_SEL = 0, 2, 8


def _bf16_split(x):
    mantissa, exponent = math.frexp(x)
    high = math.ldexp(round(mantissa * 256.0) / 256.0, exponent)
    return high, x - high


def _attn_prompt_kernel(q_ref, k_ref, v_ref, og_ref, o_ref, ka_scr, vt_scr, kmean_scr, qa_scr,
                        m_scr, alpha_scr, acc_scr, s_scr, *, n_blocks):
    cur = pl.program_id(1)

    @pl.when(cur == 0)
    def _():
        key = lax.broadcasted_iota(jnp.int32, (MOBA_BLOCK, HEAD_DIM), 0)
        col = lax.broadcasted_iota(jnp.int32, (MOBA_BLOCK, HEAD_DIM), 1)
        ones = jnp.ones((BF16_ROWS, MOBA_BLOCK), BF16)
        for n in range(n_blocks):
            keys = slice(n * MOBA_BLOCK, (n + 1) * MOBA_BLOCK)
            kb = k_ref[0, :, keys].T
            kmean_scr[n:n + 1, :] = jnp.sum(kb, axis=0, keepdims=True) * (1.0 / MOBA_BLOCK)
            vt = v_ref[0, :, keys]
            extra = jnp.where(col < COL_J, key,
                              jnp.where(col < COL_SEL, n, jnp.where(col == COL_SEL + n, 1, 0)))
            extra = extra.astype(F32).astype(BF16)
            for h in range(ATTN_HEADS):
                lanes = slice(h * HEAD_DIM, (h + 1) * HEAD_DIM)
                ka_scr[h, n] = jnp.concatenate([kb[:, lanes].astype(BF16), extra], axis=1)
                vt_scr[h, n, :HEAD_DIM, :] = vt[lanes, :].astype(BF16)
                vt_scr[h, n, HEAD_DIM:, :] = ones

    q_t = q_ref[...].T
    row = lax.broadcasted_iota(jnp.int32, (HEAD_DIM, Q_TILE), 0)
    blk = lax.broadcasted_iota(jnp.int32, (n_blocks, Q_TILE), 0)
    for h in range(ATTN_HEADS):
        lanes = slice(h * HEAD_DIM, (h + 1) * HEAD_DIM)
        hi, lo = _bf16_split(LOG2_E * 2.0 ** (-8.0 * (h + 1) / ATTN_HEADS))
        qh_t = q_t[lanes, :]
        gate = jnp.dot(kmean_scr[:, lanes], qh_t, precision=lax.Precision.HIGHEST,
                       preferred_element_type=F32)
        keep = _top_rank_select(gate, cur, MOBA_TOPK, 0) | (blk >= cur)
        sel_rows = jnp.where(keep, 0.0, MASKED)
        alibi = jnp.where(row == COL_R, hi,
                          jnp.where(row == COL_R + 1, lo,
                                    jnp.where(row == COL_J, hi * MOBA_BLOCK,
                                              jnp.where(row == COL_J + 1, lo * MOBA_BLOCK, 0.0))))
        extra = alibi + jnp.concatenate(
            [jnp.zeros((COL_SEL, Q_TILE), F32), sel_rows,
             jnp.zeros((HEAD_DIM - COL_SEL - n_blocks, Q_TILE), F32)], axis=0)
        qa_scr[h] = jnp.concatenate([qh_t * (LOG2_E * HEAD_DIM ** -0.5), extra], axis=0).astype(BF16)

    def attend_block(j, causal, first):
        for h in range(ATTN_HEADS):
            s = jnp.dot(ka_scr[h, j], qa_scr[h], preferred_element_type=F32)
            if causal is not None:
                s = jnp.where(causal, s, NEG_INF)
            s_scr[h] = s
            m_blk = jnp.max(s, axis=0, keepdims=True)
            if first:
                m_scr[h:h + 1, :] = m_blk
            else:
                m_old = m_scr[h:h + 1, :]
                m_new = jnp.maximum(m_old, m_blk)
                alpha_scr[h:h + 1, :] = jnp.exp2(m_old - m_new)
                m_scr[h:h + 1, :] = m_new
        for h in range(ATTN_HEADS):
            p = jnp.exp2(s_scr[h] - m_scr[h:h + 1, :]).astype(BF16)
            pv = jnp.dot(vt_scr[h, j], p, preferred_element_type=F32)
            acc_scr[h] = pv if first else alpha_scr[h:h + 1, :] * acc_scr[h] + pv

    causal = (lax.broadcasted_iota(jnp.int32, (MOBA_BLOCK, Q_TILE), 0)
              <= lax.broadcasted_iota(jnp.int32, (MOBA_BLOCK, Q_TILE), 1))
    attend_block(cur, causal, True)

    def past_block(j, carry):
        attend_block(j, None, False)
        return carry

    lax.fori_loop(0, cur, past_block, 0)

    outs = []
    for h in range(ATTN_HEADS):
        acc = acc_scr[h]
        outs.append(acc[:HEAD_DIM] / acc[HEAD_DIM:HEAD_DIM + 1])
    o_ref[...] = _rms(jnp.concatenate(outs, axis=0).T, og_ref[...]).astype(BF16)


def _attn_prompt(q, k_t, v_t, og_attn, batch, seq):
    n_blocks = seq // MOBA_BLOCK
    assert COL_SEL + n_blocks <= HEAD_DIM and n_blocks % 8 == 0
    q_spec = pl.BlockSpec((Q_TILE, ATTN_WIDTH), lambda b, i: (b * n_blocks + i, 0))
    return pl.pallas_call(
        functools.partial(_attn_prompt_kernel, n_blocks=n_blocks),
        grid=(batch, n_blocks),
        in_specs=[q_spec,
                  pl.BlockSpec((1, ATTN_WIDTH, seq), lambda b, i: (b, 0, 0)),
                  pl.BlockSpec((1, ATTN_WIDTH, seq), lambda b, i: (b, 0, 0)),
                  pl.BlockSpec(og_attn.shape, lambda b, i: (0, 0))],
        out_specs=q_spec,
        out_shape=jax.ShapeDtypeStruct(q.shape, BF16),
        scratch_shapes=[pltpu.VMEM((ATTN_HEADS, n_blocks, MOBA_BLOCK, 2 * HEAD_DIM), BF16),
                        pltpu.VMEM((ATTN_HEADS, n_blocks, V_ROWS, MOBA_BLOCK), BF16),
                        pltpu.VMEM((n_blocks, ATTN_WIDTH), F32),
                        pltpu.VMEM((ATTN_HEADS, 2 * HEAD_DIM, Q_TILE), BF16),
                        pltpu.VMEM((ATTN_HEADS, Q_TILE), F32),
                        pltpu.VMEM((ATTN_HEADS, Q_TILE), F32),
                        pltpu.VMEM((ATTN_HEADS, V_ROWS, Q_TILE), F32),
                        pltpu.VMEM((ATTN_HEADS, MOBA_BLOCK, Q_TILE), F32)],
        compiler_params=pltpu.CompilerParams(
            dimension_semantics=("arbitrary", "arbitrary"), vmem_limit_bytes=VMEM_LIMIT),
    )(q, k_t, v_t, og_attn)


SEQS_PER_STEP = 2


def _attn_sample_kernel(pt_ref, q_ref, kn_ref, vn_ref, og_ref, ck_hbm, cv_hbm, o_ref,
                        kbuf, vbuf, sem_k, sem_v, *, n_pages, page, t_new, seq_lo, n_steps):
    step = pl.program_id(0)
    slot = step % 2
    pages_per_step = SEQS_PER_STEP * n_pages

    def fetch(step_, slot_):
        first = (seq_lo + step_ * SEQS_PER_STEP) * n_pages
        for j in range(pages_per_step):
            page_id = pt_ref[first + j]
            pltpu.make_async_copy(ck_hbm.at[page_id], kbuf.at[slot_, j], sem_k.at[slot_]).start()
            pltpu.make_async_copy(cv_hbm.at[page_id], vbuf.at[slot_, j], sem_v.at[slot_]).start()

    @pl.when(step == 0)
    def _():
        fetch(0, 0)

    @pl.when(step + 1 < n_steps)
    def _():
        fetch(step + 1, 1 - slot)

    pltpu.make_async_copy(ck_hbm.at[pl.ds(0, pages_per_step)], kbuf.at[slot], sem_k.at[slot]).wait()
    pltpu.make_async_copy(cv_hbm.at[pl.ds(0, pages_per_step)], vbuf.at[slot], sem_v.at[slot]).wait()

    for i in range(SEQS_PER_STEP):
        kp = [kbuf.at[slot, pl.ds(i * n_pages + p, 1)] for p in range(n_pages)]
        vp = [vbuf.at[slot, pl.ds(i * n_pages + p, 1)] for p in range(n_pages)]
        rows = slice(i * t_new, (i + 1) * t_new)
        att = _attend_one_sample(q_ref[rows, :], kn_ref[rows, :], vn_ref[rows, :], kp, vp, page,
                                 t_new)
        o_ref[rows, :] = _rms(att, og_ref[...])


def _attend_one_sample(q, kn, vn, kp, vp, page, t_new):
    n_pages = len(kp)
    n_cols = t_new * ATTN_HEADS
    pages_per_block = MOBA_BLOCK // page
    n_past = n_pages // pages_per_block
    past_len = n_pages * page

    lane_h = lax.broadcasted_iota(jnp.int32, (n_cols, ATTN_WIDTH), 1) // HEAD_DIM
    row = lax.broadcasted_iota(jnp.int32, (n_cols, 1), 0)
    row_h = row % ATTN_HEADS
    row_t = row // ATTN_HEADS
    own_head = lane_h == row_h
    qrep = jnp.concatenate(
        [jnp.broadcast_to(q[t:t + 1, :], (ATTN_HEADS, ATTN_WIDTH)) for t in range(t_new)], axis=0)
    qbd = jnp.where(own_head, qrep, 0.0)
    qbd_s = (qbd * (HEAD_DIM ** -0.5)).astype(BF16)
    slope = jnp.exp2(-8.0 * (row_h + 1).astype(F32) / ATTN_HEADS)

    def block_pages(refs_, n):
        return [refs_[i][0].reshape(ATTN_WIDTH, page)
                for i in range(n * pages_per_block, (n + 1) * pages_per_block)]

    key_off = lax.broadcasted_iota(jnp.int32, (1, MOBA_BLOCK), 1)
    k_sums, s_past = [], []
    for n in range(n_past):
        pages = block_pages(kp, n)
        k_sums.append(jnp.sum(sum(pages), axis=1, keepdims=True))
        kb_t = jnp.concatenate(pages, axis=1).astype(BF16)
        dist = (past_len + row_t) - (n * MOBA_BLOCK + key_off)
        s_past.append(jnp.dot(qbd_s, kb_t, preferred_element_type=F32) - slope * dist.astype(F32))
    kmean = jnp.concatenate(k_sums, axis=1) * (1.0 / MOBA_BLOCK)
    gate = jnp.dot(qbd, kmean, precision=lax.Precision.HIGHEST, preferred_element_type=F32)
    sel = _top_rank_select(gate, n_past, MOBA_TOPK, 1)

    s_own = lax.dot_general(qbd_s, kn.astype(BF16), _NT, preferred_element_type=F32)
    dist = row_t - lax.broadcasted_iota(jnp.int32, (1, t_new), 1)
    s_own = jnp.where(dist >= 0, s_own - slope * dist.astype(F32), NEG_INF)
    m = jnp.max(s_own, axis=-1, keepdims=True)
    for n in range(n_past):
        s_past[n] = jnp.where(sel[:, n:n + 1], s_past[n], NEG_INF)
        m = jnp.maximum(m, jnp.max(s_past[n], axis=-1, keepdims=True))

    p = jnp.exp(s_own - m)
    l = jnp.sum(p, axis=-1, keepdims=True)
    acc = sum(p[:, t:t + 1] * vn[t:t + 1, :] for t in range(t_new))
    for n in range(n_past):
        vb_t = jnp.concatenate(block_pages(vp, n), axis=1).astype(BF16)
        p = jnp.exp(s_past[n] - m)
        l = l + jnp.sum(p, axis=-1, keepdims=True)
        acc = acc + lax.dot_general(p.astype(BF16), vb_t, _NT, preferred_element_type=F32)

    out = jnp.where(own_head, acc / l, 0.0)
    return jnp.sum(out.reshape(t_new, ATTN_HEADS, ATTN_WIDTH), axis=1)


def _attn_sample(q, kn, vn, og_attn, cache_kt, cache_vt, page_table, t_new, seq_lo, n_seq):
    n_pages = page_table.shape[1]
    page = cache_kt.shape[-1]
    assert n_seq % SEQS_PER_STEP == 0 and seq_lo % SEQS_PER_STEP == 0
    step_rows = SEQS_PER_STEP * t_new
    step_lo = seq_lo // SEQS_PER_STEP
    in_rows = pl.BlockSpec((step_rows, ATTN_WIDTH), lambda i, pt: (step_lo + i, 0))
    out_rows = pl.BlockSpec((step_rows, ATTN_WIDTH), lambda i, pt: (i, 0))

    n_steps = n_seq // SEQS_PER_STEP
    page_buf = pltpu.VMEM((2, SEQS_PER_STEP * n_pages, ATTN_HEADS, HEAD_DIM, page), F32)
    return pl.pallas_call(
        functools.partial(_attn_sample_kernel, n_pages=n_pages, page=page, t_new=t_new,
                          seq_lo=seq_lo, n_steps=n_steps),
        grid_spec=pltpu.PrefetchScalarGridSpec(
            num_scalar_prefetch=1,
            grid=(n_steps,),
            in_specs=([in_rows] * 3 + [pl.BlockSpec(og_attn.shape, lambda i, pt: (0, 0)),
                                       pl.BlockSpec(memory_space=pl.ANY),
                                       pl.BlockSpec(memory_space=pl.ANY)]),
            out_specs=out_rows,
            scratch_shapes=[page_buf, page_buf, pltpu.SemaphoreType.DMA((2,)),
                            pltpu.SemaphoreType.DMA((2,))]),
        out_shape=jax.ShapeDtypeStruct((n_seq * t_new, ATTN_WIDTH), F32),
        compiler_params=pltpu.CompilerParams(
            dimension_semantics=("arbitrary",), vmem_limit_bytes=VMEM_LIMIT),
    )(page_table.reshape(-1), q, kn, vn, og_attn, cache_kt, cache_vt)


def _outproj_kernel(att_ref, gmn_ref, x_ref, wo_ref, nfg_ref, wr_ref, br_ref, x1_ref, *route_refs,
                    grouped):
    mix = (jnp.dot(att_ref[...].astype(BF16), wo_ref[:ATTN_WIDTH, :], preferred_element_type=F32)
           + jnp.dot(gmn_ref[...], wo_ref[ATTN_WIDTH:, :], preferred_element_type=F32))
    x1 = x_ref[...] + mix
    x1_ref[...] = x1
    h2 = _rms(x1, nfg_ref[...])
    h2_hi = h2.astype(BF16)

    h2_lo = (h2 - h2_hi.astype(F32)).astype(BF16)
    hi_dot = jnp.dot(h2_hi, wr_ref[...], preferred_element_type=F32)
    lo_dot = jnp.dot(h2_lo, wr_ref[:, :LANES], preferred_element_type=F32)
    logits = hi_dot[:, :LANES] + hi_dot[:, LANES:] + lo_dot + br_ref[...]
    lt = logits.T
    tm = lt.shape[1]
    row4 = lax.broadcasted_iota(jnp.int32, (MOE_GROUPS, tm), 0)

    def first_argmax(v):
        vmax = jnp.max(v, axis=0, keepdims=True)
        idx = jnp.min(jnp.where(v == vmax, row4, MOE_GROUPS), axis=0, keepdims=True)
        return vmax, idx

    glog = lt[:MOE_GROUPS]
    ge = jnp.exp(glog - jnp.max(glog, axis=0, keepdims=True))
    gprob = ge / jnp.sum(ge, axis=0, keepdims=True)
    p_g, g_idx = first_argmax(gprob)
    elog = lt[MOE_GROUPS:MOE_GROUPS + EXPERTS_PER_GROUP]
    for g in range(1, MOE_GROUPS):
        lo = MOE_GROUPS + g * EXPERTS_PER_GROUP
        elog = jnp.where(g_idx == g, lt[lo:lo + EXPERTS_PER_GROUP], elog)
    l1, i1 = first_argmax(elog)
    l2, i2 = first_argmax(jnp.where(row4 == i1, NEG_INF, elog))
    e2 = jnp.exp(l2 - l1)
    denom = 1.0 + e2
    w1 = (1.0 / denom) * p_g
    w2 = (e2 / denom) * p_g
    if not grouped:
        h2_ref, comb_ref = route_refs
        h2_ref[...] = h2_hi
        lane_row = lax.broadcasted_iota(jnp.int32, (LANES, tm), 0)
        base = g_idx * EXPERTS_PER_GROUP
        comb_t = (jnp.where(lane_row == base + i1, w1, 0.0)
                  + jnp.where(lane_row == base + i2, w2, 0.0))
        comb_ref[...] = comb_t.T
        return

    bucket_ref, wpair_ref = route_refs
    e_lo = jnp.minimum(i1, i2)
    e_hi = jnp.maximum(i1, i2)
    pair = jnp.where(e_lo == 0, e_hi - 1, jnp.where(e_lo == 1, e_hi + 1, PAIRS_PER_GROUP - 1))
    row8 = lax.broadcasted_iota(jnp.int32, (8, tm), 0)
    bucket_ref[...] = jnp.where(row8 == 0, g_idx * PAIRS_PER_GROUP + pair, 0)
    w_lo = jnp.where(i1 < i2, w1, w2)
    w_hi = jnp.where(i1 < i2, w2, w1)
    wpair_ref[...] = jnp.where(row8 == 0, w_lo, jnp.where(row8 == 1, w_hi, 0.0))


def _outproj(att_n, gmn, x, wo_bf, nf_g, w_r, b_r, tm, grouped):
    rows = x.shape[0]
    row_spec = lambda width: pl.BlockSpec((tm, width), lambda i: (i, 0))
    full = lambda a: pl.BlockSpec(a.shape, lambda i: (0,) * a.ndim)
    lane_spec = pl.BlockSpec((8, tm), lambda i: (0, i))
    if grouped:
        route_specs = [lane_spec, lane_spec]
        route_shapes = [jax.ShapeDtypeStruct((8, rows), jnp.int32),
                        jax.ShapeDtypeStruct((8, rows), F32)]
    else:
        route_specs = [row_spec(D_MODEL), row_spec(LANES)]
        route_shapes = [jax.ShapeDtypeStruct((rows, D_MODEL), BF16),
                        jax.ShapeDtypeStruct((rows, LANES), F32)]
    return pl.pallas_call(
        functools.partial(_outproj_kernel, grouped=grouped),
        grid=(rows // tm,),
        in_specs=[row_spec(ATTN_WIDTH), row_spec(MLP_WIDTH), row_spec(D_MODEL), full(wo_bf),
                  full(nf_g), full(w_r), full(b_r)],
        out_specs=[row_spec(D_MODEL)] + route_specs,
        out_shape=[jax.ShapeDtypeStruct((rows, D_MODEL), F32)] + route_shapes,
        compiler_params=pltpu.CompilerParams(
            dimension_semantics=("arbitrary",), vmem_limit_bytes=VMEM_LIMIT),
    )(att_n, gmn, x, wo_bf, nf_g, w_r, b_r)


def _moe_kernel(h_ref, comb_ref, wg_ref, wu_ref, wd_ref, x1_ref, fg_ref, y_ref, acc_ref):
    e = pl.program_id(1)

    @pl.when(e == 0)
    def _():
        acc_ref[...] = jnp.zeros_like(acc_ref)

    h = h_ref[...]
    hg = jnp.dot(h, wg_ref[0], preferred_element_type=F32)
    hu = jnp.dot(h, wu_ref[0], preferred_element_type=F32)
    comb = comb_ref[...]
    lane = lax.broadcasted_iota(jnp.int32, comb.shape, 1)
    c = jnp.sum(jnp.where(lane == e, comb, 0.0), axis=-1, keepdims=True)
    act = hg * (1.0 / (1.0 + jnp.exp(-hg))) * hu * c
    acc_ref[...] += jnp.dot(act.astype(BF16), wd_ref[0], preferred_element_type=F32)

    @pl.when(e == N_EXPERTS - 1)
    def _():
        y_ref[...] = _rms(x1_ref[...] + acc_ref[...], fg_ref[...])


def _moe(h2, comb, wg_bf, wu_bf, wd_bf, x1, final_g, tm):
    rows = h2.shape[0]
    row_spec = lambda width: pl.BlockSpec((tm, width), lambda i, e: (i, 0))
    return pl.pallas_call(
        _moe_kernel,
        grid=(rows // tm, N_EXPERTS),
        in_specs=[row_spec(D_MODEL), row_spec(LANES),
                  pl.BlockSpec((1, D_MODEL, D_EXPERT), lambda i, e: (e, 0, 0)),
                  pl.BlockSpec((1, D_MODEL, D_EXPERT), lambda i, e: (e, 0, 0)),
                  pl.BlockSpec((1, D_EXPERT, D_MODEL), lambda i, e: (e, 0, 0)),
                  row_spec(D_MODEL),
                  pl.BlockSpec(final_g.shape, lambda i, e: (0, 0))],
        out_specs=row_spec(D_MODEL),
        out_shape=jax.ShapeDtypeStruct((rows, D_MODEL), F32),
        scratch_shapes=[pltpu.VMEM((tm, D_MODEL), F32)],
        compiler_params=pltpu.CompilerParams(
            dimension_semantics=("arbitrary", "arbitrary"), vmem_limit_bytes=VMEM_LIMIT),
    )(h2, comb, wg_bf, wu_bf, wd_bf, x1, final_g)


SC_CORES = 2
SC_SUBCORES = 16
SC_WINDOW = 32


def _sc_gather_rows(x, idx):
    n = idx.shape[0]
    width = x.shape[1]
    assert n % SC_WINDOW == 0
    mesh = plsc.VectorSubcoreMesh(core_axis_name="core", subcore_axis_name="subcore",
                                  num_cores=SC_CORES, num_subcores=SC_SUBCORES)

    @pl.kernel(out_type=jax.ShapeDtypeStruct((n, width), x.dtype), mesh=mesh)
    def gather_kernel(x_hbm, idx_hbm, out_hbm):
        def body(idx_vmem, out_vmem):
            pltpu.sync_copy(x_hbm.at[idx_vmem.at[0, pl.ds(0, SC_WINDOW)]], out_vmem)

        pltpu.emit_pipeline(
            body,
            grid=(n // SC_WINDOW,),
            in_specs=[pl.BlockSpec((1, LANES), lambda i: (i, 0))],
            out_specs=[pl.BlockSpec((SC_WINDOW, width), lambda i: (i, 0))],
            core_axis_name=("core", "subcore"),
            dimension_semantics=(pltpu.PARALLEL,),
        )(idx_hbm, out_hbm)

    idx_rows = jnp.pad(idx.reshape(n // SC_WINDOW, SC_WINDOW), ((0, 0), (0, LANES - SC_WINDOW)))
    return gather_kernel(x, idx_rows)


class MoePlan(NamedTuple):
    slot_row: jax.Array
    token_slot: jax.Array
    e_lo: jax.Array
    e_hi: jax.Array
    n_valid: jax.Array
    tile_block: jax.Array
    weight_plan: jax.Array
    slot_cols: jax.Array


def _moe_plan(bucket, wpair, tm):
    rows = bucket.shape[0]
    n_tiles_max = rows // tm + N_BUCKETS
    i32 = jnp.int32
    b_ids = jnp.arange(N_BUCKETS, dtype=i32)[:, None]

    def lookup(table, keys):
        return jnp.sum(jnp.where(keys[None, :] == b_ids, table[:, None], 0), axis=0)

    _, order, wlo_sorted, whi_sorted = lax.sort(
        (bucket, jnp.arange(rows, dtype=i32), wpair[0], wpair[1]), num_keys=1, is_stable=True)
    position = jnp.argsort(order).astype(i32)
    counts = jnp.sum((bucket[None, :] == b_ids).astype(i32), axis=1)
    starts = jnp.cumsum(counts) - counts
    tiles_b = (counts + tm - 1) // tm
    tile_end = jnp.cumsum(tiles_b)
    tile_start = tile_end - tiles_b
    n_tiles = tile_end[-1]
    token_slot = position + lookup(tile_start * tm - starts, bucket)
    t = jnp.arange(n_tiles_max, dtype=i32)
    tb = jnp.sum((jnp.minimum(t, n_tiles - 1)[None, :] >= tile_end[:, None]).astype(i32), axis=0)
    local = (t - lookup(tile_start, tb)) * tm
    n_valid = jnp.where(t < n_tiles, jnp.clip(lookup(counts, tb) - local, 0, tm), 0)
    slot = local[:, None] + jnp.arange(tm, dtype=i32)[None, :]
    src = ((lookup(starts, tb)[:, None] + slot) % rows).reshape(-1)
    assert rows < 2 ** 24
    sorted_cols = jnp.stack([order.astype(F32), wlo_sorted, whi_sorted], axis=1)
    slot_cols = sorted_cols[src]
    table_pad = (0,) * (N_BUCKETS - PAIRS_PER_GROUP)

    def bucket_experts(b):
        group, pair = b // PAIRS_PER_GROUP, b % PAIRS_PER_GROUP
        return (group * EXPERTS_PER_GROUP + lookup(jnp.asarray(PAIR_LO + table_pad, i32), pair),
                group * EXPERTS_PER_GROUP + lookup(jnp.asarray(PAIR_HI + table_pad, i32), pair))

    e_lo, e_hi = bucket_experts(tb)
    used = counts > 0
    b_row = jnp.arange(N_BUCKETS, dtype=i32)
    later_used = (b_row[None, :] > b_row[:, None]) & used[None, :]
    next_used = jnp.min(jnp.where(later_used, b_row[None, :], N_BUCKETS), axis=1)
    w_slot = lookup(jnp.cumsum(used.astype(i32)) - 1, tb) % 2
    first = (t == lookup(tile_start, tb)) & (t < n_tiles)
    next_t = lookup(next_used, tb)
    next_lo, next_hi = bucket_experts(jnp.minimum(next_t, N_BUCKETS - 1))
    weight_plan = jnp.concatenate([w_slot, first.astype(i32), (first & (next_t < N_BUCKETS)).astype(i32),
                                   next_lo, next_hi]).astype(i32)
    return MoePlan(slot_cols[:, 0].astype(i32), token_slot.astype(i32), e_lo.astype(i32),
                   e_hi.astype(i32), n_valid.astype(i32), jnp.minimum(t, n_tiles - 1).astype(i32),
                   weight_plan, slot_cols)


def _moe_grouped_kernel(elo_ref, ehi_ref, nv_ref, blk_ref, wp_ref, x1_ref, cols_ref, wg_hbm, wu_hbm,
                        wd_hbm, nfg_ref, fg_ref, y_ref, wg_buf, wu_buf, wd_buf, sem, *, n_steps):
    del blk_ref
    t = pl.program_id(0)
    slot, first, fetch_next, next_lo, next_hi = (wp_ref[k * n_steps + t] for k in range(5))

    def fetch(e_lo, e_hi, s):
        for src, buf in ((wg_hbm, wg_buf), (wu_hbm, wu_buf), (wd_hbm, wd_buf)):
            pltpu.make_async_copy(src.at[e_lo], buf.at[s, 0], sem.at[s]).start(priority=1)
            pltpu.make_async_copy(src.at[e_hi], buf.at[s, 1], sem.at[s]).start(priority=1)

    @pl.when(t == 0)
    def _():
        fetch(elo_ref[0], ehi_ref[0], slot)

    @pl.when(first == 1)
    def _():
        for src, buf in ((wg_hbm, wg_buf), (wu_hbm, wu_buf), (wd_hbm, wd_buf)):
            pltpu.make_async_copy(src.at[pl.ds(0, 2)], buf.at[slot], sem.at[slot]).wait()

        @pl.when(fetch_next == 1)
        def _():
            fetch(next_lo, next_hi, 1 - slot)

    @pl.when(nv_ref[t] > 0)
    def _():
        x1 = x1_ref[...]
        h = _rms(x1, nfg_ref[...]).astype(BF16)
        slot_cols = cols_ref[...]
        moe = jnp.zeros(x1.shape, F32)
        for e in range(2):
            hg = jnp.dot(h, wg_buf[slot, e], preferred_element_type=F32)
            hu = jnp.dot(h, wu_buf[slot, e], preferred_element_type=F32)
            act = hg * (1.0 / (1.0 + jnp.exp(-hg))) * hu * slot_cols[:, e + 1:e + 2]
            moe = moe + jnp.dot(act.astype(BF16), wd_buf[slot, e], preferred_element_type=F32)
        y_ref[...] = _rms(x1 + moe, fg_ref[...])


def _moe_grouped(x1_slots, plan, wg_bf, wu_bf, wd_bf, nf_g, final_g, tm):
    n_steps = plan.n_valid.shape[0]
    const = lambda a: pl.BlockSpec(a.shape, lambda t, elo, ehi, nv, blk, wp: (0, 0))
    row_spec = lambda width: pl.BlockSpec((tm, width), lambda t, elo, ehi, nv, blk, wp: (blk[t], 0))
    hbm = pl.BlockSpec(memory_space=pl.ANY)
    pair_buf = lambda w: pltpu.VMEM((2, 2) + w.shape[1:], w.dtype)
    return pl.pallas_call(
        functools.partial(_moe_grouped_kernel, n_steps=n_steps),
        grid_spec=pltpu.PrefetchScalarGridSpec(
            num_scalar_prefetch=5,
            grid=(n_steps,),
            in_specs=[row_spec(D_MODEL), row_spec(plan.slot_cols.shape[1]), hbm, hbm, hbm,
                      const(nf_g), const(final_g)],
            out_specs=row_spec(D_MODEL),
            scratch_shapes=[pair_buf(wg_bf), pair_buf(wu_bf), pair_buf(wd_bf),
                            pltpu.SemaphoreType.DMA((2,))]),
        out_shape=jax.ShapeDtypeStruct(x1_slots.shape, F32),
        compiler_params=pltpu.CompilerParams(
            dimension_semantics=("arbitrary",), vmem_limit_bytes=VMEM_LIMIT),
    )(plan.e_lo, plan.e_hi, plan.n_valid, plan.tile_block, plan.weight_plan, x1_slots,
      plan.slot_cols, wg_bf, wu_bf, wd_bf, nf_g, final_g)


def _spatial_operands(w_s, b_s, t_chunk):
    reps = CHUNK // t_chunk
    idx = jnp.arange(CHUNK)
    same = (idx[:, None] // t_chunk) == (idx[None, :] // t_chunk)
    causal = (idx[None, :] % t_chunk) <= (idx[:, None] % t_chunk)
    wm = jnp.tile(w_s[:, :t_chunk, :t_chunk], (1, reps, reps)) * (same & causal)
    wm_cat = wm.transpose(1, 0, 2).reshape(CHUNK, MLP_GROUPS * CHUNK).astype(BF16)
    bias = jnp.repeat(jnp.tile(b_s[:, :t_chunk], (1, reps)).T, MLP_CH, axis=1)
    return wm_cat, bias


def kernel(x_prompt, x_sample, cache_k, cache_v, page_table, norm_attn_g, w_in, sgu_g, w_spatial,
           b_spatial, out_g_attn, out_g_mlp, w_out, norm_ffn_g, w_group, b_group, w_router, b_router,
           w_gate, w_up, w_down, final_g):
    depth = w_in.shape[0]
    assert depth == 1, "single decoder layer"
    batch, seq, _ = x_prompt.shape
    n_seq, t_new, _ = x_sample.shape
    assert seq % MOBA_BLOCK == 0 and seq % CHUNK == 0 and CHUNK % t_new == 0
    assert MOBA_BLOCK % cache_k.shape[2] == 0

    row2 = lambda a: a.reshape(1, -1)
    w_in_bf = w_in[0].astype(BF16)
    w_out_bf = w_out[0].astype(BF16)
    n_logits = MOE_GROUPS + N_EXPERTS
    w_r = jnp.concatenate(
        [w_group[0], w_router[0].transpose(1, 0, 2).reshape(D_MODEL, N_EXPERTS),
         jnp.zeros((D_MODEL, LANES - n_logits), F32)], axis=1)
    b_r = jnp.concatenate(
        [b_group[0], b_router[0].reshape(-1), jnp.zeros((LANES - n_logits,), F32)]).reshape(1, LANES)
    w_r_hi = w_r.astype(BF16)
    w_r_hl = jnp.concatenate([w_r_hi, (w_r - w_r_hi.astype(F32)).astype(BF16)], axis=1)
    wkv_t_bf = w_in[0][:, ATTN_WIDTH:3 * ATTN_WIDTH].T.astype(BF16)
    ck_t = jnp.transpose(cache_k[0], (0, 2, 3, 1))
    cv_t = jnp.transpose(cache_v[0], (0, 2, 3, 1))

    def project_in(x, t_chunk, tm, seq_transposed, cast_along=(), emit_vgn=True):
        wm_cat, bias_full = _spatial_operands(w_spatial[0], b_spatial[0], t_chunk)
        return _inproj(x, row2(norm_attn_g[0]), w_in_bf, wkv_t_bf, row2(sgu_g[0]), wm_cat, bias_full,
                       row2(out_g_mlp[0]), tm, seq_transposed, cast_along, emit_vgn)

    def project_out(att_n, gmn, x, tm, grouped):
        return _outproj(att_n, gmn, x, w_out_bf, row2(norm_ffn_g[0]), w_r_hl, b_r, tm, grouped)

    og_attn = row2(out_g_attn[0])

    xp = x_prompt.reshape(batch * seq, D_MODEL)
    qp, kp_t, vp_t, gmn_p, wg_bf, wu_bf, wd_bf = project_in(
        xp, CHUNK, TM_PROJ_PROMPT, seq, (w_gate[0], w_up[0], w_down[0]), emit_vgn=False)
    att_p = _attn_prompt(qp, kp_t, vp_t, og_attn, batch, seq)
    x1_p, bucket_p, wpair_p = project_out(att_p, gmn_p, xp, TM_PROJ_PROMPT, True)
    plan = _moe_plan(bucket_p[0], wpair_p[:2], TM_MOE_PROMPT)
    x1_slots = _sc_gather_rows(x1_p, plan.slot_row)

    xs = x_sample.reshape(n_seq * t_new, D_MODEL)
    qs, ks, vs, gmn_s, gvs = project_in(xs, t_new, TM_PROJ_SAMPLE, None)
    half = n_seq // 2
    attend_half = lambda q_all, lo: _attn_sample(q_all, ks, vs, og_attn, ck_t, cv_t, page_table,
                                                 t_new, lo, half)
    att_s0 = attend_half(qs, 0)
    x1_slots, att_s0 = lax.optimization_barrier((x1_slots, att_s0))
    y_slots = _moe_grouped(x1_slots, plan, wg_bf, wu_bf, wd_bf, row2(norm_ffn_g[0]),
                           row2(final_g), TM_MOE_PROMPT)
    y_slots, qs_late = lax.optimization_barrier((y_slots, qs))
    yp = _sc_gather_rows(y_slots, plan.token_slot)
    att_s1 = attend_half(qs_late, half)
    att_s = jnp.concatenate([att_s0, att_s1], axis=0)
    x1_s, h2_s, comb_s = project_out(att_s, gmn_s, xs, TM_PROJ_SAMPLE, False)
    ys = _moe(h2_s, comb_s, wg_bf, wu_bf, wd_bf, x1_s, row2(final_g), TM_MOE_SAMPLE)

    heads = (ATTN_HEADS, HEAD_DIM)
    rows_last = lambda a_t: a_t.reshape(batch, *heads, seq).transpose(0, 3, 1, 2)[None]
    return (yp.reshape(batch, seq, D_MODEL),
            ys.reshape(n_seq, t_new, D_MODEL),
            rows_last(kp_t),
            rows_last(vp_t),
            ks.reshape(depth, n_seq, t_new, *heads),
            vs.reshape(depth, n_seq, t_new, *heads),
            gvs.reshape(depth, n_seq, t_new, MLP_WIDTH))
```
